```python
import math
import jax, jax.numpy as jnp
from jax import lax
import numpy as np

D_MODEL = 4096
BATCH = 1
SEQ = 8192
DEPTH = 1

N_META = 16
POOL_WIDTH = D_MODEL
POOL_WINDOWS = (2, 4, 8, 16)
POOL_GROUP = POOL_WIDTH // len(POOL_WINDOWS)
POOL_OUT_GROUP = D_MODEL // len(POOL_WINDOWS)
SSD_EXPAND = 2
D_INNER = SSD_EXPAND * D_MODEL
SSD_HEAD_DIM = 64
SSD_HEADS = D_INNER // SSD_HEAD_DIM
SSD_STATE = 128
SSD_GROUPS = 8
HEADS_PER_GROUP = SSD_HEADS // SSD_GROUPS
CONV_WIDTH = 4
CONV_CH = D_INNER + 2 * SSD_GROUPS * SSD_STATE
CHUNK = 128
N_EXPERTS = 32
TOP_K = 4
D_EXPERT = 7 * D_MODEL // 16
SWIGLU_LIMIT = 7.0
SWIGLU_ALPHA = 1.702
MOE_BLOCK = 256
NORM_EPS = 1e-5
SPLIT_POINTS = (
    POOL_WIDTH,
    POOL_WIDTH + D_INNER,
    POOL_WIDTH + D_INNER + CONV_CH,
    POOL_WIDTH + D_INNER + CONV_CH + SSD_HEADS,
    POOL_WIDTH + D_INNER + CONV_CH + SSD_HEADS + D_MODEL,
)
IN_COLS = POOL_WIDTH + D_INNER + CONV_CH + SSD_HEADS + 2 * D_MODEL

kernel_name = "hybrid_pool_ssd_moe_block"


def rms_norm(v, w):
    v32 = v.astype(jnp.float32)
    out = v32 * lax.rsqrt(jnp.mean(v32 * v32, axis=-1, keepdims=True) + NORM_EPS)
    return (out * w.astype(jnp.float32)).astype(v.dtype)


def causal_depthwise_conv(v, w, bias):
    L = v.shape[1]
    vp = jnp.pad(v, ((0, 0), (CONV_WIDTH - 1, 0), (0, 0)))
    out = bias
    for k in range(CONV_WIDTH):
        out = out + w[k] * vp[:, k:k + L]
    return out


def pool_mixer(u, pool_w, pool_scale):
    L = u.shape[1]
    u32 = u.astype(jnp.float32)
    cs = jnp.cumsum(u32, axis=1)
    t = jnp.arange(L)
    outs = []
    for gi, win in enumerate(POOL_WINDOWS):
        sl = slice(gi * POOL_GROUP, (gi + 1) * POOL_GROUP)
        csg = cs[..., sl]
        prev = jnp.pad(csg, ((0, 0), (win, 0), (0, 0)))[:, :L]
        cnt = jnp.minimum(t + 1, win).astype(jnp.float32)
        pooled = (csg - prev) / cnt[None, :, None] - u32[..., sl]
        outs.append(jnp.einsum("blc,cd->bld", pooled, pool_w[gi]))
    return jnp.concatenate(outs, axis=-1) * pool_scale


def ssd_chunked(xh, dt, a, bm, cm):
    b, Lp = xh.shape[0], xh.shape[1]
    nc = Lp // CHUNK
    G, R, P, N = SSD_GROUPS, HEADS_PER_GROUP, SSD_HEAD_DIM, SSD_STATE
    a_dt = (dt * a).reshape(b, nc, CHUNK, G, R)
    xdt = (xh * dt[..., None]).reshape(b, nc, CHUNK, G, R, P)
    bc = bm.reshape(b, nc, CHUNK, G, N)
    cc = cm.reshape(b, nc, CHUNK, G, N)
    a_cs = jnp.cumsum(a_dt, axis=2)
    seg = a_cs[:, :, :, None] - a_cs[:, :, None, :]
    causal = jnp.tril(jnp.ones((CHUNK, CHUNK), dtype=bool))[None, None, :, :, None, None]
    decay = jnp.exp(jnp.where(causal, seg, -jnp.inf))
    cb = jnp.einsum("bclgn,bcsgn->bclsg", cc, bc)
    y_diag = jnp.einsum("bclsgr,bcsgrp->bclgrp", cb[..., None] * decay, xdt)
    decay_to_end = jnp.exp(a_cs[:, :, -1:] - a_cs)
    states = jnp.einsum("bcsgn,bcsgr,bcsgrp->bcgrpn", bc, decay_to_end, xdt)
    chunk_decay = jnp.exp(a_cs[:, :, -1])

    def step(h, inp):
        d, s = inp
        return d[..., None, None] * h + s, h

    h0 = jnp.zeros((b, G, R, P, N), states.dtype)
    _, h_in = lax.scan(step, h0, (jnp.moveaxis(chunk_decay, 1, 0), jnp.moveaxis(states, 1, 0)))
    h_in = jnp.moveaxis(h_in, 0, 1)
    y_off = jnp.einsum("bclgn,bcgrpn,bclgr->bclgrp", cc, h_in, jnp.exp(a_cs))
    return (y_diag + y_off).reshape(b, Lp, SSD_HEADS, P)


def ssd_branch(z, xbc, dt_raw, conv_w, conv_b, dt_bias, a_log, d_skip, ssd_norm_w, w_ssd_out):
    b, L = z.shape[0], z.shape[1]
    xbc = jax.nn.silu(causal_depthwise_conv(xbc, conv_w, conv_b))
    xs, bm, cm = jnp.split(xbc, [D_INNER, D_INNER + SSD_GROUPS * SSD_STATE], axis=-1)
    xh = xs.reshape(b, L, SSD_HEADS, SSD_HEAD_DIM)
    bm = bm.reshape(b, L, SSD_GROUPS, SSD_STATE)
    cm = cm.reshape(b, L, SSD_GROUPS, SSD_STATE)
    dt = jax.nn.softplus(dt_raw.astype(jnp.float32) + dt_bias.astype(jnp.float32))
    a = -jnp.exp(a_log.astype(jnp.float32))
    pad = CHUNK - N_META
    p4 = ((0, 0), (pad, 0), (0, 0), (0, 0))
    y = ssd_chunked(jnp.pad(xh, p4), jnp.pad(dt, ((0, 0), (pad, 0), (0, 0))), a,
                    jnp.pad(bm, p4), jnp.pad(cm, p4))[:, pad:]
    y = y + d_skip.astype(jnp.float32)[:, None] * xh
    y = y.reshape(b, L, D_INNER).astype(z.dtype)
    y = rms_norm(y * jax.nn.silu(z), ssd_norm_w)
    return jnp.einsum("ble,ed->bld", y, w_ssd_out)


def hybrid_mixer(u, w_in, conv_w, conv_b, dt_bias, a_log, d_skip, ssd_norm_w, w_ssd_out,
                 pool_w, pool_scale, w_out):
    proj = jnp.einsum("bld,dc->blc", u, w_in)
    u_pool, z, xbc, dt_raw, g_a, g_b = jnp.split(proj, list(SPLIT_POINTS), axis=-1)
    branch_a = pool_mixer(u_pool, pool_w, pool_scale)
    branch_b = ssd_branch(z, xbc, dt_raw, conv_w, conv_b, dt_bias, a_log, d_skip,
                          ssd_norm_w, w_ssd_out)
    merged = jax.nn.sigmoid(g_a) * branch_a + jax.nn.sigmoid(g_b) * branch_b
    return jnp.einsum("bld,de->ble", merged.astype(u.dtype), w_out)


def moe_ffn(h, router_w, router_b, w_up, b_up, w_down, b_down):
    b, L, D = h.shape
    T = b * L
    xt = h.reshape(T, D)
    logits = (jnp.einsum("td,de->te", xt, router_w) + router_b).astype(jnp.float32)
    top_val, top_idx = lax.top_k(logits, TOP_K)
    gates = jax.nn.softmax(top_val, axis=-1)
    flat_e = top_idx.reshape(-1)
    flat_tok = jnp.arange(T * TOP_K, dtype=jnp.int32) // TOP_K
    flat_g = gates.reshape(-1)
    order = jnp.argsort(flat_e)
    sorted_e = flat_e[order]
    counts = jnp.bincount(flat_e, length=N_EXPERTS)
    starts = jnp.cumsum(counts) - counts
    padded = ((counts + MOE_BLOCK - 1) // MOE_BLOCK) * MOE_BLOCK
    pends = jnp.cumsum(padded)
    pstarts = pends - padded
    rank = jnp.arange(T * TOP_K) - starts[sorted_e]
    dest = pstarts[sorted_e] + rank
    n_blocks = -(-(T * TOP_K) // MOE_BLOCK) + N_EXPERTS
    buf_tok = jnp.full((n_blocks * MOE_BLOCK,), T, jnp.int32).at[dest].set(flat_tok[order])
    buf_g = jnp.zeros((n_blocks * MOE_BLOCK,), jnp.float32).at[dest].set(flat_g[order])
    block_e = jnp.minimum(
        jnp.searchsorted(pends, jnp.arange(n_blocks) * MOE_BLOCK, side="right"), N_EXPERTS - 1)
    x_pad = jnp.concatenate([xt, jnp.zeros((1, D), xt.dtype)], axis=0)

    def block_fn(args):
        tok, g, e = args
        xb = x_pad[tok]
        gu = xb @ w_up[e] + b_up[e]
        gate, up = gu[:, :D_EXPERT], gu[:, D_EXPERT:]
        gate = jnp.minimum(gate, SWIGLU_LIMIT)
        up = jnp.clip(up, -SWIGLU_LIMIT, SWIGLU_LIMIT)
        act = (up + 1.0) * (gate * jax.nn.sigmoid(SWIGLU_ALPHA * gate))
        yb = act @ w_down[e] + b_down[e]
        return yb * g[:, None]

    y_blocks = lax.map(block_fn, (buf_tok.reshape(n_blocks, MOE_BLOCK),
                                  buf_g.reshape(n_blocks, MOE_BLOCK), block_e))
    out = jnp.zeros((T + 1, D), y_blocks.dtype).at[buf_tok].add(y_blocks.reshape(-1, D))
    return out[:T].reshape(b, L, D).astype(h.dtype)


def setup_inputs(seed: int = 0) -> dict:
    key = jax.random.key(seed)
    ks = jax.random.split(key, 24)
    f32 = jnp.float32
    nrm = lambda k, shape, s: jax.random.normal(k, shape, f32) * s
    dt_init = jnp.exp(jax.random.uniform(ks[5], (DEPTH, SSD_HEADS), f32,
                                         math.log(1e-3), math.log(1e-1)))
    dt_bias = dt_init + jnp.log(-jnp.expm1(-dt_init))
    return {
        "x": nrm(ks[0], (BATCH, SEQ, D_MODEL), 1.0),
        "meta_tokens": nrm(ks[1], (N_META, D_MODEL), 1.0),
        "norm_mix_w": 1.0 + nrm(ks[2], (DEPTH, D_MODEL), 0.02),
        "w_in": nrm(ks[3], (DEPTH, D_MODEL, IN_COLS), D_MODEL ** -0.5),
        "conv_w": nrm(ks[4], (DEPTH, CONV_WIDTH, CONV_CH), CONV_WIDTH ** -0.5),
        "conv_b": nrm(ks[6], (DEPTH, CONV_CH), 0.01),
        "dt_bias": dt_bias,
        "a_log": jnp.log(jax.random.uniform(ks[7], (DEPTH, SSD_HEADS), f32, 1.0, 16.0)),
        "d_skip": 1.0 + nrm(ks[8], (DEPTH, SSD_HEADS), 0.1),
        "ssd_norm_w": 1.0 + nrm(ks[9], (DEPTH, D_INNER), 0.02),
        "w_ssd_out": nrm(ks[10], (DEPTH, D_INNER, D_MODEL), D_INNER ** -0.5),
        "pool_w": nrm(ks[11], (DEPTH, len(POOL_WINDOWS), POOL_GROUP, POOL_OUT_GROUP),
                      POOL_GROUP ** -0.5),
        "pool_scale": 1.0 + nrm(ks[12], (DEPTH, D_MODEL), 0.02),
        "w_out": nrm(ks[13], (DEPTH, D_MODEL, D_MODEL), D_MODEL ** -0.5),
        "norm_ffn_w": 1.0 + nrm(ks[14], (DEPTH, D_MODEL), 0.02),
        "router_w": nrm(ks[15], (DEPTH, D_MODEL, N_EXPERTS), D_MODEL ** -0.5),
        "router_b": nrm(ks[16], (DEPTH, N_EXPERTS), 0.01),
        "w_up": nrm(ks[17], (DEPTH, N_EXPERTS, D_MODEL, 2 * D_EXPERT), D_MODEL ** -0.5),
        "b_up": nrm(ks[18], (DEPTH, N_EXPERTS, 2 * D_EXPERT), 0.01),
        "w_down": nrm(ks[19], (DEPTH, N_EXPERTS, D_EXPERT, D_MODEL), D_EXPERT ** -0.5),
        "b_down": nrm(ks[20], (DEPTH, N_EXPERTS, D_MODEL), 0.01),
        "norm_final_w": 1.0 + nrm(ks[21], (D_MODEL,), 0.02),
    }


def reference(x, meta_tokens, norm_mix_w, w_in, conv_w, conv_b, dt_bias, a_log, d_skip,
              ssd_norm_w, w_ssd_out, pool_w, pool_scale, w_out, norm_ffn_w, router_w,
              router_b, w_up, b_up, w_down, b_down, norm_final_w):
    b = x.shape[0]
    meta = jnp.broadcast_to(meta_tokens.astype(x.dtype)[None], (b, N_META, D_MODEL))
    h = jnp.concatenate([meta, x], axis=1)
    for layer in range(DEPTH):
        u = rms_norm(h, norm_mix_w[layer])
        h = h + hybrid_mixer(u, w_in[layer], conv_w[layer], conv_b[layer], dt_bias[layer],
                             a_log[layer], d_skip[layer], ssd_norm_w[layer], w_ssd_out[layer],
                             pool_w[layer], pool_scale[layer], w_out[layer]).astype(h.dtype)
        u = rms_norm(h, norm_ffn_w[layer])
        h = h + moe_ffn(u, router_w[layer], router_b[layer], w_up[layer], b_up[layer],
                        w_down[layer], b_down[layer]).astype(h.dtype)
    h = rms_norm(h, norm_final_w)
    return h[:, N_META:]
```

```python
import functools

import jax
import jax.numpy as jnp
from jax import lax
from jax.experimental import pallas as pl
from jax.experimental.pallas import tpu as pltpu

f32 = jnp.float32
bf16 = jnp.bfloat16
i32 = jnp.int32
u32 = jnp.uint32

D_MODEL = 4096
SEQ = 8192
N_META = 16
CHUNK = 128
PAD_ROWS = CHUNK - N_META
R = PAD_ROWS + N_META + SEQ
N_CHUNKS = R // CHUNK
POOL_WINDOWS = (2, 4, 8, 16)
POOL_GROUP = D_MODEL // 4
D_INNER = 2 * D_MODEL
HEAD_DIM = 64
N_HEADS = D_INNER // HEAD_DIM
N_STATE = 128
N_GROUPS = 8
HEADS_PER_GROUP = N_HEADS // N_GROUPS
GROUP_W = HEADS_PER_GROUP * HEAD_DIM
CONV_W = 4
HALO = 16
N_EXPERTS = 32
TOP_K = 4
D_EXPERT = 7 * D_MODEL // 16
LIMIT = 7.0
ALPHA = 1.702
EPS = 1e-5
N_TOK = N_META + SEQ
N_PAIRS = N_TOK * TOP_K
MOE_TM = 256
N_TILES = -(-N_PAIRS // MOE_TM) + N_EXPERTS
N_SLOTS = N_TILES * MOE_TM
COL_Z = D_MODEL
COL_X = COL_Z + D_INNER
COL_B = COL_X + D_INNER
COL_C = COL_B + N_GROUPS * N_STATE
COL_DT = COL_C + N_GROUPS * N_STATE
COL_G = COL_DT + N_HEADS
SEG1_COLS = COL_DT
V7X_VMEM_LIMIT = 56 * 1024 * 1024


def _params(sem, vmem=V7X_VMEM_LIMIT):
    return pltpu.CompilerParams(dimension_semantics=sem, vmem_limit_bytes=vmem)


def _sigmoid(v):
    return 1.0 / (1.0 + jnp.exp(-v))


def _rms_kernel(h_ref, w_ref, o_ref):
    h = h_ref[...]
    ms = jnp.mean(h * h, axis=-1, keepdims=True)
    o_ref[...] = (h * lax.rsqrt(ms + EPS) * w_ref[...]).astype(o_ref.dtype)


def _rms_norm_rows(h, w, tm=320):
    rows, d = h.shape
    return pl.pallas_call(
        _rms_kernel,
        grid=(rows // tm,),
        in_specs=[pl.BlockSpec((tm, d), lambda i: (i, 0)), pl.BlockSpec((1, d), lambda i: (0, 0))],
        out_specs=pl.BlockSpec((tm, d), lambda i: (i, 0)),
        out_shape=jax.ShapeDtypeStruct((rows, d), bf16),
        compiler_params=_params(("parallel",)),
        name="rms_mix",
    )(h, w.reshape(1, d))


def _mm_kernel(*refs, n_extra, epilogue, nk):
    a_ref, w_ref = refs[0], refs[1]
    extra = refs[2:2 + n_extra]
    o_ref = refs[2 + n_extra]
    acc_ref = refs[3 + n_extra]
    k = pl.program_id(2)

    @pl.when(k == 0)
    def _():
        acc_ref[...] = jnp.zeros_like(acc_ref)

    acc_ref[...] += jnp.dot(a_ref[...], w_ref[...].astype(bf16), preferred_element_type=f32)

    @pl.when(k == nk - 1)
    def _():
        o_ref[...] = epilogue(acc_ref[...], *[e[...] for e in extra]).astype(o_ref.dtype)


def _matmul(a, w, *, col0, n, tm, tn, tk, out_dtype, epilogue, extras=(), extra_specs=(), name):
    m, kdim = a.shape
    nk = kdim // tk
    if col0 % tn == 0:
        w_spec = pl.BlockSpec((tk, tn), lambda i, j, k: (k, j + col0 // tn))
    else:
        w_spec = pl.BlockSpec((pl.Element(tk), pl.Element(tn)), lambda i, j, k: (k * tk, pl.multiple_of(col0 + j * tn, 128)))
    return pl.pallas_call(
        functools.partial(_mm_kernel, n_extra=len(extras), epilogue=epilogue, nk=nk),
        grid=(m // tm, n // tn, nk),
        in_specs=[pl.BlockSpec((tm, tk), lambda i, j, k: (i, k)), w_spec, *extra_specs],
        out_specs=pl.BlockSpec((tm, tn), lambda i, j, k: (i, j)),
        out_shape=jax.ShapeDtypeStruct((m, n), out_dtype),
        scratch_shapes=[pltpu.VMEM((tm, tn), f32)],
        compiler_params=_params(("parallel", "parallel", "arbitrary")),
        name=name,
    )(a, w, *extras)


def _softplus_bias(acc, bias):
    v = acc + bias
    return jnp.maximum(v, 0.0) + jnp.log1p(jnp.exp(-jnp.abs(v)))


def _in_proj_main(u, w_in):
    return _matmul(u, w_in, col0=0, n=SEG1_COLS, tm=1664, tn=1024, tk=1024, out_dtype=bf16,
                   epilogue=lambda acc: acc, name="in_proj_main")


def _in_proj_dt(u, w_in, dt_bias):
    return _matmul(u, w_in, col0=COL_DT, n=N_HEADS, tm=1664, tn=N_HEADS, tk=1024, out_dtype=f32,
                   epilogue=_softplus_bias, extras=(dt_bias.reshape(1, N_HEADS),),
                   extra_specs=(pl.BlockSpec((1, N_HEADS), lambda i, j, k: (0, 0)),), name="in_proj_dt")


def _in_proj_gates(u, w_in):
    return _matmul(u, w_in, col0=COL_G, n=2 * D_MODEL, tm=1664, tn=1024, tk=1024, out_dtype=bf16,
                   epilogue=_sigmoid, name="in_proj_gates")


def _ssd_out_merge(vw, w_ssd_out, ssq, ma, gates, tm=832, tn=1024):
    gb_off = D_MODEL // tn

    def merge(acc, ssq_t, ma_t, gb_t):
        rs = lax.rsqrt(ssq_t[:, :1] * (1.0 / D_INNER) + EPS)
        return ma_t.astype(f32) + gb_t.astype(f32) * (acc * rs)

    return _matmul(vw, w_ssd_out, col0=0, n=D_MODEL, tm=tm, tn=tn, tk=1024, out_dtype=bf16, epilogue=merge,
                   extras=(ssq, ma, gates),
                   extra_specs=(pl.BlockSpec((tm, 128), lambda i, j, k: (i, 0)),
                                pl.BlockSpec((tm, tn), lambda i, j, k: (i, j)),
                                pl.BlockSpec((tm, tn), lambda i, j, k: (i, j + gb_off))),
                   name="ssd_out_merge")


def _out_proj(merged, w_out, h0, tm=832, tn=1024):
    return _matmul(merged, w_out, col0=0, n=D_MODEL, tm=tm, tn=tn, tk=1024, out_dtype=f32,
                   epilogue=lambda acc, res: res + acc, extras=(h0,),
                   extra_specs=(pl.BlockSpec((tm, tn), lambda i, j, k: (i, j)),), name="out_proj_residual")


def _pool_kernel(x_ref, halo_ref, w_ref, scale_ref, ga_ref, o_ref, ext_ref, pooled_ref, *, tm):
    gi = pl.program_id(0)
    i = pl.program_id(1)
    halo = halo_ref[...].astype(f32)
    ext_ref[0:HALO, :] = jnp.where(i == 0, 0.0, halo)
    ext_ref[HALO:HALO + tm, :] = x_ref[...].astype(f32)
    t = i * tm + lax.broadcasted_iota(i32, (tm, 1), 0) - PAD_ROWS
    for g, win in enumerate(POOL_WINDOWS):

        @pl.when(gi == g)
        def _(win=win):
            x = ext_ref[HALO:HALO + tm, :]
            s = x
            for k in range(1, win):
                s = s + ext_ref[HALO - k:HALO - k + tm, :]
            cnt = jnp.clip(t + 1, 1, win).astype(f32)
            pooled_ref[...] = (s / cnt - x).astype(bf16)

    acc = jnp.dot(pooled_ref[...], w_ref[0].astype(bf16), preferred_element_type=f32)
    o_ref[...] = (ga_ref[...].astype(f32) * (acc * scale_ref[...])).astype(o_ref.dtype)


def _pool_branch(proj, gates, pool_w, pool_scale, tm=640):
    gw = POOL_GROUP
    return pl.pallas_call(
        functools.partial(_pool_kernel, tm=tm),
        grid=(4, R // tm),
        in_specs=[
            pl.BlockSpec((tm, gw), lambda g, i: (i, g)),
            pl.BlockSpec((HALO, gw), lambda g, i: (jnp.maximum(i * (tm // HALO) - 1, 0), g)),
            pl.BlockSpec((1, gw, gw), lambda g, i: (g, 0, 0)),
            pl.BlockSpec((1, gw), lambda g, i: (0, g)),
            pl.BlockSpec((tm, gw), lambda g, i: (i, g)),
        ],
        out_specs=pl.BlockSpec((tm, gw), lambda g, i: (i, g)),
        out_shape=jax.ShapeDtypeStruct((R, D_MODEL), bf16),
        scratch_shapes=[pltpu.VMEM((HALO + tm, gw), f32), pltpu.VMEM((tm, gw), bf16)],
        compiler_params=_params(("parallel", "arbitrary")),
        name="pool_branch",
    )(proj, proj, pool_w, pool_scale.reshape(1, D_MODEL), gates)


def _conv_silu(cur_ref, halo_ref, w_ref, b_ref, ext_ref, c, row_ok):
    halo = halo_ref[...].astype(f32)
    ext_ref[0:HALO, :] = jnp.where(c == 0, 0.0, halo)
    ext_ref[HALO:HALO + CHUNK, :] = cur_ref[...].astype(f32)
    out = b_ref[...]
    for k in range(CONV_W):
        off = HALO - (CONV_W - 1) + k
        out = out + w_ref[k:k + 1, :] * ext_ref[off:off + CHUNK, :]
    out = out * _sigmoid(out)
    return jnp.where(row_ok, out, 0.0)


def _ssd_kernel(x_ref, xh_ref, b_ref, bh_ref, c_ref, ch_ref, z_ref, dt_ref, dtt_ref,
                alog_ref, alogt_ref, wx_ref, wb_ref, wc_ref, bx_ref, bb_ref, bc_ref,
                dskip_ref, nw_ref,
                vw_ref, ssq_ref,
                state_ref, extx_ref, extb_ref, extc_ref, xd2_ref, cd_ref):
    c = pl.program_id(0)
    g = pl.program_id(1)

    @pl.when(jnp.logical_and(c == 0, g == 0))
    def _():
        state_ref[...] = jnp.zeros_like(state_ref)

    rows = c * CHUNK + lax.broadcasted_iota(i32, (CHUNK, 1), 0)
    row_ok = rows >= PAD_ROWS
    cols = c * CHUNK + lax.broadcasted_iota(i32, (1, CHUNK), 1)
    col_ok = cols >= PAD_ROWS

    xs = _conv_silu(x_ref, xh_ref, wx_ref, bx_ref, extx_ref, c, row_ok)
    bm = _conv_silu(b_ref, bh_ref, wb_ref, bb_ref, extb_ref, c, row_ok)
    cm = _conv_silu(c_ref, ch_ref, wc_ref, bc_ref, extc_ref, c, True)

    dt = jnp.where(row_ok, dt_ref[0], 0.0)
    dtt = jnp.where(col_ok, dtt_ref[0], 0.0)
    a_dt = dt * (-jnp.exp(alog_ref[0]))
    a_dtt = dtt * (-jnp.exp(alogt_ref[0]))
    li = lax.broadcasted_iota(i32, (CHUNK, CHUNK), 0)
    si = lax.broadcasted_iota(i32, (CHUNK, CHUNK), 1)
    causal = li >= si
    tril = causal.astype(f32)
    triu = (li <= si).astype(f32)
    a_cs = jnp.dot(tril, a_dt, preferred_element_type=f32, precision=lax.Precision.HIGHEST)
    a_cst = jnp.dot(a_dtt, triu, preferred_element_type=f32, precision=lax.Precision.HIGHEST)
    last = a_cs[CHUNK - 1:CHUNK, :]
    e_cs = jnp.exp(a_cs)
    e_end = jnp.exp(last - a_cs)
    e_last = jnp.exp(last)

    bmb = bm.astype(bf16)
    cmb = cm.astype(bf16)
    cb = lax.dot_general(cmb, bmb, (((1,), (1,)), ((), ())), preferred_element_type=f32)
    state = state_ref[g]
    yoff = jnp.dot(cmb, state.astype(bf16), preferred_element_type=f32)

    lane = lax.broadcasted_iota(i32, (CHUNK, 2 * HEAD_DIM), 1)
    first = lane < HEAD_DIM
    lane1 = lax.broadcasted_iota(i32, (1, 2 * HEAD_DIM), 1)
    first1 = lane1 < HEAD_DIM
    ssq = jnp.zeros((CHUNK, 1), f32)
    for q in range(HEADS_PER_GROUP // 2):
        ha, hb = 2 * q, 2 * q + 1
        sl = slice(q * 2 * HEAD_DIM, (q + 1) * 2 * HEAD_DIM)
        xq = xs[:, sl]
        dtp = jnp.where(first, dt[:, ha:ha + 1], dt[:, hb:hb + 1])
        xdt = xq * dtp
        xdtb = xdt.astype(bf16)
        wa = cb * jnp.exp(jnp.where(causal, a_cs[:, ha:ha + 1] - a_cst[ha:ha + 1, :], -1e30))
        wb = cb * jnp.exp(jnp.where(causal, a_cs[:, hb:hb + 1] - a_cst[hb:hb + 1, :], -1e30))
        ya = jnp.dot(wa.astype(bf16), xdtb, preferred_element_type=f32)
        yb = jnp.dot(wb.astype(bf16), xdtb, preferred_element_type=f32)
        ecs = jnp.where(first, e_cs[:, ha:ha + 1], e_cs[:, hb:hb + 1])
        y = jnp.where(first, ya, yb) + yoff[:, sl] * ecs + dskip_ref[:, sl] * xq
        eend = jnp.where(first, e_end[:, ha:ha + 1], e_end[:, hb:hb + 1])
        xd2_ref[:, sl] = (xdt * eend).astype(bf16)
        cd_ref[:, sl] = jnp.where(first1, e_last[:, ha:ha + 1], e_last[:, hb:hb + 1])
        zq = z_ref[:, sl].astype(f32)
        v = y * (zq * _sigmoid(zq))
        ssq = ssq + jnp.sum(v * v, axis=1, keepdims=True)
        vw_ref[:, sl] = (v * nw_ref[:, sl]).astype(vw_ref.dtype)

    bt = jnp.transpose(bm).astype(bf16)
    s_new = jnp.dot(bt, xd2_ref[...], preferred_element_type=f32)
    state_ref[g] = state * cd_ref[...] + s_new

    ssq_b = jnp.broadcast_to(ssq, (CHUNK, 128))

    @pl.when(g == 0)
    def _():
        ssq_ref[...] = ssq_b

    @pl.when(g > 0)
    def _():
        ssq_ref[...] += ssq_b


def _ssd_branch(proj, dt, conv_w, conv_b, a_log, d_skip, ssd_norm_w):
    gwb = GROUP_W // GROUP_W
    xb0 = COL_X // GROUP_W
    zb0 = COL_Z // GROUP_W
    bb0 = COL_B // N_STATE
    cb0 = COL_C // N_STATE
    hpc = CHUNK // HALO

    def cur(b0):
        return lambda c, g: (c, b0 + g * gwb)

    def halo(b0):
        return lambda c, g: (jnp.maximum(c * hpc - 1, 0), b0 + g * gwb)

    dt_g = dt.reshape(R, N_GROUPS, HEADS_PER_GROUP).transpose(1, 0, 2)
    dt_gt = dt_g.transpose(0, 2, 1)
    alog = a_log.reshape(N_GROUPS, 1, HEADS_PER_GROUP)
    alogt = a_log.reshape(N_GROUPS, HEADS_PER_GROUP, 1)
    conv_b2 = conv_b.reshape(1, -1)
    dskip_e = jnp.repeat(d_skip, HEAD_DIM).reshape(1, D_INNER)
    nb0 = D_INNER // N_STATE
    in_specs = [
        pl.BlockSpec((CHUNK, GROUP_W), cur(xb0)),
        pl.BlockSpec((HALO, GROUP_W), halo(xb0)),
        pl.BlockSpec((CHUNK, N_STATE), cur(bb0)),
        pl.BlockSpec((HALO, N_STATE), halo(bb0)),
        pl.BlockSpec((CHUNK, N_STATE), cur(cb0)),
        pl.BlockSpec((HALO, N_STATE), halo(cb0)),
        pl.BlockSpec((CHUNK, GROUP_W), cur(zb0)),
        pl.BlockSpec((1, CHUNK, HEADS_PER_GROUP), lambda c, g: (g, c, 0)),
        pl.BlockSpec((1, HEADS_PER_GROUP, CHUNK), lambda c, g: (g, 0, c)),
        pl.BlockSpec((1, 1, HEADS_PER_GROUP), lambda c, g: (g, 0, 0)),
        pl.BlockSpec((1, HEADS_PER_GROUP, 1), lambda c, g: (g, 0, 0)),
        pl.BlockSpec((CONV_W, GROUP_W), lambda c, g: (0, g)),
        pl.BlockSpec((CONV_W, N_STATE), lambda c, g: (0, nb0 + g)),
        pl.BlockSpec((CONV_W, N_STATE), lambda c, g: (0, nb0 + N_GROUPS + g)),
        pl.BlockSpec((1, GROUP_W), lambda c, g: (0, g)),
        pl.BlockSpec((1, N_STATE), lambda c, g: (0, nb0 + g)),
        pl.BlockSpec((1, N_STATE), lambda c, g: (0, nb0 + N_GROUPS + g)),
        pl.BlockSpec((1, GROUP_W), lambda c, g: (0, g)),
        pl.BlockSpec((1, GROUP_W), lambda c, g: (0, g)),
    ]
    return pl.pallas_call(
        _ssd_kernel,
        grid=(N_CHUNKS, N_GROUPS),
        in_specs=in_specs,
        out_specs=[
            pl.BlockSpec((CHUNK, GROUP_W), lambda c, g: (c, g)),
            pl.BlockSpec((CHUNK, 128), lambda c, g: (c, 0)),
        ],
        out_shape=[
            jax.ShapeDtypeStruct((R, D_INNER), bf16),
            jax.ShapeDtypeStruct((R, 128), f32),
        ],
        scratch_shapes=[
            pltpu.VMEM((N_GROUPS, N_STATE, GROUP_W), f32),
            pltpu.VMEM((HALO + CHUNK, GROUP_W), f32),
            pltpu.VMEM((HALO + CHUNK, N_STATE), f32),
            pltpu.VMEM((HALO + CHUNK, N_STATE), f32),
            pltpu.VMEM((CHUNK, GROUP_W), bf16),
            pltpu.VMEM((1, GROUP_W), f32),
        ],
        compiler_params=_params(("arbitrary", "arbitrary")),
        name="ssd_mixer",
    )(proj, proj, proj, proj, proj, proj, proj, dt_g, dt_gt, alog, alogt,
      conv_w, conv_w, conv_w, conv_b2, conv_b2, conv_b2, dskip_e, ssd_norm_w.reshape(1, D_INNER))


def _router_kernel(h_ref, nw_ref, rw_ref, rb_ref, u_ref, idx_ref, gate_ref, *, tm):
    i = pl.program_id(0)
    h = h_ref[...]
    ms = jnp.mean(h * h, axis=-1, keepdims=True)
    u = h * lax.rsqrt(ms + EPS) * nw_ref[...]
    rows = i * tm + lax.broadcasted_iota(i32, (tm, 1), 0)
    u = jnp.where(rows >= PAD_ROWS, u, 0.0)
    ub = lax.bitcast_convert_type(u.astype(bf16).astype(f32), u32)
    half = D_MODEL // 2
    u_ref[...] = ub[:, half:] | (ub[:, :half] >> 16)
    logits = jnp.dot(u, rw_ref[...], preferred_element_type=f32, precision=lax.Precision.HIGHEST) + rb_ref[...]
    e_iota = lax.broadcasted_iota(i32, (tm, N_EXPERTS), 1)
    lane = lax.broadcasted_iota(i32, (tm, 128), 1)
    idx_out = jnp.zeros((tm, 128), i32)
    val_out = jnp.zeros((tm, 128), f32)
    vals = []
    for k in range(TOP_K):
        m = jnp.max(logits, axis=1, keepdims=True)
        sel = jnp.min(jnp.where(logits == m, e_iota, N_EXPERTS), axis=1, keepdims=True)
        vals.append(m)
        idx_out = jnp.where(lane == k, sel, idx_out)
        logits = jnp.where(e_iota == sel, -jnp.inf, logits)
    exps = [jnp.exp(v - vals[0]) for v in vals]
    denom = exps[0] + exps[1] + exps[2] + exps[3]
    for k in range(TOP_K):
        val_out = jnp.where(lane == k, exps[k] / denom, val_out)
    idx_ref[...] = idx_out
    gate_ref[...] = val_out


def _router(h1, norm_w, router_w, router_b, tm=320):
    return pl.pallas_call(
        functools.partial(_router_kernel, tm=tm),
        grid=(R // tm,),
        in_specs=[
            pl.BlockSpec((tm, D_MODEL), lambda i: (i, 0)),
            pl.BlockSpec((1, D_MODEL), lambda i: (0, 0)),
            pl.BlockSpec((D_MODEL, N_EXPERTS), lambda i: (0, 0)),
            pl.BlockSpec((1, N_EXPERTS), lambda i: (0, 0)),
        ],
        out_specs=[
            pl.BlockSpec((tm, D_MODEL // 2), lambda i: (i, 0)),
            pl.BlockSpec((tm, 128), lambda i: (i, 0)),
            pl.BlockSpec((tm, 128), lambda i: (i, 0)),
        ],
        out_shape=[
            jax.ShapeDtypeStruct((R, D_MODEL // 2), u32),
            jax.ShapeDtypeStruct((R, 128), i32),
            jax.ShapeDtypeStruct((R, 128), f32),
        ],
        compiler_params=_params(("parallel",)),
        name="router",
    )(h1, norm_w.reshape(1, D_MODEL), router_w, router_b.reshape(1, N_EXPERTS))


def _gather_kernel(idx_ref, src_ref, dst_ref, sem, *, rows_per_step):
    base = pl.program_id(0) * rows_per_step

    def row_copy(src_row, dst_row):
        return pltpu.make_async_copy(src_ref.at[pl.ds(src_row, 1)], dst_ref.at[pl.ds(dst_row, 1)], sem)

    def issue(r, carry):
        row_copy(idx_ref[base + r], base + r).start()
        return carry

    lax.fori_loop(0, rows_per_step, issue, 0)

    def drain(r, carry):
        row_copy(0, base + r).wait()
        return carry

    lax.fori_loop(0, rows_per_step, drain, 0)


def _gather_rows(idx, src, n_rows, rows_per_step=256):
    return pl.pallas_call(
        functools.partial(_gather_kernel, rows_per_step=rows_per_step),
        grid_spec=pltpu.PrefetchScalarGridSpec(
            num_scalar_prefetch=1,
            grid=(n_rows // rows_per_step,),
            in_specs=[pl.BlockSpec(memory_space=pl.ANY)],
            out_specs=pl.BlockSpec(memory_space=pl.ANY),
            scratch_shapes=[pltpu.SemaphoreType.DMA(())],
        ),
        out_shape=jax.ShapeDtypeStruct((n_rows, src.shape[1]), src.dtype),
        compiler_params=pltpu.CompilerParams(dimension_semantics=("arbitrary",)),
        name="gather_rows",
    )(idx, src)


def _unpack_halves(xw):
    lo = lax.bitcast_convert_type(xw << 16, f32).astype(bf16)
    hi = lax.bitcast_convert_type(xw & jnp.uint32(0xFFFF0000), f32).astype(bf16)
    return lo, hi


def _moe_up_kernel(te_ref, nu_ref, x_ref, wg_ref, wu_ref, bg_ref, bu_ref, o_ref):
    i = pl.program_id(1)

    @pl.when(i < nu_ref[0])
    def _():
        lo, hi = _unpack_halves(x_ref[...])
        half = D_MODEL // 2

        def proj(w_ref, b_ref):
            w = w_ref[0]
            acc = jnp.dot(lo, w[:half].astype(bf16), preferred_element_type=f32)
            acc = acc + jnp.dot(hi, w[half:].astype(bf16), preferred_element_type=f32)
            return acc + b_ref[0]

        gate = jnp.minimum(proj(wg_ref, bg_ref), LIMIT)
        up = jnp.clip(proj(wu_ref, bu_ref), -LIMIT, LIMIT)
        o_ref[...] = ((up + 1.0) * (gate * _sigmoid(ALPHA * gate))).astype(o_ref.dtype)

    @pl.when(i >= nu_ref[0])
    def _():
        o_ref[...] = jnp.zeros_like(o_ref)


def _moe_up(tile_e, n_used, xs, w_up, b_up, tn=256):
    nj = D_EXPERT // tn

    def ti(i, nu):
        return jnp.minimum(i, nu[0] - 1)

    return pl.pallas_call(
        _moe_up_kernel,
        grid_spec=pltpu.PrefetchScalarGridSpec(
            num_scalar_prefetch=2,
            grid=(nj, N_TILES),
            in_specs=[
                pl.BlockSpec((MOE_TM, D_MODEL // 2), lambda j, i, te, nu: (ti(i, nu), 0)),
                pl.BlockSpec((1, D_MODEL, tn), lambda j, i, te, nu: (te[ti(i, nu)], 0, j)),
                pl.BlockSpec((1, D_MODEL, tn), lambda j, i, te, nu: (te[ti(i, nu)], 0, nj + j)),
                pl.BlockSpec((1, 1, tn), lambda j, i, te, nu: (te[ti(i, nu)], 0, j)),
                pl.BlockSpec((1, 1, tn), lambda j, i, te, nu: (te[ti(i, nu)], 0, nj + j)),
            ],
            out_specs=pl.BlockSpec((MOE_TM, tn), lambda j, i, te, nu: (i, j)),
        ),
        out_shape=jax.ShapeDtypeStruct((N_SLOTS, D_EXPERT), bf16),
        compiler_params=_params(("arbitrary", "arbitrary")),
        name="moe_up",
    )(tile_e, n_used, xs, w_up, w_up, b_up.reshape(N_EXPERTS, 1, -1), b_up.reshape(N_EXPERTS, 1, -1))


def _moe_down_kernel(te_ref, nu_ref, a_ref, w_ref, b_ref, o_ref):
    i = pl.program_id(1)

    @pl.when(i < nu_ref[0])
    def _():
        o_ref[...] = jnp.dot(a_ref[...], w_ref[0].astype(bf16), preferred_element_type=f32) + b_ref[0]

    @pl.when(i >= nu_ref[0])
    def _():
        o_ref[...] = jnp.zeros_like(o_ref)


def _moe_down(tile_e, n_used, act, w_down, b_down, tn=1024):
    def ti(i, nu):
        return jnp.minimum(i, nu[0] - 1)

    return pl.pallas_call(
        _moe_down_kernel,
        grid_spec=pltpu.PrefetchScalarGridSpec(
            num_scalar_prefetch=2,
            grid=(D_MODEL // tn, N_TILES),
            in_specs=[
                pl.BlockSpec((MOE_TM, D_EXPERT), lambda j, i, te, nu: (ti(i, nu), 0)),
                pl.BlockSpec((1, D_EXPERT, tn), lambda j, i, te, nu: (te[ti(i, nu)], 0, j)),
                pl.BlockSpec((1, 1, tn), lambda j, i, te, nu: (te[ti(i, nu)], 0, j)),
            ],
            out_specs=pl.BlockSpec((MOE_TM, tn), lambda j, i, te, nu: (i, j)),
        ),
        out_shape=jax.ShapeDtypeStruct((N_SLOTS, D_MODEL), f32),
        compiler_params=_params(("arbitrary", "arbitrary")),
        name="moe_down",
    )(tile_e, n_used, act, w_down, b_down.reshape(N_EXPERTS, 1, -1))


def _combine_kernel(h_ref, y0_ref, y1_ref, y2_ref, y3_ref, g_ref, nw_ref, o_ref):
    g = g_ref[...]
    h = h_ref[...]
    for k, y_ref in enumerate((y0_ref, y1_ref, y2_ref, y3_ref)):
        h = h + g[:, k:k + 1] * y_ref[...]
    ms = jnp.mean(h * h, axis=-1, keepdims=True)
    o_ref[...] = h * lax.rsqrt(ms + EPS) * nw_ref[...]


def _combine(h1, yg, gates, norm_w, tm=128):
    nb = SEQ // tm
    off = (PAD_ROWS + N_META) // tm
    y_specs = [pl.BlockSpec((tm, D_MODEL), functools.partial(lambda i, k: (k * nb + i, 0), k=k)) for k in range(TOP_K)]
    return pl.pallas_call(
        _combine_kernel,
        grid=(nb,),
        in_specs=[
            pl.BlockSpec((tm, D_MODEL), lambda i: (i + off, 0)),
            *y_specs,
            pl.BlockSpec((tm, 128), lambda i: (i + off, 0)),
            pl.BlockSpec((1, D_MODEL), lambda i: (0, 0)),
        ],
        out_specs=pl.BlockSpec((tm, D_MODEL), lambda i: (i, 0)),
        out_shape=jax.ShapeDtypeStruct((SEQ, D_MODEL), f32),
        compiler_params=_params(("parallel",)),
        name="combine_final_norm",
    )(h1, yg, yg, yg, yg, gates, norm_w.reshape(1, D_MODEL))


def _routing_tables(top_idx):
    flat_e = top_idx.reshape(-1)
    order = jnp.argsort(flat_e, stable=True).astype(i32)
    sorted_e = flat_e[order]
    counts = jnp.bincount(flat_e, length=N_EXPERTS).astype(i32)
    starts = jnp.cumsum(counts) - counts
    padded = ((counts + MOE_TM - 1) // MOE_TM) * MOE_TM
    pends = jnp.cumsum(padded)
    pstarts = pends - padded
    dest = pstarts[sorted_e] + jnp.arange(N_PAIRS, dtype=i32) - starts[sorted_e]
    src_rows = jnp.zeros((N_SLOTS,), i32).at[dest].set(order // TOP_K + PAD_ROWS)
    pos = jnp.zeros((N_PAIRS,), i32).at[order].set(dest)
    tile_e = jnp.minimum(
        jnp.searchsorted(pends, jnp.arange(N_TILES, dtype=i32) * MOE_TM, side="right"), N_EXPERTS - 1).astype(i32)
    n_used = (pends[-1] // MOE_TM).astype(i32).reshape(1)
    return src_rows, pos.reshape(N_TOK, TOP_K), tile_e, n_used


def kernel(x, meta_tokens, norm_mix_w, w_in, conv_w, conv_b, dt_bias, a_log, d_skip, ssd_norm_w, w_ssd_out,
           pool_w, pool_scale, w_out, norm_ffn_w, router_w, router_b, w_up, b_up, w_down, b_down, norm_final_w):
    assert x.shape == (1, SEQ, D_MODEL) and norm_mix_w.shape[0] == 1
    h0 = jnp.concatenate([jnp.zeros((PAD_ROWS, D_MODEL), f32), meta_tokens.astype(f32), x[0]], axis=0)

    u = _rms_norm_rows(h0, norm_mix_w[0])
    proj = _in_proj_main(u, w_in[0])
    dt = _in_proj_dt(u, w_in[0], dt_bias[0])
    gates = _in_proj_gates(u, w_in[0])
    ma = _pool_branch(proj, gates, pool_w[0], pool_scale[0])
    vw, ssq = _ssd_branch(proj, dt, conv_w[0], conv_b[0], a_log[0], d_skip[0], ssd_norm_w[0])
    merged = _ssd_out_merge(vw, w_ssd_out[0], ssq, ma, gates)
    h1 = _out_proj(merged, w_out[0], h0)

    u2, top_idx, top_gate = _router(h1, norm_ffn_w[0], router_w[0], router_b[0])
    src_rows, pos, tile_e, n_used = _routing_tables(top_idx[PAD_ROWS:, :TOP_K])
    xs = _gather_rows(src_rows, u2, N_SLOTS)
    act = _moe_up(tile_e, n_used, xs, w_up[0], b_up[0])
    y = _moe_down(tile_e, n_used, act, w_down[0], b_down[0])
    back = pos[N_META:].T.reshape(-1)
    yg = _gather_rows(back, y, TOP_K * SEQ)
    out = _combine(h1, yg, top_gate, norm_final_w)
    return out[None]
```

```python
import functools

import jax
import jax.numpy as jnp
from jax import lax
from jax.experimental import pallas as pl
from jax.experimental.pallas import tpu as pltpu

f32 = jnp.float32
bf16 = jnp.bfloat16
i32 = jnp.int32
u32 = jnp.uint32

D_MODEL = 4096
SEQ = 8192
N_META = 16
CHUNK = 128
PAD_ROWS = CHUNK - N_META
R = PAD_ROWS + N_META + SEQ
N_CHUNKS = R // CHUNK
POOL_WINDOWS = (2, 4, 8, 16)
POOL_GROUP = D_MODEL // 4
D_INNER = 2 * D_MODEL
HEAD_DIM = 64
N_HEADS = D_INNER // HEAD_DIM
N_STATE = 128
N_GROUPS = 8
HEADS_PER_GROUP = N_HEADS // N_GROUPS
GROUP_W = HEADS_PER_GROUP * HEAD_DIM
CONV_W = 4
HALO = 16
N_EXPERTS = 32
TOP_K = 4
D_EXPERT = 7 * D_MODEL // 16
LIMIT = 7.0
ALPHA = 1.702
EPS = 1e-5
N_TOK = N_META + SEQ
N_PAIRS = N_TOK * TOP_K
MOE_TM = 256
N_TILES = -(-N_PAIRS // MOE_TM) + N_EXPERTS
N_SLOTS = N_TILES * MOE_TM
COL_Z = D_MODEL
COL_X = COL_Z + D_INNER
COL_B = COL_X + D_INNER
COL_C = COL_B + N_GROUPS * N_STATE
COL_DT = COL_C + N_GROUPS * N_STATE
COL_G = COL_DT + N_HEADS
SEG1_COLS = COL_DT
V7X_VMEM_LIMIT = 56 * 1024 * 1024


def _params(sem, vmem=V7X_VMEM_LIMIT):
    return pltpu.CompilerParams(dimension_semantics=sem, vmem_limit_bytes=vmem)


def _sigmoid(v):
    return 1.0 / (1.0 + jnp.exp(-v))


def _rms_kernel(h_ref, w_ref, o_ref):
    h = h_ref[...]
    ms = jnp.mean(h * h, axis=-1, keepdims=True)
    o_ref[...] = (h * lax.rsqrt(ms + EPS) * w_ref[...]).astype(o_ref.dtype)


def _rms_norm_rows(h, w, tm=320):
    rows, d = h.shape
    return pl.pallas_call(
        _rms_kernel,
        grid=(rows // tm,),
        in_specs=[pl.BlockSpec((tm, d), lambda i: (i, 0)), pl.BlockSpec((1, d), lambda i: (0, 0))],
        out_specs=pl.BlockSpec((tm, d), lambda i: (i, 0)),
        out_shape=jax.ShapeDtypeStruct((rows, d), bf16),
        compiler_params=_params(("parallel",)),
        name="rms_mix",
    )(h, w.reshape(1, d))


def _mm_kernel(*refs, n_extra, epilogue, nk):
    a_ref, w_ref = refs[0], refs[1]
    extra = refs[2:2 + n_extra]
    o_ref = refs[2 + n_extra]
    acc_ref = refs[3 + n_extra]
    k = pl.program_id(2)

    @pl.when(k == 0)
    def _():
        acc_ref[...] = jnp.zeros_like(acc_ref)

    acc_ref[...] += jnp.dot(a_ref[...], w_ref[...].astype(bf16), preferred_element_type=f32)

    @pl.when(k == nk - 1)
    def _():
        o_ref[...] = epilogue(acc_ref[...], *[e[...] for e in extra]).astype(o_ref.dtype)


def _matmul(a, w, *, col0, n, tm, tn, tk, out_dtype, epilogue, extras=(), extra_specs=(), name):
    m, kdim = a.shape
    nk = kdim // tk
    if col0 % tn == 0:
        w_spec = pl.BlockSpec((tk, tn), lambda i, j, k: (k, j + col0 // tn))
    else:
        w_spec = pl.BlockSpec((pl.Element(tk), pl.Element(tn)), lambda i, j, k: (k * tk, pl.multiple_of(col0 + j * tn, 128)))
    return pl.pallas_call(
        functools.partial(_mm_kernel, n_extra=len(extras), epilogue=epilogue, nk=nk),
        grid=(m // tm, n // tn, nk),
        in_specs=[pl.BlockSpec((tm, tk), lambda i, j, k: (i, k)), w_spec, *extra_specs],
        out_specs=pl.BlockSpec((tm, tn), lambda i, j, k: (i, j)),
        out_shape=jax.ShapeDtypeStruct((m, n), out_dtype),
        scratch_shapes=[pltpu.VMEM((tm, tn), f32)],
        compiler_params=_params(("parallel", "parallel", "arbitrary")),
        name=name,
    )(a, w, *extras)


def _softplus_bias(acc, bias):
    v = acc + bias
    return jnp.maximum(v, 0.0) + jnp.log1p(jnp.exp(-jnp.abs(v)))


def _in_proj_main(u, w_in):
    return _matmul(u, w_in, col0=0, n=SEG1_COLS, tm=1664, tn=1024, tk=1024, out_dtype=bf16,
                   epilogue=lambda acc: acc, name="in_proj_main")


def _in_proj_dt(u, w_in, dt_bias):
    return _matmul(u, w_in, col0=COL_DT, n=N_HEADS, tm=1664, tn=N_HEADS, tk=1024, out_dtype=f32,
                   epilogue=_softplus_bias, extras=(dt_bias.reshape(1, N_HEADS),),
                   extra_specs=(pl.BlockSpec((1, N_HEADS), lambda i, j, k: (0, 0)),), name="in_proj_dt")


def _in_proj_gates(u, w_in):
    return _matmul(u, w_in, col0=COL_G, n=2 * D_MODEL, tm=1664, tn=1024, tk=1024, out_dtype=bf16,
                   epilogue=_sigmoid, name="in_proj_gates")


def _ssd_out_merge(vw, w_ssd_out, ssq, ma, gates, tm=832, tn=1024):
    gb_off = D_MODEL // tn

    def merge(acc, ssq_t, ma_t, gb_t):
        rs = lax.rsqrt(ssq_t[:, :1] * (1.0 / D_INNER) + EPS)
        return ma_t.astype(f32) + gb_t.astype(f32) * (acc * rs)

    return _matmul(vw, w_ssd_out, col0=0, n=D_MODEL, tm=tm, tn=tn, tk=1024, out_dtype=bf16, epilogue=merge,
                   extras=(ssq, ma, gates),
                   extra_specs=(pl.BlockSpec((tm, 128), lambda i, j, k: (i, 0)),
                                pl.BlockSpec((tm, tn), lambda i, j, k: (i, j)),
                                pl.BlockSpec((tm, tn), lambda i, j, k: (i, j + gb_off))),
                   name="ssd_out_merge")


def _out_proj(merged, w_out, h0, tm=832, tn=1024):
    return _matmul(merged, w_out, col0=0, n=D_MODEL, tm=tm, tn=tn, tk=1024, out_dtype=f32,
                   epilogue=lambda acc, res: res + acc, extras=(h0,),
                   extra_specs=(pl.BlockSpec((tm, tn), lambda i, j, k: (i, j)),), name="out_proj_residual")


def _pool_kernel(x_ref, halo_ref, w_ref, scale_ref, ga_ref, o_ref, ext_ref, pooled_ref, *, tm):
    gi = pl.program_id(0)
    i = pl.program_id(1)
    halo = halo_ref[...].astype(f32)
    ext_ref[0:HALO, :] = jnp.where(i == 0, 0.0, halo)
    ext_ref[HALO:HALO + tm, :] = x_ref[...].astype(f32)
    t = i * tm + lax.broadcasted_iota(i32, (tm, 1), 0) - PAD_ROWS
    for g, win in enumerate(POOL_WINDOWS):

        @pl.when(gi == g)
        def _(win=win):
            x = ext_ref[HALO:HALO + tm, :]
            s = x
            for k in range(1, win):
                s = s + ext_ref[HALO - k:HALO - k + tm, :]
            cnt = jnp.clip(t + 1, 1, win).astype(f32)
            pooled_ref[...] = (s / cnt - x).astype(bf16)

    acc = jnp.dot(pooled_ref[...], w_ref[0].astype(bf16), preferred_element_type=f32)
    o_ref[...] = (ga_ref[...].astype(f32) * (acc * scale_ref[...])).astype(o_ref.dtype)


def _pool_branch(proj, gates, pool_w, pool_scale, tm=640):
    gw = POOL_GROUP
    return pl.pallas_call(
        functools.partial(_pool_kernel, tm=tm),
        grid=(4, R // tm),
        in_specs=[
            pl.BlockSpec((tm, gw), lambda g, i: (i, g)),
            pl.BlockSpec((HALO, gw), lambda g, i: (jnp.maximum(i * (tm // HALO) - 1, 0), g)),
            pl.BlockSpec((1, gw, gw), lambda g, i: (g, 0, 0)),
            pl.BlockSpec((1, gw), lambda g, i: (0, g)),
            pl.BlockSpec((tm, gw), lambda g, i: (i, g)),
        ],
        out_specs=pl.BlockSpec((tm, gw), lambda g, i: (i, g)),
        out_shape=jax.ShapeDtypeStruct((R, D_MODEL), bf16),
        scratch_shapes=[pltpu.VMEM((HALO + tm, gw), f32), pltpu.VMEM((tm, gw), bf16)],
        compiler_params=_params(("parallel", "arbitrary")),
        name="pool_branch",
    )(proj, proj, pool_w, pool_scale.reshape(1, D_MODEL), gates)


def _conv_silu(cur_ref, halo_ref, w_ref, b_ref, ext_ref, c, row_ok):
    halo = halo_ref[...].astype(f32)
    ext_ref[0:HALO, :] = jnp.where(c == 0, 0.0, halo)
    ext_ref[HALO:HALO + CHUNK, :] = cur_ref[...].astype(f32)
    out = b_ref[...]
    for k in range(CONV_W):
        off = HALO - (CONV_W - 1) + k
        out = out + w_ref[k:k + 1, :] * ext_ref[off:off + CHUNK, :]
    out = out * _sigmoid(out)
    return jnp.where(row_ok, out, 0.0)


def _ssd_kernel(x_ref, xh_ref, b_ref, bh_ref, c_ref, ch_ref, z_ref, dt_ref, dtt_ref,
                alog_ref, alogt_ref, wx_ref, wb_ref, wc_ref, bx_ref, bb_ref, bc_ref,
                dskip_ref, nw_ref,
                vw_ref, ssq_ref,
                state_ref, extx_ref, extb_ref, extc_ref, xd2_ref, cd_ref):
    c = pl.program_id(0)
    g = pl.program_id(1)

    @pl.when(jnp.logical_and(c == 0, g == 0))
    def _():
        state_ref[...] = jnp.zeros_like(state_ref)

    rows = c * CHUNK + lax.broadcasted_iota(i32, (CHUNK, 1), 0)
    row_ok = rows >= PAD_ROWS
    cols = c * CHUNK + lax.broadcasted_iota(i32, (1, CHUNK), 1)
    col_ok = cols >= PAD_ROWS

    xs = _conv_silu(x_ref, xh_ref, wx_ref, bx_ref, extx_ref, c, row_ok)
    bm = _conv_silu(b_ref, bh_ref, wb_ref, bb_ref, extb_ref, c, row_ok)
    cm = _conv_silu(c_ref, ch_ref, wc_ref, bc_ref, extc_ref, c, True)

    dt = jnp.where(row_ok, dt_ref[0], 0.0)
    dtt = jnp.where(col_ok, dtt_ref[0], 0.0)
    a_dt = dt * (-jnp.exp(alog_ref[0]))
    a_dtt = dtt * (-jnp.exp(alogt_ref[0]))
    li = lax.broadcasted_iota(i32, (CHUNK, CHUNK), 0)
    si = lax.broadcasted_iota(i32, (CHUNK, CHUNK), 1)
    causal = li >= si
    tril = causal.astype(f32)
    triu = (li <= si).astype(f32)
    a_cs = jnp.dot(tril, a_dt, preferred_element_type=f32, precision=lax.Precision.HIGHEST)
    a_cst = jnp.dot(a_dtt, triu, preferred_element_type=f32, precision=lax.Precision.HIGHEST)
    last = a_cs[CHUNK - 1:CHUNK, :]
    e_cs = jnp.exp(a_cs)
    e_end = jnp.exp(last - a_cs)
    e_last = jnp.exp(last)

    bmb = bm.astype(bf16)
    cmb = cm.astype(bf16)
    cb = lax.dot_general(cmb, bmb, (((1,), (1,)), ((), ())), preferred_element_type=f32)
    state = state_ref[g]
    yoff = jnp.dot(cmb, state.astype(bf16), preferred_element_type=f32)

    lane = lax.broadcasted_iota(i32, (CHUNK, 2 * HEAD_DIM), 1)
    first = lane < HEAD_DIM
    lane1 = lax.broadcasted_iota(i32, (1, 2 * HEAD_DIM), 1)
    first1 = lane1 < HEAD_DIM
    ssq = jnp.zeros((CHUNK, 1), f32)
    for q in range(HEADS_PER_GROUP // 2):
        ha, hb = 2 * q, 2 * q + 1
        sl = slice(q * 2 * HEAD_DIM, (q + 1) * 2 * HEAD_DIM)
        xq = xs[:, sl]
        dtp = jnp.where(first, dt[:, ha:ha + 1], dt[:, hb:hb + 1])
        xdt = xq * dtp
        xdtb = xdt.astype(bf16)
        wa = cb * jnp.exp(jnp.where(causal, a_cs[:, ha:ha + 1] - a_cst[ha:ha + 1, :], -1e30))
        wb = cb * jnp.exp(jnp.where(causal, a_cs[:, hb:hb + 1] - a_cst[hb:hb + 1, :], -1e30))
        ya = jnp.dot(wa.astype(bf16), xdtb, preferred_element_type=f32)
        yb = jnp.dot(wb.astype(bf16), xdtb, preferred_element_type=f32)
        ecs = jnp.where(first, e_cs[:, ha:ha + 1], e_cs[:, hb:hb + 1])
        y = jnp.where(first, ya, yb) + yoff[:, sl] * ecs + dskip_ref[:, sl] * xq
        eend = jnp.where(first, e_end[:, ha:ha + 1], e_end[:, hb:hb + 1])
        xd2_ref[:, sl] = (xdt * eend).astype(bf16)
        cd_ref[:, sl] = jnp.where(first1, e_last[:, ha:ha + 1], e_last[:, hb:hb + 1])
        zq = z_ref[:, sl].astype(f32)
        v = y * (zq * _sigmoid(zq))
        ssq = ssq + jnp.sum(v * v, axis=1, keepdims=True)
        vw_ref[:, sl] = (v * nw_ref[:, sl]).astype(vw_ref.dtype)

    bt = jnp.transpose(bm).astype(bf16)
    s_new = jnp.dot(bt, xd2_ref[...], preferred_element_type=f32)
    state_ref[g] = state * cd_ref[...] + s_new

    ssq_b = jnp.broadcast_to(ssq, (CHUNK, 128))

    @pl.when(g == 0)
    def _():
        ssq_ref[...] = ssq_b

    @pl.when(g > 0)
    def _():
        ssq_ref[...] += ssq_b


def _ssd_branch(proj, dt, conv_w, conv_b, a_log, d_skip, ssd_norm_w):
    gwb = GROUP_W // GROUP_W
    xb0 = COL_X // GROUP_W
    zb0 = COL_Z // GROUP_W
    bb0 = COL_B // N_STATE
    cb0 = COL_C // N_STATE
    hpc = CHUNK // HALO

    def cur(b0):
        return lambda c, g: (c, b0 + g * gwb)

    def halo(b0):
        return lambda c, g: (jnp.maximum(c * hpc - 1, 0), b0 + g * gwb)

    dt_g = dt.reshape(R, N_GROUPS, HEADS_PER_GROUP).transpose(1, 0, 2)
    dt_gt = dt_g.transpose(0, 2, 1)
    alog = a_log.reshape(N_GROUPS, 1, HEADS_PER_GROUP)
    alogt = a_log.reshape(N_GROUPS, HEADS_PER_GROUP, 1)
    conv_b2 = conv_b.reshape(1, -1)
    dskip_e = jnp.repeat(d_skip, HEAD_DIM).reshape(1, D_INNER)
    nb0 = D_INNER // N_STATE
    in_specs = [
        pl.BlockSpec((CHUNK, GROUP_W), cur(xb0)),
        pl.BlockSpec((HALO, GROUP_W), halo(xb0)),
        pl.BlockSpec((CHUNK, N_STATE), cur(bb0)),
        pl.BlockSpec((HALO, N_STATE), halo(bb0)),
        pl.BlockSpec((CHUNK, N_STATE), cur(cb0)),
        pl.BlockSpec((HALO, N_STATE), halo(cb0)),
        pl.BlockSpec((CHUNK, GROUP_W), cur(zb0)),
        pl.BlockSpec((1, CHUNK, HEADS_PER_GROUP), lambda c, g: (g, c, 0)),
        pl.BlockSpec((1, HEADS_PER_GROUP, CHUNK), lambda c, g: (g, 0, c)),
        pl.BlockSpec((1, 1, HEADS_PER_GROUP), lambda c, g: (g, 0, 0)),
        pl.BlockSpec((1, HEADS_PER_GROUP, 1), lambda c, g: (g, 0, 0)),
        pl.BlockSpec((CONV_W, GROUP_W), lambda c, g: (0, g)),
        pl.BlockSpec((CONV_W, N_STATE), lambda c, g: (0, nb0 + g)),
        pl.BlockSpec((CONV_W, N_STATE), lambda c, g: (0, nb0 + N_GROUPS + g)),
        pl.BlockSpec((1, GROUP_W), lambda c, g: (0, g)),
        pl.BlockSpec((1, N_STATE), lambda c, g: (0, nb0 + g)),
        pl.BlockSpec((1, N_STATE), lambda c, g: (0, nb0 + N_GROUPS + g)),
        pl.BlockSpec((1, GROUP_W), lambda c, g: (0, g)),
        pl.BlockSpec((1, GROUP_W), lambda c, g: (0, g)),
    ]
    return pl.pallas_call(
        _ssd_kernel,
        grid=(N_CHUNKS, N_GROUPS),
        in_specs=in_specs,
        out_specs=[
            pl.BlockSpec((CHUNK, GROUP_W), lambda c, g: (c, g)),
            pl.BlockSpec((CHUNK, 128), lambda c, g: (c, 0)),
        ],
        out_shape=[
            jax.ShapeDtypeStruct((R, D_INNER), bf16),
            jax.ShapeDtypeStruct((R, 128), f32),
        ],
        scratch_shapes=[
            pltpu.VMEM((N_GROUPS, N_STATE, GROUP_W), f32),
            pltpu.VMEM((HALO + CHUNK, GROUP_W), f32),
            pltpu.VMEM((HALO + CHUNK, N_STATE), f32),
            pltpu.VMEM((HALO + CHUNK, N_STATE), f32),
            pltpu.VMEM((CHUNK, GROUP_W), bf16),
            pltpu.VMEM((1, GROUP_W), f32),
        ],
        compiler_params=_params(("arbitrary", "arbitrary")),
        name="ssd_mixer",
    )(proj, proj, proj, proj, proj, proj, proj, dt_g, dt_gt, alog, alogt,
      conv_w, conv_w, conv_w, conv_b2, conv_b2, conv_b2, dskip_e, ssd_norm_w.reshape(1, D_INNER))


def _router_kernel(h_ref, nw_ref, rw_ref, rb_ref, u_ref, idx_ref, gate_ref, *, tm):
    i = pl.program_id(0)
    h = h_ref[...]
    ms = jnp.mean(h * h, axis=-1, keepdims=True)
    u = h * lax.rsqrt(ms + EPS) * nw_ref[...]
    rows = i * tm + lax.broadcasted_iota(i32, (tm, 1), 0)
    u = jnp.where(rows >= PAD_ROWS, u, 0.0)
    ub = lax.bitcast_convert_type(u.astype(bf16).astype(f32), u32)
    half = D_MODEL // 2
    u_ref[...] = ub[:, half:] | (ub[:, :half] >> 16)
    logits = jnp.dot(u, rw_ref[...], preferred_element_type=f32, precision=lax.Precision.HIGHEST) + rb_ref[...]
    e_iota = lax.broadcasted_iota(i32, (tm, N_EXPERTS), 1)
    lane = lax.broadcasted_iota(i32, (tm, 128), 1)
    idx_out = jnp.zeros((tm, 128), i32)
    val_out = jnp.zeros((tm, 128), f32)
    vals = []
    for k in range(TOP_K):
        m = jnp.max(logits, axis=1, keepdims=True)
        sel = jnp.min(jnp.where(logits == m, e_iota, N_EXPERTS), axis=1, keepdims=True)
        vals.append(m)
        idx_out = jnp.where(lane == k, sel, idx_out)
        logits = jnp.where(e_iota == sel, -jnp.inf, logits)
    exps = [jnp.exp(v - vals[0]) for v in vals]
    denom = exps[0] + exps[1] + exps[2] + exps[3]
    for k in range(TOP_K):
        val_out = jnp.where(lane == k, exps[k] / denom, val_out)
    idx_ref[...] = idx_out
    gate_ref[...] = val_out


def _router(h1, norm_w, router_w, router_b, tm=320):
    return pl.pallas_call(
        functools.partial(_router_kernel, tm=tm),
        grid=(R // tm,),
        in_specs=[
            pl.BlockSpec((tm, D_MODEL), lambda i: (i, 0)),
            pl.BlockSpec((1, D_MODEL), lambda i: (0, 0)),
            pl.BlockSpec((D_MODEL, N_EXPERTS), lambda i: (0, 0)),
            pl.BlockSpec((1, N_EXPERTS), lambda i: (0, 0)),
        ],
        out_specs=[
            pl.BlockSpec((tm, D_MODEL // 2), lambda i: (i, 0)),
            pl.BlockSpec((tm, 128), lambda i: (i, 0)),
            pl.BlockSpec((tm, 128), lambda i: (i, 0)),
        ],
        out_shape=[
            jax.ShapeDtypeStruct((R, D_MODEL // 2), u32),
            jax.ShapeDtypeStruct((R, 128), i32),
            jax.ShapeDtypeStruct((R, 128), f32),
        ],
        compiler_params=_params(("parallel",)),
        name="router",
    )(h1, norm_w.reshape(1, D_MODEL), router_w, router_b.reshape(1, N_EXPERTS))


def _unpack_halves(xw):
    lo = lax.bitcast_convert_type(xw << 16, f32).astype(bf16)
    hi = lax.bitcast_convert_type(xw & jnp.uint32(0xFFFF0000), f32).astype(bf16)
    return lo, hi


def _gather_x_kernel(idx_ref, nu_ref, src_ref, o_ref, buf_ref, sem, *, tm):
    i = pl.program_id(0)
    n_used = nu_ref[0]

    def row_copy(src_row, slot, r):
        return pltpu.make_async_copy(src_ref.at[pl.ds(src_row, 1)], buf_ref.at[slot, pl.ds(r, 1)], sem.at[slot])

    def issue_tile(step):
        slot = step % 2

        def issue(r, carry):
            row_copy(idx_ref[step * tm + r], slot, r).start()
            return carry

        lax.fori_loop(0, tm, issue, 0)

    @pl.when(i == 0)
    def _():
        issue_tile(0)

    @pl.when(i + 1 < n_used)
    def _():
        issue_tile(i + 1)

    @pl.when(i < n_used)
    def _():
        slot = i % 2

        def drain(r, carry):
            row_copy(0, slot, r).wait()
            return carry

        lax.fori_loop(0, tm, drain, 0)
        lo, hi = _unpack_halves(buf_ref[slot])
        half = D_MODEL // 2
        o_ref[:, :half] = lo
        o_ref[:, half:] = hi

    @pl.when(i >= n_used)
    def _():
        o_ref[...] = jnp.zeros_like(o_ref)


def _gather_x(src_rows, n_used, u2, tm=MOE_TM):
    return pl.pallas_call(
        functools.partial(_gather_x_kernel, tm=tm),
        grid_spec=pltpu.PrefetchScalarGridSpec(
            num_scalar_prefetch=2,
            grid=(N_TILES,),
            in_specs=[pl.BlockSpec(memory_space=pl.ANY)],
            out_specs=pl.BlockSpec((tm, D_MODEL), lambda i, idx, nu: (i, 0)),
            scratch_shapes=[pltpu.VMEM((2, tm, D_MODEL // 2), u32), pltpu.SemaphoreType.DMA((2,))],
        ),
        out_shape=jax.ShapeDtypeStruct((N_SLOTS, D_MODEL), bf16),
        compiler_params=_params(("arbitrary",)),
        name="gather_x",
    )(src_rows, n_used, u2)


def _expert_changed(te_ref, nu_ref, i):
    t = jnp.minimum(i, nu_ref[0] - 1)
    return jnp.logical_or(i == 0, te_ref[t] != te_ref[jnp.maximum(t - 1, 0)])


def _moe_up_kernel(te_ref, nu_ref, x_ref, wg_ref, wu_ref, bg_ref, bu_ref, o_ref, wgb_ref, wub_ref):
    i = pl.program_id(1)

    @pl.when(jnp.logical_and(i < nu_ref[0], _expert_changed(te_ref, nu_ref, i)))
    def _():
        wgb_ref[...] = wg_ref[0].astype(bf16)
        wub_ref[...] = wu_ref[0].astype(bf16)

    @pl.when(i < nu_ref[0])
    def _():
        x = x_ref[...]
        gate = jnp.dot(x, wgb_ref[...], preferred_element_type=f32) + bg_ref[0]
        up = jnp.dot(x, wub_ref[...], preferred_element_type=f32) + bu_ref[0]
        gate = jnp.minimum(gate, LIMIT)
        up = jnp.clip(up, -LIMIT, LIMIT)
        o_ref[...] = ((up + 1.0) * (gate * _sigmoid(ALPHA * gate))).astype(o_ref.dtype)

    @pl.when(i >= nu_ref[0])
    def _():
        o_ref[...] = jnp.zeros_like(o_ref)


def _moe_up(tile_e, n_used, xs, w_up, b_up, tn=256):
    nj = D_EXPERT // tn

    def ti(i, nu):
        return jnp.minimum(i, nu[0] - 1)

    return pl.pallas_call(
        _moe_up_kernel,
        grid_spec=pltpu.PrefetchScalarGridSpec(
            num_scalar_prefetch=2,
            grid=(nj, N_TILES),
            in_specs=[
                pl.BlockSpec((MOE_TM, D_MODEL), lambda j, i, te, nu: (ti(i, nu), 0)),
                pl.BlockSpec((1, D_MODEL, tn), lambda j, i, te, nu: (te[ti(i, nu)], 0, j)),
                pl.BlockSpec((1, D_MODEL, tn), lambda j, i, te, nu: (te[ti(i, nu)], 0, nj + j)),
                pl.BlockSpec((1, 1, tn), lambda j, i, te, nu: (te[ti(i, nu)], 0, j)),
                pl.BlockSpec((1, 1, tn), lambda j, i, te, nu: (te[ti(i, nu)], 0, nj + j)),
            ],
            out_specs=pl.BlockSpec((MOE_TM, tn), lambda j, i, te, nu: (i, j)),
            scratch_shapes=[pltpu.VMEM((D_MODEL, tn), bf16), pltpu.VMEM((D_MODEL, tn), bf16)],
        ),
        out_shape=jax.ShapeDtypeStruct((N_SLOTS, D_EXPERT), bf16),
        compiler_params=_params(("arbitrary", "arbitrary")),
        name="moe_up",
    )(tile_e, n_used, xs, w_up, w_up, b_up.reshape(N_EXPERTS, 1, -1), b_up.reshape(N_EXPERTS, 1, -1))


def _moe_down_kernel(te_ref, nu_ref, a_ref, w_ref, b_ref, o_ref, wb_ref):
    i = pl.program_id(1)

    @pl.when(jnp.logical_and(i < nu_ref[0], _expert_changed(te_ref, nu_ref, i)))
    def _():
        wb_ref[...] = w_ref[0].astype(bf16)

    @pl.when(i < nu_ref[0])
    def _():
        o_ref[...] = jnp.dot(a_ref[...], wb_ref[...], preferred_element_type=f32) + b_ref[0]

    @pl.when(i >= nu_ref[0])
    def _():
        o_ref[...] = jnp.zeros_like(o_ref)


def _moe_down(tile_e, n_used, act, w_down, b_down, tn=1024):
    def ti(i, nu):
        return jnp.minimum(i, nu[0] - 1)

    return pl.pallas_call(
        _moe_down_kernel,
        grid_spec=pltpu.PrefetchScalarGridSpec(
            num_scalar_prefetch=2,
            grid=(D_MODEL // tn, N_TILES),
            in_specs=[
                pl.BlockSpec((MOE_TM, D_EXPERT), lambda j, i, te, nu: (ti(i, nu), 0)),
                pl.BlockSpec((1, D_EXPERT, tn), lambda j, i, te, nu: (te[ti(i, nu)], 0, j)),
                pl.BlockSpec((1, 1, tn), lambda j, i, te, nu: (te[ti(i, nu)], 0, j)),
            ],
            out_specs=pl.BlockSpec((MOE_TM, tn), lambda j, i, te, nu: (i, j)),
            scratch_shapes=[pltpu.VMEM((D_EXPERT, tn), bf16)],
        ),
        out_shape=jax.ShapeDtypeStruct((N_SLOTS, D_MODEL), f32),
        compiler_params=_params(("arbitrary", "arbitrary")),
        name="moe_down",
    )(tile_e, n_used, act, w_down, b_down.reshape(N_EXPERTS, 1, -1))


def _combine_kernel(pos_ref, h_ref, g_ref, nw_ref, y_ref, o_ref, buf_ref, sem, *, tm):
    i = pl.program_id(0)
    n = pl.num_programs(0)

    def row_copy(src_row, slot, k, r):
        return pltpu.make_async_copy(y_ref.at[pl.ds(src_row, 1)], buf_ref.at[slot, k, pl.ds(r, 1)], sem.at[slot])

    def issue_tile(step):
        slot = step % 2
        for k in range(TOP_K):

            def issue(r, carry, k=k):
                row_copy(pos_ref[k * SEQ + step * tm + r], slot, k, r).start()
                return carry

            lax.fori_loop(0, tm, issue, 0)

    @pl.when(i == 0)
    def _():
        issue_tile(0)

    @pl.when(i + 1 < n)
    def _():
        issue_tile(i + 1)

    slot = i % 2

    def drain(r, carry):
        row_copy(0, slot, 0, 0).wait()
        return carry

    lax.fori_loop(0, TOP_K * tm, drain, 0)

    g = g_ref[...]
    h = h_ref[...]
    for k in range(TOP_K):
        h = h + g[:, k:k + 1] * buf_ref[slot, k]
    ms = jnp.mean(h * h, axis=-1, keepdims=True)
    o_ref[...] = h * lax.rsqrt(ms + EPS) * nw_ref[...]


def _combine(pos_kt, h1, y, gates, norm_w, tm=128):
    nb = SEQ // tm
    off = (PAD_ROWS + N_META) // tm
    return pl.pallas_call(
        functools.partial(_combine_kernel, tm=tm),
        grid_spec=pltpu.PrefetchScalarGridSpec(
            num_scalar_prefetch=1,
            grid=(nb,),
            in_specs=[
                pl.BlockSpec((tm, D_MODEL), lambda i, pos: (i + off, 0)),
                pl.BlockSpec((tm, 128), lambda i, pos: (i + off, 0)),
                pl.BlockSpec((1, D_MODEL), lambda i, pos: (0, 0)),
                pl.BlockSpec(memory_space=pl.ANY),
            ],
            out_specs=pl.BlockSpec((tm, D_MODEL), lambda i, pos: (i, 0)),
            scratch_shapes=[pltpu.VMEM((2, TOP_K, tm, D_MODEL), f32), pltpu.SemaphoreType.DMA((2,))],
        ),
        out_shape=jax.ShapeDtypeStruct((SEQ, D_MODEL), f32),
        compiler_params=_params(("arbitrary",)),
        name="combine_final_norm",
    )(pos_kt, h1, gates, norm_w.reshape(1, D_MODEL), y)


def _routing_tables(top_idx):
    flat_e = top_idx.reshape(-1)
    order = jnp.argsort(flat_e, stable=True).astype(i32)
    sorted_e = flat_e[order]
    counts = jnp.bincount(flat_e, length=N_EXPERTS).astype(i32)
    starts = jnp.cumsum(counts) - counts
    padded = ((counts + MOE_TM - 1) // MOE_TM) * MOE_TM
    pends = jnp.cumsum(padded)
    pstarts = pends - padded
    dest = pstarts[sorted_e] + jnp.arange(N_PAIRS, dtype=i32) - starts[sorted_e]
    src_rows = jnp.zeros((N_SLOTS,), i32).at[dest].set(order // TOP_K + PAD_ROWS)
    pos = jnp.zeros((N_PAIRS,), i32).at[order].set(dest)
    tile_e = jnp.minimum(
        jnp.searchsorted(pends, jnp.arange(N_TILES, dtype=i32) * MOE_TM, side="right"), N_EXPERTS - 1).astype(i32)
    n_used = (pends[-1] // MOE_TM).astype(i32).reshape(1)
    return src_rows, pos.reshape(N_TOK, TOP_K), tile_e, n_used


def kernel(x, meta_tokens, norm_mix_w, w_in, conv_w, conv_b, dt_bias, a_log, d_skip, ssd_norm_w, w_ssd_out,
           pool_w, pool_scale, w_out, norm_ffn_w, router_w, router_b, w_up, b_up, w_down, b_down, norm_final_w):
    assert x.shape == (1, SEQ, D_MODEL) and norm_mix_w.shape[0] == 1
    h0 = jnp.concatenate([jnp.zeros((PAD_ROWS, D_MODEL), f32), meta_tokens.astype(f32), x[0]], axis=0)

    u = _rms_norm_rows(h0, norm_mix_w[0])
    proj = _in_proj_main(u, w_in[0])
    dt = _in_proj_dt(u, w_in[0], dt_bias[0])
    gates = _in_proj_gates(u, w_in[0])
    ma = _pool_branch(proj, gates, pool_w[0], pool_scale[0])
    vw, ssq = _ssd_branch(proj, dt, conv_w[0], conv_b[0], a_log[0], d_skip[0], ssd_norm_w[0])
    merged = _ssd_out_merge(vw, w_ssd_out[0], ssq, ma, gates)
    h1 = _out_proj(merged, w_out[0], h0)

    u2, top_idx, top_gate = _router(h1, norm_ffn_w[0], router_w[0], router_b[0])
    src_rows, pos, tile_e, n_used = _routing_tables(top_idx[PAD_ROWS:, :TOP_K])
    xs = _gather_x(src_rows, n_used, u2)
    act = _moe_up(tile_e, n_used, xs, w_up[0], b_up[0])
    y = _moe_down(tile_e, n_used, act, w_down[0], b_down[0])
    pos_kt = pos[N_META:].T.reshape(-1)
    out = _combine(pos_kt, h1, y, top_gate, norm_final_w)
    return out[None]
```

```python
import functools

import jax
import jax.numpy as jnp
from jax import lax
from jax.experimental import pallas as pl
from jax.experimental.pallas import tpu as pltpu

f32 = jnp.float32
bf16 = jnp.bfloat16
i32 = jnp.int32
u32 = jnp.uint32

D_MODEL = 4096
SEQ = 8192
N_META = 16
CHUNK = 128
PAD_ROWS = CHUNK - N_META
R = PAD_ROWS + N_META + SEQ
N_CHUNKS = R // CHUNK
POOL_WINDOWS = (2, 4, 8, 16)
POOL_GROUP = D_MODEL // 4
D_INNER = 2 * D_MODEL
HEAD_DIM = 64
N_HEADS = D_INNER // HEAD_DIM
N_STATE = 128
N_GROUPS = 8
HEADS_PER_GROUP = N_HEADS // N_GROUPS
GROUP_W = HEADS_PER_GROUP * HEAD_DIM
CONV_W = 4
HALO = 16
N_EXPERTS = 32
TOP_K = 4
D_EXPERT = 7 * D_MODEL // 16
LIMIT = 7.0
ALPHA = 1.702
EPS = 1e-5
N_TOK = N_META + SEQ
N_PAIRS = N_TOK * TOP_K
MOE_TM = 256
N_TILES = -(-N_PAIRS // MOE_TM) + N_EXPERTS
N_SLOTS = N_TILES * MOE_TM
MOE_CAP = 6
GROUP_ROWS = MOE_CAP * MOE_TM
MAX_GROUPS = N_EXPERTS + N_TILES // MOE_CAP
XS_ROWS = N_SLOTS + GROUP_ROWS
COL_Z = D_MODEL
COL_X = COL_Z + D_INNER
COL_B = COL_X + D_INNER
COL_C = COL_B + N_GROUPS * N_STATE
COL_DT = COL_C + N_GROUPS * N_STATE
COL_G = COL_DT + N_HEADS
SEG1_COLS = COL_DT
V7X_VMEM_LIMIT = 56 * 1024 * 1024


def _params(sem, vmem=V7X_VMEM_LIMIT):
    return pltpu.CompilerParams(dimension_semantics=sem, vmem_limit_bytes=vmem)


def _sigmoid(v):
    return 1.0 / (1.0 + jnp.exp(-v))


def _rms_kernel(h_ref, w_ref, o_ref):
    h = h_ref[...]
    ms = jnp.mean(h * h, axis=-1, keepdims=True)
    o_ref[...] = (h * lax.rsqrt(ms + EPS) * w_ref[...]).astype(o_ref.dtype)


def _rms_norm_rows(h, w, tm=320):
    rows, d = h.shape
    return pl.pallas_call(
        _rms_kernel,
        grid=(rows // tm,),
        in_specs=[pl.BlockSpec((tm, d), lambda i: (i, 0)), pl.BlockSpec((1, d), lambda i: (0, 0))],
        out_specs=pl.BlockSpec((tm, d), lambda i: (i, 0)),
        out_shape=jax.ShapeDtypeStruct((rows, d), bf16),
        compiler_params=_params(("parallel",)),
        name="rms_mix",
    )(h, w.reshape(1, d))


def _mm_kernel(*refs, n_extra, epilogue, nk):
    a_ref, w_ref = refs[0], refs[1]
    extra = refs[2:2 + n_extra]
    o_ref = refs[2 + n_extra]
    acc_ref = refs[3 + n_extra]
    k = pl.program_id(2)

    @pl.when(k == 0)
    def _():
        acc_ref[...] = jnp.dot(a_ref[...], w_ref[...].astype(bf16), preferred_element_type=f32)

    @pl.when(k > 0)
    def _():
        acc_ref[...] += jnp.dot(a_ref[...], w_ref[...].astype(bf16), preferred_element_type=f32)

    @pl.when(k == nk - 1)
    def _():
        o_ref[...] = epilogue(acc_ref[...], *[e[...] for e in extra]).astype(o_ref.dtype)


def _matmul(a, w, *, col0, n, tm, tn, tk, out_dtype, epilogue, extras=(), extra_specs=(), name):
    m, kdim = a.shape
    nk = kdim // tk
    if col0 % tn == 0:
        w_spec = pl.BlockSpec((tk, tn), lambda i, j, k: (k, j + col0 // tn))
    else:
        w_spec = pl.BlockSpec((pl.Element(tk), pl.Element(tn)), lambda i, j, k: (k * tk, pl.multiple_of(col0 + j * tn, 128)))
    return pl.pallas_call(
        functools.partial(_mm_kernel, n_extra=len(extras), epilogue=epilogue, nk=nk),
        grid=(m // tm, n // tn, nk),
        in_specs=[pl.BlockSpec((tm, tk), lambda i, j, k: (i, k)), w_spec, *extra_specs],
        out_specs=pl.BlockSpec((tm, tn), lambda i, j, k: (i, j)),
        out_shape=jax.ShapeDtypeStruct((m, n), out_dtype),
        scratch_shapes=[pltpu.VMEM((tm, tn), f32)],
        compiler_params=_params(("parallel", "parallel", "arbitrary")),
        name=name,
    )(a, w, *extras)


def _softplus_bias(acc, bias):
    v = acc + bias
    return jnp.maximum(v, 0.0) + jnp.log1p(jnp.exp(-jnp.abs(v)))


def _in_proj_main(u, w_in):
    return _matmul(u, w_in, col0=0, n=SEG1_COLS, tm=1664, tn=1024, tk=1024, out_dtype=bf16,
                   epilogue=lambda acc: acc, name="in_proj_main")


def _in_proj_dt(u, w_in, dt_bias):
    return _matmul(u, w_in, col0=COL_DT, n=N_HEADS, tm=1664, tn=N_HEADS, tk=1024, out_dtype=f32,
                   epilogue=_softplus_bias, extras=(dt_bias.reshape(1, N_HEADS),),
                   extra_specs=(pl.BlockSpec((1, N_HEADS), lambda i, j, k: (0, 0)),), name="in_proj_dt")


def _in_proj_gates(u, w_in):
    return _matmul(u, w_in, col0=COL_G, n=2 * D_MODEL, tm=1664, tn=1024, tk=1024, out_dtype=bf16,
                   epilogue=_sigmoid, name="in_proj_gates")


def _ssd_out_merge(vw, w_ssd_out, ssq, ma, gates, tm=1664, tn=1024):
    gb_off = D_MODEL // tn

    def merge(acc, ssq_t, ma_t, gb_t):
        rs = lax.rsqrt(ssq_t[:, :1] * (1.0 / D_INNER) + EPS)
        return ma_t.astype(f32) + gb_t.astype(f32) * (acc * rs)

    return _matmul(vw, w_ssd_out, col0=0, n=D_MODEL, tm=tm, tn=tn, tk=512, out_dtype=bf16, epilogue=merge,
                   extras=(ssq, ma, gates),
                   extra_specs=(pl.BlockSpec((tm, 128), lambda i, j, k: (i, 0)),
                                pl.BlockSpec((tm, tn), lambda i, j, k: (i, j)),
                                pl.BlockSpec((tm, tn), lambda i, j, k: (i, j + gb_off))),
                   name="ssd_out_merge")


def _out_proj(merged, w_out, h0, tm=1664, tn=1024):
    return _matmul(merged, w_out, col0=0, n=D_MODEL, tm=tm, tn=tn, tk=512, out_dtype=f32,
                   epilogue=lambda acc, res: res + acc, extras=(h0,),
                   extra_specs=(pl.BlockSpec((tm, tn), lambda i, j, k: (i, j)),), name="out_proj_residual")


def _pool_kernel(x_ref, halo_ref, w_ref, scale_ref, ga_ref, o_ref, ext_ref, pooled_ref, *, tm):
    gi = pl.program_id(0)
    i = pl.program_id(1)
    halo = halo_ref[...].astype(f32)
    ext_ref[0:HALO, :] = jnp.where(i == 0, 0.0, halo)
    ext_ref[HALO:HALO + tm, :] = x_ref[...].astype(f32)
    t = i * tm + lax.broadcasted_iota(i32, (tm, 1), 0) - PAD_ROWS
    for g, win in enumerate(POOL_WINDOWS):

        @pl.when(gi == g)
        def _(win=win):
            x = ext_ref[HALO:HALO + tm, :]
            s = x
            for k in range(1, win):
                s = s + ext_ref[HALO - k:HALO - k + tm, :]
            cnt = jnp.clip(t + 1, 1, win).astype(f32)
            pooled_ref[...] = (s / cnt - x).astype(bf16)

    acc = jnp.dot(pooled_ref[...], w_ref[0].astype(bf16), preferred_element_type=f32)
    o_ref[...] = (ga_ref[...].astype(f32) * (acc * scale_ref[...])).astype(o_ref.dtype)


def _pool_branch(proj, gates, pool_w, pool_scale, tm=640):
    gw = POOL_GROUP
    return pl.pallas_call(
        functools.partial(_pool_kernel, tm=tm),
        grid=(4, R // tm),
        in_specs=[
            pl.BlockSpec((tm, gw), lambda g, i: (i, g)),
            pl.BlockSpec((HALO, gw), lambda g, i: (jnp.maximum(i * (tm // HALO) - 1, 0), g)),
            pl.BlockSpec((1, gw, gw), lambda g, i: (g, 0, 0)),
            pl.BlockSpec((1, gw), lambda g, i: (0, g)),
            pl.BlockSpec((tm, gw), lambda g, i: (i, g)),
        ],
        out_specs=pl.BlockSpec((tm, gw), lambda g, i: (i, g)),
        out_shape=jax.ShapeDtypeStruct((R, D_MODEL), bf16),
        scratch_shapes=[pltpu.VMEM((HALO + tm, gw), f32), pltpu.VMEM((tm, gw), bf16)],
        compiler_params=_params(("parallel", "arbitrary")),
        name="pool_branch",
    )(proj, proj, pool_w, pool_scale.reshape(1, D_MODEL), gates)


def _conv_silu(cur_ref, halo_ref, w_ref, b_ref, ext_ref, c, row_ok):
    halo = halo_ref[...].astype(f32)
    ext_ref[0:HALO, :] = jnp.where(c == 0, 0.0, halo)
    ext_ref[HALO:HALO + CHUNK, :] = cur_ref[...].astype(f32)
    out = b_ref[...]
    for k in range(CONV_W):
        off = HALO - (CONV_W - 1) + k
        out = out + w_ref[k:k + 1, :] * ext_ref[off:off + CHUNK, :]
    out = out * _sigmoid(out)
    return jnp.where(row_ok, out, 0.0)


def _ssd_kernel(x_ref, xh_ref, b_ref, bh_ref, c_ref, ch_ref, z_ref, dt_ref, dtt_ref,
                alog_ref, alogt_ref, wx_ref, wb_ref, wc_ref, bx_ref, bb_ref, bc_ref,
                dskip_ref, nw_ref,
                vw_ref, ssq_ref,
                state_ref, extx_ref, extb_ref, extc_ref, xd2_ref, cd_ref):
    c = pl.program_id(0)
    g = pl.program_id(1)

    @pl.when(jnp.logical_and(c == 0, g == 0))
    def _():
        state_ref[...] = jnp.zeros_like(state_ref)

    rows = c * CHUNK + lax.broadcasted_iota(i32, (CHUNK, 1), 0)
    row_ok = rows >= PAD_ROWS
    cols = c * CHUNK + lax.broadcasted_iota(i32, (1, CHUNK), 1)
    col_ok = cols >= PAD_ROWS

    xs = _conv_silu(x_ref, xh_ref, wx_ref, bx_ref, extx_ref, c, row_ok)
    bm = _conv_silu(b_ref, bh_ref, wb_ref, bb_ref, extb_ref, c, row_ok)
    cm = _conv_silu(c_ref, ch_ref, wc_ref, bc_ref, extc_ref, c, True)

    dt = jnp.where(row_ok, dt_ref[0], 0.0)
    dtt = jnp.where(col_ok, dtt_ref[0], 0.0)
    a_dt = dt * (-jnp.exp(alog_ref[0]))
    a_dtt = dtt * (-jnp.exp(alogt_ref[0]))
    li = lax.broadcasted_iota(i32, (CHUNK, CHUNK), 0)
    si = lax.broadcasted_iota(i32, (CHUNK, CHUNK), 1)
    causal = li >= si
    tril = causal.astype(f32)
    triu = (li <= si).astype(f32)
    a_cs = jnp.dot(tril, a_dt, preferred_element_type=f32, precision=lax.Precision.HIGHEST)
    a_cst = jnp.dot(a_dtt, triu, preferred_element_type=f32, precision=lax.Precision.HIGHEST)
    last = a_cs[CHUNK - 1:CHUNK, :]
    e_cs = jnp.exp(a_cs)
    e_end = jnp.exp(last - a_cs)
    e_last = jnp.exp(last)

    bmb = bm.astype(bf16)
    cmb = cm.astype(bf16)
    cb = lax.dot_general(cmb, bmb, (((1,), (1,)), ((), ())), preferred_element_type=f32)
    state = state_ref[g]
    yoff = jnp.dot(cmb, state.astype(bf16), preferred_element_type=f32)

    lane = lax.broadcasted_iota(i32, (CHUNK, 2 * HEAD_DIM), 1)
    first = lane < HEAD_DIM
    lane1 = lax.broadcasted_iota(i32, (1, 2 * HEAD_DIM), 1)
    first1 = lane1 < HEAD_DIM
    ssq = jnp.zeros((CHUNK, 1), f32)
    for q in range(HEADS_PER_GROUP // 2):
        ha, hb = 2 * q, 2 * q + 1
        sl = slice(q * 2 * HEAD_DIM, (q + 1) * 2 * HEAD_DIM)
        xq = xs[:, sl]
        dtp = jnp.where(first, dt[:, ha:ha + 1], dt[:, hb:hb + 1])
        xdt = xq * dtp
        xdtb = xdt.astype(bf16)
        wa = cb * jnp.exp(jnp.where(causal, a_cs[:, ha:ha + 1] - a_cst[ha:ha + 1, :], -1e30))
        wb = cb * jnp.exp(jnp.where(causal, a_cs[:, hb:hb + 1] - a_cst[hb:hb + 1, :], -1e30))
        ya = jnp.dot(wa.astype(bf16), xdtb, preferred_element_type=f32)
        yb = jnp.dot(wb.astype(bf16), xdtb, preferred_element_type=f32)
        ecs = jnp.where(first, e_cs[:, ha:ha + 1], e_cs[:, hb:hb + 1])
        y = jnp.where(first, ya, yb) + yoff[:, sl] * ecs + dskip_ref[:, sl] * xq
        eend = jnp.where(first, e_end[:, ha:ha + 1], e_end[:, hb:hb + 1])
        xd2_ref[:, sl] = (xdt * eend).astype(bf16)
        cd_ref[:, sl] = jnp.where(first1, e_last[:, ha:ha + 1], e_last[:, hb:hb + 1])
        zq = z_ref[:, sl].astype(f32)
        v = y * (zq * _sigmoid(zq))
        ssq = ssq + jnp.sum(v * v, axis=1, keepdims=True)
        vw_ref[:, sl] = (v * nw_ref[:, sl]).astype(vw_ref.dtype)

    bt = jnp.transpose(bm).astype(bf16)
    s_new = jnp.dot(bt, xd2_ref[...], preferred_element_type=f32)
    state_ref[g] = state * cd_ref[...] + s_new

    ssq_b = jnp.broadcast_to(ssq, (CHUNK, 128))

    @pl.when(g == 0)
    def _():
        ssq_ref[...] = ssq_b

    @pl.when(g > 0)
    def _():
        ssq_ref[...] += ssq_b


def _ssd_branch(proj, dt, conv_w, conv_b, a_log, d_skip, ssd_norm_w):
    gwb = GROUP_W // GROUP_W
    xb0 = COL_X // GROUP_W
    zb0 = COL_Z // GROUP_W
    bb0 = COL_B // N_STATE
    cb0 = COL_C // N_STATE
    hpc = CHUNK // HALO

    def cur(b0):
        return lambda c, g: (c, b0 + g * gwb)

    def halo(b0):
        return lambda c, g: (jnp.maximum(c * hpc - 1, 0), b0 + g * gwb)

    dt_g = dt.reshape(R, N_GROUPS, HEADS_PER_GROUP).transpose(1, 0, 2)
    dt_gt = dt_g.transpose(0, 2, 1)
    alog = a_log.reshape(N_GROUPS, 1, HEADS_PER_GROUP)
    alogt = a_log.reshape(N_GROUPS, HEADS_PER_GROUP, 1)
    conv_b2 = conv_b.reshape(1, -1)
    dskip_e = jnp.repeat(d_skip, HEAD_DIM).reshape(1, D_INNER)
    nb0 = D_INNER // N_STATE
    in_specs = [
        pl.BlockSpec((CHUNK, GROUP_W), cur(xb0)),
        pl.BlockSpec((HALO, GROUP_W), halo(xb0)),
        pl.BlockSpec((CHUNK, N_STATE), cur(bb0)),
        pl.BlockSpec((HALO, N_STATE), halo(bb0)),
        pl.BlockSpec((CHUNK, N_STATE), cur(cb0)),
        pl.BlockSpec((HALO, N_STATE), halo(cb0)),
        pl.BlockSpec((CHUNK, GROUP_W), cur(zb0)),
        pl.BlockSpec((1, CHUNK, HEADS_PER_GROUP), lambda c, g: (g, c, 0)),
        pl.BlockSpec((1, HEADS_PER_GROUP, CHUNK), lambda c, g: (g, 0, c)),
        pl.BlockSpec((1, 1, HEADS_PER_GROUP), lambda c, g: (g, 0, 0)),
        pl.BlockSpec((1, HEADS_PER_GROUP, 1), lambda c, g: (g, 0, 0)),
        pl.BlockSpec((CONV_W, GROUP_W), lambda c, g: (0, g)),
        pl.BlockSpec((CONV_W, N_STATE), lambda c, g: (0, nb0 + g)),
        pl.BlockSpec((CONV_W, N_STATE), lambda c, g: (0, nb0 + N_GROUPS + g)),
        pl.BlockSpec((1, GROUP_W), lambda c, g: (0, g)),
        pl.BlockSpec((1, N_STATE), lambda c, g: (0, nb0 + g)),
        pl.BlockSpec((1, N_STATE), lambda c, g: (0, nb0 + N_GROUPS + g)),
        pl.BlockSpec((1, GROUP_W), lambda c, g: (0, g)),
        pl.BlockSpec((1, GROUP_W), lambda c, g: (0, g)),
    ]
    return pl.pallas_call(
        _ssd_kernel,
        grid=(N_CHUNKS, N_GROUPS),
        in_specs=in_specs,
        out_specs=[
            pl.BlockSpec((CHUNK, GROUP_W), lambda c, g: (c, g)),
            pl.BlockSpec((CHUNK, 128), lambda c, g: (c, 0)),
        ],
        out_shape=[
            jax.ShapeDtypeStruct((R, D_INNER), bf16),
            jax.ShapeDtypeStruct((R, 128), f32),
        ],
        scratch_shapes=[
            pltpu.VMEM((N_GROUPS, N_STATE, GROUP_W), f32),
            pltpu.VMEM((HALO + CHUNK, GROUP_W), f32),
            pltpu.VMEM((HALO + CHUNK, N_STATE), f32),
            pltpu.VMEM((HALO + CHUNK, N_STATE), f32),
            pltpu.VMEM((CHUNK, GROUP_W), bf16),
            pltpu.VMEM((1, GROUP_W), f32),
        ],
        compiler_params=_params(("arbitrary", "arbitrary")),
        name="ssd_mixer",
    )(proj, proj, proj, proj, proj, proj, proj, dt_g, dt_gt, alog, alogt,
      conv_w, conv_w, conv_w, conv_b2, conv_b2, conv_b2, dskip_e, ssd_norm_w.reshape(1, D_INNER))


def _router_kernel(h_ref, nw_ref, rw_ref, rb_ref, u_ref, idx_ref, gate_ref, *, tm):
    i = pl.program_id(0)
    h = h_ref[...]
    ms = jnp.mean(h * h, axis=-1, keepdims=True)
    u = h * lax.rsqrt(ms + EPS) * nw_ref[...]
    rows = i * tm + lax.broadcasted_iota(i32, (tm, 1), 0)
    u = jnp.where(rows >= PAD_ROWS, u, 0.0)
    ub = lax.bitcast_convert_type(u.astype(bf16).astype(f32), u32)
    half = D_MODEL // 2
    u_ref[...] = ub[:, half:] | (ub[:, :half] >> 16)
    logits = jnp.dot(u, rw_ref[...], preferred_element_type=f32, precision=lax.Precision.HIGHEST) + rb_ref[...]
    e_iota = lax.broadcasted_iota(i32, (tm, N_EXPERTS), 1)
    lane = lax.broadcasted_iota(i32, (tm, 128), 1)
    idx_out = jnp.zeros((tm, 128), i32)
    val_out = jnp.zeros((tm, 128), f32)
    vals = []
    for k in range(TOP_K):
        m = jnp.max(logits, axis=1, keepdims=True)
        sel = jnp.min(jnp.where(logits == m, e_iota, N_EXPERTS), axis=1, keepdims=True)
        vals.append(m)
        idx_out = jnp.where(lane == k, sel, idx_out)
        logits = jnp.where(e_iota == sel, -jnp.inf, logits)
    exps = [jnp.exp(v - vals[0]) for v in vals]
    denom = exps[0] + exps[1] + exps[2] + exps[3]
    for k in range(TOP_K):
        val_out = jnp.where(lane == k, exps[k] / denom, val_out)
    idx_ref[...] = idx_out
    gate_ref[...] = val_out


def _router(h1, norm_w, router_w, router_b, tm=320):
    return pl.pallas_call(
        functools.partial(_router_kernel, tm=tm),
        grid=(R // tm,),
        in_specs=[
            pl.BlockSpec((tm, D_MODEL), lambda i: (i, 0)),
            pl.BlockSpec((1, D_MODEL), lambda i: (0, 0)),
            pl.BlockSpec((D_MODEL, N_EXPERTS), lambda i: (0, 0)),
            pl.BlockSpec((1, N_EXPERTS), lambda i: (0, 0)),
        ],
        out_specs=[
            pl.BlockSpec((tm, D_MODEL // 2), lambda i: (i, 0)),
            pl.BlockSpec((tm, 128), lambda i: (i, 0)),
            pl.BlockSpec((tm, 128), lambda i: (i, 0)),
        ],
        out_shape=[
            jax.ShapeDtypeStruct((R, D_MODEL // 2), u32),
            jax.ShapeDtypeStruct((R, 128), i32),
            jax.ShapeDtypeStruct((R, 128), f32),
        ],
        compiler_params=_params(("parallel",)),
        name="router",
    )(h1, norm_w.reshape(1, D_MODEL), router_w, router_b.reshape(1, N_EXPERTS))


def _unpack_halves(xw):
    lo = lax.bitcast_convert_type(xw << 16, f32).astype(bf16)
    hi = lax.bitcast_convert_type(xw & jnp.uint32(0xFFFF0000), f32).astype(bf16)
    return lo, hi


def _gather_x_kernel(idx_ref, nu_ref, src_ref, o_ref, buf_ref, sem, *, tm):
    i = pl.program_id(0)
    n_used = nu_ref[0]

    def row_copy(src_row, slot, r):
        return pltpu.make_async_copy(src_ref.at[pl.ds(src_row, 1)], buf_ref.at[slot, pl.ds(r, 1)], sem.at[slot])

    def issue_tile(step):
        slot = step % 2

        def issue(r, carry):
            row_copy(idx_ref[step * tm + r], slot, r).start()
            return carry

        lax.fori_loop(0, tm, issue, 0, unroll=8)

    @pl.when(i == 0)
    def _():
        issue_tile(0)

    @pl.when(i + 1 < n_used)
    def _():
        issue_tile(i + 1)

    @pl.when(i < n_used)
    def _():
        slot = i % 2
        pltpu.make_async_copy(src_ref.at[pl.ds(0, tm)], buf_ref.at[slot], sem.at[slot]).wait()
        lo, hi = _unpack_halves(buf_ref[slot])
        half = D_MODEL // 2
        o_ref[:, :half] = lo
        o_ref[:, half:] = hi

    @pl.when(i >= n_used)
    def _():
        o_ref[...] = jnp.zeros_like(o_ref)


def _gather_x(src_rows, n_used, u2, tm=MOE_TM):
    return pl.pallas_call(
        functools.partial(_gather_x_kernel, tm=tm),
        grid_spec=pltpu.PrefetchScalarGridSpec(
            num_scalar_prefetch=2,
            grid=(XS_ROWS // tm,),
            in_specs=[pl.BlockSpec(memory_space=pl.ANY)],
            out_specs=pl.BlockSpec((tm, D_MODEL), lambda i, idx, nu: (i, 0)),
            scratch_shapes=[pltpu.VMEM((2, tm, D_MODEL // 2), u32), pltpu.SemaphoreType.DMA((2,))],
        ),
        out_shape=jax.ShapeDtypeStruct((XS_ROWS, D_MODEL), bf16),
        compiler_params=_params(("arbitrary",)),
        name="gather_x",
    )(src_rows, n_used, u2)


def _expert_changed(te_ref, nu_ref, i):
    t = jnp.minimum(i, nu_ref[0] - 1)
    return jnp.logical_or(i == 0, te_ref[t] != te_ref[jnp.maximum(t - 1, 0)])


def _moe_up_kernel(ge_ref, gt_ref, gn_ref, ng_ref, x_ref, wg_ref, wu_ref, bg_ref, bu_ref, init_ref, o_ref,
                   wgb_ref, wub_ref, obuf_ref, sem, *, tn):
    del init_ref
    g = pl.program_id(0)
    j = pl.program_id(1)
    active = g < ng_ref[0]
    gc = jnp.minimum(g, ng_ref[0] - 1)
    n_tiles = jnp.where(active, gn_ref[gc], 0)
    row0 = gt_ref[gc] * MOE_TM
    col0 = pl.multiple_of(j * tn, tn)

    def out_copy(slot, t):
        rows = pl.ds(pl.multiple_of(row0 + t * MOE_TM, MOE_TM), MOE_TM)
        return pltpu.make_async_copy(obuf_ref.at[slot], o_ref.at[rows, pl.ds(col0, tn)], sem.at[slot])

    @pl.when(active)
    def _():
        wgb_ref[...] = wg_ref[0].astype(bf16)
        wub_ref[...] = wu_ref[0].astype(bf16)

    def tile(t, carry):
        slot = t % 2

        @pl.when(t >= 2)
        def _():
            out_copy(slot, t - 2).wait()

        x = x_ref[pl.ds(pl.multiple_of(t * MOE_TM, MOE_TM), MOE_TM), :]
        gate = jnp.dot(x, wgb_ref[...], preferred_element_type=f32) + bg_ref[0]
        up = jnp.dot(x, wub_ref[...], preferred_element_type=f32) + bu_ref[0]
        gate = jnp.minimum(gate, LIMIT)
        up = jnp.clip(up, -LIMIT, LIMIT)
        obuf_ref[slot] = ((up + 1.0) * (gate * _sigmoid(ALPHA * gate))).astype(obuf_ref.dtype)
        out_copy(slot, t).start()
        return carry

    lax.fori_loop(0, n_tiles, tile, 0)

    for back in (1, 2):

        @pl.when(n_tiles >= back)
        def _(back=back):
            t = n_tiles - back
            out_copy(t % 2, t).wait()


def _moe_up(g_expert, g_tile0, g_ntiles, n_groups, xs, w_up, b_up, tn=256):
    nj = D_EXPERT // tn

    def gi(g, ng):
        return jnp.minimum(g, ng[0] - 1)

    def x_map(g, j, ge, gt, gn, ng):
        return (pl.multiple_of(gt[gi(g, ng)] * MOE_TM, MOE_TM), 0)

    def w_map(off):
        return lambda g, j, ge, gt, gn, ng: (ge[gi(g, ng)], 0, off + jnp.where(g < ng[0], j, nj - 1))

    n_prefetch = 4
    act_init = jnp.zeros((N_SLOTS, D_EXPERT), bf16)
    return pl.pallas_call(
        functools.partial(_moe_up_kernel, tn=tn),
        grid_spec=pltpu.PrefetchScalarGridSpec(
            num_scalar_prefetch=n_prefetch,
            grid=(MAX_GROUPS, nj),
            in_specs=[
                pl.BlockSpec((pl.Element(GROUP_ROWS), pl.Element(D_MODEL)), x_map),
                pl.BlockSpec((1, D_MODEL, tn), w_map(0)),
                pl.BlockSpec((1, D_MODEL, tn), w_map(nj)),
                pl.BlockSpec((1, 1, tn), w_map(0)),
                pl.BlockSpec((1, 1, tn), w_map(nj)),
                pl.BlockSpec(memory_space=pl.ANY),
            ],
            out_specs=pl.BlockSpec(memory_space=pl.ANY),
            scratch_shapes=[pltpu.VMEM((D_MODEL, tn), bf16), pltpu.VMEM((D_MODEL, tn), bf16),
                            pltpu.VMEM((2, MOE_TM, tn), bf16), pltpu.SemaphoreType.DMA((2,))],
        ),
        out_shape=jax.ShapeDtypeStruct((N_SLOTS, D_EXPERT), bf16),
        input_output_aliases={n_prefetch + 5: 0},
        compiler_params=_params(("arbitrary", "arbitrary")),
        name="moe_up",
    )(g_expert, g_tile0, g_ntiles, n_groups, xs, w_up, w_up,
      b_up.reshape(N_EXPERTS, 1, -1), b_up.reshape(N_EXPERTS, 1, -1), act_init)


def _moe_down_kernel(te_ref, nu_ref, a_ref, w_ref, b_ref, o_ref, wb_ref):
    i = pl.program_id(1)

    @pl.when(jnp.logical_and(i < nu_ref[0], _expert_changed(te_ref, nu_ref, i)))
    def _():
        wb_ref[...] = w_ref[0].astype(bf16)

    @pl.when(i < nu_ref[0])
    def _():
        o_ref[...] = jnp.dot(a_ref[...], wb_ref[...], preferred_element_type=f32) + b_ref[0]

    @pl.when(i >= nu_ref[0])
    def _():
        o_ref[...] = jnp.zeros_like(o_ref)


def _moe_down(tile_e, n_used, act, w_down, b_down, tn=1024):
    def ti(i, nu):
        return jnp.minimum(i, nu[0] - 1)

    return pl.pallas_call(
        _moe_down_kernel,
        grid_spec=pltpu.PrefetchScalarGridSpec(
            num_scalar_prefetch=2,
            grid=(D_MODEL // tn, N_TILES),
            in_specs=[
                pl.BlockSpec((MOE_TM, D_EXPERT), lambda j, i, te, nu: (ti(i, nu), 0)),
                pl.BlockSpec((1, D_EXPERT, tn), lambda j, i, te, nu: (te[ti(i, nu)], 0, j)),
                pl.BlockSpec((1, 1, tn), lambda j, i, te, nu: (te[ti(i, nu)], 0, j)),
            ],
            out_specs=pl.BlockSpec((MOE_TM, tn), lambda j, i, te, nu: (i, j)),
            scratch_shapes=[pltpu.VMEM((D_EXPERT, tn), bf16)],
        ),
        out_shape=jax.ShapeDtypeStruct((N_SLOTS, D_MODEL), f32),
        compiler_params=_params(("arbitrary", "arbitrary")),
        name="moe_down",
    )(tile_e, n_used, act, w_down, b_down.reshape(N_EXPERTS, 1, -1))


def _combine_kernel(pos_ref, h_ref, g_ref, nw_ref, y_ref, o_ref, buf_ref, sem, *, tm):
    i = pl.program_id(0)
    n = pl.num_programs(0)

    def row_copy(src_row, slot, k, r):
        return pltpu.make_async_copy(y_ref.at[pl.ds(src_row, 1)], buf_ref.at[slot, k, pl.ds(r, 1)], sem.at[slot])

    def issue_tile(step):
        slot = step % 2
        for k in range(TOP_K):

            def issue(r, carry, k=k):
                row_copy(pos_ref[k * SEQ + step * tm + r], slot, k, r).start()
                return carry

            lax.fori_loop(0, tm, issue, 0, unroll=8)

    @pl.when(i == 0)
    def _():
        issue_tile(0)

    @pl.when(i + 1 < n)
    def _():
        issue_tile(i + 1)

    slot = i % 2
    for k in range(TOP_K):
        pltpu.make_async_copy(y_ref.at[pl.ds(0, tm)], buf_ref.at[slot, k], sem.at[slot]).wait()

    g = g_ref[...]
    h = h_ref[...]
    for k in range(TOP_K):
        h = h + g[:, k:k + 1] * buf_ref[slot, k]
    ms = jnp.mean(h * h, axis=-1, keepdims=True)
    o_ref[...] = h * lax.rsqrt(ms + EPS) * nw_ref[...]


def _combine(pos_kt, h1, y, gates, norm_w, tm=128):
    nb = SEQ // tm
    off = (PAD_ROWS + N_META) // tm
    return pl.pallas_call(
        functools.partial(_combine_kernel, tm=tm),
        grid_spec=pltpu.PrefetchScalarGridSpec(
            num_scalar_prefetch=1,
            grid=(nb,),
            in_specs=[
                pl.BlockSpec((tm, D_MODEL), lambda i, pos: (i + off, 0)),
                pl.BlockSpec((tm, 128), lambda i, pos: (i + off, 0)),
                pl.BlockSpec((1, D_MODEL), lambda i, pos: (0, 0)),
                pl.BlockSpec(memory_space=pl.ANY),
            ],
            out_specs=pl.BlockSpec((tm, D_MODEL), lambda i, pos: (i, 0)),
            scratch_shapes=[pltpu.VMEM((2, TOP_K, tm, D_MODEL), f32), pltpu.SemaphoreType.DMA((2,))],
        ),
        out_shape=jax.ShapeDtypeStruct((SEQ, D_MODEL), f32),
        compiler_params=_params(("arbitrary",)),
        name="combine_final_norm",
    )(pos_kt, h1, gates, norm_w.reshape(1, D_MODEL), y)


def _rank_kernel(idx_ref, rank_ref, cnt_ref, carry_ref, *, tm):
    i = pl.program_id(0)

    @pl.when(i == 0)
    def _():
        carry_ref[...] = jnp.zeros_like(carry_ref)

    idx = idx_ref[...]
    rows = i * tm + lax.broadcasted_iota(i32, (tm, 1), 0)
    valid = rows >= PAD_ROWS
    e_iota = lax.broadcasted_iota(i32, (tm, N_EXPERTS), 1)
    hits = [jnp.logical_and(idx[:, k:k + 1] == e_iota, valid) for k in range(TOP_K)]
    onehot = sum(jnp.where(h, 1.0, 0.0) for h in hits)
    earlier = lax.broadcasted_iota(i32, (tm, tm), 0) > lax.broadcasted_iota(i32, (tm, tm), 1)
    before = jnp.dot(jnp.where(earlier, 1.0, 0.0).astype(bf16), onehot.astype(bf16),
                     preferred_element_type=f32) + carry_ref[...]
    lane = lax.broadcasted_iota(i32, (tm, 128), 1)
    out = jnp.zeros((tm, 128), i32)
    for k in range(TOP_K):
        rk = jnp.sum(jnp.where(hits[k], before, 0.0), axis=1, keepdims=True)
        out = jnp.where(lane == k, rk.astype(i32), out)
    rank_ref[...] = out
    carry_ref[...] += jnp.sum(onehot, axis=0, keepdims=True)
    cnt_ref[...] = carry_ref[...]


def _rank_pairs(top_idx, tm=320):
    return pl.pallas_call(
        functools.partial(_rank_kernel, tm=tm),
        grid=(R // tm,),
        in_specs=[pl.BlockSpec((tm, 128), lambda i: (i, 0))],
        out_specs=[pl.BlockSpec((tm, 128), lambda i: (i, 0)), pl.BlockSpec((1, N_EXPERTS), lambda i: (0, 0))],
        out_shape=[jax.ShapeDtypeStruct((R, 128), i32), jax.ShapeDtypeStruct((1, N_EXPERTS), f32)],
        scratch_shapes=[pltpu.VMEM((1, N_EXPERTS), f32)],
        compiler_params=_params(("arbitrary",)),
        name="rank_pairs",
    )(top_idx)


def _routing_tables(top_idx, rank, counts):
    counts = counts.reshape(N_EXPERTS).astype(i32)
    tiles = (counts + MOE_TM - 1) // MOE_TM
    tile_end = jnp.cumsum(tiles)
    tile_start = tile_end - tiles
    e_ids = jnp.arange(N_EXPERTS, dtype=i32)
    tok_e = top_idx[PAD_ROWS:, :TOP_K]
    tok_start = jnp.sum(jnp.where(tok_e[..., None] == e_ids, tile_start * MOE_TM, 0), axis=-1)
    pos = tok_start + rank[PAD_ROWS:, :TOP_K]
    pair_row = jnp.arange(N_PAIRS, dtype=i32) // TOP_K + PAD_ROWS
    src_rows = jnp.zeros((XS_ROWS,), i32).at[pos.reshape(-1)].set(pair_row)
    tile_ids = jnp.arange(N_TILES, dtype=i32)
    tile_e = jnp.minimum(jnp.sum(tile_end[None, :] <= tile_ids[:, None], axis=1), N_EXPERTS - 1).astype(i32)
    n_used = tile_end[-1:].astype(i32)
    groups = (tiles + MOE_CAP - 1) // MOE_CAP
    group_end = jnp.cumsum(groups)
    g_ids = jnp.arange(MAX_GROUPS, dtype=i32)
    g_expert = jnp.minimum(jnp.sum(group_end[None, :] <= g_ids[:, None], axis=1), N_EXPERTS - 1).astype(i32)
    g_local = g_ids - (group_end - groups)[g_expert]
    g_tile0 = (tile_start[g_expert] + g_local * MOE_CAP).astype(i32)
    g_ntiles = jnp.clip(tiles[g_expert] - g_local * MOE_CAP, 0, MOE_CAP).astype(i32)
    n_groups = group_end[-1:].astype(i32)
    return src_rows, pos, tile_e, n_used, (g_expert, g_tile0, g_ntiles, n_groups)


def kernel(x, meta_tokens, norm_mix_w, w_in, conv_w, conv_b, dt_bias, a_log, d_skip, ssd_norm_w, w_ssd_out,
           pool_w, pool_scale, w_out, norm_ffn_w, router_w, router_b, w_up, b_up, w_down, b_down, norm_final_w):
    assert x.shape == (1, SEQ, D_MODEL) and norm_mix_w.shape[0] == 1
    h0 = jnp.concatenate([jnp.zeros((PAD_ROWS, D_MODEL), f32), meta_tokens.astype(f32), x[0]], axis=0)

    u = _rms_norm_rows(h0, norm_mix_w[0])
    proj = _in_proj_main(u, w_in[0])
    dt = _in_proj_dt(u, w_in[0], dt_bias[0])
    gates = _in_proj_gates(u, w_in[0])
    ma = _pool_branch(proj, gates, pool_w[0], pool_scale[0])
    vw, ssq = _ssd_branch(proj, dt, conv_w[0], conv_b[0], a_log[0], d_skip[0], ssd_norm_w[0])
    merged = _ssd_out_merge(vw, w_ssd_out[0], ssq, ma, gates)
    h1 = _out_proj(merged, w_out[0], h0)

    u2, top_idx, top_gate = _router(h1, norm_ffn_w[0], router_w[0], router_b[0])
    rank, counts = _rank_pairs(top_idx)
    src_rows, pos, tile_e, n_used, groups = _routing_tables(top_idx, rank, counts)
    xs = _gather_x(src_rows, n_used, u2)
    act = _moe_up(*groups, xs, w_up[0], b_up[0])
    y = _moe_down(tile_e, n_used, act, w_down[0], b_down[0])
    pos_kt = pos[N_META:].T.reshape(-1)
    out = _combine(pos_kt, h1, y, top_gate, norm_final_w)
    return out[None]
```

```python
import functools

import jax
import jax.numpy as jnp
from jax import lax
from jax.experimental import pallas as pl
from jax.experimental.pallas import tpu as pltpu

f32 = jnp.float32
bf16 = jnp.bfloat16
i32 = jnp.int32
u32 = jnp.uint32

D_MODEL = 4096
SEQ = 8192
N_META = 16
CHUNK = 128
PAD_ROWS = CHUNK - N_META
R = PAD_ROWS + N_META + SEQ
N_CHUNKS = R // CHUNK
POOL_WINDOWS = (2, 4, 8, 16)
POOL_GROUP = D_MODEL // 4
D_INNER = 2 * D_MODEL
HEAD_DIM = 64
N_HEADS = D_INNER // HEAD_DIM
N_STATE = 128
N_GROUPS = 8
HEADS_PER_GROUP = N_HEADS // N_GROUPS
GROUP_W = HEADS_PER_GROUP * HEAD_DIM
CONV_W = 4
HALO = 16
N_EXPERTS = 32
TOP_K = 4
D_EXPERT = 7 * D_MODEL // 16
LIMIT = 7.0
ALPHA = 1.702
EPS = 1e-5
N_TOK = N_META + SEQ
N_PAIRS = N_TOK * TOP_K
MOE_TM = 256
N_TILES = -(-N_PAIRS // MOE_TM) + N_EXPERTS
N_SLOTS = N_TILES * MOE_TM
MOE_CAP = 6
GROUP_ROWS = MOE_CAP * MOE_TM
MAX_GROUPS = N_EXPERTS + N_TILES // MOE_CAP
XS_ROWS = N_SLOTS + GROUP_ROWS
COL_Z = D_MODEL
COL_X = COL_Z + D_INNER
COL_B = COL_X + D_INNER
COL_C = COL_B + N_GROUPS * N_STATE
COL_DT = COL_C + N_GROUPS * N_STATE
COL_G = COL_DT + N_HEADS
SEG1_COLS = COL_DT
V7X_VMEM_LIMIT = 56 * 1024 * 1024


def _params(sem, vmem=V7X_VMEM_LIMIT):
    return pltpu.CompilerParams(dimension_semantics=sem, vmem_limit_bytes=vmem)


def _sigmoid(v):
    return 1.0 / (1.0 + jnp.exp(-v))


def _rms_kernel(h_ref, w_ref, o_ref):
    h = h_ref[...]
    ms = jnp.mean(h * h, axis=-1, keepdims=True)
    o_ref[...] = (h * lax.rsqrt(ms + EPS) * w_ref[...]).astype(o_ref.dtype)


def _rms_norm_rows(h, w, tm=320):
    rows, d = h.shape
    return pl.pallas_call(
        _rms_kernel,
        grid=(rows // tm,),
        in_specs=[pl.BlockSpec((tm, d), lambda i: (i, 0)), pl.BlockSpec((1, d), lambda i: (0, 0))],
        out_specs=pl.BlockSpec((tm, d), lambda i: (i, 0)),
        out_shape=jax.ShapeDtypeStruct((rows, d), bf16),
        compiler_params=_params(("parallel",)),
        name="rms_mix",
    )(h, w.reshape(1, d))


def _mm_kernel(*refs, n_extra, epilogue, nk):
    a_ref, w_ref = refs[0], refs[1]
    extra = refs[2:2 + n_extra]
    o_ref = refs[2 + n_extra]
    acc_ref = refs[3 + n_extra]
    k = pl.program_id(2)

    @pl.when(k == 0)
    def _():
        acc_ref[...] = jnp.dot(a_ref[...], w_ref[...].astype(bf16), preferred_element_type=f32)

    @pl.when(k > 0)
    def _():
        acc_ref[...] += jnp.dot(a_ref[...], w_ref[...].astype(bf16), preferred_element_type=f32)

    @pl.when(k == nk - 1)
    def _():
        o_ref[...] = epilogue(acc_ref[...], *[e[...] for e in extra]).astype(o_ref.dtype)


def _matmul(a, w, *, col0, n, tm, tn, tk, out_dtype, epilogue, extras=(), extra_specs=(), name):
    m, kdim = a.shape
    nk = kdim // tk
    if col0 % tn == 0:
        w_spec = pl.BlockSpec((tk, tn), lambda i, j, k: (k, j + col0 // tn))
    else:
        w_spec = pl.BlockSpec((pl.Element(tk), pl.Element(tn)), lambda i, j, k: (k * tk, pl.multiple_of(col0 + j * tn, 128)))
    return pl.pallas_call(
        functools.partial(_mm_kernel, n_extra=len(extras), epilogue=epilogue, nk=nk),
        grid=(m // tm, n // tn, nk),
        in_specs=[pl.BlockSpec((tm, tk), lambda i, j, k: (i, k)), w_spec, *extra_specs],
        out_specs=pl.BlockSpec((tm, tn), lambda i, j, k: (i, j)),
        out_shape=jax.ShapeDtypeStruct((m, n), out_dtype),
        scratch_shapes=[pltpu.VMEM((tm, tn), f32)],
        compiler_params=_params(("parallel", "parallel", "arbitrary")),
        name=name,
    )(a, w, *extras)


def _softplus_bias(acc, bias):
    v = acc + bias
    return jnp.maximum(v, 0.0) + jnp.log1p(jnp.exp(-jnp.abs(v)))


def _in_proj_main(u, w_in):
    return _matmul(u, w_in, col0=0, n=SEG1_COLS, tm=1664, tn=1024, tk=2048, out_dtype=bf16,
                   epilogue=lambda acc: acc, name="in_proj_main")


def _in_proj_dt(u, w_in, dt_bias):
    return _matmul(u, w_in, col0=COL_DT, n=N_HEADS, tm=1664, tn=N_HEADS, tk=1024, out_dtype=f32,
                   epilogue=_softplus_bias, extras=(dt_bias.reshape(1, N_HEADS),),
                   extra_specs=(pl.BlockSpec((1, N_HEADS), lambda i, j, k: (0, 0)),), name="in_proj_dt")


def _in_proj_gates(u, w_in):
    return _matmul(u, w_in, col0=COL_G, n=2 * D_MODEL, tm=1664, tn=1024, tk=2048, out_dtype=bf16,
                   epilogue=_sigmoid, name="in_proj_gates")


def _ssd_out_merge(vw, w_ssd_out, ssq, ma, gates, tm=1664, tn=1024):
    gb_off = D_MODEL // tn

    def merge(acc, ssq_t, ma_t, gb_t):
        rs = lax.rsqrt(ssq_t[:, :1] * (1.0 / D_INNER) + EPS)
        return ma_t.astype(f32) + gb_t.astype(f32) * (acc * rs)

    return _matmul(vw, w_ssd_out, col0=0, n=D_MODEL, tm=tm, tn=tn, tk=512, out_dtype=bf16, epilogue=merge,
                   extras=(ssq, ma, gates),
                   extra_specs=(pl.BlockSpec((tm, 128), lambda i, j, k: (i, 0)),
                                pl.BlockSpec((tm, tn), lambda i, j, k: (i, j)),
                                pl.BlockSpec((tm, tn), lambda i, j, k: (i, j + gb_off))),
                   name="ssd_out_merge")


def _out_proj(merged, w_out, h0, tm=1664, tn=1024):
    return _matmul(merged, w_out, col0=0, n=D_MODEL, tm=tm, tn=tn, tk=512, out_dtype=f32,
                   epilogue=lambda acc, res: res + acc, extras=(h0,),
                   extra_specs=(pl.BlockSpec((tm, tn), lambda i, j, k: (i, j)),), name="out_proj_residual")


def _pool_kernel(x_ref, halo_ref, w_ref, scale_ref, ga_ref, o_ref, ext_ref, pooled_ref, *, tm):
    gi = pl.program_id(0)
    i = pl.program_id(1)
    halo = halo_ref[...].astype(f32)
    ext_ref[0:HALO, :] = jnp.where(i == 0, 0.0, halo)
    ext_ref[HALO:HALO + tm, :] = x_ref[...].astype(f32)
    t = i * tm + lax.broadcasted_iota(i32, (tm, 1), 0) - PAD_ROWS
    for g, win in enumerate(POOL_WINDOWS):

        @pl.when(gi == g)
        def _(win=win):
            x = ext_ref[HALO:HALO + tm, :]
            s = x
            for k in range(1, win):
                s = s + ext_ref[HALO - k:HALO - k + tm, :]
            cnt = jnp.clip(t + 1, 1, win).astype(f32)
            pooled_ref[...] = (s / cnt - x).astype(bf16)

    acc = jnp.dot(pooled_ref[...], w_ref[0].astype(bf16), preferred_element_type=f32)
    o_ref[...] = (ga_ref[...].astype(f32) * (acc * scale_ref[...])).astype(o_ref.dtype)


def _pool_branch(proj, gates, pool_w, pool_scale, tm=640):
    gw = POOL_GROUP
    return pl.pallas_call(
        functools.partial(_pool_kernel, tm=tm),
        grid=(4, R // tm),
        in_specs=[
            pl.BlockSpec((tm, gw), lambda g, i: (i, g)),
            pl.BlockSpec((HALO, gw), lambda g, i: (jnp.maximum(i * (tm // HALO) - 1, 0), g)),
            pl.BlockSpec((1, gw, gw), lambda g, i: (g, 0, 0)),
            pl.BlockSpec((1, gw), lambda g, i: (0, g)),
            pl.BlockSpec((tm, gw), lambda g, i: (i, g)),
        ],
        out_specs=pl.BlockSpec((tm, gw), lambda g, i: (i, g)),
        out_shape=jax.ShapeDtypeStruct((R, D_MODEL), bf16),
        scratch_shapes=[pltpu.VMEM((HALO + tm, gw), f32), pltpu.VMEM((tm, gw), bf16)],
        compiler_params=_params(("parallel", "arbitrary")),
        name="pool_branch",
    )(proj, proj, pool_w, pool_scale.reshape(1, D_MODEL), gates)


def _conv_silu(cur_ref, halo_ref, w_ref, b_ref, ext_ref, c, row_ok):
    halo = halo_ref[...].astype(f32)
    ext_ref[0:HALO, :] = jnp.where(c == 0, 0.0, halo)
    ext_ref[HALO:HALO + CHUNK, :] = cur_ref[...].astype(f32)
    out = b_ref[...]
    for k in range(CONV_W):
        off = HALO - (CONV_W - 1) + k
        out = out + w_ref[k:k + 1, :] * ext_ref[off:off + CHUNK, :]
    out = out * _sigmoid(out)
    return jnp.where(row_ok, out, 0.0)


def _ssd_kernel(x_ref, xh_ref, b_ref, bh_ref, c_ref, ch_ref, z_ref, dt_ref, dtt_ref,
                alog_ref, alogt_ref, wx_ref, wb_ref, wc_ref, bx_ref, bb_ref, bc_ref,
                dskip_ref, nw_ref,
                vw_ref, ssq_ref,
                state_ref, extx_ref, extb_ref, extc_ref, xd2_ref, cd_ref):
    c = pl.program_id(0)
    g = pl.program_id(1)

    @pl.when(jnp.logical_and(c == 0, g == 0))
    def _():
        state_ref[...] = jnp.zeros_like(state_ref)

    rows = c * CHUNK + lax.broadcasted_iota(i32, (CHUNK, 1), 0)
    row_ok = rows >= PAD_ROWS
    cols = c * CHUNK + lax.broadcasted_iota(i32, (1, CHUNK), 1)
    col_ok = cols >= PAD_ROWS

    xs = _conv_silu(x_ref, xh_ref, wx_ref, bx_ref, extx_ref, c, row_ok)
    bm = _conv_silu(b_ref, bh_ref, wb_ref, bb_ref, extb_ref, c, row_ok)
    cm = _conv_silu(c_ref, ch_ref, wc_ref, bc_ref, extc_ref, c, True)

    dt = jnp.where(row_ok, dt_ref[0], 0.0)
    dtt = jnp.where(col_ok, dtt_ref[0], 0.0)
    a_dt = dt * (-jnp.exp(alog_ref[0]))
    a_dtt = dtt * (-jnp.exp(alogt_ref[0]))
    li = lax.broadcasted_iota(i32, (CHUNK, CHUNK), 0)
    si = lax.broadcasted_iota(i32, (CHUNK, CHUNK), 1)
    causal = li >= si
    tril = causal.astype(f32)
    triu = (li <= si).astype(f32)
    a_cs = jnp.dot(tril, a_dt, preferred_element_type=f32, precision=lax.Precision.HIGHEST)
    a_cst = jnp.dot(a_dtt, triu, preferred_element_type=f32, precision=lax.Precision.HIGHEST)
    last = a_cs[CHUNK - 1:CHUNK, :]
    e_cs = jnp.exp(a_cs)
    e_end = jnp.exp(last - a_cs)
    e_last = jnp.exp(last)

    bmb = bm.astype(bf16)
    cmb = cm.astype(bf16)
    cb = lax.dot_general(cmb, bmb, (((1,), (1,)), ((), ())), preferred_element_type=f32)
    state = state_ref[g]
    yoff = jnp.dot(cmb, state.astype(bf16), preferred_element_type=f32)

    lane = lax.broadcasted_iota(i32, (CHUNK, 2 * HEAD_DIM), 1)
    first = lane < HEAD_DIM
    lane1 = lax.broadcasted_iota(i32, (1, 2 * HEAD_DIM), 1)
    first1 = lane1 < HEAD_DIM
    ssq = jnp.zeros((CHUNK, 1), f32)
    for q in range(HEADS_PER_GROUP // 2):
        ha, hb = 2 * q, 2 * q + 1
        sl = slice(q * 2 * HEAD_DIM, (q + 1) * 2 * HEAD_DIM)
        xq = xs[:, sl]
        dtp = jnp.where(first, dt[:, ha:ha + 1], dt[:, hb:hb + 1])
        xdt = xq * dtp
        xdtb = xdt.astype(bf16)
        wa = cb * jnp.exp(jnp.where(causal, a_cs[:, ha:ha + 1] - a_cst[ha:ha + 1, :], -1e30))
        wb = cb * jnp.exp(jnp.where(causal, a_cs[:, hb:hb + 1] - a_cst[hb:hb + 1, :], -1e30))
        ya = jnp.dot(wa.astype(bf16), xdtb, preferred_element_type=f32)
        yb = jnp.dot(wb.astype(bf16), xdtb, preferred_element_type=f32)
        ecs = jnp.where(first, e_cs[:, ha:ha + 1], e_cs[:, hb:hb + 1])
        y = jnp.where(first, ya, yb) + yoff[:, sl] * ecs + dskip_ref[:, sl] * xq
        eend = jnp.where(first, e_end[:, ha:ha + 1], e_end[:, hb:hb + 1])
        xd2_ref[:, sl] = (xdt * eend).astype(bf16)
        cd_ref[:, sl] = jnp.where(first1, e_last[:, ha:ha + 1], e_last[:, hb:hb + 1])
        zq = z_ref[:, sl].astype(f32)
        v = y * (zq * _sigmoid(zq))
        ssq = ssq + jnp.sum(v * v, axis=1, keepdims=True)
        vw_ref[:, sl] = (v * nw_ref[:, sl]).astype(vw_ref.dtype)

    bt = jnp.transpose(bm).astype(bf16)
    s_new = jnp.dot(bt, xd2_ref[...], preferred_element_type=f32)
    state_ref[g] = state * cd_ref[...] + s_new

    ssq_b = jnp.broadcast_to(ssq, (CHUNK, 128))

    @pl.when(g == 0)
    def _():
        ssq_ref[...] = ssq_b

    @pl.when(g > 0)
    def _():
        ssq_ref[...] += ssq_b


def _ssd_branch(proj, dt, conv_w, conv_b, a_log, d_skip, ssd_norm_w):
    gwb = GROUP_W // GROUP_W
    xb0 = COL_X // GROUP_W
    zb0 = COL_Z // GROUP_W
    bb0 = COL_B // N_STATE
    cb0 = COL_C // N_STATE
    hpc = CHUNK // HALO

    def cur(b0):
        return lambda c, g: (c, b0 + g * gwb)

    def halo(b0):
        return lambda c, g: (jnp.maximum(c * hpc - 1, 0), b0 + g * gwb)

    dt_g = dt.reshape(R, N_GROUPS, HEADS_PER_GROUP).transpose(1, 0, 2)
    dt_gt = dt_g.transpose(0, 2, 1)
    alog = a_log.reshape(N_GROUPS, 1, HEADS_PER_GROUP)
    alogt = a_log.reshape(N_GROUPS, HEADS_PER_GROUP, 1)
    conv_b2 = conv_b.reshape(1, -1)
    dskip_e = jnp.repeat(d_skip, HEAD_DIM).reshape(1, D_INNER)
    nb0 = D_INNER // N_STATE
    in_specs = [
        pl.BlockSpec((CHUNK, GROUP_W), cur(xb0)),
        pl.BlockSpec((HALO, GROUP_W), halo(xb0)),
        pl.BlockSpec((CHUNK, N_STATE), cur(bb0)),
        pl.BlockSpec((HALO, N_STATE), halo(bb0)),
        pl.BlockSpec((CHUNK, N_STATE), cur(cb0)),
        pl.BlockSpec((HALO, N_STATE), halo(cb0)),
        pl.BlockSpec((CHUNK, GROUP_W), cur(zb0)),
        pl.BlockSpec((1, CHUNK, HEADS_PER_GROUP), lambda c, g: (g, c, 0)),
        pl.BlockSpec((1, HEADS_PER_GROUP, CHUNK), lambda c, g: (g, 0, c)),
        pl.BlockSpec((1, 1, HEADS_PER_GROUP), lambda c, g: (g, 0, 0)),
        pl.BlockSpec((1, HEADS_PER_GROUP, 1), lambda c, g: (g, 0, 0)),
        pl.BlockSpec((CONV_W, GROUP_W), lambda c, g: (0, g)),
        pl.BlockSpec((CONV_W, N_STATE), lambda c, g: (0, nb0 + g)),
        pl.BlockSpec((CONV_W, N_STATE), lambda c, g: (0, nb0 + N_GROUPS + g)),
        pl.BlockSpec((1, GROUP_W), lambda c, g: (0, g)),
        pl.BlockSpec((1, N_STATE), lambda c, g: (0, nb0 + g)),
        pl.BlockSpec((1, N_STATE), lambda c, g: (0, nb0 + N_GROUPS + g)),
        pl.BlockSpec((1, GROUP_W), lambda c, g: (0, g)),
        pl.BlockSpec((1, GROUP_W), lambda c, g: (0, g)),
    ]
    return pl.pallas_call(
        _ssd_kernel,
        grid=(N_CHUNKS, N_GROUPS),
        in_specs=in_specs,
        out_specs=[
            pl.BlockSpec((CHUNK, GROUP_W), lambda c, g: (c, g)),
            pl.BlockSpec((CHUNK, 128), lambda c, g: (c, 0)),
        ],
        out_shape=[
            jax.ShapeDtypeStruct((R, D_INNER), bf16),
            jax.ShapeDtypeStruct((R, 128), f32),
        ],
        scratch_shapes=[
            pltpu.VMEM((N_GROUPS, N_STATE, GROUP_W), f32),
            pltpu.VMEM((HALO + CHUNK, GROUP_W), f32),
            pltpu.VMEM((HALO + CHUNK, N_STATE), f32),
            pltpu.VMEM((HALO + CHUNK, N_STATE), f32),
            pltpu.VMEM((CHUNK, GROUP_W), bf16),
            pltpu.VMEM((1, GROUP_W), f32),
        ],
        compiler_params=_params(("arbitrary", "arbitrary")),
        name="ssd_mixer",
    )(proj, proj, proj, proj, proj, proj, proj, dt_g, dt_gt, alog, alogt,
      conv_w, conv_w, conv_w, conv_b2, conv_b2, conv_b2, dskip_e, ssd_norm_w.reshape(1, D_INNER))


def _router_kernel(h_ref, nw_ref, rw_ref, rb_ref, u_ref, idx_ref, gate_ref, *, tm):
    i = pl.program_id(0)
    h = h_ref[...]
    ms = jnp.mean(h * h, axis=-1, keepdims=True)
    u = h * lax.rsqrt(ms + EPS) * nw_ref[...]
    rows = i * tm + lax.broadcasted_iota(i32, (tm, 1), 0)
    u = jnp.where(rows >= PAD_ROWS, u, 0.0)
    ub = lax.bitcast_convert_type(u.astype(bf16).astype(f32), u32)
    half = D_MODEL // 2
    u_ref[...] = ub[:, half:] | (ub[:, :half] >> 16)
    logits = jnp.dot(u, rw_ref[...], preferred_element_type=f32, precision=lax.Precision.HIGHEST) + rb_ref[...]
    e_iota = lax.broadcasted_iota(i32, (tm, N_EXPERTS), 1)
    lane = lax.broadcasted_iota(i32, (tm, 128), 1)
    idx_out = jnp.zeros((tm, 128), i32)
    val_out = jnp.zeros((tm, 128), f32)
    vals = []
    for k in range(TOP_K):
        m = jnp.max(logits, axis=1, keepdims=True)
        sel = jnp.min(jnp.where(logits == m, e_iota, N_EXPERTS), axis=1, keepdims=True)
        vals.append(m)
        idx_out = jnp.where(lane == k, sel, idx_out)
        logits = jnp.where(e_iota == sel, -jnp.inf, logits)
    exps = [jnp.exp(v - vals[0]) for v in vals]
    denom = exps[0] + exps[1] + exps[2] + exps[3]
    for k in range(TOP_K):
        val_out = jnp.where(lane == k, exps[k] / denom, val_out)
    idx_ref[...] = idx_out
    gate_ref[...] = val_out


def _router(h1, norm_w, router_w, router_b, tm=320):
    return pl.pallas_call(
        functools.partial(_router_kernel, tm=tm),
        grid=(R // tm,),
        in_specs=[
            pl.BlockSpec((tm, D_MODEL), lambda i: (i, 0)),
            pl.BlockSpec((1, D_MODEL), lambda i: (0, 0)),
            pl.BlockSpec((D_MODEL, N_EXPERTS), lambda i: (0, 0)),
            pl.BlockSpec((1, N_EXPERTS), lambda i: (0, 0)),
        ],
        out_specs=[
            pl.BlockSpec((tm, D_MODEL // 2), lambda i: (i, 0)),
            pl.BlockSpec((tm, 128), lambda i: (i, 0)),
            pl.BlockSpec((tm, 128), lambda i: (i, 0)),
        ],
        out_shape=[
            jax.ShapeDtypeStruct((R, D_MODEL // 2), u32),
            jax.ShapeDtypeStruct((R, 128), i32),
            jax.ShapeDtypeStruct((R, 128), f32),
        ],
        compiler_params=_params(("parallel",)),
        name="router",
    )(h1, norm_w.reshape(1, D_MODEL), router_w, router_b.reshape(1, N_EXPERTS))


def _unpack_halves(xw):
    lo = lax.bitcast_convert_type(xw << 16, f32).astype(bf16)
    hi = lax.bitcast_convert_type(xw & jnp.uint32(0xFFFF0000), f32).astype(bf16)
    return lo, hi


def _gather_x_kernel(idx_ref, nu_ref, src_ref, o_ref, buf_ref, sem, *, tm):
    i = pl.program_id(0)
    n_used = nu_ref[0]

    def row_copy(src_row, slot, r):
        return pltpu.make_async_copy(src_ref.at[pl.ds(src_row, 1)], buf_ref.at[slot, pl.ds(r, 1)], sem.at[slot])

    def issue_tile(step):
        slot = step % 2

        def issue(r, carry):
            row_copy(idx_ref[step * tm + r], slot, r).start()
            return carry

        lax.fori_loop(0, tm, issue, 0, unroll=8)

    @pl.when(i == 0)
    def _():
        issue_tile(0)

    @pl.when(i + 1 < n_used)
    def _():
        issue_tile(i + 1)

    @pl.when(i < n_used)
    def _():
        slot = i % 2
        pltpu.make_async_copy(src_ref.at[pl.ds(0, tm)], buf_ref.at[slot], sem.at[slot]).wait()
        lo, hi = _unpack_halves(buf_ref[slot])
        half = D_MODEL // 2
        o_ref[:, :half] = lo
        o_ref[:, half:] = hi

    @pl.when(i >= n_used)
    def _():
        o_ref[...] = jnp.zeros_like(o_ref)


def _gather_x(src_rows, n_used, u2, tm=MOE_TM):
    return pl.pallas_call(
        functools.partial(_gather_x_kernel, tm=tm),
        grid_spec=pltpu.PrefetchScalarGridSpec(
            num_scalar_prefetch=2,
            grid=(XS_ROWS // tm,),
            in_specs=[pl.BlockSpec(memory_space=pl.ANY)],
            out_specs=pl.BlockSpec((tm, D_MODEL), lambda i, idx, nu: (i, 0)),
            scratch_shapes=[pltpu.VMEM((2, tm, D_MODEL // 2), u32), pltpu.SemaphoreType.DMA((2,))],
        ),
        out_shape=jax.ShapeDtypeStruct((XS_ROWS, D_MODEL), bf16),
        compiler_params=_params(("arbitrary",)),
        name="gather_x",
    )(src_rows, n_used, u2)


def _grouped_kernel(ge_ref, gt_ref, gn_ref, ng_ref, nu_ref, x_ref, *refs, n_w, tn, ocols, tile_fn):
    w_refs, b_refs, o_ref = refs[:n_w], refs[n_w:2 * n_w], refs[2 * n_w]
    wb_ref, obuf_ref, zbuf_ref, sem, zsem = refs[2 * n_w + 1:]
    g = pl.program_id(0)
    j = pl.program_id(1)
    active = g < ng_ref[0]
    gc = jnp.minimum(g, ng_ref[0] - 1)
    n_tiles = jnp.where(active, gn_ref[gc], 0)
    tile0 = gt_ref[gc]
    col0 = pl.multiple_of(j * ocols, ocols)
    total_tiles = o_ref.shape[0] // MOE_TM

    @pl.when(jnp.logical_and(g == 0, j == 0))
    def _():
        zbuf_ref[...] = jnp.zeros_like(zbuf_ref)

        def zero_copy(t):
            rows = pl.ds(pl.multiple_of(t * MOE_TM, MOE_TM), MOE_TM)
            return pltpu.make_async_copy(zbuf_ref, o_ref.at[rows], zsem)

        def z_start(t, carry):
            zero_copy(t).start()
            return carry

        def z_wait(t, carry):
            zero_copy(t).wait()
            return carry

        lax.fori_loop(nu_ref[0], total_tiles, z_start, 0)
        lax.fori_loop(nu_ref[0], total_tiles, z_wait, 0)

    @pl.when(active)
    def _():
        for k in range(n_w):
            wb_ref[:, k * tn:(k + 1) * tn] = w_refs[k][0].astype(bf16)

    def rows_of(tile, n):
        return pl.ds(pl.multiple_of(tile * MOE_TM, MOE_TM), n * MOE_TM)

    def pair_copy(slot, tile):
        return pltpu.make_async_copy(obuf_ref.at[slot], o_ref.at[rows_of(tile0 + tile, 2), pl.ds(col0, ocols)],
                                     sem.at[slot])

    def tail_copy(slot, tile):
        return pltpu.make_async_copy(obuf_ref.at[slot, pl.ds(0, MOE_TM)],
                                     o_ref.at[rows_of(tile0 + tile, 1), pl.ds(col0, ocols)], sem.at[slot])

    def compute(tile, n):
        acc = jnp.dot(x_ref[rows_of(tile, n), :], wb_ref[...], preferred_element_type=f32)
        return tile_fn(acc, [b[0] for b in b_refs])

    n_pairs = n_tiles // 2
    has_tail = n_tiles % 2 == 1

    def pair(c, carry):
        slot = c % 2

        @pl.when(c >= 2)
        def _():
            pair_copy(slot, 2 * (c - 2)).wait()

        obuf_ref[slot] = compute(2 * c, 2)
        pair_copy(slot, 2 * c).start()
        return carry

    lax.fori_loop(0, n_pairs, pair, 0)

    @pl.when(has_tail)
    def _():
        slot = n_pairs % 2

        @pl.when(n_pairs >= 2)
        def _():
            pair_copy(slot, 2 * (n_pairs - 2)).wait()

        obuf_ref[slot, pl.ds(0, MOE_TM)] = compute(n_tiles - 1, 1)
        tail_copy(slot, n_tiles - 1).start()

    n_chunks = n_pairs + n_tiles % 2
    for back in (1, 2):
        q = n_chunks - back

        @pl.when(jnp.logical_and(q >= 0, q < n_pairs))
        def _(q=q):
            pair_copy(q % 2, 2 * q).wait()

        @pl.when(jnp.logical_and(q >= 0, q == n_pairs))
        def _(q=q):
            tail_copy(q % 2, n_tiles - 1).wait()


def _grouped_call(groups, n_used, x, ws, bs, *, k_dim, tn, nj, w_col_offsets, ocols, out_shape, tile_fn, name):
    g_expert, g_tile0, g_ntiles, n_groups = groups
    n_w = len(ws)

    def gi(g, ng):
        return jnp.minimum(g, ng[0] - 1)

    def x_map(g, j, ge, gt, gn, ng, nu):
        return (pl.multiple_of(gt[gi(g, ng)] * MOE_TM, MOE_TM), 0)

    def w_map(off):
        return lambda g, j, ge, gt, gn, ng, nu: (ge[gi(g, ng)], 0, off + jnp.where(g < ng[0], j, nj - 1))

    return pl.pallas_call(
        functools.partial(_grouped_kernel, n_w=n_w, tn=tn, ocols=ocols, tile_fn=tile_fn),
        grid_spec=pltpu.PrefetchScalarGridSpec(
            num_scalar_prefetch=5,
            grid=(MAX_GROUPS, nj),
            in_specs=[
                pl.BlockSpec((pl.Element(GROUP_ROWS), pl.Element(k_dim)), x_map),
                *[pl.BlockSpec((1, k_dim, tn), w_map(off)) for off in w_col_offsets],
                *[pl.BlockSpec((1, 1, tn), w_map(off)) for off in w_col_offsets],
            ],
            out_specs=pl.BlockSpec(memory_space=pl.ANY),
            scratch_shapes=[
                pltpu.VMEM((k_dim, n_w * tn), bf16),
                pltpu.VMEM((2, 2 * MOE_TM, ocols), out_shape.dtype),
                pltpu.VMEM((MOE_TM, out_shape.shape[1]), out_shape.dtype),
                pltpu.SemaphoreType.DMA((2,)),
                pltpu.SemaphoreType.DMA(()),
            ],
        ),
        out_shape=out_shape,
        compiler_params=_params(("arbitrary", "arbitrary")),
        name=name,
    )(g_expert, g_tile0, g_ntiles, n_groups, n_used, x, *ws, *bs)


def _moe_up(groups, n_used, xs, w_up, b_up, tn=256):
    nj = D_EXPERT // tn

    def tile_fn(acc, biases):
        gate = jnp.minimum(acc[:, :tn] + biases[0], LIMIT)
        up = jnp.clip(acc[:, tn:] + biases[1], -LIMIT, LIMIT)
        return ((up + 1.0) * (gate * _sigmoid(ALPHA * gate))).astype(bf16)

    b3 = b_up.reshape(N_EXPERTS, 1, -1)
    return _grouped_call(groups, n_used, xs, (w_up, w_up), (b3, b3), k_dim=D_MODEL, tn=tn, nj=nj,
                         w_col_offsets=(0, nj), ocols=tn,
                         out_shape=jax.ShapeDtypeStruct((XS_ROWS, D_EXPERT), bf16), tile_fn=tile_fn, name="moe_up")


def _moe_down(groups, n_used, act, w_down, b_down, tn=1024):
    half = tn // 2

    def tile_fn(acc, biases):
        yb = lax.bitcast_convert_type((acc + biases[0]).astype(bf16).astype(f32), u32)
        return yb[:, half:] | (yb[:, :half] >> 16)

    return _grouped_call(groups, n_used, act, (w_down,), (b_down.reshape(N_EXPERTS, 1, -1),), k_dim=D_EXPERT,
                         tn=tn, nj=D_MODEL // tn, w_col_offsets=(0,), ocols=half,
                         out_shape=jax.ShapeDtypeStruct((N_SLOTS, D_MODEL // 2), u32), tile_fn=tile_fn,
                         name="moe_down")


def _combine_kernel(pos_ref, h_ref, g_ref, nw_ref, y_ref, o_ref, buf_ref, sem, *, tm, y_tn):
    i = pl.program_id(0)
    n = pl.num_programs(0)

    def row_copy(src_row, slot, k, r):
        return pltpu.make_async_copy(y_ref.at[pl.ds(src_row, 1)], buf_ref.at[slot, k, pl.ds(r, 1)], sem.at[slot])

    def issue_tile(step):
        slot = step % 2
        for k in range(TOP_K):

            def issue(r, carry, k=k):
                row_copy(pos_ref[k * SEQ + step * tm + r], slot, k, r).start()
                return carry

            lax.fori_loop(0, tm, issue, 0, unroll=8)

    @pl.when(i == 0)
    def _():
        issue_tile(0)

    @pl.when(i + 1 < n)
    def _():
        issue_tile(i + 1)

    slot = i % 2
    for k in range(TOP_K):
        pltpu.make_async_copy(y_ref.at[pl.ds(0, tm)], buf_ref.at[slot, k], sem.at[slot]).wait()

    g = g_ref[...]
    h = h_ref[...]
    half = y_tn // 2
    for k in range(TOP_K):
        words = buf_ref[slot, k]
        parts = []
        for jb in range(D_MODEL // y_tn):
            wj = words[:, jb * half:(jb + 1) * half]
            parts.append(lax.bitcast_convert_type(wj << 16, f32))
            parts.append(lax.bitcast_convert_type(wj & jnp.uint32(0xFFFF0000), f32))
        h = h + g[:, k:k + 1] * jnp.concatenate(parts, axis=1)
    ms = jnp.mean(h * h, axis=-1, keepdims=True)
    o_ref[...] = h * lax.rsqrt(ms + EPS) * nw_ref[...]


def _combine(pos_kt, h1, y, gates, norm_w, tm=128, y_tn=1024):
    nb = SEQ // tm
    off = (PAD_ROWS + N_META) // tm
    return pl.pallas_call(
        functools.partial(_combine_kernel, tm=tm, y_tn=y_tn),
        grid_spec=pltpu.PrefetchScalarGridSpec(
            num_scalar_prefetch=1,
            grid=(nb,),
            in_specs=[
                pl.BlockSpec((tm, D_MODEL), lambda i, pos: (i + off, 0)),
                pl.BlockSpec((tm, 128), lambda i, pos: (i + off, 0)),
                pl.BlockSpec((1, D_MODEL), lambda i, pos: (0, 0)),
                pl.BlockSpec(memory_space=pl.ANY),
            ],
            out_specs=pl.BlockSpec((tm, D_MODEL), lambda i, pos: (i, 0)),
            scratch_shapes=[pltpu.VMEM((2, TOP_K, tm, D_MODEL // 2), u32), pltpu.SemaphoreType.DMA((2,))],
        ),
        out_shape=jax.ShapeDtypeStruct((SEQ, D_MODEL), f32),
        compiler_params=_params(("arbitrary",)),
        name="combine_final_norm",
    )(pos_kt, h1, gates, norm_w.reshape(1, D_MODEL), y)


def _rank_kernel(idx_ref, rank_ref, cnt_ref, carry_ref, *, tm):
    i = pl.program_id(0)

    @pl.when(i == 0)
    def _():
        carry_ref[...] = jnp.zeros_like(carry_ref)

    idx = idx_ref[...]
    rows = i * tm + lax.broadcasted_iota(i32, (tm, 1), 0)
    valid = rows >= PAD_ROWS
    e_iota = lax.broadcasted_iota(i32, (tm, N_EXPERTS), 1)
    hits = [jnp.logical_and(idx[:, k:k + 1] == e_iota, valid) for k in range(TOP_K)]
    onehot = sum(jnp.where(h, 1.0, 0.0) for h in hits)
    earlier = lax.broadcasted_iota(i32, (tm, tm), 0) > lax.broadcasted_iota(i32, (tm, tm), 1)
    before = jnp.dot(jnp.where(earlier, 1.0, 0.0).astype(bf16), onehot.astype(bf16),
                     preferred_element_type=f32) + carry_ref[...]
    lane = lax.broadcasted_iota(i32, (tm, 128), 1)
    out = jnp.zeros((tm, 128), i32)
    for k in range(TOP_K):
        rk = jnp.sum(jnp.where(hits[k], before, 0.0), axis=1, keepdims=True)
        out = jnp.where(lane == k, rk.astype(i32), out)
    rank_ref[...] = out
    carry_ref[...] += jnp.sum(onehot, axis=0, keepdims=True)
    cnt_ref[...] = carry_ref[...]


def _rank_pairs(top_idx, tm=320):
    return pl.pallas_call(
        functools.partial(_rank_kernel, tm=tm),
        grid=(R // tm,),
        in_specs=[pl.BlockSpec((tm, 128), lambda i: (i, 0))],
        out_specs=[pl.BlockSpec((tm, 128), lambda i: (i, 0)), pl.BlockSpec((1, N_EXPERTS), lambda i: (0, 0))],
        out_shape=[jax.ShapeDtypeStruct((R, 128), i32), jax.ShapeDtypeStruct((1, N_EXPERTS), f32)],
        scratch_shapes=[pltpu.VMEM((1, N_EXPERTS), f32)],
        compiler_params=_params(("arbitrary",)),
        name="rank_pairs",
    )(top_idx)


def _routing_tables(top_idx, rank, counts):
    counts = counts.reshape(N_EXPERTS).astype(i32)
    tiles = (counts + MOE_TM - 1) // MOE_TM
    tile_end = jnp.cumsum(tiles)
    tile_start = tile_end - tiles
    e_ids = jnp.arange(N_EXPERTS, dtype=i32)
    tok_e = top_idx[PAD_ROWS:, :TOP_K]
    tok_start = jnp.sum(jnp.where(tok_e[..., None] == e_ids, tile_start * MOE_TM, 0), axis=-1)
    pos = tok_start + rank[PAD_ROWS:, :TOP_K]
    pair_row = jnp.arange(N_PAIRS, dtype=i32) // TOP_K + PAD_ROWS
    src_rows = jnp.zeros((XS_ROWS,), i32).at[pos.reshape(-1)].set(pair_row)
    n_used = tile_end[-1:].astype(i32)
    groups = (tiles + MOE_CAP - 1) // MOE_CAP
    group_end = jnp.cumsum(groups)
    g_ids = jnp.arange(MAX_GROUPS, dtype=i32)
    g_expert = jnp.minimum(jnp.sum(group_end[None, :] <= g_ids[:, None], axis=1), N_EXPERTS - 1).astype(i32)
    g_local = g_ids - (group_end - groups)[g_expert]
    g_tile0 = (tile_start[g_expert] + g_local * MOE_CAP).astype(i32)
    g_ntiles = jnp.clip(tiles[g_expert] - g_local * MOE_CAP, 0, MOE_CAP).astype(i32)
    n_groups = group_end[-1:].astype(i32)
    return src_rows, pos, n_used, (g_expert, g_tile0, g_ntiles, n_groups)


def kernel(x, meta_tokens, norm_mix_w, w_in, conv_w, conv_b, dt_bias, a_log, d_skip, ssd_norm_w, w_ssd_out,
           pool_w, pool_scale, w_out, norm_ffn_w, router_w, router_b, w_up, b_up, w_down, b_down, norm_final_w):
    assert x.shape == (1, SEQ, D_MODEL) and norm_mix_w.shape[0] == 1
    h0 = jnp.concatenate([jnp.zeros((PAD_ROWS, D_MODEL), f32), meta_tokens.astype(f32), x[0]], axis=0)

    u = _rms_norm_rows(h0, norm_mix_w[0])
    proj = _in_proj_main(u, w_in[0])
    dt = _in_proj_dt(u, w_in[0], dt_bias[0])
    gates = _in_proj_gates(u, w_in[0])
    ma = _pool_branch(proj, gates, pool_w[0], pool_scale[0])
    vw, ssq = _ssd_branch(proj, dt, conv_w[0], conv_b[0], a_log[0], d_skip[0], ssd_norm_w[0])
    merged = _ssd_out_merge(vw, w_ssd_out[0], ssq, ma, gates)
    h1 = _out_proj(merged, w_out[0], h0)

    u2, top_idx, top_gate = _router(h1, norm_ffn_w[0], router_w[0], router_b[0])
    rank, counts = _rank_pairs(top_idx)
    src_rows, pos, n_used, groups = _routing_tables(top_idx, rank, counts)
    xs = _gather_x(src_rows, n_used, u2)
    act = _moe_up(groups, n_used, xs, w_up[0], b_up[0])
    y = _moe_down(groups, n_used, act, w_down[0], b_down[0])
    pos_kt = pos[N_META:].T.reshape(-1)
    out = _combine(pos_kt, h1, y, top_gate, norm_final_w)
    return out[None]
```

```python
import functools

import jax
import jax.numpy as jnp
from jax import lax
from jax.experimental import pallas as pl
from jax.experimental.pallas import tpu as pltpu

f32 = jnp.float32
bf16 = jnp.bfloat16
i32 = jnp.int32
u32 = jnp.uint32

D_MODEL = 4096
SEQ = 8192
N_META = 16
CHUNK = 128
PAD_ROWS = CHUNK - N_META
R = PAD_ROWS + N_META + SEQ
N_CHUNKS = R // CHUNK
POOL_WINDOWS = (2, 4, 8, 16)
POOL_GROUP = D_MODEL // 4
D_INNER = 2 * D_MODEL
HEAD_DIM = 64
N_HEADS = D_INNER // HEAD_DIM
N_STATE = 128
N_GROUPS = 8
HEADS_PER_GROUP = N_HEADS // N_GROUPS
GROUP_W = HEADS_PER_GROUP * HEAD_DIM
CONV_W = 4
HALO = 16
POOL_HALO = 2 * max(POOL_WINDOWS)
N_EXPERTS = 32
TOP_K = 4
D_EXPERT = 7 * D_MODEL // 16
LIMIT = 7.0
ALPHA = 1.702
EPS = 1e-5
LOG2E = 1.4426950408889634
N_TOK = N_META + SEQ
N_PAIRS = N_TOK * TOP_K
MOE_TM = 256
N_TILES = -(-N_PAIRS // MOE_TM) + N_EXPERTS
N_SLOTS = N_TILES * MOE_TM
MOE_CAP = 6
GROUP_ROWS = MOE_CAP * MOE_TM
MAX_GROUPS = N_EXPERTS + N_TILES // MOE_CAP
XS_ROWS = N_SLOTS + GROUP_ROWS
COL_Z = D_MODEL
COL_X = COL_Z + D_INNER
COL_B = COL_X + D_INNER
COL_C = COL_B + N_GROUPS * N_STATE
COL_DT = COL_C + N_GROUPS * N_STATE
COL_G = COL_DT + N_HEADS
SEG1_COLS = COL_DT
V7X_VMEM_LIMIT = 56 * 1024 * 1024
DMA_ISSUE_UNROLL = 8


def _params(sem, vmem=V7X_VMEM_LIMIT):
    return pltpu.CompilerParams(dimension_semantics=sem, vmem_limit_bytes=vmem)


def _sigmoid(v):
    return 1.0 / (1.0 + jnp.exp(-v))


def _x_rows_spec(tm, tn, col_of):
    return pl.BlockSpec((pl.Element(tm), pl.Element(tn)),
                        lambda i, *rest: (pl.multiple_of(jnp.maximum(i * tm - CHUNK, 0), CHUNK), col_of(i, *rest)))


def _padded_rows(i, head, xb):
    first = jnp.concatenate([head, xb[:xb.shape[0] - CHUNK]], axis=0)
    return jnp.where(i == 0, first, xb)


def _rms_kernel(x_ref, head_ref, w_ref, o_ref):
    h = _padded_rows(pl.program_id(0), head_ref[...], x_ref[...])
    ms = jnp.mean(h * h, axis=-1, keepdims=True)
    o_ref[...] = (h * lax.rsqrt(ms + EPS) * w_ref[...]).astype(o_ref.dtype)


def _rms_norm_rows(x2, head, w, tm=640):
    d = x2.shape[1]
    return pl.pallas_call(
        _rms_kernel,
        grid=(R // tm,),
        in_specs=[_x_rows_spec(tm, d, lambda i: 0), pl.BlockSpec((CHUNK, d), lambda i: (0, 0)),
                  pl.BlockSpec((1, d), lambda i: (0, 0))],
        out_specs=pl.BlockSpec((tm, d), lambda i: (i, 0)),
        out_shape=jax.ShapeDtypeStruct((R, d), bf16),
        compiler_params=_params(("parallel",)),
        name="rms_mix",
    )(x2, head, w.reshape(1, d))


def _mm_kernel(*refs, n_extra, epilogue, nk):
    a_ref, w_ref = refs[0], refs[1]
    extra = refs[2:2 + n_extra]
    o_ref = refs[2 + n_extra]
    acc_ref = refs[3 + n_extra]
    k = pl.program_id(2)

    @pl.when(k == 0)
    def _():
        acc_ref[...] = jnp.dot(a_ref[...], w_ref[...].astype(bf16), preferred_element_type=f32)

    @pl.when(k > 0)
    def _():
        acc_ref[...] += jnp.dot(a_ref[...], w_ref[...].astype(bf16), preferred_element_type=f32)

    @pl.when(k == nk - 1)
    def _():
        o_ref[...] = epilogue(acc_ref[...], *[e[...] for e in extra]).astype(o_ref.dtype)


def _matmul(a, w, *, col0, n, tm, tn, tk, out_dtype, epilogue, extras=(), extra_specs=(), name):
    m, kdim = a.shape
    nk = kdim // tk
    if col0 % tn == 0:
        w_spec = pl.BlockSpec((tk, tn), lambda i, j, k: (k, j + col0 // tn))
    else:
        w_spec = pl.BlockSpec((pl.Element(tk), pl.Element(tn)), lambda i, j, k: (k * tk, pl.multiple_of(col0 + j * tn, 128)))
    return pl.pallas_call(
        functools.partial(_mm_kernel, n_extra=len(extras), epilogue=epilogue, nk=nk),
        grid=(m // tm, n // tn, nk),
        in_specs=[pl.BlockSpec((tm, tk), lambda i, j, k: (i, k)), w_spec, *extra_specs],
        out_specs=pl.BlockSpec((tm, tn), lambda i, j, k: (i, j)),
        out_shape=jax.ShapeDtypeStruct((m, n), out_dtype),
        scratch_shapes=[pltpu.VMEM((tm, tn), f32)],
        compiler_params=_params(("parallel", "parallel", "arbitrary")),
        name=name,
    )(a, w, *extras)


def _softplus_bias(acc, bias):
    v = acc + bias
    return jnp.maximum(v, 0.0) + jnp.log1p(jnp.exp(-jnp.abs(v)))


def _in_proj_main(u, w_in):
    return _matmul(u, w_in, col0=0, n=SEG1_COLS, tm=1664, tn=1024, tk=2048, out_dtype=bf16,
                   epilogue=lambda acc: acc, name="in_proj_main")


def _in_proj_dt(u, w_in, dt_bias):
    return _matmul(u, w_in, col0=COL_DT, n=N_HEADS, tm=1664, tn=N_HEADS, tk=1024, out_dtype=f32,
                   epilogue=_softplus_bias, extras=(dt_bias.reshape(1, N_HEADS),),
                   extra_specs=(pl.BlockSpec((1, N_HEADS), lambda i, j, k: (0, 0)),), name="in_proj_dt")


def _in_proj_gates(u, w_in):
    return _matmul(u, w_in, col0=COL_G, n=2 * D_MODEL, tm=1664, tn=1024, tk=2048, out_dtype=bf16,
                   epilogue=_sigmoid, name="in_proj_gates")


def _ssd_out_merge(vw, w_ssd_out, ssq, ma, gates, tm=1664, tn=1024):
    gb_off = D_MODEL // tn

    def merge(acc, ssq_t, ma_t, gb_t):
        rs = lax.rsqrt(ssq_t[:, :1] * (1.0 / D_INNER) + EPS)
        return ma_t.astype(f32) + gb_t.astype(f32) * (acc * rs)

    return _matmul(vw, w_ssd_out, col0=0, n=D_MODEL, tm=tm, tn=tn, tk=1024, out_dtype=bf16, epilogue=merge,
                   extras=(ssq, ma, gates),
                   extra_specs=(pl.BlockSpec((tm, 128), lambda i, j, k: (i, 0)),
                                pl.BlockSpec((tm, tn), lambda i, j, k: (i, j)),
                                pl.BlockSpec((tm, tn), lambda i, j, k: (i, j + gb_off))),
                   name="ssd_out_merge")


def _out_proj(merged, w_out, head, x2, tm=1664, tn=1024):
    def add_residual(acc, head_t, x_t):
        return _padded_rows(pl.program_id(0), head_t, x_t) + acc

    return _matmul(merged, w_out, col0=0, n=D_MODEL, tm=tm, tn=tn, tk=1024, out_dtype=f32,
                   epilogue=add_residual, extras=(head, x2),
                   extra_specs=(pl.BlockSpec((CHUNK, tn), lambda i, j, k: (0, j)),
                                _x_rows_spec(tm, tn, lambda i, j, k: pl.multiple_of(j * tn, tn))),
                   name="out_proj_residual")


def _pool_kernel(x_ref, halo_ref, w_ref, scale_ref, ga_ref, o_ref, a_ref, b_ref, pooled_ref, *, tm):
    assert POOL_WINDOWS == (2, 4, 8, 16)
    gi = pl.program_id(0)
    i = pl.program_id(1)
    n = POOL_HALO + tm
    halo = halo_ref[...].astype(f32)
    a_ref[0:POOL_HALO, :] = jnp.where(i == 0, 0.0, halo)
    a_ref[POOL_HALO:n, :] = x_ref[...].astype(f32)
    t = i * tm + lax.broadcasted_iota(i32, (tm, 1), 0) - PAD_ROWS

    def pair_sum(src_ref, shift, start):
        return src_ref[start:n, :] + src_ref[start - shift:n - shift, :]

    def finish(s, win):
        cnt = jnp.clip(t + 1, 1, win).astype(f32)
        pooled_ref[...] = (s / cnt - x_ref[...].astype(f32)).astype(bf16)

    @pl.when(gi == 0)
    def _():
        finish(pair_sum(a_ref, 1, POOL_HALO), 2)

    @pl.when(gi == 1)
    def _():
        b_ref[8:n, :] = pair_sum(a_ref, 1, 8)
        finish(pair_sum(b_ref, 2, POOL_HALO), 4)

    @pl.when(gi == 2)
    def _():
        b_ref[8:n, :] = pair_sum(a_ref, 1, 8)
        a_ref[16:n, :] = pair_sum(b_ref, 2, 16)
        finish(pair_sum(a_ref, 4, POOL_HALO), 8)

    @pl.when(gi == 3)
    def _():
        b_ref[8:n, :] = pair_sum(a_ref, 1, 8)
        a_ref[16:n, :] = pair_sum(b_ref, 2, 16)
        b_ref[24:n, :] = pair_sum(a_ref, 4, 24)
        finish(pair_sum(b_ref, 8, POOL_HALO), 16)

    acc = jnp.dot(pooled_ref[...], w_ref[0].astype(bf16), preferred_element_type=f32)
    o_ref[...] = (ga_ref[...].astype(f32) * (acc * scale_ref[...])).astype(o_ref.dtype)


def _pool_branch(proj, gates, pool_w, pool_scale, tm=640):
    gw = POOL_GROUP
    return pl.pallas_call(
        functools.partial(_pool_kernel, tm=tm),
        grid=(4, R // tm),
        in_specs=[
            pl.BlockSpec((tm, gw), lambda g, i: (i, g)),
            pl.BlockSpec((POOL_HALO, gw), lambda g, i: (jnp.maximum(i * (tm // POOL_HALO) - 1, 0), g)),
            pl.BlockSpec((1, gw, gw), lambda g, i: (g, 0, 0)),
            pl.BlockSpec((1, gw), lambda g, i: (0, g)),
            pl.BlockSpec((tm, gw), lambda g, i: (i, g)),
        ],
        out_specs=pl.BlockSpec((tm, gw), lambda g, i: (i, g)),
        out_shape=jax.ShapeDtypeStruct((R, D_MODEL), bf16),
        scratch_shapes=[pltpu.VMEM((POOL_HALO + tm, gw), f32), pltpu.VMEM((POOL_HALO + tm, gw), f32),
                        pltpu.VMEM((tm, gw), bf16)],
        compiler_params=_params(("parallel", "arbitrary")),
        name="pool_branch",
    )(proj, proj, pool_w, pool_scale.reshape(1, D_MODEL), gates)


def _fill_ext(cur_ref, halo_ref, ext_ref, c):
    halo = halo_ref[...].astype(f32)
    ext_ref[0:HALO, :] = jnp.where(c == 0, 0.0, halo)
    ext_ref[HALO:HALO + CHUNK, :] = cur_ref[...].astype(f32)


def _conv_silu(ext_ref, w_ref, b_ref, lanes):
    out = b_ref[:, lanes]
    for k in range(CONV_W):
        off = HALO - (CONV_W - 1) + k
        out = out + w_ref[k:k + 1, lanes] * ext_ref[off:off + CHUNK, lanes]
    return out * _sigmoid(out)


def _ssd_kernel(x_ref, xh_ref, bc_ref, bch_ref, z_ref, dt_ref, dtt_ref, alog_ref, alogt_ref,
                wx_ref, wbc_ref, bx_ref, bbc_ref, dskip_ref, nw_ref,
                vw_ref, ssq_ref,
                state_ref, extx_ref, extbc_ref):
    c = pl.program_id(0)

    @pl.when(c == 0)
    def _():
        state_ref[...] = jnp.zeros_like(state_ref)

    _fill_ext(x_ref, xh_ref, extx_ref, c)
    _fill_ext(bc_ref, bch_ref, extbc_ref, c)

    rows = c * CHUNK + lax.broadcasted_iota(i32, (CHUNK, 1), 0)
    row_ok = rows >= PAD_ROWS
    cols = c * CHUNK + lax.broadcasted_iota(i32, (1, CHUNK), 1)
    col_ok = cols >= PAD_ROWS
    li = lax.broadcasted_iota(i32, (CHUNK, CHUNK), 0)
    si = lax.broadcasted_iota(i32, (CHUNK, CHUNK), 1)
    causal = li >= si
    tril = causal.astype(f32)
    triu = (li <= si).astype(f32)
    lane = lax.broadcasted_iota(i32, (CHUNK, 2 * HEAD_DIM), 1)
    first = lane < HEAD_DIM
    first1 = lax.broadcasted_iota(i32, (1, 2 * HEAD_DIM), 1) < HEAD_DIM

    def group(g, ssq):
        lo = pl.multiple_of(g * GROUP_W, GROUP_W)
        blo = pl.multiple_of(g * N_STATE, N_STATE)
        xs = _conv_silu(extx_ref, wx_ref, bx_ref, pl.ds(lo, GROUP_W))
        bm = _conv_silu(extbc_ref, wbc_ref, bbc_ref, pl.ds(blo, N_STATE))
        cm = _conv_silu(extbc_ref, wbc_ref, bbc_ref, pl.ds(N_GROUPS * N_STATE + blo, N_STATE))

        dt = jnp.where(row_ok, dt_ref[g], 0.0)
        dtt = jnp.where(col_ok, dtt_ref[g], 0.0)
        a_dt = dt * (-jnp.exp(alog_ref[g]))
        a_dtt = dtt * (-jnp.exp(alogt_ref[g]))
        a_cs = jnp.dot(tril, a_dt, preferred_element_type=f32, precision=lax.Precision.HIGHEST)
        a_cst = jnp.dot(a_dtt, triu, preferred_element_type=f32, precision=lax.Precision.HIGHEST)
        last = a_cs[CHUNK - 1:CHUNK, :]
        e_cs = jnp.exp(a_cs)
        e_end = jnp.exp(last - a_cs)
        e_last = jnp.exp(last)
        a2 = a_cs * LOG2E
        a2t = a_cst * LOG2E

        bmb = bm.astype(bf16)
        cmb = cm.astype(bf16)
        cb = lax.dot_general(cmb, bmb, (((1,), (1,)), ((), ())), preferred_element_type=f32)
        state = state_ref[g]
        yoff = jnp.dot(cmb, state.astype(bf16), preferred_element_type=f32)

        xd2, cd = [], []
        for q in range(HEADS_PER_GROUP // 2):
            ha, hb = 2 * q, 2 * q + 1
            sl = slice(q * 2 * HEAD_DIM, (q + 1) * 2 * HEAD_DIM)
            gl = pl.ds(pl.multiple_of(lo + q * 2 * HEAD_DIM, 2 * HEAD_DIM), 2 * HEAD_DIM)
            xq = xs[:, sl]
            dtp = jnp.where(first, dt[:, ha:ha + 1], dt[:, hb:hb + 1])
            xdt = xq * dtp
            xdtb = xdt.astype(bf16)
            wa = cb * jnp.exp2(jnp.where(causal, a2[:, ha:ha + 1] - a2t[ha:ha + 1, :], -1e30))
            wb = cb * jnp.exp2(jnp.where(causal, a2[:, hb:hb + 1] - a2t[hb:hb + 1, :], -1e30))
            ya = jnp.dot(wa.astype(bf16), xdtb, preferred_element_type=f32)
            yb = jnp.dot(wb.astype(bf16), xdtb, preferred_element_type=f32)
            ecs = jnp.where(first, e_cs[:, ha:ha + 1], e_cs[:, hb:hb + 1])
            y = jnp.where(first, ya, yb) + yoff[:, sl] * ecs + dskip_ref[:, gl] * xq
            eend = jnp.where(first, e_end[:, ha:ha + 1], e_end[:, hb:hb + 1])
            xd2.append((xdt * eend).astype(bf16))
            cd.append(jnp.where(first1, e_last[:, ha:ha + 1], e_last[:, hb:hb + 1]))
            zq = z_ref[:, gl].astype(f32)
            v = y * (zq * _sigmoid(zq))
            ssq = ssq + v * v
            vw_ref[:, gl] = (v * nw_ref[:, gl]).astype(vw_ref.dtype)

        bt = jnp.transpose(bm).astype(bf16)
        s_new = jnp.dot(bt, jnp.concatenate(xd2, axis=1), preferred_element_type=f32)
        state_ref[g] = state * jnp.concatenate(cd, axis=1) + s_new
        return ssq

    ssq = lax.fori_loop(0, N_GROUPS, group, jnp.zeros((CHUNK, 2 * HEAD_DIM), f32))
    ssq_ref[...] = jnp.broadcast_to(jnp.sum(ssq, axis=1, keepdims=True), (CHUNK, 128))


def _ssd_branch(proj, dt, conv_w, conv_b, a_log, d_skip, ssd_norm_w):
    bc_w = 2 * N_GROUPS * N_STATE
    dt_g = dt.reshape(R, N_GROUPS, HEADS_PER_GROUP).transpose(1, 0, 2)
    dt_gt = dt_g.transpose(0, 2, 1)
    alog = a_log.reshape(N_GROUPS, 1, HEADS_PER_GROUP)
    alogt = a_log.reshape(N_GROUPS, HEADS_PER_GROUP, 1)
    conv_b2 = conv_b.reshape(1, -1)
    dskip_e = jnp.repeat(d_skip, HEAD_DIM).reshape(1, D_INNER)

    def cur(col, width):
        return pl.BlockSpec((pl.Element(CHUNK), pl.Element(width)),
                            lambda c: (pl.multiple_of(c * CHUNK, CHUNK), col))

    def halo(col, width):
        return pl.BlockSpec((pl.Element(HALO), pl.Element(width)),
                            lambda c: (pl.multiple_of(jnp.maximum(c * CHUNK - HALO, 0), HALO), col))

    full = lambda shape: pl.BlockSpec(shape, lambda c: (0,) * len(shape))
    in_specs = [
        cur(COL_X, D_INNER), halo(COL_X, D_INNER),
        cur(COL_B, bc_w), halo(COL_B, bc_w),
        cur(COL_Z, D_INNER),
        pl.BlockSpec((N_GROUPS, CHUNK, HEADS_PER_GROUP), lambda c: (0, c, 0)),
        pl.BlockSpec((N_GROUPS, HEADS_PER_GROUP, CHUNK), lambda c: (0, 0, c)),
        full((N_GROUPS, 1, HEADS_PER_GROUP)),
        full((N_GROUPS, HEADS_PER_GROUP, 1)),
        pl.BlockSpec((CONV_W, D_INNER), lambda c: (0, 0)),
        pl.BlockSpec((CONV_W, bc_w), lambda c: (0, D_INNER // bc_w)),
        pl.BlockSpec((1, D_INNER), lambda c: (0, 0)),
        pl.BlockSpec((1, bc_w), lambda c: (0, D_INNER // bc_w)),
        full((1, D_INNER)),
        full((1, D_INNER)),
    ]
    return pl.pallas_call(
        _ssd_kernel,
        grid=(N_CHUNKS,),
        in_specs=in_specs,
        out_specs=[
            pl.BlockSpec((CHUNK, D_INNER), lambda c: (c, 0)),
            pl.BlockSpec((CHUNK, 128), lambda c: (c, 0)),
        ],
        out_shape=[
            jax.ShapeDtypeStruct((R, D_INNER), bf16),
            jax.ShapeDtypeStruct((R, 128), f32),
        ],
        scratch_shapes=[
            pltpu.VMEM((N_GROUPS, N_STATE, GROUP_W), f32),
            pltpu.VMEM((HALO + CHUNK, D_INNER), f32),
            pltpu.VMEM((HALO + CHUNK, bc_w), f32),
        ],
        compiler_params=_params(("arbitrary",)),
        name="ssd_mixer",
    )(proj, proj, proj, proj, proj, dt_g, dt_gt, alog, alogt,
      conv_w, conv_w, conv_b2, conv_b2, dskip_e, ssd_norm_w.reshape(1, D_INNER))


def _router_kernel(h_ref, nw_ref, rw_ref, rb_ref, u_ref, idx_ref, gate_ref, *, tm):
    i = pl.program_id(0)
    h = h_ref[...]
    ms = jnp.mean(h * h, axis=-1, keepdims=True)
    u = h * lax.rsqrt(ms + EPS) * nw_ref[...]
    rows = i * tm + lax.broadcasted_iota(i32, (tm, 1), 0)
    u = jnp.where(rows >= PAD_ROWS, u, 0.0)
    u_ref[...] = u
    logits =jnp.dot(u, rw_ref[...], preferred_element_type=f32, precision=lax.Precision.HIGHEST) + rb_ref[...]
    e_iota = lax.broadcasted_iota(i32, (tm, N_EXPERTS), 1)
    lane = lax.broadcasted_iota(i32, (tm, 128), 1)
    idx_out = jnp.zeros((tm, 128), i32)
    val_out = jnp.zeros((tm, 128), f32)
    vals = []
    for k in range(TOP_K):
        m = jnp.max(logits, axis=1, keepdims=True)
        sel = jnp.min(jnp.where(logits == m, e_iota, N_EXPERTS), axis=1, keepdims=True)
        vals.append(m)
        idx_out = jnp.where(lane == k, sel, idx_out)
        logits = jnp.where(e_iota == sel, -jnp.inf, logits)
    exps = [jnp.exp(v - vals[0]) for v in vals]
    denom = exps[0] + exps[1] + exps[2] + exps[3]
    for k in range(TOP_K):
        val_out = jnp.where(lane == k, exps[k] / denom, val_out)
    idx_ref[...] = idx_out
    gate_ref[...] = val_out


def _router(h1, norm_w, router_w, router_b, tm=320):
    return pl.pallas_call(
        functools.partial(_router_kernel, tm=tm),
        grid=(R // tm,),
        in_specs=[
            pl.BlockSpec((tm, D_MODEL), lambda i: (i, 0)),
            pl.BlockSpec((1, D_MODEL), lambda i: (0, 0)),
            pl.BlockSpec((D_MODEL, N_EXPERTS), lambda i: (0, 0)),
            pl.BlockSpec((1, N_EXPERTS), lambda i: (0, 0)),
        ],
        out_specs=[
            pl.BlockSpec((tm, D_MODEL), lambda i: (i, 0)),
            pl.BlockSpec((tm, 128), lambda i: (i, 0)),
            pl.BlockSpec((tm, 128), lambda i: (i, 0)),
        ],
        out_shape=[
            jax.ShapeDtypeStruct((R, D_MODEL), f32),
            jax.ShapeDtypeStruct((R, 128), i32),
            jax.ShapeDtypeStruct((R, 128), f32),
        ],
        compiler_params=_params(("parallel",)),
        name="router",
    )(h1, norm_w.reshape(1, D_MODEL), router_w, router_b.reshape(1, N_EXPERTS))


def _gather_x_kernel(idx_ref, nu_ref, src_ref, o_ref, buf_ref, sem, *, tm):
    i = pl.program_id(0)
    n_used = nu_ref[0]

    def row_copy(src_row, slot, r):
        return pltpu.make_async_copy(src_ref.at[pl.ds(src_row, 1)], buf_ref.at[slot, pl.ds(r, 1)], sem.at[slot])

    def issue_tile(step):
        slot = step % 2

        def issue(r, carry):
            row_copy(idx_ref[step * tm + r], slot, r).start()
            return carry

        lax.fori_loop(0, tm, issue, 0, unroll=DMA_ISSUE_UNROLL)

    @pl.when(i == 0)
    def _():
        issue_tile(0)

    @pl.when(i + 1 < n_used)
    def _():
        issue_tile(i + 1)

    @pl.when(i < n_used)
    def _():
        slot = i % 2
        pltpu.make_async_copy(src_ref.at[pl.ds(0, tm)], buf_ref.at[slot], sem.at[slot]).wait()
        o_ref[...] = buf_ref[slot].astype(o_ref.dtype)

    @pl.when(i >= n_used)
    def _():
        o_ref[...] = jnp.zeros_like(o_ref)


def _gather_x(src_rows, n_used, u2, tm=MOE_TM):
    return pl.pallas_call(
        functools.partial(_gather_x_kernel, tm=tm),
        grid_spec=pltpu.PrefetchScalarGridSpec(
            num_scalar_prefetch=2,
            grid=(XS_ROWS // tm,),
            in_specs=[pl.BlockSpec(memory_space=pl.ANY)],
            out_specs=pl.BlockSpec((tm, D_MODEL), lambda i, idx, nu: (i, 0)),
            scratch_shapes=[pltpu.VMEM((2, tm, D_MODEL), f32), pltpu.SemaphoreType.DMA((2,))],
        ),
        out_shape=jax.ShapeDtypeStruct((XS_ROWS, D_MODEL), bf16),
        compiler_params=_params(("arbitrary",)),
        name="gather_x",
    )(src_rows, n_used, u2)


def _grouped_kernel(ge_ref, gt_ref, gn_ref, ng_ref, nu_ref, x_ref, *refs, n_w, tn, nj, ocols, tile_fn):
    w_refs, b_refs, o_ref = refs[:n_w], refs[n_w:2 * n_w], refs[2 * n_w]
    wb_ref, obuf_ref, zbuf_ref, sem, zsem = refs[2 * n_w + 1:]
    g = pl.program_id(0)
    j = pl.program_id(1)
    active = g < ng_ref[0]
    gc = jnp.minimum(g, ng_ref[0] - 1)
    n_tiles = jnp.where(active, gn_ref[gc], 0)
    tile0 = gt_ref[gc]
    col0 = pl.multiple_of(j * ocols, ocols)
    total_tiles = o_ref.shape[0] // MOE_TM

    @pl.when(jnp.logical_and(g == 0, j == 0))
    def _():
        zbuf_ref[...] = jnp.zeros_like(zbuf_ref)

        def zero_copy(t, jc):
            rows = pl.ds(pl.multiple_of(t * MOE_TM, MOE_TM), MOE_TM)
            return pltpu.make_async_copy(zbuf_ref, o_ref.at[rows, pl.ds(jc * ocols, ocols)], zsem)

        def z_start(t, carry):
            for jc in range(nj):
                zero_copy(t, jc).start()
            return carry

        def z_wait(t, carry):
            for jc in range(nj):
                zero_copy(t, jc).wait()
            return carry

        lax.fori_loop(nu_ref[0], total_tiles, z_start, 0)
        lax.fori_loop(nu_ref[0], total_tiles, z_wait, 0)

    @pl.when(active)
    def _():
        for k in range(n_w):
            wb_ref[:, k * tn:(k + 1) * tn] = w_refs[k][0].astype(bf16)

    step = g * nj + j
    slot = step % 2

    def out_copy(slot_, n, tile):
        rows = pl.ds(pl.multiple_of(tile * MOE_TM, MOE_TM), n * MOE_TM)
        return pltpu.make_async_copy(obuf_ref.at[slot_, pl.ds(0, n * MOE_TM)], o_ref.at[rows, pl.ds(col0, ocols)],
                                     sem.at[slot_])

    def wait_step(s):
        gs = s // nj
        sent = jnp.where(jnp.logical_and(s >= 0, gs < ng_ref[0]), gn_ref[jnp.clip(gs, 0, MAX_GROUPS - 1)], 0)
        for n in range(1, MOE_CAP + 1):

            @pl.when(sent == n)
            def _(n=n):
                out_copy(s % 2, n, 0).wait()

    wait_step(step - 2)
    for n in range(1, MOE_CAP + 1):

        @pl.when(n_tiles == n)
        def _(n=n):
            acc = jnp.dot(x_ref[0:n * MOE_TM, :], wb_ref[...], preferred_element_type=f32)
            obuf_ref[slot, pl.ds(0, n * MOE_TM)] = tile_fn(acc, [b[0] for b in b_refs])
            out_copy(slot, n, tile0).start()

    @pl.when(step == MAX_GROUPS * nj - 1)
    def _():
        wait_step(step - 1)
        wait_step(step)


def _grouped_call(groups, n_used, x, ws, bs, *, k_dim, tn, nj, w_col_offsets, ocols, out_shape, tile_fn, name):
    g_expert, g_tile0, g_ntiles, n_groups = groups
    n_w = len(ws)

    def gi(g, ng):
        return jnp.minimum(g, ng[0] - 1)

    def x_map(g, j, ge, gt, gn, ng, nu):
        return (pl.multiple_of(gt[gi(g, ng)] * MOE_TM, MOE_TM), 0)

    def w_map(off):
        return lambda g, j, ge, gt, gn, ng, nu: (ge[gi(g, ng)], 0, off + jnp.where(g < ng[0], j, nj - 1))

    return pl.pallas_call(
        functools.partial(_grouped_kernel, n_w=n_w, tn=tn, nj=nj, ocols=ocols, tile_fn=tile_fn),
        grid_spec=pltpu.PrefetchScalarGridSpec(
            num_scalar_prefetch=5,
            grid=(MAX_GROUPS, nj),
            in_specs=[
                pl.BlockSpec((pl.Element(GROUP_ROWS), pl.Element(k_dim)), x_map),
                *[pl.BlockSpec((1, k_dim, tn), w_map(off)) for off in w_col_offsets],
                *[pl.BlockSpec((1, 1, tn), w_map(off)) for off in w_col_offsets],
            ],
            out_specs=pl.BlockSpec(memory_space=pl.ANY),
            scratch_shapes=[
                pltpu.VMEM((k_dim, n_w * tn), bf16),
                pltpu.VMEM((2, GROUP_ROWS, ocols), out_shape.dtype),
                pltpu.VMEM((MOE_TM, ocols), out_shape.dtype),
                pltpu.SemaphoreType.DMA((2,)),
                pltpu.SemaphoreType.DMA(()),
            ],
        ),
        out_shape=out_shape,
        compiler_params=_params(("arbitrary", "arbitrary")),
        name=name,
    )(g_expert, g_tile0, g_ntiles, n_groups, n_used, x, *ws, *bs)


def _moe_up(groups, n_used, xs, w_up, b_up, tn=256):
    nj = D_EXPERT // tn

    def tile_fn(acc, biases):
        gate = jnp.minimum(acc[:, :tn] + biases[0], LIMIT)
        up = jnp.clip(acc[:, tn:] + biases[1], -LIMIT, LIMIT)
        return ((up + 1.0) * (gate * _sigmoid(ALPHA * gate))).astype(bf16)

    b3 = b_up.reshape(N_EXPERTS, 1, -1)
    return _grouped_call(groups, n_used, xs, (w_up, w_up), (b3, b3), k_dim=D_MODEL, tn=tn, nj=nj,
                         w_col_offsets=(0, nj), ocols=tn,
                         out_shape=jax.ShapeDtypeStruct((XS_ROWS, D_EXPERT), bf16), tile_fn=tile_fn, name="moe_up")


def _moe_down(groups, n_used, act, w_down, b_down, tn=1024):
    return _grouped_call(groups, n_used, act, (w_down,), (b_down.reshape(N_EXPERTS, 1, -1),), k_dim=D_EXPERT,
                         tn=tn, nj=D_MODEL // tn, w_col_offsets=(0,), ocols=tn,
                         out_shape=jax.ShapeDtypeStruct((N_SLOTS, D_MODEL), f32),
                         tile_fn=lambda acc, biases: acc + biases[0], name="moe_down")


def _combine_kernel(pos_ref, h_ref, g_ref, nw_ref, y_ref, o_ref, buf_ref, sem, *, tm):
    i = pl.program_id(0)
    n = pl.num_programs(0)

    def row_copy(src_row, slot, k, r):
        return pltpu.make_async_copy(y_ref.at[pl.ds(src_row, 1)], buf_ref.at[slot, k, pl.ds(r, 1)], sem.at[slot])

    def issue_tile(step):
        slot = step % 2
        for k in range(TOP_K):

            def issue(r, carry, k=k):
                row_copy(pos_ref[k * SEQ + step * tm + r], slot, k, r).start()
                return carry

            lax.fori_loop(0, tm, issue, 0, unroll=DMA_ISSUE_UNROLL)

    @pl.when(i == 0)
    def _():
        issue_tile(0)

    @pl.when(i + 1 < n)
    def _():
        issue_tile(i + 1)

    slot = i % 2
    for k in range(TOP_K):
        pltpu.make_async_copy(y_ref.at[pl.ds(0, tm)], buf_ref.at[slot, k], sem.at[slot]).wait()

    g = g_ref[...]
    h = h_ref[...]
    for k in range(TOP_K):
        h = h + g[:, k:k + 1] * buf_ref[slot, k]
    ms = jnp.mean(h * h, axis=-1, keepdims=True)
    o_ref[...] = h * lax.rsqrt(ms + EPS) * nw_ref[...]


def _combine(pos_kt, h1, y, gates, norm_w, tm=128):
    nb = SEQ // tm
    off = (PAD_ROWS + N_META) // tm
    return pl.pallas_call(
        functools.partial(_combine_kernel, tm=tm),
        grid_spec=pltpu.PrefetchScalarGridSpec(
            num_scalar_prefetch=1,
            grid=(nb,),
            in_specs=[
                pl.BlockSpec((tm, D_MODEL), lambda i, pos: (i + off, 0)),
                pl.BlockSpec((tm, 128), lambda i, pos: (i + off, 0)),
                pl.BlockSpec((1, D_MODEL), lambda i, pos: (0, 0)),
                pl.BlockSpec(memory_space=pl.ANY),
            ],
            out_specs=pl.BlockSpec((tm, D_MODEL), lambda i, pos: (i, 0)),
            scratch_shapes=[pltpu.VMEM((2, TOP_K, tm, D_MODEL), f32), pltpu.SemaphoreType.DMA((2,))],
        ),
        out_shape=jax.ShapeDtypeStruct((SEQ, D_MODEL), f32),
        compiler_params=_params(("arbitrary",)),
        name="combine_final_norm",
    )(pos_kt, h1, gates, norm_w.reshape(1, D_MODEL), y)


def _rank_kernel(idx_ref, rank_ref, cnt_ref, carry_ref, *, tm):
    i = pl.program_id(0)

    @pl.when(i == 0)
    def _():
        carry_ref[...] = jnp.zeros_like(carry_ref)

    idx = idx_ref[...]
    rows = i * tm + lax.broadcasted_iota(i32, (tm, 1), 0)
    valid = rows >= PAD_ROWS
    e_iota = lax.broadcasted_iota(i32, (tm, N_EXPERTS), 1)
    hits = [jnp.logical_and(idx[:, k:k + 1] == e_iota, valid) for k in range(TOP_K)]
    onehot = sum(jnp.where(h, 1.0, 0.0) for h in hits)
    earlier = lax.broadcasted_iota(i32, (tm, tm), 0) > lax.broadcasted_iota(i32, (tm, tm), 1)
    before = jnp.dot(jnp.where(earlier, 1.0, 0.0).astype(bf16), onehot.astype(bf16),
                     preferred_element_type=f32) + carry_ref[...]
    lane = lax.broadcasted_iota(i32, (tm, 128), 1)
    out = jnp.zeros((tm, 128), i32)
    for k in range(TOP_K):
        rk = jnp.sum(jnp.where(hits[k], before, 0.0), axis=1, keepdims=True)
        out = jnp.where(lane == k, rk.astype(i32), out)
    rank_ref[...] = out
    carry_ref[...] += jnp.sum(onehot, axis=0, keepdims=True)
    cnt_ref[...] = carry_ref[...]


def _rank_pairs(top_idx, tm=320):
    return pl.pallas_call(
        functools.partial(_rank_kernel, tm=tm),
        grid=(R // tm,),
        in_specs=[pl.BlockSpec((tm, 128), lambda i: (i, 0))],
        out_specs=[pl.BlockSpec((tm, 128), lambda i: (i, 0)), pl.BlockSpec((1, N_EXPERTS), lambda i: (0, 0))],
        out_shape=[jax.ShapeDtypeStruct((R, 128), i32), jax.ShapeDtypeStruct((1, N_EXPERTS), f32)],
        scratch_shapes=[pltpu.VMEM((1, N_EXPERTS), f32)],
        compiler_params=_params(("arbitrary",)),
        name="rank_pairs",
    )(top_idx)


def _routing_tables(top_idx, rank, counts):
    counts = counts.reshape(N_EXPERTS).astype(i32)
    tiles = (counts + MOE_TM - 1) // MOE_TM
    tile_end = jnp.cumsum(tiles)
    tile_start = tile_end - tiles
    e_ids = jnp.arange(N_EXPERTS, dtype=i32)
    tok_e = top_idx[PAD_ROWS:, :TOP_K]
    tok_start = jnp.sum(jnp.where(tok_e[..., None] == e_ids, tile_start * MOE_TM, 0), axis=-1)
    pos = tok_start + rank[PAD_ROWS:, :TOP_K]
    pair_row = jnp.arange(N_PAIRS, dtype=i32) // TOP_K + PAD_ROWS
    src_rows = jnp.zeros((XS_ROWS,), i32).at[pos.reshape(-1)].set(pair_row)
    n_used = tile_end[-1:].astype(i32)
    groups = (tiles + MOE_CAP - 1) // MOE_CAP
    group_end = jnp.cumsum(groups)
    g_ids = jnp.arange(MAX_GROUPS, dtype=i32)
    g_expert = jnp.minimum(jnp.sum(group_end[None, :] <= g_ids[:, None], axis=1), N_EXPERTS - 1).astype(i32)
    g_local = g_ids - (group_end - groups)[g_expert]
    g_tile0 = (tile_start[g_expert] + g_local * MOE_CAP).astype(i32)
    g_ntiles = jnp.clip(tiles[g_expert] - g_local * MOE_CAP, 0, MOE_CAP).astype(i32)
    n_groups = group_end[-1:].astype(i32)
    return src_rows, pos, n_used, (g_expert, g_tile0, g_ntiles, n_groups)


def kernel(x, meta_tokens, norm_mix_w, w_in, conv_w, conv_b, dt_bias, a_log, d_skip, ssd_norm_w, w_ssd_out,
           pool_w, pool_scale, w_out, norm_ffn_w, router_w, router_b, w_up, b_up, w_down, b_down, norm_final_w):
    assert x.shape == (1, SEQ, D_MODEL) and norm_mix_w.shape[0] == 1
    x2 = x[0]
    head = jnp.concatenate([jnp.zeros((PAD_ROWS, D_MODEL), f32), meta_tokens.astype(f32)], axis=0)

    u = _rms_norm_rows(x2, head, norm_mix_w[0])
    proj = _in_proj_main(u, w_in[0])
    dt = _in_proj_dt(u, w_in[0], dt_bias[0])
    gates = _in_proj_gates(u, w_in[0])
    ma = _pool_branch(proj, gates, pool_w[0], pool_scale[0])
    vw, ssq = _ssd_branch(proj, dt, conv_w[0], conv_b[0], a_log[0], d_skip[0], ssd_norm_w[0])
    merged = _ssd_out_merge(vw, w_ssd_out[0], ssq, ma, gates)
    h1 = _out_proj(merged, w_out[0], head, x2)

    u2, top_idx, top_gate = _router(h1, norm_ffn_w[0], router_w[0], router_b[0])
    rank, counts = _rank_pairs(top_idx)
    src_rows, pos, n_used, groups = _routing_tables(top_idx, rank, counts)
    xs = _gather_x(src_rows, n_used, u2)
    act = _moe_up(groups, n_used, xs, w_up[0], b_up[0])
    y = _moe_down(groups, n_used, act, w_down[0], b_down[0])
    pos_kt = pos[N_META:].T.reshape(-1)
    out = _combine(pos_kt, h1, y, top_gate, norm_final_w)
    return out[None]
```

```python
import functools

import jax
import jax.numpy as jnp
from jax import lax
from jax.experimental import pallas as pl
from jax.experimental.pallas import tpu as pltpu

f32 = jnp.float32
bf16 = jnp.bfloat16
i32 = jnp.int32
u32 = jnp.uint32

D_MODEL = 4096
SEQ = 8192
N_META = 16
CHUNK = 128
PAD_ROWS = CHUNK - N_META
R = PAD_ROWS + N_META + SEQ
N_CHUNKS = R // CHUNK
POOL_WINDOWS = (2, 4, 8, 16)
POOL_GROUP = D_MODEL // 4
D_INNER = 2 * D_MODEL
HEAD_DIM = 64
N_HEADS = D_INNER // HEAD_DIM
N_STATE = 128
N_GROUPS = 8
HEADS_PER_GROUP = N_HEADS // N_GROUPS
GROUP_W = HEADS_PER_GROUP * HEAD_DIM
CONV_W = 4
HALO = 16
POOL_HALO = 2 * max(POOL_WINDOWS)
N_EXPERTS = 32
TOP_K = 4
D_EXPERT = 7 * D_MODEL // 16
LIMIT = 7.0
ALPHA = 1.702
EPS = 1e-5
LOG2E = 1.4426950408889634
N_TOK = N_META + SEQ
N_PAIRS = N_TOK * TOP_K
MOE_TM = 256
N_TILES = -(-N_PAIRS // MOE_TM) + N_EXPERTS
N_SLOTS = N_TILES * MOE_TM
MOE_CAP = 6
GROUP_ROWS = MOE_CAP * MOE_TM
MAX_GROUPS = N_EXPERTS + N_TILES // MOE_CAP
XS_ROWS = N_SLOTS + GROUP_ROWS
COL_Z = D_MODEL
COL_X = COL_Z + D_INNER
COL_B = COL_X + D_INNER
COL_C = COL_B + N_GROUPS * N_STATE
COL_DT = COL_C + N_GROUPS * N_STATE
COL_G = COL_DT + N_HEADS
SEG1_COLS = COL_DT
V7X_VMEM_LIMIT = 56 * 1024 * 1024
DMA_ISSUE_UNROLL = 8


def _params(sem, vmem=V7X_VMEM_LIMIT):
    return pltpu.CompilerParams(dimension_semantics=sem, vmem_limit_bytes=vmem)


def _sigmoid(v):
    return 1.0 / (1.0 + jnp.exp(-v))


def _x_rows_spec(tm, tn, col_of):
    return pl.BlockSpec((pl.Element(tm), pl.Element(tn)),
                        lambda i, *rest: (pl.multiple_of(jnp.maximum(i * tm - CHUNK, 0), CHUNK), col_of(i, *rest)))


def _padded_rows(i, head, xb):
    first = jnp.concatenate([head, xb[:xb.shape[0] - CHUNK]], axis=0)
    return jnp.where(i == 0, first, xb)


def _rms_kernel(x_ref, head_ref, w_ref, o_ref):
    h = _padded_rows(pl.program_id(0), head_ref[...], x_ref[...])
    ms = jnp.mean(h * h, axis=-1, keepdims=True)
    o_ref[...] = (h * lax.rsqrt(ms + EPS) * w_ref[...]).astype(o_ref.dtype)


def _rms_norm_rows(x2, head, w, tm=640):
    d = x2.shape[1]
    return pl.pallas_call(
        _rms_kernel,
        grid=(R // tm,),
        in_specs=[_x_rows_spec(tm, d, lambda i: 0), pl.BlockSpec((CHUNK, d), lambda i: (0, 0)),
                  pl.BlockSpec((1, d), lambda i: (0, 0))],
        out_specs=pl.BlockSpec((tm, d), lambda i: (i, 0)),
        out_shape=jax.ShapeDtypeStruct((R, d), bf16),
        compiler_params=_params(("parallel",)),
        name="rms_mix",
    )(x2, head, w.reshape(1, d))


def _mm_kernel(*refs, n_extra, epilogue, nk):
    a_ref, w_ref = refs[0], refs[1]
    extra = refs[2:2 + n_extra]
    o_ref = refs[2 + n_extra]
    acc_ref = refs[3 + n_extra]
    k = pl.program_id(2)

    @pl.when(k == 0)
    def _():
        acc_ref[...] = jnp.dot(a_ref[...], w_ref[...].astype(bf16), preferred_element_type=f32)

    @pl.when(k > 0)
    def _():
        acc_ref[...] += jnp.dot(a_ref[...], w_ref[...].astype(bf16), preferred_element_type=f32)

    @pl.when(k == nk - 1)
    def _():
        o_ref[...] = epilogue(acc_ref[...], *[e[...] for e in extra]).astype(o_ref.dtype)


def _matmul(a, w, *, col0, n, tm, tn, tk, out_dtype, epilogue, extras=(), extra_specs=(), name):
    m, kdim = a.shape
    nk = kdim // tk
    if col0 % tn == 0:
        w_spec = pl.BlockSpec((tk, tn), lambda i, j, k: (k, j + col0 // tn))
    else:
        w_spec = pl.BlockSpec((pl.Element(tk), pl.Element(tn)), lambda i, j, k: (k * tk, pl.multiple_of(col0 + j * tn, 128)))
    return pl.pallas_call(
        functools.partial(_mm_kernel, n_extra=len(extras), epilogue=epilogue, nk=nk),
        grid=(m // tm, n // tn, nk),
        in_specs=[pl.BlockSpec((tm, tk), lambda i, j, k: (i, k)), w_spec, *extra_specs],
        out_specs=pl.BlockSpec((tm, tn), lambda i, j, k: (i, j)),
        out_shape=jax.ShapeDtypeStruct((m, n), out_dtype),
        scratch_shapes=[pltpu.VMEM((tm, tn), f32)],
        compiler_params=_params(("parallel", "parallel", "arbitrary")),
        name=name,
    )(a, w, *extras)


def _softplus_bias(acc, bias):
    v = acc + bias
    return jnp.maximum(v, 0.0) + jnp.log1p(jnp.exp(-jnp.abs(v)))


def _in_proj_main(u, w_in):
    return _matmul(u, w_in, col0=0, n=SEG1_COLS, tm=1664, tn=1024, tk=2048, out_dtype=bf16,
                   epilogue=lambda acc: acc, name="in_proj_main")


def _in_proj_dt(u, w_in, dt_bias):
    return _matmul(u, w_in, col0=COL_DT, n=N_HEADS, tm=1664, tn=N_HEADS, tk=1024, out_dtype=f32,
                   epilogue=_softplus_bias, extras=(dt_bias.reshape(1, N_HEADS),),
                   extra_specs=(pl.BlockSpec((1, N_HEADS), lambda i, j, k: (0, 0)),), name="in_proj_dt")


def _in_proj_gates(u, w_in):
    return _matmul(u, w_in, col0=COL_G, n=2 * D_MODEL, tm=1664, tn=1024, tk=2048, out_dtype=bf16,
                   epilogue=_sigmoid, name="in_proj_gates")


def _ssd_out_merge(vw, w_ssd_out, ssq, ma, gates, tm=1664, tn=1024):
    gb_off = D_MODEL // tn

    def merge(acc, ssq_t, ma_t, gb_t):
        rs = lax.rsqrt(ssq_t[:, :1] * (1.0 / D_INNER) + EPS)
        return ma_t.astype(f32) + gb_t.astype(f32) * (acc * rs)

    return _matmul(vw, w_ssd_out, col0=0, n=D_MODEL, tm=tm, tn=tn, tk=1024, out_dtype=bf16, epilogue=merge,
                   extras=(ssq, ma, gates),
                   extra_specs=(pl.BlockSpec((tm, 128), lambda i, j, k: (i, 0)),
                                pl.BlockSpec((tm, tn), lambda i, j, k: (i, j)),
                                pl.BlockSpec((tm, tn), lambda i, j, k: (i, j + gb_off))),
                   name="ssd_out_merge")


def _out_proj(merged, w_out, head, x2, tm=1664, tn=1024):
    def add_residual(acc, head_t, x_t):
        return _padded_rows(pl.program_id(0), head_t, x_t) + acc

    return _matmul(merged, w_out, col0=0, n=D_MODEL, tm=tm, tn=tn, tk=1024, out_dtype=f32,
                   epilogue=add_residual, extras=(head, x2),
                   extra_specs=(pl.BlockSpec((CHUNK, tn), lambda i, j, k: (0, j)),
                                _x_rows_spec(tm, tn, lambda i, j, k: pl.multiple_of(j * tn, tn))),
                   name="out_proj_residual")


def _pool_kernel(x_ref, halo_ref, w_ref, scale_ref, ga_ref, o_ref, a_ref, b_ref, pooled_ref, *, tm):
    assert POOL_WINDOWS == (2, 4, 8, 16)
    gi = pl.program_id(0)
    i = pl.program_id(1)
    n = POOL_HALO + tm
    halo = halo_ref[...].astype(f32)
    a_ref[0:POOL_HALO, :] = jnp.where(i == 0, 0.0, halo)
    a_ref[POOL_HALO:n, :] = x_ref[...].astype(f32)
    t = i * tm + lax.broadcasted_iota(i32, (tm, 1), 0) - PAD_ROWS

    def pair_sum(src_ref, shift, start):
        return src_ref[start:n, :] + src_ref[start - shift:n - shift, :]

    def finish(s, win):
        cnt = jnp.clip(t + 1, 1, win).astype(f32)
        pooled_ref[...] = (s / cnt - x_ref[...].astype(f32)).astype(bf16)

    @pl.when(gi == 0)
    def _():
        finish(pair_sum(a_ref, 1, POOL_HALO), 2)

    @pl.when(gi == 1)
    def _():
        b_ref[8:n, :] = pair_sum(a_ref, 1, 8)
        finish(pair_sum(b_ref, 2, POOL_HALO), 4)

    @pl.when(gi == 2)
    def _():
        b_ref[8:n, :] = pair_sum(a_ref, 1, 8)
        a_ref[16:n, :] = pair_sum(b_ref, 2, 16)
        finish(pair_sum(a_ref, 4, POOL_HALO), 8)

    @pl.when(gi == 3)
    def _():
        b_ref[8:n, :] = pair_sum(a_ref, 1, 8)
        a_ref[16:n, :] = pair_sum(b_ref, 2, 16)
        b_ref[24:n, :] = pair_sum(a_ref, 4, 24)
        finish(pair_sum(b_ref, 8, POOL_HALO), 16)

    acc = jnp.dot(pooled_ref[...], w_ref[0].astype(bf16), preferred_element_type=f32)
    o_ref[...] = (ga_ref[...].astype(f32) * (acc * scale_ref[...])).astype(o_ref.dtype)


def _pool_branch(proj, gates, pool_w, pool_scale, tm=640):
    gw = POOL_GROUP
    return pl.pallas_call(
        functools.partial(_pool_kernel, tm=tm),
        grid=(4, R // tm),
        in_specs=[
            pl.BlockSpec((tm, gw), lambda g, i: (i, g)),
            pl.BlockSpec((POOL_HALO, gw), lambda g, i: (jnp.maximum(i * (tm // POOL_HALO) - 1, 0), g)),
            pl.BlockSpec((1, gw, gw), lambda g, i: (g, 0, 0)),
            pl.BlockSpec((1, gw), lambda g, i: (0, g)),
            pl.BlockSpec((tm, gw), lambda g, i: (i, g)),
        ],
        out_specs=pl.BlockSpec((tm, gw), lambda g, i: (i, g)),
        out_shape=jax.ShapeDtypeStruct((R, D_MODEL), bf16),
        scratch_shapes=[pltpu.VMEM((POOL_HALO + tm, gw), f32), pltpu.VMEM((POOL_HALO + tm, gw), f32),
                        pltpu.VMEM((tm, gw), bf16)],
        compiler_params=_params(("parallel", "arbitrary")),
        name="pool_branch",
    )(proj, proj, pool_w, pool_scale.reshape(1, D_MODEL), gates)


def _fill_ext(cur_ref, halo_ref, ext_ref, c):
    halo = halo_ref[...].astype(f32)
    ext_ref[0:HALO, :] = jnp.where(c == 0, 0.0, halo)
    ext_ref[HALO:HALO + CHUNK, :] = cur_ref[...].astype(f32)


def _conv_silu(ext_ref, w_ref, b_ref, lanes):
    out = b_ref[:, lanes]
    for k in range(CONV_W):
        off = HALO - (CONV_W - 1) + k
        out = out + w_ref[k:k + 1, lanes] * ext_ref[off:off + CHUNK, lanes]
    return out * _sigmoid(out)


def _ssd_kernel(x_ref, xh_ref, bc_ref, bch_ref, z_ref, dt_ref, dtt_ref, alog_ref, alogt_ref,
                wx_ref, wbc_ref, bx_ref, bbc_ref, dskip_ref, nw_ref,
                vw_ref, ssq_ref,
                state_ref, extx_ref, extbc_ref):
    c = pl.program_id(0)

    @pl.when(c == 0)
    def _():
        state_ref[...] = jnp.zeros_like(state_ref)

    _fill_ext(x_ref, xh_ref, extx_ref, c)
    _fill_ext(bc_ref, bch_ref, extbc_ref, c)

    rows = c * CHUNK + lax.broadcasted_iota(i32, (CHUNK, 1), 0)
    row_ok = rows >= PAD_ROWS
    cols = c * CHUNK + lax.broadcasted_iota(i32, (1, CHUNK), 1)
    col_ok = cols >= PAD_ROWS
    li = lax.broadcasted_iota(i32, (CHUNK, CHUNK), 0)
    si = lax.broadcasted_iota(i32, (CHUNK, CHUNK), 1)
    causal = li >= si
    tril = causal.astype(f32)
    triu = (li <= si).astype(f32)
    lane = lax.broadcasted_iota(i32, (CHUNK, 2 * HEAD_DIM), 1)
    first = lane < HEAD_DIM
    first1 = lax.broadcasted_iota(i32, (1, 2 * HEAD_DIM), 1) < HEAD_DIM
    hh = lax.broadcasted_iota(i32, (HEADS_PER_GROUP, GROUP_W), 0)
    hl = lax.broadcasted_iota(i32, (HEADS_PER_GROUP, GROUP_W), 1)
    head_lanes = jnp.where(hl // HEAD_DIM == hh, 1.0, 0.0).astype(bf16)

    def group(g, ssq):
        lo = pl.multiple_of(g * GROUP_W, GROUP_W)
        blo = pl.multiple_of(g * N_STATE, N_STATE)
        xs = _conv_silu(extx_ref, wx_ref, bx_ref, pl.ds(lo, GROUP_W))
        bm = _conv_silu(extbc_ref, wbc_ref, bbc_ref, pl.ds(blo, N_STATE))
        cm = _conv_silu(extbc_ref, wbc_ref, bbc_ref, pl.ds(N_GROUPS * N_STATE + blo, N_STATE))

        dt = jnp.where(row_ok, dt_ref[g], 0.0)
        dtt = jnp.where(col_ok, dtt_ref[g], 0.0)
        a_dt = dt * (-jnp.exp(alog_ref[g]))
        a_dtt = dtt * (-jnp.exp(alogt_ref[g]))
        a_cs = jnp.dot(tril, a_dt, preferred_element_type=f32, precision=lax.Precision.HIGHEST)
        a_cst = jnp.dot(a_dtt, triu, preferred_element_type=f32, precision=lax.Precision.HIGHEST)
        last = a_cs[CHUNK - 1:CHUNK, :]
        e_cs = jnp.exp(a_cs)
        e_end = jnp.exp(last - a_cs)
        e_last = jnp.exp(last)
        a2 = a_cs * LOG2E
        a2t = a_cst * LOG2E

        bmb = bm.astype(bf16)
        cmb = cm.astype(bf16)
        cb = lax.dot_general(cmb, bmb, (((1,), (1,)), ((), ())), preferred_element_type=f32)
        state = state_ref[g]
        yoff = jnp.dot(cmb, state.astype(bf16), preferred_element_type=f32)

        scales = jnp.concatenate([dt, e_cs, e_end], axis=0).astype(bf16)
        spread = jnp.dot(scales, head_lanes, preferred_element_type=f32)
        xdt_g = xs * spread[0:CHUNK]
        xd2 = (xdt_g * spread[2 * CHUNK:3 * CHUNK]).astype(bf16)
        cd = []
        for q in range(HEADS_PER_GROUP // 2):
            ha, hb = 2 * q, 2 * q + 1
            sl = slice(q * 2 * HEAD_DIM, (q + 1) * 2 * HEAD_DIM)
            gl = pl.ds(pl.multiple_of(lo + q * 2 * HEAD_DIM, 2 * HEAD_DIM), 2 * HEAD_DIM)
            xq = xs[:, sl]
            xdtb = xdt_g[:, sl].astype(bf16)
            wa = cb * jnp.exp2(jnp.where(causal, a2[:, ha:ha + 1] - a2t[ha:ha + 1, :], -1e30))
            wb = cb * jnp.exp2(jnp.where(causal, a2[:, hb:hb + 1] - a2t[hb:hb + 1, :], -1e30))
            ya = jnp.dot(wa.astype(bf16), xdtb, preferred_element_type=f32)
            yb = jnp.dot(wb.astype(bf16), xdtb, preferred_element_type=f32)
            y = jnp.where(first, ya, yb) + yoff[:, sl] * spread[CHUNK:2 * CHUNK, sl] + dskip_ref[:, gl] * xq
            cd.append(jnp.where(first1, e_last[:, ha:ha + 1], e_last[:, hb:hb + 1]))
            zq = z_ref[:, gl].astype(f32)
            v = y * (zq * _sigmoid(zq))
            ssq = ssq + v * v
            vw_ref[:, gl] = (v * nw_ref[:, gl]).astype(vw_ref.dtype)

        bt = jnp.transpose(bm).astype(bf16)
        s_new = jnp.dot(bt, xd2, preferred_element_type=f32)
        state_ref[g] = state * jnp.concatenate(cd, axis=1) + s_new
        return ssq

    ssq = lax.fori_loop(0, N_GROUPS, group, jnp.zeros((CHUNK, 2 * HEAD_DIM), f32))
    ssq_ref[...] = jnp.broadcast_to(jnp.sum(ssq, axis=1, keepdims=True), (CHUNK, 128))


def _ssd_branch(proj, dt, conv_w, conv_b, a_log, d_skip, ssd_norm_w):
    bc_w = 2 * N_GROUPS * N_STATE
    dt_g = dt.reshape(R, N_GROUPS, HEADS_PER_GROUP).transpose(1, 0, 2)
    dt_gt = dt_g.transpose(0, 2, 1)
    alog = a_log.reshape(N_GROUPS, 1, HEADS_PER_GROUP)
    alogt = a_log.reshape(N_GROUPS, HEADS_PER_GROUP, 1)
    conv_b2 = conv_b.reshape(1, -1)
    dskip_e = jnp.repeat(d_skip, HEAD_DIM).reshape(1, D_INNER)

    def cur(col, width):
        return pl.BlockSpec((pl.Element(CHUNK), pl.Element(width)),
                            lambda c: (pl.multiple_of(c * CHUNK, CHUNK), col))

    def halo(col, width):
        return pl.BlockSpec((pl.Element(HALO), pl.Element(width)),
                            lambda c: (pl.multiple_of(jnp.maximum(c * CHUNK - HALO, 0), HALO), col))

    full = lambda shape: pl.BlockSpec(shape, lambda c: (0,) * len(shape))
    in_specs = [
        cur(COL_X, D_INNER), halo(COL_X, D_INNER),
        cur(COL_B, bc_w), halo(COL_B, bc_w),
        cur(COL_Z, D_INNER),
        pl.BlockSpec((N_GROUPS, CHUNK, HEADS_PER_GROUP), lambda c: (0, c, 0)),
        pl.BlockSpec((N_GROUPS, HEADS_PER_GROUP, CHUNK), lambda c: (0, 0, c)),
        full((N_GROUPS, 1, HEADS_PER_GROUP)),
        full((N_GROUPS, HEADS_PER_GROUP, 1)),
        pl.BlockSpec((CONV_W, D_INNER), lambda c: (0, 0)),
        pl.BlockSpec((CONV_W, bc_w), lambda c: (0, D_INNER // bc_w)),
        pl.BlockSpec((1, D_INNER), lambda c: (0, 0)),
        pl.BlockSpec((1, bc_w), lambda c: (0, D_INNER // bc_w)),
        full((1, D_INNER)),
        full((1, D_INNER)),
    ]
    return pl.pallas_call(
        _ssd_kernel,
        grid=(N_CHUNKS,),
        in_specs=in_specs,
        out_specs=[
            pl.BlockSpec((CHUNK, D_INNER), lambda c: (c, 0)),
            pl.BlockSpec((CHUNK, 128), lambda c: (c, 0)),
        ],
        out_shape=[
            jax.ShapeDtypeStruct((R, D_INNER), bf16),
            jax.ShapeDtypeStruct((R, 128), f32),
        ],
        scratch_shapes=[
            pltpu.VMEM((N_GROUPS, N_STATE, GROUP_W), f32),
            pltpu.VMEM((HALO + CHUNK, D_INNER), f32),
            pltpu.VMEM((HALO + CHUNK, bc_w), f32),
        ],
        compiler_params=_params(("arbitrary",)),
        name="ssd_mixer",
    )(proj, proj, proj, proj, proj, dt_g, dt_gt, alog, alogt,
      conv_w, conv_w, conv_b2, conv_b2, dskip_e, ssd_norm_w.reshape(1, D_INNER))


def _router_kernel(h_ref, nw_ref, rw_ref, rb_ref, u_ref, idx_ref, gate_ref, *, tm):
    i = pl.program_id(0)
    h = h_ref[...]
    ms = jnp.mean(h * h, axis=-1, keepdims=True)
    u = h * lax.rsqrt(ms + EPS) * nw_ref[...]
    rows = i * tm + lax.broadcasted_iota(i32, (tm, 1), 0)
    u = jnp.where(rows >= PAD_ROWS, u, 0.0)
    u_ref[...] = u
    logits =jnp.dot(u, rw_ref[...], preferred_element_type=f32, precision=lax.Precision.HIGHEST) + rb_ref[...]
    e_iota = lax.broadcasted_iota(i32, (tm, N_EXPERTS), 1)
    lane = lax.broadcasted_iota(i32, (tm, 128), 1)
    idx_out = jnp.zeros((tm, 128), i32)
    val_out = jnp.zeros((tm, 128), f32)
    vals = []
    for k in range(TOP_K):
        m = jnp.max(logits, axis=1, keepdims=True)
        sel = jnp.min(jnp.where(logits == m, e_iota, N_EXPERTS), axis=1, keepdims=True)
        vals.append(m)
        idx_out = jnp.where(lane == k, sel, idx_out)
        logits = jnp.where(e_iota == sel, -jnp.inf, logits)
    exps = [jnp.exp(v - vals[0]) for v in vals]
    denom = exps[0] + exps[1] + exps[2] + exps[3]
    for k in range(TOP_K):
        val_out = jnp.where(lane == k, exps[k] / denom, val_out)
    idx_ref[...] = idx_out
    gate_ref[...] = val_out


def _router(h1, norm_w, router_w, router_b, tm=320):
    return pl.pallas_call(
        functools.partial(_router_kernel, tm=tm),
        grid=(R // tm,),
        in_specs=[
            pl.BlockSpec((tm, D_MODEL), lambda i: (i, 0)),
            pl.BlockSpec((1, D_MODEL), lambda i: (0, 0)),
            pl.BlockSpec((D_MODEL, N_EXPERTS), lambda i: (0, 0)),
            pl.BlockSpec((1, N_EXPERTS), lambda i: (0, 0)),
        ],
        out_specs=[
            pl.BlockSpec((tm, D_MODEL), lambda i: (i, 0)),
            pl.BlockSpec((tm, 128), lambda i: (i, 0)),
            pl.BlockSpec((tm, 128), lambda i: (i, 0)),
        ],
        out_shape=[
            jax.ShapeDtypeStruct((R, D_MODEL), f32),
            jax.ShapeDtypeStruct((R, 128), i32),
            jax.ShapeDtypeStruct((R, 128), f32),
        ],
        compiler_params=_params(("parallel",)),
        name="router",
    )(h1, norm_w.reshape(1, D_MODEL), router_w, router_b.reshape(1, N_EXPERTS))


def _gather_x_kernel(idx_ref, nu_ref, nv_ref, src_ref, o_ref, buf_ref, sem, *, tm):
    i = pl.program_id(0)
    n_used = nu_ref[0]

    def row_copy(src_row, slot, r):
        return pltpu.make_async_copy(src_ref.at[pl.ds(src_row, 1)], buf_ref.at[slot, pl.ds(r, 1)], sem.at[slot])

    def issue_tile(step):
        slot = step % 2
        n_blocks = nv_ref[step] // DMA_ISSUE_UNROLL

        def issue_block(b, carry):
            for u in range(DMA_ISSUE_UNROLL):
                r = b * DMA_ISSUE_UNROLL + u
                row_copy(idx_ref[step * tm + r], slot, r).start()
            return carry

        def issue(r, carry):
            row_copy(idx_ref[step * tm + r], slot, r).start()
            return carry

        lax.fori_loop(0, n_blocks, issue_block, 0)
        lax.fori_loop(n_blocks * DMA_ISSUE_UNROLL, nv_ref[step], issue, 0)

    @pl.when(i == 0)
    def _():
        buf_ref[...] = jnp.zeros_like(buf_ref)
        issue_tile(0)

    @pl.when(i + 1 < n_used)
    def _():
        issue_tile(i + 1)

    @pl.when(i < n_used)
    def _():
        slot = i % 2

        @pl.when(nv_ref[i] == tm)
        def _():
            pltpu.make_async_copy(src_ref.at[pl.ds(0, tm)], buf_ref.at[slot], sem.at[slot]).wait()

        @pl.when(nv_ref[i] < tm)
        def _():
            def drain(r, carry):
                row_copy(0, slot, 0).wait()
                return carry

            lax.fori_loop(0, nv_ref[i], drain, 0)

        o_ref[...] = buf_ref[slot].astype(o_ref.dtype)

    @pl.when(i >= n_used)
    def _():
        o_ref[...] = jnp.zeros_like(o_ref)


def _gather_x(src_rows, n_used, tile_rows, u2, tm=MOE_TM):
    return pl.pallas_call(
        functools.partial(_gather_x_kernel, tm=tm),
        grid_spec=pltpu.PrefetchScalarGridSpec(
            num_scalar_prefetch=3,
            grid=(XS_ROWS // tm,),
            in_specs=[pl.BlockSpec(memory_space=pl.ANY)],
            out_specs=pl.BlockSpec((tm, D_MODEL), lambda i, idx, nu, nv: (i, 0)),
            scratch_shapes=[pltpu.VMEM((2, tm, D_MODEL), f32), pltpu.SemaphoreType.DMA((2,))],
        ),
        out_shape=jax.ShapeDtypeStruct((XS_ROWS, D_MODEL), bf16),
        compiler_params=_params(("arbitrary",)),
        name="gather_x",
    )(src_rows, n_used, tile_rows, u2)


def _grouped_kernel(ge_ref, gt_ref, gn_ref, ng_ref, nu_ref, x_ref, *refs, n_w, tn, nj, ocols, tile_fn):
    w_refs, b_refs, o_ref = refs[:n_w], refs[n_w:2 * n_w], refs[2 * n_w]
    wb_ref, obuf_ref, zbuf_ref, sem, zsem = refs[2 * n_w + 1:]
    g = pl.program_id(0)
    j = pl.program_id(1)
    active = g < ng_ref[0]
    gc = jnp.minimum(g, ng_ref[0] - 1)
    n_tiles = jnp.where(active, gn_ref[gc], 0)
    tile0 = gt_ref[gc]
    col0 = pl.multiple_of(j * ocols, ocols)
    total_tiles = o_ref.shape[0] // MOE_TM

    @pl.when(jnp.logical_and(g == 0, j == 0))
    def _():
        zbuf_ref[...] = jnp.zeros_like(zbuf_ref)

        def zero_copy(t, jc):
            rows = pl.ds(pl.multiple_of(t * MOE_TM, MOE_TM), MOE_TM)
            return pltpu.make_async_copy(zbuf_ref, o_ref.at[rows, pl.ds(jc * ocols, ocols)], zsem)

        def z_start(t, carry):
            for jc in range(nj):
                zero_copy(t, jc).start()
            return carry

        def z_wait(t, carry):
            for jc in range(nj):
                zero_copy(t, jc).wait()
            return carry

        lax.fori_loop(nu_ref[0], total_tiles, z_start, 0)
        lax.fori_loop(nu_ref[0], total_tiles, z_wait, 0)

    @pl.when(active)
    def _():
        for k in range(n_w):
            wb_ref[:, k * tn:(k + 1) * tn] = w_refs[k][0].astype(bf16)

    step = g * nj + j
    slot = step % 2

    def out_copy(slot_, n, tile):
        rows = pl.ds(pl.multiple_of(tile * MOE_TM, MOE_TM), n * MOE_TM)
        return pltpu.make_async_copy(obuf_ref.at[slot_, pl.ds(0, n * MOE_TM)], o_ref.at[rows, pl.ds(col0, ocols)],
                                     sem.at[slot_])

    def wait_step(s):
        gs = s // nj
        sent = jnp.where(jnp.logical_and(s >= 0, gs < ng_ref[0]), gn_ref[jnp.clip(gs, 0, MAX_GROUPS - 1)], 0)
        for n in range(1, MOE_CAP + 1):

            @pl.when(sent == n)
            def _(n=n):
                out_copy(s % 2, n, 0).wait()

    wait_step(step - 2)
    for n in range(1, MOE_CAP + 1):

        @pl.when(n_tiles == n)
        def _(n=n):
            acc = jnp.dot(x_ref[0:n * MOE_TM, :], wb_ref[...], preferred_element_type=f32)
            obuf_ref[slot, pl.ds(0, n * MOE_TM)] = tile_fn(acc, [b[0] for b in b_refs])
            out_copy(slot, n, tile0).start()

    @pl.when(step == MAX_GROUPS * nj - 1)
    def _():
        wait_step(step - 1)
        wait_step(step)


def _grouped_call(groups, n_used, x, ws, bs, *, k_dim, tn, nj, w_col_offsets, ocols, out_shape, tile_fn, name):
    g_expert, g_tile0, g_ntiles, n_groups = groups
    n_w = len(ws)

    def gi(g, ng):
        return jnp.minimum(g, ng[0] - 1)

    def x_map(g, j, ge, gt, gn, ng, nu):
        return (pl.multiple_of(gt[gi(g, ng)] * MOE_TM, MOE_TM), 0)

    def w_map(off):
        return lambda g, j, ge, gt, gn, ng, nu: (ge[gi(g, ng)], 0, off + jnp.where(g < ng[0], j, nj - 1))

    return pl.pallas_call(
        functools.partial(_grouped_kernel, n_w=n_w, tn=tn, nj=nj, ocols=ocols, tile_fn=tile_fn),
        grid_spec=pltpu.PrefetchScalarGridSpec(
            num_scalar_prefetch=5,
            grid=(MAX_GROUPS, nj),
            in_specs=[
                pl.BlockSpec((pl.Element(GROUP_ROWS), pl.Element(k_dim)), x_map),
                *[pl.BlockSpec((1, k_dim, tn), w_map(off)) for off in w_col_offsets],
                *[pl.BlockSpec((1, 1, tn), w_map(off)) for off in w_col_offsets],
            ],
            out_specs=pl.BlockSpec(memory_space=pl.ANY),
            scratch_shapes=[
                pltpu.VMEM((k_dim, n_w * tn), bf16),
                pltpu.VMEM((2, GROUP_ROWS, ocols), out_shape.dtype),
                pltpu.VMEM((MOE_TM, ocols), out_shape.dtype),
                pltpu.SemaphoreType.DMA((2,)),
                pltpu.SemaphoreType.DMA(()),
            ],
        ),
        out_shape=out_shape,
        compiler_params=_params(("arbitrary", "arbitrary")),
        name=name,
    )(g_expert, g_tile0, g_ntiles, n_groups, n_used, x, *ws, *bs)


def _moe_up(groups, n_used, xs, w_up, b_up, tn=256):
    nj = D_EXPERT // tn

    def tile_fn(acc, biases):
        gate = jnp.minimum(acc[:, :tn] + biases[0], LIMIT)
        up = jnp.clip(acc[:, tn:] + biases[1], -LIMIT, LIMIT)
        return ((up + 1.0) * (gate * _sigmoid(ALPHA * gate))).astype(bf16)

    b3 = b_up.reshape(N_EXPERTS, 1, -1)
    return _grouped_call(groups, n_used, xs, (w_up, w_up), (b3, b3), k_dim=D_MODEL, tn=tn, nj=nj,
                         w_col_offsets=(0, nj), ocols=tn,
                         out_shape=jax.ShapeDtypeStruct((XS_ROWS, D_EXPERT), bf16), tile_fn=tile_fn, name="moe_up")


def _moe_down(groups, n_used, act, w_down, b_down, tn=1024):
    return _grouped_call(groups, n_used, act, (w_down,), (b_down.reshape(N_EXPERTS, 1, -1),), k_dim=D_EXPERT,
                         tn=tn, nj=D_MODEL // tn, w_col_offsets=(0,), ocols=tn,
                         out_shape=jax.ShapeDtypeStruct((N_SLOTS, D_MODEL), f32),
                         tile_fn=lambda acc, biases: acc + biases[0], name="moe_down")


def _combine_kernel(pos_ref, h_ref, g_ref, nw_ref, y_ref, o_ref, buf_ref, sem, *, tm):
    i = pl.program_id(0)
    n = pl.num_programs(0)

    def row_copy(src_row, slot, k, r):
        return pltpu.make_async_copy(y_ref.at[pl.ds(src_row, 1)], buf_ref.at[slot, k, pl.ds(r, 1)], sem.at[slot])

    def issue_tile(step):
        slot = step % 2
        for k in range(TOP_K):

            def issue(r, carry, k=k):
                row_copy(pos_ref[k * SEQ + step * tm + r], slot, k, r).start()
                return carry

            lax.fori_loop(0, tm, issue, 0, unroll=DMA_ISSUE_UNROLL)

    @pl.when(i == 0)
    def _():
        issue_tile(0)

    @pl.when(i + 1 < n)
    def _():
        issue_tile(i + 1)

    slot = i % 2
    for k in range(TOP_K):
        pltpu.make_async_copy(y_ref.at[pl.ds(0, tm)], buf_ref.at[slot, k], sem.at[slot]).wait()

    g = g_ref[...]
    h = h_ref[...]
    for k in range(TOP_K):
        h = h + g[:, k:k + 1] * buf_ref[slot, k]
    ms = jnp.mean(h * h, axis=-1, keepdims=True)
    o_ref[...] = h * lax.rsqrt(ms + EPS) * nw_ref[...]


def _combine(pos_kt, h1, y, gates, norm_w, tm=128):
    nb = SEQ // tm
    off = (PAD_ROWS + N_META) // tm
    return pl.pallas_call(
        functools.partial(_combine_kernel, tm=tm),
        grid_spec=pltpu.PrefetchScalarGridSpec(
            num_scalar_prefetch=1,
            grid=(nb,),
            in_specs=[
                pl.BlockSpec((tm, D_MODEL), lambda i, pos: (i + off, 0)),
                pl.BlockSpec((tm, 128), lambda i, pos: (i + off, 0)),
                pl.BlockSpec((1, D_MODEL), lambda i, pos: (0, 0)),
                pl.BlockSpec(memory_space=pl.ANY),
            ],
            out_specs=pl.BlockSpec((tm, D_MODEL), lambda i, pos: (i, 0)),
            scratch_shapes=[pltpu.VMEM((2, TOP_K, tm, D_MODEL), f32), pltpu.SemaphoreType.DMA((2,))],
        ),
        out_shape=jax.ShapeDtypeStruct((SEQ, D_MODEL), f32),
        compiler_params=_params(("arbitrary",)),
        name="combine_final_norm",
    )(pos_kt, h1, gates, norm_w.reshape(1, D_MODEL), y)


def _rank_kernel(idx_ref, rank_ref, cnt_ref, carry_ref, *, tm):
    i = pl.program_id(0)

    @pl.when(i == 0)
    def _():
        carry_ref[...] = jnp.zeros_like(carry_ref)

    idx = idx_ref[...]
    rows = i * tm + lax.broadcasted_iota(i32, (tm, 1), 0)
    valid = rows >= PAD_ROWS
    e_iota = lax.broadcasted_iota(i32, (tm, N_EXPERTS), 1)
    hits = [jnp.logical_and(idx[:, k:k + 1] == e_iota, valid) for k in range(TOP_K)]
    onehot = sum(jnp.where(h, 1.0, 0.0) for h in hits)
    earlier = lax.broadcasted_iota(i32, (tm, tm), 0) > lax.broadcasted_iota(i32, (tm, tm), 1)
    before = jnp.dot(jnp.where(earlier, 1.0, 0.0).astype(bf16), onehot.astype(bf16),
                     preferred_element_type=f32) + carry_ref[...]
    lane = lax.broadcasted_iota(i32, (tm, 128), 1)
    out = jnp.zeros((tm, 128), i32)
    for k in range(TOP_K):
        rk = jnp.sum(jnp.where(hits[k], before, 0.0), axis=1, keepdims=True)
        out = jnp.where(lane == k, rk.astype(i32), out)
    rank_ref[...] = out
    carry_ref[...] += jnp.sum(onehot, axis=0, keepdims=True)
    cnt_ref[...] = carry_ref[...]


def _rank_pairs(top_idx, tm=320):
    return pl.pallas_call(
        functools.partial(_rank_kernel, tm=tm),
        grid=(R // tm,),
        in_specs=[pl.BlockSpec((tm, 128), lambda i: (i, 0))],
        out_specs=[pl.BlockSpec((tm, 128), lambda i: (i, 0)), pl.BlockSpec((1, N_EXPERTS), lambda i: (0, 0))],
        out_shape=[jax.ShapeDtypeStruct((R, 128), i32), jax.ShapeDtypeStruct((1, N_EXPERTS), f32)],
        scratch_shapes=[pltpu.VMEM((1, N_EXPERTS), f32)],
        compiler_params=_params(("arbitrary",)),
        name="rank_pairs",
    )(top_idx)


def _routing_tables(top_idx, rank, counts):
    counts = counts.reshape(N_EXPERTS).astype(i32)
    tiles = (counts + MOE_TM - 1) // MOE_TM
    tile_end = jnp.cumsum(tiles)
    tile_start = tile_end - tiles
    e_ids = jnp.arange(N_EXPERTS, dtype=i32)
    tok_e = top_idx[PAD_ROWS:, :TOP_K]
    tok_start = jnp.sum(jnp.where(tok_e[..., None] == e_ids, tile_start * MOE_TM, 0), axis=-1)
    pos = tok_start + rank[PAD_ROWS:, :TOP_K]
    pair_row = jnp.arange(N_PAIRS, dtype=i32) // TOP_K + PAD_ROWS
    src_rows = jnp.zeros((XS_ROWS,), i32).at[pos.reshape(-1)].set(pair_row)
    n_used = tile_end[-1:].astype(i32)
    tile_ids = jnp.arange(XS_ROWS // MOE_TM, dtype=i32)
    tile_e = jnp.minimum(jnp.sum(tile_end[None, :] <= tile_ids[:, None], axis=1), N_EXPERTS - 1)
    tile_rows = jnp.clip(counts[tile_e] - (tile_ids - tile_start[tile_e]) * MOE_TM, 0, MOE_TM).astype(i32)
    tile_rows = jnp.where(tile_ids < n_used[0], tile_rows, 0)
    groups = (tiles + MOE_CAP - 1) // MOE_CAP
    group_end = jnp.cumsum(groups)
    g_ids = jnp.arange(MAX_GROUPS, dtype=i32)
    g_expert = jnp.minimum(jnp.sum(group_end[None, :] <= g_ids[:, None], axis=1), N_EXPERTS - 1).astype(i32)
    g_local = g_ids - (group_end - groups)[g_expert]
    g_tile0 = (tile_start[g_expert] + g_local * MOE_CAP).astype(i32)
    g_ntiles = jnp.clip(tiles[g_expert] - g_local * MOE_CAP, 0, MOE_CAP).astype(i32)
    n_groups = group_end[-1:].astype(i32)
    return src_rows, pos, n_used, tile_rows, (g_expert, g_tile0, g_ntiles, n_groups)


def kernel(x, meta_tokens, norm_mix_w, w_in, conv_w, conv_b, dt_bias, a_log, d_skip, ssd_norm_w, w_ssd_out,
           pool_w, pool_scale, w_out, norm_ffn_w, router_w, router_b, w_up, b_up, w_down, b_down, norm_final_w):
    assert x.shape == (1, SEQ, D_MODEL) and norm_mix_w.shape[0] == 1
    x2 = x[0]
    head = jnp.concatenate([jnp.zeros((PAD_ROWS, D_MODEL), f32), meta_tokens.astype(f32)], axis=0)

    u = _rms_norm_rows(x2, head, norm_mix_w[0])
    proj = _in_proj_main(u, w_in[0])
    dt = _in_proj_dt(u, w_in[0], dt_bias[0])
    gates = _in_proj_gates(u, w_in[0])
    ma = _pool_branch(proj, gates, pool_w[0], pool_scale[0])
    vw, ssq = _ssd_branch(proj, dt, conv_w[0], conv_b[0], a_log[0], d_skip[0], ssd_norm_w[0])
    merged = _ssd_out_merge(vw, w_ssd_out[0], ssq, ma, gates)
    h1 = _out_proj(merged, w_out[0], head, x2)

    u2, top_idx, top_gate = _router(h1, norm_ffn_w[0], router_w[0], router_b[0])
    rank, counts = _rank_pairs(top_idx)
    src_rows, pos, n_used, tile_rows, groups = _routing_tables(top_idx, rank, counts)
    xs = _gather_x(src_rows, n_used, tile_rows, u2)
    act = _moe_up(groups, n_used, xs, w_up[0], b_up[0])
    y = _moe_down(groups, n_used, act, w_down[0], b_down[0])
    pos_kt = pos[N_META:].T.reshape(-1)
    out = _combine(pos_kt, h1, y, top_gate, norm_final_w)
    return out[None]
```

```python
import functools

import jax
import jax.numpy as jnp
from jax import lax
from jax.experimental import pallas as pl
from jax.experimental.pallas import tpu as pltpu

f32 = jnp.float32
bf16 = jnp.bfloat16
i32 = jnp.int32
u32 = jnp.uint32

D_MODEL = 4096
SEQ = 8192
N_META = 16
CHUNK = 128
PAD_ROWS = CHUNK - N_META
R = PAD_ROWS + N_META + SEQ
N_CHUNKS = R // CHUNK
POOL_WINDOWS = (2, 4, 8, 16)
POOL_GROUP = D_MODEL // 4
D_INNER = 2 * D_MODEL
HEAD_DIM = 64
N_HEADS = D_INNER // HEAD_DIM
N_STATE = 128
N_GROUPS = 8
HEADS_PER_GROUP = N_HEADS // N_GROUPS
GROUP_W = HEADS_PER_GROUP * HEAD_DIM
CONV_W = 4
HALO = 16
POOL_HALO = 2 * max(POOL_WINDOWS)
N_EXPERTS = 32
TOP_K = 4
D_EXPERT = 7 * D_MODEL // 16
LIMIT = 7.0
ALPHA = 1.702
EPS = 1e-5
LOG2E = 1.4426950408889634
N_TOK = N_META + SEQ
N_PAIRS = N_TOK * TOP_K
MOE_TM = 256
N_TILES = -(-N_PAIRS // MOE_TM) + N_EXPERTS
N_SLOTS = N_TILES * MOE_TM
MOE_CAP = 6
GROUP_ROWS = MOE_CAP * MOE_TM
MAX_GROUPS = N_EXPERTS + N_TILES // MOE_CAP
XS_ROWS = N_SLOTS + GROUP_ROWS
COL_Z = D_MODEL
COL_X = COL_Z + D_INNER
COL_B = COL_X + D_INNER
COL_C = COL_B + N_GROUPS * N_STATE
COL_DT = COL_C + N_GROUPS * N_STATE
COL_G = COL_DT + N_HEADS
SEG1_COLS = COL_DT
V7X_VMEM_LIMIT = 56 * 1024 * 1024
DMA_ISSUE_UNROLL = 8


def _params(sem, vmem=V7X_VMEM_LIMIT):
    return pltpu.CompilerParams(dimension_semantics=sem, vmem_limit_bytes=vmem)


def _sigmoid(v):
    return 1.0 / (1.0 + jnp.exp(-v))


def _sigmoid_tanh(v):
    return 0.5 * jnp.tanh(0.5 * v) + 0.5


def _x_rows_spec(tm, tn, col_of):
    return pl.BlockSpec((pl.Element(tm), pl.Element(tn)),
                        lambda i, *rest: (pl.multiple_of(jnp.maximum(i * tm - CHUNK, 0), CHUNK), col_of(i, *rest)))


def _rms_kernel(x_ref, head_ref, w_ref, o_ref):
    i = pl.program_id(0)
    tm = o_ref.shape[0]

    def norm(h):
        ms = jnp.mean(h * h, axis=-1, keepdims=True)
        return (h * lax.rsqrt(ms + EPS) * w_ref[...]).astype(o_ref.dtype)

    @pl.when(i == 0)
    def _():
        o_ref[0:CHUNK, :] = norm(head_ref[...])
        o_ref[CHUNK:tm, :] = norm(x_ref[0:tm - CHUNK, :])

    @pl.when(i > 0)
    def _():
        o_ref[...] = norm(x_ref[...])


def _rms_norm_rows(x2, head, w, tm=640):
    d = x2.shape[1]
    return pl.pallas_call(
        _rms_kernel,
        grid=(R // tm,),
        in_specs=[_x_rows_spec(tm, d, lambda i: 0), pl.BlockSpec((CHUNK, d), lambda i: (0, 0)),
                  pl.BlockSpec((1, d), lambda i: (0, 0))],
        out_specs=pl.BlockSpec((tm, d), lambda i: (i, 0)),
        out_shape=jax.ShapeDtypeStruct((R, d), bf16),
        compiler_params=_params(("parallel",)),
        name="rms_mix",
    )(x2, head, w.reshape(1, d))


def _mm_kernel(*refs, n_extra, epilogue, nk):
    a_ref, w_ref = refs[0], refs[1]
    extra = refs[2:2 + n_extra]
    o_ref = refs[2 + n_extra]
    acc_ref = refs[3 + n_extra]
    k = pl.program_id(2)

    @pl.when(k == 0)
    def _():
        acc_ref[...] = jnp.dot(a_ref[...], w_ref[...].astype(bf16), preferred_element_type=f32)

    @pl.when(k > 0)
    def _():
        acc_ref[...] += jnp.dot(a_ref[...], w_ref[...].astype(bf16), preferred_element_type=f32)

    @pl.when(k == nk - 1)
    def _():
        epilogue(o_ref, acc_ref, *extra)


def _store(fn):
    def epilogue(o_ref, acc_ref, *extra):
        o_ref[...] = fn(acc_ref[...], *[e[...] for e in extra]).astype(o_ref.dtype)

    return epilogue


def _matmul(a, w, *, col0, n, tm, tn, tk, out_dtype, epilogue, extras=(), extra_specs=(), name):
    m, kdim = a.shape
    nk = kdim // tk
    if col0 % tn == 0:
        w_spec = pl.BlockSpec((tk, tn), lambda i, j, k: (k, j + col0 // tn))
    else:
        w_spec = pl.BlockSpec((pl.Element(tk), pl.Element(tn)), lambda i, j, k: (k * tk, pl.multiple_of(col0 + j * tn, 128)))
    return pl.pallas_call(
        functools.partial(_mm_kernel, n_extra=len(extras), epilogue=epilogue, nk=nk),
        grid=(m // tm, n // tn, nk),
        in_specs=[pl.BlockSpec((tm, tk), lambda i, j, k: (i, k)), w_spec, *extra_specs],
        out_specs=pl.BlockSpec((tm, tn), lambda i, j, k: (i, j)),
        out_shape=jax.ShapeDtypeStruct((m, n), out_dtype),
        scratch_shapes=[pltpu.VMEM((tm, tn), f32)],
        compiler_params=_params(("parallel", "parallel", "arbitrary")),
        name=name,
    )(a, w, *extras)


def _softplus_bias(acc, bias):
    v = acc + bias
    return jnp.maximum(v, 0.0) + jnp.log1p(jnp.exp(-jnp.abs(v)))


def _in_proj_main(u, w_in):
    return _matmul(u, w_in, col0=0, n=SEG1_COLS, tm=1664, tn=1024, tk=2048, out_dtype=bf16,
                   epilogue=_store(lambda acc: acc), name="in_proj_main")


def _in_proj_dt(u, w_in, dt_bias):
    return _matmul(u, w_in, col0=COL_DT, n=N_HEADS, tm=1664, tn=N_HEADS, tk=1024, out_dtype=f32,
                   epilogue=_store(_softplus_bias), extras=(dt_bias.reshape(1, N_HEADS),),
                   extra_specs=(pl.BlockSpec((1, N_HEADS), lambda i, j, k: (0, 0)),), name="in_proj_dt")


def _in_proj_gates(u, w_in):
    return _matmul(u, w_in, col0=COL_G, n=2 * D_MODEL, tm=1664, tn=1024, tk=2048, out_dtype=bf16,
                   epilogue=_store(_sigmoid_tanh), name="in_proj_gates")


def _ssd_out_merge(vw, w_ssd_out, ssq, ma, gates, tm=1664, tn=1024):
    gb_off = D_MODEL // tn

    def merge(acc, ssq_t, ma_t, gb_t):
        rs = lax.rsqrt(ssq_t[:, :1] * (1.0 / D_INNER) + EPS)
        return ma_t.astype(f32) + gb_t.astype(f32) * (acc * rs)

    return _matmul(vw, w_ssd_out, col0=0, n=D_MODEL, tm=tm, tn=tn, tk=1024, out_dtype=bf16, epilogue=_store(merge),
                   extras=(ssq, ma, gates),
                   extra_specs=(pl.BlockSpec((tm, 128), lambda i, j, k: (i, 0)),
                                pl.BlockSpec((tm, tn), lambda i, j, k: (i, j)),
                                pl.BlockSpec((tm, tn), lambda i, j, k: (i, j + gb_off))),
                   name="ssd_out_merge")


def _out_proj(merged, w_out, head, x2, tm=1664, tn=1024):
    def add_residual(o_ref, acc_ref, head_ref, x_ref):
        i = pl.program_id(0)

        @pl.when(i == 0)
        def _():
            o_ref[0:CHUNK, :] = head_ref[...] + acc_ref[0:CHUNK, :]
            o_ref[CHUNK:tm, :] = x_ref[0:tm - CHUNK, :] + acc_ref[CHUNK:tm, :]

        @pl.when(i > 0)
        def _():
            o_ref[...] = x_ref[...] + acc_ref[...]

    return _matmul(merged, w_out, col0=0, n=D_MODEL, tm=tm, tn=tn, tk=1024, out_dtype=f32,
                   epilogue=add_residual, extras=(head, x2),
                   extra_specs=(pl.BlockSpec((CHUNK, tn), lambda i, j, k: (0, j)),
                                _x_rows_spec(tm, tn, lambda i, j, k: pl.multiple_of(j * tn, tn))),
                   name="out_proj_residual")


def _pool_kernel(x_ref, halo_ref, w_ref, scale_ref, ga_ref, o_ref, a_ref, b_ref, pooled_ref, *, tm):
    assert POOL_WINDOWS == (2, 4, 8, 16)
    gi = pl.program_id(0)
    i = pl.program_id(1)
    n = POOL_HALO + tm
    halo = halo_ref[...].astype(f32)
    a_ref[0:POOL_HALO, :] = jnp.where(i == 0, 0.0, halo)
    a_ref[POOL_HALO:n, :] = x_ref[...].astype(f32)
    t = i * tm + lax.broadcasted_iota(i32, (tm, 1), 0) - PAD_ROWS

    def pair_sum(src_ref, shift, start):
        return src_ref[start:n, :] + src_ref[start - shift:n - shift, :]

    def finish(s, win):
        cnt = jnp.clip(t + 1, 1, win).astype(f32)
        pooled_ref[...] = (s / cnt - x_ref[...].astype(f32)).astype(bf16)

    @pl.when(gi == 0)
    def _():
        finish(pair_sum(a_ref, 1, POOL_HALO), 2)

    @pl.when(gi == 1)
    def _():
        b_ref[8:n, :] = pair_sum(a_ref, 1, 8)
        finish(pair_sum(b_ref, 2, POOL_HALO), 4)

    @pl.when(gi == 2)
    def _():
        b_ref[8:n, :] = pair_sum(a_ref, 1, 8)
        a_ref[16:n, :] = pair_sum(b_ref, 2, 16)
        finish(pair_sum(a_ref, 4, POOL_HALO), 8)

    @pl.when(gi == 3)
    def _():
        b_ref[8:n, :] = pair_sum(a_ref, 1, 8)
        a_ref[16:n, :] = pair_sum(b_ref, 2, 16)
        b_ref[24:n, :] = pair_sum(a_ref, 4, 24)
        finish(pair_sum(b_ref, 8, POOL_HALO), 16)

    acc = jnp.dot(pooled_ref[...], w_ref[0].astype(bf16), preferred_element_type=f32)
    o_ref[...] = (ga_ref[...].astype(f32) * (acc * scale_ref[...])).astype(o_ref.dtype)


def _pool_branch(proj, gates, pool_w, pool_scale, tm=640):
    gw = POOL_GROUP
    return pl.pallas_call(
        functools.partial(_pool_kernel, tm=tm),
        grid=(4, R // tm),
        in_specs=[
            pl.BlockSpec((tm, gw), lambda g, i: (i, g)),
            pl.BlockSpec((POOL_HALO, gw), lambda g, i: (jnp.maximum(i * (tm // POOL_HALO) - 1, 0), g)),
            pl.BlockSpec((1, gw, gw), lambda g, i: (g, 0, 0)),
            pl.BlockSpec((1, gw), lambda g, i: (0, g)),
            pl.BlockSpec((tm, gw), lambda g, i: (i, g)),
        ],
        out_specs=pl.BlockSpec((tm, gw), lambda g, i: (i, g)),
        out_shape=jax.ShapeDtypeStruct((R, D_MODEL), bf16),
        scratch_shapes=[pltpu.VMEM((POOL_HALO + tm, gw), f32), pltpu.VMEM((POOL_HALO + tm, gw), f32),
                        pltpu.VMEM((tm, gw), bf16)],
        compiler_params=_params(("parallel", "arbitrary")),
        name="pool_branch",
    )(proj, proj, pool_w, pool_scale.reshape(1, D_MODEL), gates)


def _fill_ext(cur_ref, halo_ref, ext_ref, c):
    halo = halo_ref[...]
    ext_ref[0:HALO, :] = jnp.where(c == 0, jnp.zeros_like(halo), halo)
    ext_ref[HALO:HALO + CHUNK, :] = cur_ref[...]


def _conv_silu(ext_ref, sel, w_ref, b_ref, lanes):
    taps = jnp.dot(sel, ext_ref[:, lanes], preferred_element_type=f32)
    out = b_ref[:, lanes]
    for k in range(CONV_W):
        out = out + w_ref[k:k + 1, lanes] * taps[k * CHUNK:(k + 1) * CHUNK]
    return out * _sigmoid(out)


def _ssd_kernel(x_ref, xh_ref, bc_ref, bch_ref, z_ref, dt_ref, dtt_ref, alog_ref, alogt_ref,
                wx_ref, wbc_ref, bx_ref, bbc_ref, dskip_ref, nw_ref,
                vw_ref, ssq_ref,
                state_ref, extx_ref, extbc_ref):
    c = pl.program_id(0)

    @pl.when(c == 0)
    def _():
        state_ref[...] = jnp.zeros_like(state_ref)

    _fill_ext(x_ref, xh_ref, extx_ref, c)
    _fill_ext(bc_ref, bch_ref, extbc_ref, c)

    rows = c * CHUNK + lax.broadcasted_iota(i32, (CHUNK, 1), 0)
    row_ok = rows >= PAD_ROWS
    cols = c * CHUNK + lax.broadcasted_iota(i32, (1, CHUNK), 1)
    col_ok = cols >= PAD_ROWS
    li = lax.broadcasted_iota(i32, (CHUNK, CHUNK), 0)
    si = lax.broadcasted_iota(i32, (CHUNK, CHUNK), 1)
    causal = li >= si
    tril = causal.astype(f32)
    triu = (li <= si).astype(f32)
    lane = lax.broadcasted_iota(i32, (CHUNK, 2 * HEAD_DIM), 1)
    first = lane < HEAD_DIM
    first1 = lax.broadcasted_iota(i32, (1, 2 * HEAD_DIM), 1) < HEAD_DIM
    sr = lax.broadcasted_iota(i32, (CONV_W * CHUNK, HALO + CHUNK), 0)
    se = lax.broadcasted_iota(i32, (CONV_W * CHUNK, HALO + CHUNK), 1)
    sel = jnp.where(se == (sr % CHUNK) + (sr // CHUNK) + HALO - (CONV_W - 1), 1.0, 0.0).astype(bf16)
    hh = lax.broadcasted_iota(i32, (HEADS_PER_GROUP, GROUP_W), 0)
    hl = lax.broadcasted_iota(i32, (HEADS_PER_GROUP, GROUP_W), 1)
    head_lanes = jnp.where(hl // HEAD_DIM == hh, 1.0, 0.0).astype(bf16)

    def group(g, ssq):
        lo = pl.multiple_of(g * GROUP_W, GROUP_W)
        blo = pl.multiple_of(g * N_STATE, N_STATE)
        xs = _conv_silu(extx_ref, sel, wx_ref, bx_ref, pl.ds(lo, GROUP_W))
        bm = _conv_silu(extbc_ref, sel, wbc_ref, bbc_ref, pl.ds(blo, N_STATE))
        cm = _conv_silu(extbc_ref, sel, wbc_ref, bbc_ref, pl.ds(N_GROUPS * N_STATE + blo, N_STATE))

        dt = jnp.where(row_ok, dt_ref[g], 0.0)
        dtt = jnp.where(col_ok, dtt_ref[g], 0.0)
        a_dt = dt * (-jnp.exp(alog_ref[g]))
        a_dtt = dtt * (-jnp.exp(alogt_ref[g]))
        a_cs = jnp.dot(tril, a_dt, preferred_element_type=f32, precision=lax.Precision.HIGHEST)
        a_cst = jnp.dot(a_dtt, triu, preferred_element_type=f32, precision=lax.Precision.HIGHEST)
        last = a_cs[CHUNK - 1:CHUNK, :]
        e_cs = jnp.exp(a_cs)
        e_end = jnp.exp(last - a_cs)
        e_last = jnp.exp(last)
        a2 = a_cs * LOG2E
        a2t = a_cst * LOG2E

        bmb = bm.astype(bf16)
        cmb = cm.astype(bf16)
        cb = lax.dot_general(cmb, bmb, (((1,), (1,)), ((), ())), preferred_element_type=f32)
        state = state_ref[g]
        yoff = jnp.dot(cmb, state.astype(bf16), preferred_element_type=f32)

        scales = jnp.concatenate([dt, e_cs, e_end], axis=0).astype(bf16)
        spread = jnp.dot(scales, head_lanes, preferred_element_type=f32)
        xdt_g = xs * spread[0:CHUNK]
        xd2 = (xdt_g * spread[2 * CHUNK:3 * CHUNK]).astype(bf16)
        cd = []
        for q in range(HEADS_PER_GROUP // 2):
            ha, hb = 2 * q, 2 * q + 1
            sl = slice(q * 2 * HEAD_DIM, (q + 1) * 2 * HEAD_DIM)
            gl = pl.ds(pl.multiple_of(lo + q * 2 * HEAD_DIM, 2 * HEAD_DIM), 2 * HEAD_DIM)
            xq = xs[:, sl]
            xdtb = xdt_g[:, sl].astype(bf16)
            wa = cb * jnp.exp2(jnp.where(causal, a2[:, ha:ha + 1] - a2t[ha:ha + 1, :], -1e30))
            wb = cb * jnp.exp2(jnp.where(causal, a2[:, hb:hb + 1] - a2t[hb:hb + 1, :], -1e30))
            ya = jnp.dot(wa.astype(bf16), xdtb, preferred_element_type=f32)
            yb = jnp.dot(wb.astype(bf16), xdtb, preferred_element_type=f32)
            y = jnp.where(first, ya, yb) + yoff[:, sl] * spread[CHUNK:2 * CHUNK, sl] + dskip_ref[:, gl] * xq
            cd.append(jnp.where(first1, e_last[:, ha:ha + 1], e_last[:, hb:hb + 1]))
            zq = z_ref[:, gl].astype(f32)
            v = y * (zq * _sigmoid(zq))
            ssq = ssq + v * v
            vw_ref[:, gl] = (v * nw_ref[:, gl]).astype(vw_ref.dtype)

        bt = jnp.transpose(bm).astype(bf16)
        s_new = jnp.dot(bt, xd2, preferred_element_type=f32)
        state_ref[g] = state * jnp.concatenate(cd, axis=1) + s_new
        return ssq

    ssq = lax.fori_loop(0, N_GROUPS, group, jnp.zeros((CHUNK, 2 * HEAD_DIM), f32))
    ssq_ref[...] = jnp.broadcast_to(jnp.sum(ssq, axis=1, keepdims=True), (CHUNK, 128))


def _ssd_branch(proj, dt, conv_w, conv_b, a_log, d_skip, ssd_norm_w):
    bc_w = 2 * N_GROUPS * N_STATE
    dt_g = dt.reshape(R, N_GROUPS, HEADS_PER_GROUP).transpose(1, 0, 2)
    dt_gt = dt_g.transpose(0, 2, 1)
    alog = a_log.reshape(N_GROUPS, 1, HEADS_PER_GROUP)
    alogt = a_log.reshape(N_GROUPS, HEADS_PER_GROUP, 1)
    conv_b2 = conv_b.reshape(1, -1)
    dskip_e = jnp.repeat(d_skip, HEAD_DIM).reshape(1, D_INNER)

    def cur(col, width):
        return pl.BlockSpec((pl.Element(CHUNK), pl.Element(width)),
                            lambda c: (pl.multiple_of(c * CHUNK, CHUNK), col))

    def halo(col, width):
        return pl.BlockSpec((pl.Element(HALO), pl.Element(width)),
                            lambda c: (pl.multiple_of(jnp.maximum(c * CHUNK - HALO, 0), HALO), col))

    full = lambda shape: pl.BlockSpec(shape, lambda c: (0,) * len(shape))
    in_specs = [
        cur(COL_X, D_INNER), halo(COL_X, D_INNER),
        cur(COL_B, bc_w), halo(COL_B, bc_w),
        cur(COL_Z, D_INNER),
        pl.BlockSpec((N_GROUPS, CHUNK, HEADS_PER_GROUP), lambda c: (0, c, 0)),
        pl.BlockSpec((N_GROUPS, HEADS_PER_GROUP, CHUNK), lambda c: (0, 0, c)),
        full((N_GROUPS, 1, HEADS_PER_GROUP)),
        full((N_GROUPS, HEADS_PER_GROUP, 1)),
        pl.BlockSpec((CONV_W, D_INNER), lambda c: (0, 0)),
        pl.BlockSpec((CONV_W, bc_w), lambda c: (0, D_INNER // bc_w)),
        pl.BlockSpec((1, D_INNER), lambda c: (0, 0)),
        pl.BlockSpec((1, bc_w), lambda c: (0, D_INNER // bc_w)),
        full((1, D_INNER)),
        full((1, D_INNER)),
    ]
    return pl.pallas_call(
        _ssd_kernel,
        grid=(N_CHUNKS,),
        in_specs=in_specs,
        out_specs=[
            pl.BlockSpec((CHUNK, D_INNER), lambda c: (c, 0)),
            pl.BlockSpec((CHUNK, 128), lambda c: (c, 0)),
        ],
        out_shape=[
            jax.ShapeDtypeStruct((R, D_INNER), bf16),
            jax.ShapeDtypeStruct((R, 128), f32),
        ],
        scratch_shapes=[
            pltpu.VMEM((N_GROUPS, N_STATE, GROUP_W), f32),
            pltpu.VMEM((HALO + CHUNK, D_INNER), bf16),
            pltpu.VMEM((HALO + CHUNK, bc_w), bf16),
        ],
        compiler_params=_params(("arbitrary",)),
        name="ssd_mixer",
    )(proj, proj, proj, proj, proj, dt_g, dt_gt, alog, alogt,
      conv_w, conv_w, conv_b2, conv_b2, dskip_e, ssd_norm_w.reshape(1, D_INNER))


def _router_kernel(h_ref, nw_ref, rw_ref, rb_ref, u_ref, idx_ref, gate_ref, *, tm):
    i = pl.program_id(0)
    h = h_ref[...]
    ms = jnp.mean(h * h, axis=-1, keepdims=True)
    u = h * lax.rsqrt(ms + EPS) * nw_ref[...]
    rows = i * tm + lax.broadcasted_iota(i32, (tm, 1), 0)
    u = jnp.where(rows >= PAD_ROWS, u, 0.0)
    u_ref[...] = u
    logits =jnp.dot(u, rw_ref[...], preferred_element_type=f32, precision=lax.Precision.HIGHEST) + rb_ref[...]
    e_iota = lax.broadcasted_iota(i32, (tm, N_EXPERTS), 1)
    lane = lax.broadcasted_iota(i32, (tm, 128), 1)
    idx_out = jnp.zeros((tm, 128), i32)
    val_out = jnp.zeros((tm, 128), f32)
    vals = []
    for k in range(TOP_K):
        m = jnp.max(logits, axis=1, keepdims=True)
        sel = jnp.min(jnp.where(logits == m, e_iota, N_EXPERTS), axis=1, keepdims=True)
        vals.append(m)
        idx_out = jnp.where(lane == k, sel, idx_out)
        logits = jnp.where(e_iota == sel, -jnp.inf, logits)
    exps = [jnp.exp(v - vals[0]) for v in vals]
    denom = exps[0] + exps[1] + exps[2] + exps[3]
    for k in range(TOP_K):
        val_out = jnp.where(lane == k, exps[k] / denom, val_out)
    idx_ref[...] = idx_out
    gate_ref[...] = val_out


def _router(h1, norm_w, router_w, router_b, tm=320):
    return pl.pallas_call(
        functools.partial(_router_kernel, tm=tm),
        grid=(R // tm,),
        in_specs=[
            pl.BlockSpec((tm, D_MODEL), lambda i: (i, 0)),
            pl.BlockSpec((1, D_MODEL), lambda i: (0, 0)),
            pl.BlockSpec((D_MODEL, N_EXPERTS), lambda i: (0, 0)),
            pl.BlockSpec((1, N_EXPERTS), lambda i: (0, 0)),
        ],
        out_specs=[
            pl.BlockSpec((tm, D_MODEL), lambda i: (i, 0)),
            pl.BlockSpec((tm, 128), lambda i: (i, 0)),
            pl.BlockSpec((tm, 128), lambda i: (i, 0)),
        ],
        out_shape=[
            jax.ShapeDtypeStruct((R, D_MODEL), f32),
            jax.ShapeDtypeStruct((R, 128), i32),
            jax.ShapeDtypeStruct((R, 128), f32),
        ],
        compiler_params=_params(("parallel",)),
        name="router",
    )(h1, norm_w.reshape(1, D_MODEL), router_w, router_b.reshape(1, N_EXPERTS))


def _gather_x_kernel(idx_ref, nu_ref, nv_ref, src_ref, o_ref, buf_ref, sem, *, tm):
    i = pl.program_id(0)
    n_used = nu_ref[0]

    def row_copy(src_row, slot, r):
        return pltpu.make_async_copy(src_ref.at[pl.ds(src_row, 1)], buf_ref.at[slot, pl.ds(r, 1)], sem.at[slot])

    def issue_tile(step):
        slot = step % 2
        n_blocks = nv_ref[step] // DMA_ISSUE_UNROLL

        def issue_block(b, carry):
            for u in range(DMA_ISSUE_UNROLL):
                r = b * DMA_ISSUE_UNROLL + u
                row_copy(idx_ref[step * tm + r], slot, r).start()
            return carry

        def issue(r, carry):
            row_copy(idx_ref[step * tm + r], slot, r).start()
            return carry

        lax.fori_loop(0, n_blocks, issue_block, 0)
        lax.fori_loop(n_blocks * DMA_ISSUE_UNROLL, nv_ref[step], issue, 0)

    @pl.when(i == 0)
    def _():
        buf_ref[...] = jnp.zeros_like(buf_ref)
        issue_tile(0)

    @pl.when(i + 1 < n_used)
    def _():
        issue_tile(i + 1)

    @pl.when(i < n_used)
    def _():
        slot = i % 2

        @pl.when(nv_ref[i] == tm)
        def _():
            pltpu.make_async_copy(src_ref.at[pl.ds(0, tm)], buf_ref.at[slot], sem.at[slot]).wait()

        @pl.when(nv_ref[i] < tm)
        def _():
            def drain(r, carry):
                row_copy(0, slot, 0).wait()
                return carry

            lax.fori_loop(0, nv_ref[i], drain, 0)

        o_ref[...] = buf_ref[slot].astype(o_ref.dtype)

    @pl.when(i >= n_used)
    def _():
        o_ref[...] = jnp.zeros_like(o_ref)


def _gather_x(src_rows, n_used, tile_rows, u2, tm=MOE_TM):
    return pl.pallas_call(
        functools.partial(_gather_x_kernel, tm=tm),
        grid_spec=pltpu.PrefetchScalarGridSpec(
            num_scalar_prefetch=3,
            grid=(XS_ROWS // tm,),
            in_specs=[pl.BlockSpec(memory_space=pl.ANY)],
            out_specs=pl.BlockSpec((tm, D_MODEL), lambda i, idx, nu, nv: (i, 0)),
            scratch_shapes=[pltpu.VMEM((2, tm, D_MODEL), f32), pltpu.SemaphoreType.DMA((2,))],
        ),
        out_shape=jax.ShapeDtypeStruct((XS_ROWS, D_MODEL), bf16),
        compiler_params=_params(("arbitrary",)),
        name="gather_x",
    )(src_rows, n_used, tile_rows, u2)


def _grouped_kernel(ge_ref, gt_ref, gn_ref, ng_ref, nu_ref, x_ref, *refs, n_w, tn, nj, ocols, tile_fn):
    w_refs, b_refs, o_ref = refs[:n_w], refs[n_w:2 * n_w], refs[2 * n_w]
    wb_ref, obuf_ref, zbuf_ref, sem, zsem = refs[2 * n_w + 1:]
    g = pl.program_id(0)
    j = pl.program_id(1)
    active = g < ng_ref[0]
    gc = jnp.minimum(g, ng_ref[0] - 1)
    n_tiles = jnp.where(active, gn_ref[gc], 0)
    tile0 = gt_ref[gc]
    col0 = pl.multiple_of(j * ocols, ocols)
    total_tiles = o_ref.shape[0] // MOE_TM

    @pl.when(jnp.logical_and(g == 0, j == 0))
    def _():
        zbuf_ref[...] = jnp.zeros_like(zbuf_ref)

        def zero_copy(t, jc):
            rows = pl.ds(pl.multiple_of(t * MOE_TM, MOE_TM), MOE_TM)
            return pltpu.make_async_copy(zbuf_ref, o_ref.at[rows, pl.ds(jc * ocols, ocols)], zsem)

        def z_start(t, carry):
            for jc in range(nj):
                zero_copy(t, jc).start()
            return carry

        def z_wait(t, carry):
            for jc in range(nj):
                zero_copy(t, jc).wait()
            return carry

        lax.fori_loop(nu_ref[0], total_tiles, z_start, 0)
        lax.fori_loop(nu_ref[0], total_tiles, z_wait, 0)

    @pl.when(active)
    def _():
        for k in range(n_w):
            wb_ref[:, k * tn:(k + 1) * tn] = w_refs[k][0].astype(bf16)

    step = g * nj + j
    slot = step % 2

    def out_copy(slot_, n, tile):
        rows = pl.ds(pl.multiple_of(tile * MOE_TM, MOE_TM), n * MOE_TM)
        return pltpu.make_async_copy(obuf_ref.at[slot_, pl.ds(0, n * MOE_TM)], o_ref.at[rows, pl.ds(col0, ocols)],
                                     sem.at[slot_])

    def wait_step(s):
        gs = s // nj
        sent = jnp.where(jnp.logical_and(s >= 0, gs < ng_ref[0]), gn_ref[jnp.clip(gs, 0, MAX_GROUPS - 1)], 0)
        for n in range(1, MOE_CAP + 1):

            @pl.when(sent == n)
            def _(n=n):
                out_copy(s % 2, n, 0).wait()

    wait_step(step - 2)
    for n in range(1, MOE_CAP + 1):

        @pl.when(n_tiles == n)
        def _(n=n):
            acc = jnp.dot(x_ref[0:n * MOE_TM, :], wb_ref[...], preferred_element_type=f32)
            obuf_ref[slot, pl.ds(0, n * MOE_TM)] = tile_fn(acc, [b[0] for b in b_refs])
            out_copy(slot, n, tile0).start()

    @pl.when(step == MAX_GROUPS * nj - 1)
    def _():
        wait_step(step - 1)
        wait_step(step)


def _grouped_call(groups, n_used, x, ws, bs, *, k_dim, tn, nj, w_col_offsets, ocols, out_shape, tile_fn, name):
    g_expert, g_tile0, g_ntiles, n_groups = groups
    n_w = len(ws)

    def gi(g, ng):
        return jnp.minimum(g, ng[0] - 1)

    def x_map(g, j, ge, gt, gn, ng, nu):
        return (pl.multiple_of(gt[gi(g, ng)] * MOE_TM, MOE_TM), 0)

    def w_map(off):
        return lambda g, j, ge, gt, gn, ng, nu: (ge[gi(g, ng)], 0, off + jnp.where(g < ng[0], j, nj - 1))

    return pl.pallas_call(
        functools.partial(_grouped_kernel, n_w=n_w, tn=tn, nj=nj, ocols=ocols, tile_fn=tile_fn),
        grid_spec=pltpu.PrefetchScalarGridSpec(
            num_scalar_prefetch=5,
            grid=(MAX_GROUPS, nj),
            in_specs=[
                pl.BlockSpec((pl.Element(GROUP_ROWS), pl.Element(k_dim)), x_map),
                *[pl.BlockSpec((1, k_dim, tn), w_map(off)) for off in w_col_offsets],
                *[pl.BlockSpec((1, 1, tn), w_map(off)) for off in w_col_offsets],
            ],
            out_specs=pl.BlockSpec(memory_space=pl.ANY),
            scratch_shapes=[
                pltpu.VMEM((k_dim, n_w * tn), bf16),
                pltpu.VMEM((2, GROUP_ROWS, ocols), out_shape.dtype),
                pltpu.VMEM((MOE_TM, ocols), out_shape.dtype),
                pltpu.SemaphoreType.DMA((2,)),
                pltpu.SemaphoreType.DMA(()),
            ],
        ),
        out_shape=out_shape,
        compiler_params=_params(("arbitrary", "arbitrary")),
        name=name,
    )(g_expert, g_tile0, g_ntiles, n_groups, n_used, x, *ws, *bs)


def _moe_up(groups, n_used, xs, w_up, b_up, tn=256):
    nj = D_EXPERT // tn

    def tile_fn(acc, biases):
        gate = jnp.minimum(acc[:, :tn] + biases[0], LIMIT)
        up = jnp.clip(acc[:, tn:] + biases[1], -LIMIT, LIMIT)
        return ((up + 1.0) * (gate * _sigmoid(ALPHA * gate))).astype(bf16)

    b3 = b_up.reshape(N_EXPERTS, 1, -1)
    return _grouped_call(groups, n_used, xs, (w_up, w_up), (b3, b3), k_dim=D_MODEL, tn=tn, nj=nj,
                         w_col_offsets=(0, nj), ocols=tn,
                         out_shape=jax.ShapeDtypeStruct((XS_ROWS, D_EXPERT), bf16), tile_fn=tile_fn, name="moe_up")


def _moe_down(groups, n_used, act, w_down, b_down, tn=1024):
    return _grouped_call(groups, n_used, act, (w_down,), (b_down.reshape(N_EXPERTS, 1, -1),), k_dim=D_EXPERT,
                         tn=tn, nj=D_MODEL // tn, w_col_offsets=(0,), ocols=tn,
                         out_shape=jax.ShapeDtypeStruct((N_SLOTS, D_MODEL), f32),
                         tile_fn=lambda acc, biases: acc + biases[0], name="moe_down")


def _combine_kernel(pos_ref, h_ref, g_ref, nw_ref, y_ref, o_ref, buf_ref, sem, *, tm):
    i = pl.program_id(0)
    n = pl.num_programs(0)

    def row_copy(src_row, slot, k, r):
        return pltpu.make_async_copy(y_ref.at[pl.ds(src_row, 1)], buf_ref.at[slot, k, pl.ds(r, 1)], sem.at[slot])

    def issue_tile(step):
        slot = step % 2
        for k in range(TOP_K):

            def issue(r, carry, k=k):
                row_copy(pos_ref[k * SEQ + step * tm + r], slot, k, r).start()
                return carry

            lax.fori_loop(0, tm, issue, 0, unroll=DMA_ISSUE_UNROLL)

    @pl.when(i == 0)
    def _():
        issue_tile(0)

    @pl.when(i + 1 < n)
    def _():
        issue_tile(i + 1)

    slot = i % 2
    for k in range(TOP_K):
        pltpu.make_async_copy(y_ref.at[pl.ds(0, tm)], buf_ref.at[slot, k], sem.at[slot]).wait()

    g = g_ref[...]
    h = h_ref[...]
    for k in range(TOP_K):
        h = h + g[:, k:k + 1] * buf_ref[slot, k]
    ms = jnp.mean(h * h, axis=-1, keepdims=True)
    o_ref[...] = h * lax.rsqrt(ms + EPS) * nw_ref[...]


def _combine(pos_kt, h1, y, gates, norm_w, tm=128):
    nb = SEQ // tm
    off = (PAD_ROWS + N_META) // tm
    return pl.pallas_call(
        functools.partial(_combine_kernel, tm=tm),
        grid_spec=pltpu.PrefetchScalarGridSpec(
            num_scalar_prefetch=1,
            grid=(nb,),
            in_specs=[
                pl.BlockSpec((tm, D_MODEL), lambda i, pos: (i + off, 0)),
                pl.BlockSpec((tm, 128), lambda i, pos: (i + off, 0)),
                pl.BlockSpec((1, D_MODEL), lambda i, pos: (0, 0)),
                pl.BlockSpec(memory_space=pl.ANY),
            ],
            out_specs=pl.BlockSpec((tm, D_MODEL), lambda i, pos: (i, 0)),
            scratch_shapes=[pltpu.VMEM((2, TOP_K, tm, D_MODEL), f32), pltpu.SemaphoreType.DMA((2,))],
        ),
        out_shape=jax.ShapeDtypeStruct((SEQ, D_MODEL), f32),
        compiler_params=_params(("arbitrary",)),
        name="combine_final_norm",
    )(pos_kt, h1, gates, norm_w.reshape(1, D_MODEL), y)


def _rank_kernel(idx_ref, rank_ref, cnt_ref, carry_ref, *, tm):
    i = pl.program_id(0)

    @pl.when(i == 0)
    def _():
        carry_ref[...] = jnp.zeros_like(carry_ref)

    idx = idx_ref[...]
    rows = i * tm + lax.broadcasted_iota(i32, (tm, 1), 0)
    valid = rows >= PAD_ROWS
    e_iota = lax.broadcasted_iota(i32, (tm, N_EXPERTS), 1)
    hits = [jnp.logical_and(idx[:, k:k + 1] == e_iota, valid) for k in range(TOP_K)]
    onehot = sum(jnp.where(h, 1.0, 0.0) for h in hits)
    earlier = lax.broadcasted_iota(i32, (tm, tm), 0) > lax.broadcasted_iota(i32, (tm, tm), 1)
    before = jnp.dot(jnp.where(earlier, 1.0, 0.0).astype(bf16), onehot.astype(bf16),
                     preferred_element_type=f32) + carry_ref[...]
    lane = lax.broadcasted_iota(i32, (tm, 128), 1)
    out = jnp.zeros((tm, 128), i32)
    for k in range(TOP_K):
        rk = jnp.sum(jnp.where(hits[k], before, 0.0), axis=1, keepdims=True)
        out = jnp.where(lane == k, rk.astype(i32), out)
    rank_ref[...] = out
    carry_ref[...] += jnp.sum(onehot, axis=0, keepdims=True)
    cnt_ref[...] = carry_ref[...]


def _rank_pairs(top_idx, tm=320):
    return pl.pallas_call(
        functools.partial(_rank_kernel, tm=tm),
        grid=(R // tm,),
        in_specs=[pl.BlockSpec((tm, 128), lambda i: (i, 0))],
        out_specs=[pl.BlockSpec((tm, 128), lambda i: (i, 0)), pl.BlockSpec((1, N_EXPERTS), lambda i: (0, 0))],
        out_shape=[jax.ShapeDtypeStruct((R, 128), i32), jax.ShapeDtypeStruct((1, N_EXPERTS), f32)],
        scratch_shapes=[pltpu.VMEM((1, N_EXPERTS), f32)],
        compiler_params=_params(("arbitrary",)),
        name="rank_pairs",
    )(top_idx)


def _routing_tables(top_idx, rank, counts):
    counts = counts.reshape(N_EXPERTS).astype(i32)
    tiles = (counts + MOE_TM - 1) // MOE_TM
    tile_end = jnp.cumsum(tiles)
    tile_start = tile_end - tiles
    e_ids = jnp.arange(N_EXPERTS, dtype=i32)
    tok_e = top_idx[PAD_ROWS:, :TOP_K]
    tok_start = jnp.sum(jnp.where(tok_e[..., None] == e_ids, tile_start * MOE_TM, 0), axis=-1)
    pos = tok_start + rank[PAD_ROWS:, :TOP_K]
    pair_row = jnp.arange(N_PAIRS, dtype=i32) // TOP_K + PAD_ROWS
    src_rows = jnp.zeros((XS_ROWS,), i32).at[pos.reshape(-1)].set(pair_row)
    n_used = tile_end[-1:].astype(i32)
    tile_ids = jnp.arange(XS_ROWS // MOE_TM, dtype=i32)
    tile_e = jnp.minimum(jnp.sum(tile_end[None, :] <= tile_ids[:, None], axis=1), N_EXPERTS - 1)
    tile_rows = jnp.clip(counts[tile_e] - (tile_ids - tile_start[tile_e]) * MOE_TM, 0, MOE_TM).astype(i32)
    tile_rows = jnp.where(tile_ids < n_used[0], tile_rows, 0)
    groups = (tiles + MOE_CAP - 1) // MOE_CAP
    group_end = jnp.cumsum(groups)
    g_ids = jnp.arange(MAX_GROUPS, dtype=i32)
    g_expert = jnp.minimum(jnp.sum(group_end[None, :] <= g_ids[:, None], axis=1), N_EXPERTS - 1).astype(i32)
    g_local = g_ids - (group_end - groups)[g_expert]
    g_tile0 = (tile_start[g_expert] + g_local * MOE_CAP).astype(i32)
    g_ntiles = jnp.clip(tiles[g_expert] - g_local * MOE_CAP, 0, MOE_CAP).astype(i32)
    n_groups = group_end[-1:].astype(i32)
    return src_rows, pos, n_used, tile_rows, (g_expert, g_tile0, g_ntiles, n_groups)


def kernel(x, meta_tokens, norm_mix_w, w_in, conv_w, conv_b, dt_bias, a_log, d_skip, ssd_norm_w, w_ssd_out,
           pool_w, pool_scale, w_out, norm_ffn_w, router_w, router_b, w_up, b_up, w_down, b_down, norm_final_w):
    assert x.shape == (1, SEQ, D_MODEL) and norm_mix_w.shape[0] == 1
    x2 = x[0]
    head = jnp.concatenate([jnp.zeros((PAD_ROWS, D_MODEL), f32), meta_tokens.astype(f32)], axis=0)

    u = _rms_norm_rows(x2, head, norm_mix_w[0])
    proj = _in_proj_main(u, w_in[0])
    dt = _in_proj_dt(u, w_in[0], dt_bias[0])
    gates = _in_proj_gates(u, w_in[0])
    ma = _pool_branch(proj, gates, pool_w[0], pool_scale[0])
    vw, ssq = _ssd_branch(proj, dt, conv_w[0], conv_b[0], a_log[0], d_skip[0], ssd_norm_w[0])
    merged = _ssd_out_merge(vw, w_ssd_out[0], ssq, ma, gates)
    h1 = _out_proj(merged, w_out[0], head, x2)

    u2, top_idx, top_gate = _router(h1, norm_ffn_w[0], router_w[0], router_b[0])
    rank, counts = _rank_pairs(top_idx)
    src_rows, pos, n_used, tile_rows, groups = _routing_tables(top_idx, rank, counts)
    xs = _gather_x(src_rows, n_used, tile_rows, u2)
    act = _moe_up(groups, n_used, xs, w_up[0], b_up[0])
    y = _moe_down(groups, n_used, act, w_down[0], b_down[0])
    pos_kt = pos[N_META:].T.reshape(-1)
    out = _combine(pos_kt, h1, y, top_gate, norm_final_w)
    return out[None]
```

```python
import functools

import jax
import jax.numpy as jnp
from jax import lax
from jax.experimental import pallas as pl
from jax.experimental.pallas import tpu as pltpu

f32 = jnp.float32
bf16 = jnp.bfloat16
i32 = jnp.int32
u32 = jnp.uint32

D_MODEL = 4096
SEQ = 8192
N_META = 16
CHUNK = 128
PAD_ROWS = CHUNK - N_META
R = PAD_ROWS + N_META + SEQ
N_CHUNKS = R // CHUNK
POOL_WINDOWS = (2, 4, 8, 16)
POOL_GROUP = D_MODEL // 4
D_INNER = 2 * D_MODEL
HEAD_DIM = 64
N_HEADS = D_INNER // HEAD_DIM
N_STATE = 128
N_GROUPS = 8
HEADS_PER_GROUP = N_HEADS // N_GROUPS
GROUP_W = HEADS_PER_GROUP * HEAD_DIM
CONV_W = 4
HALO = 16
POOL_HALO = 2 * max(POOL_WINDOWS)
N_EXPERTS = 32
TOP_K = 4
D_EXPERT = 7 * D_MODEL // 16
LIMIT = 7.0
ALPHA = 1.702
EPS = 1e-5
LOG2E = 1.4426950408889634
N_TOK = N_META + SEQ
N_PAIRS = N_TOK * TOP_K
MOE_TM = 256
N_TILES = -(-N_PAIRS // MOE_TM) + N_EXPERTS
N_SLOTS = N_TILES * MOE_TM
MOE_CAP = 6
GROUP_ROWS = MOE_CAP * MOE_TM
MAX_GROUPS = N_EXPERTS + N_TILES // MOE_CAP
XS_ROWS = N_SLOTS + GROUP_ROWS
COL_Z = D_MODEL
COL_X = COL_Z + D_INNER
COL_B = COL_X + D_INNER
COL_C = COL_B + N_GROUPS * N_STATE
COL_DT = COL_C + N_GROUPS * N_STATE
COL_G = COL_DT + N_HEADS
SEG1_COLS = COL_DT
V7X_VMEM_LIMIT = 56 * 1024 * 1024
DMA_ISSUE_UNROLL = 8


def _params(sem, vmem=V7X_VMEM_LIMIT):
    return pltpu.CompilerParams(dimension_semantics=sem, vmem_limit_bytes=vmem)


def _sigmoid(v):
    return 1.0 / (1.0 + jnp.exp(-v))


def _sigmoid_tanh(v):
    return 0.5 * jnp.tanh(0.5 * v) + 0.5


def _x_rows_spec(tm, tn, col_of):
    return pl.BlockSpec((pl.Element(tm), pl.Element(tn)),
                        lambda i, *rest: (pl.multiple_of(jnp.maximum(i * tm - CHUNK, 0), CHUNK), col_of(i, *rest)))


def _rms_kernel(x_ref, head_ref, w_ref, o_ref):
    i = pl.program_id(0)
    tm = o_ref.shape[0]

    def norm(h):
        ms = jnp.mean(h * h, axis=-1, keepdims=True)
        return (h * lax.rsqrt(ms + EPS) * w_ref[...]).astype(o_ref.dtype)

    @pl.when(i == 0)
    def _():
        o_ref[0:CHUNK, :] = norm(head_ref[...])
        o_ref[CHUNK:tm, :] = norm(x_ref[0:tm - CHUNK, :])

    @pl.when(i > 0)
    def _():
        o_ref[...] = norm(x_ref[...])


def _rms_norm_rows(x2, head, w, tm=640):
    d = x2.shape[1]
    return pl.pallas_call(
        _rms_kernel,
        grid=(R // tm,),
        in_specs=[_x_rows_spec(tm, d, lambda i: 0), pl.BlockSpec((CHUNK, d), lambda i: (0, 0)),
                  pl.BlockSpec((1, d), lambda i: (0, 0))],
        out_specs=pl.BlockSpec((tm, d), lambda i: (i, 0)),
        out_shape=jax.ShapeDtypeStruct((R, d), bf16),
        compiler_params=_params(("parallel",)),
        name="rms_mix",
    )(x2, head, w.reshape(1, d))


def _mm_kernel(*refs, n_extra, epilogue, nk):
    a_ref, w_ref = refs[0], refs[1]
    extra = refs[2:2 + n_extra]
    o_ref = refs[2 + n_extra]
    acc_ref = refs[3 + n_extra]
    k = pl.program_id(2)

    @pl.when(k == 0)
    def _():
        acc_ref[...] = jnp.dot(a_ref[...], w_ref[...].astype(bf16), preferred_element_type=f32)

    @pl.when(k > 0)
    def _():
        acc_ref[...] += jnp.dot(a_ref[...], w_ref[...].astype(bf16), preferred_element_type=f32)

    @pl.when(k == nk - 1)
    def _():
        epilogue(o_ref, acc_ref, *extra)


def _store(fn):
    def epilogue(o_ref, acc_ref, *extra):
        o_ref[...] = fn(acc_ref[...], *[e[...] for e in extra]).astype(o_ref.dtype)

    return epilogue


def _matmul(a, w, *, col0, n, tm, tn, tk, out_dtype, epilogue, extras=(), extra_specs=(), name):
    m, kdim = a.shape
    nk = kdim // tk
    if col0 % tn == 0:
        w_spec = pl.BlockSpec((tk, tn), lambda i, j, k: (k, j + col0 // tn))
    else:
        w_spec = pl.BlockSpec((pl.Element(tk), pl.Element(tn)), lambda i, j, k: (k * tk, pl.multiple_of(col0 + j * tn, 128)))
    return pl.pallas_call(
        functools.partial(_mm_kernel, n_extra=len(extras), epilogue=epilogue, nk=nk),
        grid=(m // tm, n // tn, nk),
        in_specs=[pl.BlockSpec((tm, tk), lambda i, j, k: (i, k)), w_spec, *extra_specs],
        out_specs=pl.BlockSpec((tm, tn), lambda i, j, k: (i, j)),
        out_shape=jax.ShapeDtypeStruct((m, n), out_dtype),
        scratch_shapes=[pltpu.VMEM((tm, tn), f32)],
        compiler_params=_params(("parallel", "parallel", "arbitrary")),
        name=name,
    )(a, w, *extras)


def _softplus_bias(acc, bias):
    v = acc + bias
    return jnp.maximum(v, 0.0) + jnp.log1p(jnp.exp(-jnp.abs(v)))


def _in_proj_main(u, w_in):
    return _matmul(u, w_in, col0=0, n=SEG1_COLS, tm=1664, tn=1024, tk=2048, out_dtype=bf16,
                   epilogue=_store(lambda acc: acc), name="in_proj_main")


def _in_proj_dt(u, w_in, dt_bias):
    return _matmul(u, w_in, col0=COL_DT, n=N_HEADS, tm=1664, tn=N_HEADS, tk=1024, out_dtype=f32,
                   epilogue=_store(_softplus_bias), extras=(dt_bias.reshape(1, N_HEADS),),
                   extra_specs=(pl.BlockSpec((1, N_HEADS), lambda i, j, k: (0, 0)),), name="in_proj_dt")


def _in_proj_gates(u, w_in):
    return _matmul(u, w_in, col0=COL_G, n=2 * D_MODEL, tm=1664, tn=1024, tk=2048, out_dtype=bf16,
                   epilogue=_store(_sigmoid_tanh), name="in_proj_gates")


def _ssd_out_merge(vw, w_ssd_out, ssq, ma, gates, tm=1664, tn=1024):
    gb_off = D_MODEL // tn

    def merge(acc, ssq_t, ma_t, gb_t):
        rs = lax.rsqrt(ssq_t[:, :1] * (1.0 / D_INNER) + EPS)
        return ma_t.astype(f32) + gb_t.astype(f32) * (acc * rs)

    return _matmul(vw, w_ssd_out, col0=0, n=D_MODEL, tm=tm, tn=tn, tk=1024, out_dtype=bf16, epilogue=_store(merge),
                   extras=(ssq, ma, gates),
                   extra_specs=(pl.BlockSpec((tm, 128), lambda i, j, k: (i, 0)),
                                pl.BlockSpec((tm, tn), lambda i, j, k: (i, j)),
                                pl.BlockSpec((tm, tn), lambda i, j, k: (i, j + gb_off))),
                   name="ssd_out_merge")


def _out_proj(merged, w_out, head, x2, tm=1664, tn=1024):
    def add_residual(o_ref, acc_ref, head_ref, x_ref):
        i = pl.program_id(0)

        @pl.when(i == 0)
        def _():
            o_ref[0:CHUNK, :] = head_ref[...] + acc_ref[0:CHUNK, :]
            o_ref[CHUNK:tm, :] = x_ref[0:tm - CHUNK, :] + acc_ref[CHUNK:tm, :]

        @pl.when(i > 0)
        def _():
            o_ref[...] = x_ref[...] + acc_ref[...]

    return _matmul(merged, w_out, col0=0, n=D_MODEL, tm=tm, tn=tn, tk=1024, out_dtype=f32,
                   epilogue=add_residual, extras=(head, x2),
                   extra_specs=(pl.BlockSpec((CHUNK, tn), lambda i, j, k: (0, j)),
                                _x_rows_spec(tm, tn, lambda i, j, k: pl.multiple_of(j * tn, tn))),
                   name="out_proj_residual")


def _pool_kernel(x_ref, halo_ref, w_ref, scale_ref, ga_ref, o_ref, a_ref, b_ref, pooled_ref, *, tm):
    assert POOL_WINDOWS == (2, 4, 8, 16)
    gi = pl.program_id(0)
    i = pl.program_id(1)
    n = POOL_HALO + tm
    halo = halo_ref[...].astype(f32)
    a_ref[0:POOL_HALO, :] = jnp.where(i == 0, 0.0, halo)
    a_ref[POOL_HALO:n, :] = x_ref[...].astype(f32)
    t = i * tm + lax.broadcasted_iota(i32, (tm, 1), 0) - PAD_ROWS

    def pair_sum(src_ref, shift, start):
        return src_ref[start:n, :] + src_ref[start - shift:n - shift, :]

    def finish(s, win):
        cnt = jnp.clip(t + 1, 1, win).astype(f32)
        pooled_ref[...] = (s / cnt - x_ref[...].astype(f32)).astype(bf16)

    @pl.when(gi == 0)
    def _():
        finish(pair_sum(a_ref, 1, POOL_HALO), 2)

    @pl.when(gi == 1)
    def _():
        b_ref[8:n, :] = pair_sum(a_ref, 1, 8)
        finish(pair_sum(b_ref, 2, POOL_HALO), 4)

    @pl.when(gi == 2)
    def _():
        b_ref[8:n, :] = pair_sum(a_ref, 1, 8)
        a_ref[16:n, :] = pair_sum(b_ref, 2, 16)
        finish(pair_sum(a_ref, 4, POOL_HALO), 8)

    @pl.when(gi == 3)
    def _():
        b_ref[8:n, :] = pair_sum(a_ref, 1, 8)
        a_ref[16:n, :] = pair_sum(b_ref, 2, 16)
        b_ref[24:n, :] = pair_sum(a_ref, 4, 24)
        finish(pair_sum(b_ref, 8, POOL_HALO), 16)

    acc = jnp.dot(pooled_ref[...], w_ref[0].astype(bf16), preferred_element_type=f32)
    o_ref[...] = (ga_ref[...].astype(f32) * (acc * scale_ref[...])).astype(o_ref.dtype)


def _pool_branch(proj, gates, pool_w, pool_scale, tm=640):
    gw = POOL_GROUP
    return pl.pallas_call(
        functools.partial(_pool_kernel, tm=tm),
        grid=(4, R // tm),
        in_specs=[
            pl.BlockSpec((tm, gw), lambda g, i: (i, g)),
            pl.BlockSpec((POOL_HALO, gw), lambda g, i: (jnp.maximum(i * (tm // POOL_HALO) - 1, 0), g)),
            pl.BlockSpec((1, gw, gw), lambda g, i: (g, 0, 0)),
            pl.BlockSpec((1, gw), lambda g, i: (0, g)),
            pl.BlockSpec((tm, gw), lambda g, i: (i, g)),
        ],
        out_specs=pl.BlockSpec((tm, gw), lambda g, i: (i, g)),
        out_shape=jax.ShapeDtypeStruct((R, D_MODEL), bf16),
        scratch_shapes=[pltpu.VMEM((POOL_HALO + tm, gw), f32), pltpu.VMEM((POOL_HALO + tm, gw), f32),
                        pltpu.VMEM((tm, gw), bf16)],
        compiler_params=_params(("parallel", "arbitrary")),
        name="pool_branch",
    )(proj, proj, pool_w, pool_scale.reshape(1, D_MODEL), gates)


def _fill_ext(cur_ref, halo_ref, ext_ref, c):
    halo = halo_ref[...]
    ext_ref[0:HALO, :] = jnp.where(c == 0, jnp.zeros_like(halo), halo)
    ext_ref[HALO:HALO + CHUNK, :] = cur_ref[...]


def _conv_silu(ext_ref, sel, w_ref, b_ref, lanes):
    taps = jnp.dot(sel, ext_ref[:, lanes], preferred_element_type=f32)
    out = b_ref[:, lanes]
    for k in range(CONV_W):
        out = out + w_ref[k:k + 1, lanes] * taps[k * CHUNK:(k + 1) * CHUNK]
    return out * _sigmoid(out)


def _ssd_kernel(x_ref, xh_ref, bc_ref, bch_ref, z_ref, dt_ref, dtt_ref, alog_ref, alogt_ref,
                wx_ref, wbc_ref, bx_ref, bbc_ref, dskip_ref, nw_ref,
                vw_ref, ssq_ref,
                state_ref, extx_ref, extbc_ref):
    c = pl.program_id(0)

    @pl.when(c == 0)
    def _():
        state_ref[...] = jnp.zeros_like(state_ref)

    _fill_ext(x_ref, xh_ref, extx_ref, c)
    _fill_ext(bc_ref, bch_ref, extbc_ref, c)

    rows = c * CHUNK + lax.broadcasted_iota(i32, (CHUNK, 1), 0)
    row_ok = rows >= PAD_ROWS
    cols = c * CHUNK + lax.broadcasted_iota(i32, (1, CHUNK), 1)
    col_ok = cols >= PAD_ROWS
    li = lax.broadcasted_iota(i32, (CHUNK, CHUNK), 0)
    si = lax.broadcasted_iota(i32, (CHUNK, CHUNK), 1)
    causal = li >= si
    tril = causal.astype(f32)
    triu = (li <= si).astype(f32)
    lane = lax.broadcasted_iota(i32, (CHUNK, 2 * HEAD_DIM), 1)
    first = lane < HEAD_DIM
    first1 = lax.broadcasted_iota(i32, (1, 2 * HEAD_DIM), 1) < HEAD_DIM
    sr = lax.broadcasted_iota(i32, (CONV_W * CHUNK, HALO + CHUNK), 0)
    se = lax.broadcasted_iota(i32, (CONV_W * CHUNK, HALO + CHUNK), 1)
    sel = jnp.where(se == (sr % CHUNK) + (sr // CHUNK) + HALO - (CONV_W - 1), 1.0, 0.0).astype(bf16)
    hh = lax.broadcasted_iota(i32, (HEADS_PER_GROUP, GROUP_W), 0)
    hl = lax.broadcasted_iota(i32, (HEADS_PER_GROUP, GROUP_W), 1)
    head_lanes = jnp.where(hl // HEAD_DIM == hh, 1.0, 0.0).astype(bf16)

    def group(g, ssq):
        lo = pl.multiple_of(g * GROUP_W, GROUP_W)
        blo = pl.multiple_of(g * N_STATE, N_STATE)
        xs = _conv_silu(extx_ref, sel, wx_ref, bx_ref, pl.ds(lo, GROUP_W))
        bm = _conv_silu(extbc_ref, sel, wbc_ref, bbc_ref, pl.ds(blo, N_STATE))
        cm = _conv_silu(extbc_ref, sel, wbc_ref, bbc_ref, pl.ds(N_GROUPS * N_STATE + blo, N_STATE))

        dt = jnp.where(row_ok, dt_ref[g], 0.0)
        dtt = jnp.where(col_ok, dtt_ref[g], 0.0)
        a_dt = dt * (-jnp.exp(alog_ref[g]))
        a_dtt = dtt * (-jnp.exp(alogt_ref[g]))
        a_cs = jnp.dot(tril, a_dt, preferred_element_type=f32, precision=lax.Precision.HIGHEST)
        a_cst = jnp.dot(a_dtt, triu, preferred_element_type=f32, precision=lax.Precision.HIGHEST)
        last = a_cs[CHUNK - 1:CHUNK, :]
        e_cs = jnp.exp(a_cs)
        e_end = jnp.exp(last - a_cs)
        e_last = jnp.exp(last)
        a2 = a_cs * LOG2E
        a2t = a_cst * LOG2E

        bmb = bm.astype(bf16)
        cmb = cm.astype(bf16)
        cb = lax.dot_general(cmb, bmb, (((1,), (1,)), ((), ())), preferred_element_type=f32)
        state = state_ref[g]
        yoff = jnp.dot(cmb, state.astype(bf16), preferred_element_type=f32)

        scales = jnp.concatenate([dt, e_cs, e_end], axis=0).astype(bf16)
        spread = jnp.dot(scales, head_lanes, preferred_element_type=f32)
        xdt_g = xs * spread[0:CHUNK]
        xd2 = (xdt_g * spread[2 * CHUNK:3 * CHUNK]).astype(bf16)
        cd = []
        for q in range(HEADS_PER_GROUP // 2):
            ha, hb = 2 * q, 2 * q + 1
            sl = slice(q * 2 * HEAD_DIM, (q + 1) * 2 * HEAD_DIM)
            gl = pl.ds(pl.multiple_of(lo + q * 2 * HEAD_DIM, 2 * HEAD_DIM), 2 * HEAD_DIM)
            xq = xs[:, sl]
            xdtb = xdt_g[:, sl].astype(bf16)
            wa = cb * jnp.exp2(jnp.where(causal, a2[:, ha:ha + 1] - a2t[ha:ha + 1, :], -1e30))
            wb = cb * jnp.exp2(jnp.where(causal, a2[:, hb:hb + 1] - a2t[hb:hb + 1, :], -1e30))
            ya = jnp.dot(wa.astype(bf16), xdtb, preferred_element_type=f32)
            yb = jnp.dot(wb.astype(bf16), xdtb, preferred_element_type=f32)
            y = jnp.where(first, ya, yb) + yoff[:, sl] * spread[CHUNK:2 * CHUNK, sl] + dskip_ref[:, gl] * xq
            cd.append(jnp.where(first1, e_last[:, ha:ha + 1], e_last[:, hb:hb + 1]))
            zq = z_ref[:, gl].astype(f32)
            v = y * (zq * _sigmoid(zq))
            ssq = ssq + v * v
            vw_ref[:, gl] = (v * nw_ref[:, gl]).astype(vw_ref.dtype)

        bt = jnp.transpose(bm).astype(bf16)
        s_new = jnp.dot(bt, xd2, preferred_element_type=f32)
        state_ref[g] = state * jnp.concatenate(cd, axis=1) + s_new
        return ssq

    ssq = lax.fori_loop(0, N_GROUPS, group, jnp.zeros((CHUNK, 2 * HEAD_DIM), f32))
    ssq_ref[...] = jnp.broadcast_to(jnp.sum(ssq, axis=1, keepdims=True), (CHUNK, 128))


def _ssd_branch(proj, dt, conv_w, conv_b, a_log, d_skip, ssd_norm_w):
    bc_w = 2 * N_GROUPS * N_STATE
    dt_g = dt.reshape(R, N_GROUPS, HEADS_PER_GROUP).transpose(1, 0, 2)
    dt_gt = dt_g.transpose(0, 2, 1)
    alog = a_log.reshape(N_GROUPS, 1, HEADS_PER_GROUP)
    alogt = a_log.reshape(N_GROUPS, HEADS_PER_GROUP, 1)
    conv_b2 = conv_b.reshape(1, -1)
    dskip_e = jnp.repeat(d_skip, HEAD_DIM).reshape(1, D_INNER)

    def cur(col, width):
        return pl.BlockSpec((pl.Element(CHUNK), pl.Element(width)),
                            lambda c: (pl.multiple_of(c * CHUNK, CHUNK), col))

    def halo(col, width):
        return pl.BlockSpec((pl.Element(HALO), pl.Element(width)),
                            lambda c: (pl.multiple_of(jnp.maximum(c * CHUNK - HALO, 0), HALO), col))

    full = lambda shape: pl.BlockSpec(shape, lambda c: (0,) * len(shape))
    in_specs = [
        cur(COL_X, D_INNER), halo(COL_X, D_INNER),
        cur(COL_B, bc_w), halo(COL_B, bc_w),
        cur(COL_Z, D_INNER),
        pl.BlockSpec((N_GROUPS, CHUNK, HEADS_PER_GROUP), lambda c: (0, c, 0)),
        pl.BlockSpec((N_GROUPS, HEADS_PER_GROUP, CHUNK), lambda c: (0, 0, c)),
        full((N_GROUPS, 1, HEADS_PER_GROUP)),
        full((N_GROUPS, HEADS_PER_GROUP, 1)),
        pl.BlockSpec((CONV_W, D_INNER), lambda c: (0, 0)),
        pl.BlockSpec((CONV_W, bc_w), lambda c: (0, D_INNER // bc_w)),
        pl.BlockSpec((1, D_INNER), lambda c: (0, 0)),
        pl.BlockSpec((1, bc_w), lambda c: (0, D_INNER // bc_w)),
        full((1, D_INNER)),
        full((1, D_INNER)),
    ]
    return pl.pallas_call(
        _ssd_kernel,
        grid=(N_CHUNKS,),
        in_specs=in_specs,
        out_specs=[
            pl.BlockSpec((CHUNK, D_INNER), lambda c: (c, 0)),
            pl.BlockSpec((CHUNK, 128), lambda c: (c, 0)),
        ],
        out_shape=[
            jax.ShapeDtypeStruct((R, D_INNER), bf16),
            jax.ShapeDtypeStruct((R, 128), f32),
        ],
        scratch_shapes=[
            pltpu.VMEM((N_GROUPS, N_STATE, GROUP_W), f32),
            pltpu.VMEM((HALO + CHUNK, D_INNER), bf16),
            pltpu.VMEM((HALO + CHUNK, bc_w), bf16),
        ],
        compiler_params=_params(("arbitrary",)),
        name="ssd_mixer",
    )(proj, proj, proj, proj, proj, dt_g, dt_gt, alog, alogt,
      conv_w, conv_w, conv_b2, conv_b2, dskip_e, ssd_norm_w.reshape(1, D_INNER))


def _router_kernel(h_ref, nw_ref, rw_ref, rb_ref, u_ref, idx_ref, gate_ref, *, tm):
    i = pl.program_id(0)
    h = h_ref[...]
    ms = jnp.mean(h * h, axis=-1, keepdims=True)
    u = h * lax.rsqrt(ms + EPS) * nw_ref[...]
    rows = i * tm + lax.broadcasted_iota(i32, (tm, 1), 0)
    u = jnp.where(rows >= PAD_ROWS, u, 0.0)
    u_ref[...] = u
    logits =jnp.dot(u, rw_ref[...], preferred_element_type=f32, precision=lax.Precision.HIGHEST) + rb_ref[...]
    e_iota = lax.broadcasted_iota(i32, (tm, N_EXPERTS), 1)
    lane = lax.broadcasted_iota(i32, (tm, 128), 1)
    idx_out = jnp.zeros((tm, 128), i32)
    val_out = jnp.zeros((tm, 128), f32)
    vals = []
    for k in range(TOP_K):
        m = jnp.max(logits, axis=1, keepdims=True)
        sel = jnp.min(jnp.where(logits == m, e_iota, N_EXPERTS), axis=1, keepdims=True)
        vals.append(m)
        idx_out = jnp.where(lane == k, sel, idx_out)
        logits = jnp.where(e_iota == sel, -jnp.inf, logits)
    exps = [jnp.exp(v - vals[0]) for v in vals]
    denom = exps[0] + exps[1] + exps[2] + exps[3]
    for k in range(TOP_K):
        val_out = jnp.where(lane == k, exps[k] / denom, val_out)
    idx_ref[...] = idx_out
    gate_ref[...] = val_out


def _router(h1, norm_w, router_w, router_b, tm=320):
    return pl.pallas_call(
        functools.partial(_router_kernel, tm=tm),
        grid=(R // tm,),
        in_specs=[
            pl.BlockSpec((tm, D_MODEL), lambda i: (i, 0)),
            pl.BlockSpec((1, D_MODEL), lambda i: (0, 0)),
            pl.BlockSpec((D_MODEL, N_EXPERTS), lambda i: (0, 0)),
            pl.BlockSpec((1, N_EXPERTS), lambda i: (0, 0)),
        ],
        out_specs=[
            pl.BlockSpec((tm, D_MODEL), lambda i: (i, 0)),
            pl.BlockSpec((tm, 128), lambda i: (i, 0)),
            pl.BlockSpec((tm, 128), lambda i: (i, 0)),
        ],
        out_shape=[
            jax.ShapeDtypeStruct((R, D_MODEL), f32),
            jax.ShapeDtypeStruct((R, 128), i32),
            jax.ShapeDtypeStruct((R, 128), f32),
        ],
        compiler_params=_params(("parallel",)),
        name="router",
    )(h1, norm_w.reshape(1, D_MODEL), router_w, router_b.reshape(1, N_EXPERTS))


def _gather_x_kernel(pos_ref, nu_ref, nv_ref, src_ref, o_ref, idx_ref, buf_ref, sem, *, tm):
    i = pl.program_id(0)
    n_used = nu_ref[0]

    @pl.when(i == 0)
    def _():
        def invert(b, carry):
            for u in range(DMA_ISSUE_UNROLL):
                p = b * DMA_ISSUE_UNROLL + u
                idx_ref[pos_ref[p]] = lax.shift_right_logical(p, 2) + PAD_ROWS
            return carry

        lax.fori_loop(0, N_PAIRS // DMA_ISSUE_UNROLL, invert, 0)

    def row_copy(src_row, slot, r):
        return pltpu.make_async_copy(src_ref.at[pl.ds(src_row, 1)], buf_ref.at[slot, pl.ds(r, 1)], sem.at[slot])

    def issue_tile(step):
        slot = step % 2
        n_blocks = nv_ref[step] // DMA_ISSUE_UNROLL

        def issue_block(b, carry):
            for u in range(DMA_ISSUE_UNROLL):
                r = b * DMA_ISSUE_UNROLL + u
                row_copy(idx_ref[step * tm + r], slot, r).start()
            return carry

        def issue(r, carry):
            row_copy(idx_ref[step * tm + r], slot, r).start()
            return carry

        lax.fori_loop(0, n_blocks, issue_block, 0)
        lax.fori_loop(n_blocks * DMA_ISSUE_UNROLL, nv_ref[step], issue, 0)

    @pl.when(i == 0)
    def _():
        buf_ref[...] = jnp.zeros_like(buf_ref)
        issue_tile(0)

    @pl.when(i + 1 < n_used)
    def _():
        issue_tile(i + 1)

    @pl.when(i < n_used)
    def _():
        slot = i % 2

        @pl.when(nv_ref[i] == tm)
        def _():
            pltpu.make_async_copy(src_ref.at[pl.ds(0, tm)], buf_ref.at[slot], sem.at[slot]).wait()

        @pl.when(nv_ref[i] < tm)
        def _():
            def drain(r, carry):
                row_copy(0, slot, 0).wait()
                return carry

            lax.fori_loop(0, nv_ref[i], drain, 0)

        o_ref[...] = buf_ref[slot].astype(o_ref.dtype)

    @pl.when(i >= n_used)
    def _():
        o_ref[...] = jnp.zeros_like(o_ref)


def _gather_x(pos_flat, n_used, tile_rows, u2, tm=MOE_TM):
    assert TOP_K == 4
    return pl.pallas_call(
        functools.partial(_gather_x_kernel, tm=tm),
        grid_spec=pltpu.PrefetchScalarGridSpec(
            num_scalar_prefetch=3,
            grid=(XS_ROWS // tm,),
            in_specs=[pl.BlockSpec(memory_space=pl.ANY)],
            out_specs=pl.BlockSpec((tm, D_MODEL), lambda i, idx, nu, nv: (i, 0)),
            scratch_shapes=[pltpu.SMEM((XS_ROWS,), i32), pltpu.VMEM((2, tm, D_MODEL), f32),
                            pltpu.SemaphoreType.DMA((2,))],
        ),
        out_shape=jax.ShapeDtypeStruct((XS_ROWS, D_MODEL), bf16),
        compiler_params=_params(("arbitrary",)),
        name="gather_x",
    )(pos_flat, n_used, tile_rows, u2)


def _grouped_kernel(ge_ref, gt_ref, gn_ref, ng_ref, nu_ref, x_ref, *refs, n_w, tn, nj, ocols, tile_fn):
    w_refs, b_refs, o_ref = refs[:n_w], refs[n_w:2 * n_w], refs[2 * n_w]
    wb_ref, obuf_ref, zbuf_ref, sem, zsem = refs[2 * n_w + 1:]
    g = pl.program_id(0)
    j = pl.program_id(1)
    active = g < ng_ref[0]
    gc = jnp.minimum(g, ng_ref[0] - 1)
    n_tiles = jnp.where(active, gn_ref[gc], 0)
    tile0 = gt_ref[gc]
    col0 = pl.multiple_of(j * ocols, ocols)
    total_tiles = o_ref.shape[0] // MOE_TM

    @pl.when(jnp.logical_and(g == 0, j == 0))
    def _():
        zbuf_ref[...] = jnp.zeros_like(zbuf_ref)

        def zero_copy(t, jc):
            rows = pl.ds(pl.multiple_of(t * MOE_TM, MOE_TM), MOE_TM)
            return pltpu.make_async_copy(zbuf_ref, o_ref.at[rows, pl.ds(jc * ocols, ocols)], zsem)

        def z_start(t, carry):
            for jc in range(nj):
                zero_copy(t, jc).start()
            return carry

        def z_wait(t, carry):
            for jc in range(nj):
                zero_copy(t, jc).wait()
            return carry

        lax.fori_loop(nu_ref[0], total_tiles, z_start, 0)
        lax.fori_loop(nu_ref[0], total_tiles, z_wait, 0)

    @pl.when(active)
    def _():
        for k in range(n_w):
            wb_ref[:, k * tn:(k + 1) * tn] = w_refs[k][0].astype(bf16)

    step = g * nj + j
    slot = step % 2

    def out_copy(slot_, n, tile):
        rows = pl.ds(pl.multiple_of(tile * MOE_TM, MOE_TM), n * MOE_TM)
        return pltpu.make_async_copy(obuf_ref.at[slot_, pl.ds(0, n * MOE_TM)], o_ref.at[rows, pl.ds(col0, ocols)],
                                     sem.at[slot_])

    def wait_step(s):
        gs = s // nj
        sent = jnp.where(jnp.logical_and(s >= 0, gs < ng_ref[0]), gn_ref[jnp.clip(gs, 0, MAX_GROUPS - 1)], 0)
        for n in range(1, MOE_CAP + 1):

            @pl.when(sent == n)
            def _(n=n):
                out_copy(s % 2, n, 0).wait()

    wait_step(step - 2)
    for n in range(1, MOE_CAP + 1):

        @pl.when(n_tiles == n)
        def _(n=n):
            acc = jnp.dot(x_ref[0:n * MOE_TM, :], wb_ref[...], preferred_element_type=f32)
            obuf_ref[slot, pl.ds(0, n * MOE_TM)] = tile_fn(acc, [b[0] for b in b_refs])
            out_copy(slot, n, tile0).start()

    @pl.when(step == pl.num_programs(0) * nj - 1)
    def _():
        wait_step(step - 1)
        wait_step(step)


def _grouped_call(groups, n_used, x, ws, bs, *, k_dim, tn, nj, w_col_offsets, ocols, out_shape, tile_fn, name):
    g_expert, g_tile0, g_ntiles, n_groups = groups
    n_w = len(ws)

    def gi(g, ng):
        return jnp.minimum(g, ng[0] - 1)

    def x_map(g, j, ge, gt, gn, ng, nu):
        return (pl.multiple_of(gt[gi(g, ng)] * MOE_TM, MOE_TM), 0)

    def w_map(off):
        return lambda g, j, ge, gt, gn, ng, nu: (ge[gi(g, ng)], 0, off + jnp.where(g < ng[0], j, nj - 1))

    return pl.pallas_call(
        functools.partial(_grouped_kernel, n_w=n_w, tn=tn, nj=nj, ocols=ocols, tile_fn=tile_fn),
        grid_spec=pltpu.PrefetchScalarGridSpec(
            num_scalar_prefetch=5,
            grid=(n_groups[0], nj),
            in_specs=[
                pl.BlockSpec((pl.Element(GROUP_ROWS), pl.Element(k_dim)), x_map),
                *[pl.BlockSpec((1, k_dim, tn), w_map(off)) for off in w_col_offsets],
                *[pl.BlockSpec((1, 1, tn), w_map(off)) for off in w_col_offsets],
            ],
            out_specs=pl.BlockSpec(memory_space=pl.ANY),
            scratch_shapes=[
                pltpu.VMEM((k_dim, n_w * tn), bf16),
                pltpu.VMEM((2, GROUP_ROWS, ocols), out_shape.dtype),
                pltpu.VMEM((MOE_TM, ocols), out_shape.dtype),
                pltpu.SemaphoreType.DMA((2,)),
                pltpu.SemaphoreType.DMA(()),
            ],
        ),
        out_shape=out_shape,
        compiler_params=_params(("arbitrary", "arbitrary")),
        name=name,
    )(g_expert, g_tile0, g_ntiles, n_groups, n_used, x, *ws, *bs)


def _moe_up(groups, n_used, xs, w_up, b_up, tn=256):
    nj = D_EXPERT // tn

    def tile_fn(acc, biases):
        gate = jnp.minimum(acc[:, :tn] + biases[0], LIMIT)
        up = jnp.clip(acc[:, tn:] + biases[1], -LIMIT, LIMIT)
        return ((up + 1.0) * (gate * _sigmoid(ALPHA * gate))).astype(bf16)

    b3 = b_up.reshape(N_EXPERTS, 1, -1)
    return _grouped_call(groups, n_used, xs, (w_up, w_up), (b3, b3), k_dim=D_MODEL, tn=tn, nj=nj,
                         w_col_offsets=(0, nj), ocols=tn,
                         out_shape=jax.ShapeDtypeStruct((XS_ROWS, D_EXPERT), bf16), tile_fn=tile_fn, name="moe_up")


def _moe_down(groups, n_used, act, w_down, b_down, tn=1024):
    return _grouped_call(groups, n_used, act, (w_down,), (b_down.reshape(N_EXPERTS, 1, -1),), k_dim=D_EXPERT,
                         tn=tn, nj=D_MODEL // tn, w_col_offsets=(0,), ocols=tn,
                         out_shape=jax.ShapeDtypeStruct((N_SLOTS, D_MODEL), f32),
                         tile_fn=lambda acc, biases: acc + biases[0], name="moe_down")


def _combine_kernel(pos_ref, h_ref, g_ref, nw_ref, y_ref, o_ref, buf_ref, sem, *, tm):
    i = pl.program_id(0)
    n = pl.num_programs(0)

    def row_copy(src_row, slot, k, r):
        return pltpu.make_async_copy(y_ref.at[pl.ds(src_row, 1)], buf_ref.at[slot, k, pl.ds(r, 1)], sem.at[slot])

    def issue_tile(step):
        slot = step % 2
        for k in range(TOP_K):

            def issue(r, carry, k=k):
                row_copy(pos_ref[k * SEQ + step * tm + r], slot, k, r).start()
                return carry

            lax.fori_loop(0, tm, issue, 0, unroll=DMA_ISSUE_UNROLL)

    @pl.when(i == 0)
    def _():
        issue_tile(0)

    @pl.when(i + 1 < n)
    def _():
        issue_tile(i + 1)

    slot = i % 2
    for k in range(TOP_K):
        pltpu.make_async_copy(y_ref.at[pl.ds(0, tm)], buf_ref.at[slot, k], sem.at[slot]).wait()

    g = g_ref[...]
    h = h_ref[...]
    for k in range(TOP_K):
        h = h + g[:, k:k + 1] * buf_ref[slot, k]
    ms = jnp.mean(h * h, axis=-1, keepdims=True)
    o_ref[...] = h * lax.rsqrt(ms + EPS) * nw_ref[...]


def _combine(pos_kt, h1, y, gates, norm_w, tm=128):
    nb = SEQ // tm
    off = (PAD_ROWS + N_META) // tm
    return pl.pallas_call(
        functools.partial(_combine_kernel, tm=tm),
        grid_spec=pltpu.PrefetchScalarGridSpec(
            num_scalar_prefetch=1,
            grid=(nb,),
            in_specs=[
                pl.BlockSpec((tm, D_MODEL), lambda i, pos: (i + off, 0)),
                pl.BlockSpec((tm, 128), lambda i, pos: (i + off, 0)),
                pl.BlockSpec((1, D_MODEL), lambda i, pos: (0, 0)),
                pl.BlockSpec(memory_space=pl.ANY),
            ],
            out_specs=pl.BlockSpec((tm, D_MODEL), lambda i, pos: (i, 0)),
            scratch_shapes=[pltpu.VMEM((2, TOP_K, tm, D_MODEL), f32), pltpu.SemaphoreType.DMA((2,))],
        ),
        out_shape=jax.ShapeDtypeStruct((SEQ, D_MODEL), f32),
        compiler_params=_params(("arbitrary",)),
        name="combine_final_norm",
    )(pos_kt, h1, gates, norm_w.reshape(1, D_MODEL), y)


def _rank_kernel(idx_ref, rank_ref, cnt_ref, carry_ref, *, tm):
    i = pl.program_id(0)

    @pl.when(i == 0)
    def _():
        carry_ref[...] = jnp.zeros_like(carry_ref)

    idx = idx_ref[...]
    rows = i * tm + lax.broadcasted_iota(i32, (tm, 1), 0)
    valid = rows >= PAD_ROWS
    e_iota = lax.broadcasted_iota(i32, (tm, N_EXPERTS), 1)
    hits = [jnp.logical_and(idx[:, k:k + 1] == e_iota, valid) for k in range(TOP_K)]
    onehot = sum(jnp.where(h, 1.0, 0.0) for h in hits)
    earlier = lax.broadcasted_iota(i32, (tm, tm), 0) > lax.broadcasted_iota(i32, (tm, tm), 1)
    before = jnp.dot(jnp.where(earlier, 1.0, 0.0).astype(bf16), onehot.astype(bf16),
                     preferred_element_type=f32) + carry_ref[...]
    lane = lax.broadcasted_iota(i32, (tm, 128), 1)
    out = jnp.zeros((tm, 128), i32)
    for k in range(TOP_K):
        rk = jnp.sum(jnp.where(hits[k], before, 0.0), axis=1, keepdims=True)
        out = jnp.where(lane == k, rk.astype(i32), out)
    rank_ref[...] = out
    carry_ref[...] += jnp.sum(onehot, axis=0, keepdims=True)
    cnt_ref[...] = carry_ref[...]


def _rank_pairs(top_idx, tm=320):
    return pl.pallas_call(
        functools.partial(_rank_kernel, tm=tm),
        grid=(R // tm,),
        in_specs=[pl.BlockSpec((tm, 128), lambda i: (i, 0))],
        out_specs=[pl.BlockSpec((tm, 128), lambda i: (i, 0)), pl.BlockSpec((1, N_EXPERTS), lambda i: (0, 0))],
        out_shape=[jax.ShapeDtypeStruct((R, 128), i32), jax.ShapeDtypeStruct((1, N_EXPERTS), f32)],
        scratch_shapes=[pltpu.VMEM((1, N_EXPERTS), f32)],
        compiler_params=_params(("arbitrary",)),
        name="rank_pairs",
    )(top_idx)


def _routing_tables(top_idx, rank, counts):
    counts = counts.reshape(N_EXPERTS).astype(i32)
    tiles = (counts + MOE_TM - 1) // MOE_TM
    tile_end = jnp.cumsum(tiles)
    tile_start = tile_end - tiles
    e_ids = jnp.arange(N_EXPERTS, dtype=i32)
    tok_e = top_idx[PAD_ROWS:, :TOP_K]
    tok_start = jnp.sum(jnp.where(tok_e[..., None] == e_ids, tile_start * MOE_TM, 0), axis=-1)
    pos = tok_start + rank[PAD_ROWS:, :TOP_K]
    n_used = tile_end[-1:].astype(i32)
    tile_ids = jnp.arange(XS_ROWS // MOE_TM, dtype=i32)
    tile_e = jnp.minimum(jnp.sum(tile_end[None, :] <= tile_ids[:, None], axis=1), N_EXPERTS - 1)
    tile_rows = jnp.clip(counts[tile_e] - (tile_ids - tile_start[tile_e]) * MOE_TM, 0, MOE_TM).astype(i32)
    tile_rows = jnp.where(tile_ids < n_used[0], tile_rows, 0)
    groups = (tiles + MOE_CAP - 1) // MOE_CAP
    group_end = jnp.cumsum(groups)
    g_ids = jnp.arange(MAX_GROUPS, dtype=i32)
    g_expert = jnp.minimum(jnp.sum(group_end[None, :] <= g_ids[:, None], axis=1), N_EXPERTS - 1).astype(i32)
    g_local = g_ids - (group_end - groups)[g_expert]
    g_tile0 = (tile_start[g_expert] + g_local * MOE_CAP).astype(i32)
    g_ntiles = jnp.clip(tiles[g_expert] - g_local * MOE_CAP, 0, MOE_CAP).astype(i32)
    n_groups = group_end[-1:].astype(i32)
    return pos, n_used, tile_rows, (g_expert, g_tile0, g_ntiles, n_groups)


def kernel(x, meta_tokens, norm_mix_w, w_in, conv_w, conv_b, dt_bias, a_log, d_skip, ssd_norm_w, w_ssd_out,
           pool_w, pool_scale, w_out, norm_ffn_w, router_w, router_b, w_up, b_up, w_down, b_down, norm_final_w):
    assert x.shape == (1, SEQ, D_MODEL) and norm_mix_w.shape[0] == 1
    x2 = x[0]
    head = jnp.concatenate([jnp.zeros((PAD_ROWS, D_MODEL), f32), meta_tokens.astype(f32)], axis=0)

    u = _rms_norm_rows(x2, head, norm_mix_w[0])
    proj = _in_proj_main(u, w_in[0])
    dt = _in_proj_dt(u, w_in[0], dt_bias[0])
    gates = _in_proj_gates(u, w_in[0])
    ma = _pool_branch(proj, gates, pool_w[0], pool_scale[0])
    vw, ssq = _ssd_branch(proj, dt, conv_w[0], conv_b[0], a_log[0], d_skip[0], ssd_norm_w[0])
    merged = _ssd_out_merge(vw, w_ssd_out[0], ssq, ma, gates)
    h1 = _out_proj(merged, w_out[0], head, x2)

    u2, top_idx, top_gate = _router(h1, norm_ffn_w[0], router_w[0], router_b[0])
    rank, counts = _rank_pairs(top_idx)
    pos, n_used, tile_rows, groups = _routing_tables(top_idx, rank, counts)
    xs = _gather_x(pos.reshape(-1), n_used, tile_rows, u2)
    act = _moe_up(groups, n_used, xs, w_up[0], b_up[0])
    y = _moe_down(groups, n_used, act, w_down[0], b_down[0])
    pos_kt = pos[N_META:].T.reshape(-1)
    out = _combine(pos_kt, h1, y, top_gate, norm_final_w)
    return out[None]
```

```python
import functools

import jax
import jax.numpy as jnp
from jax import lax
from jax.experimental import pallas as pl
from jax.experimental.pallas import tpu as pltpu

f32 = jnp.float32
bf16 = jnp.bfloat16
i32 = jnp.int32

D_MODEL = 4096
SEQ = 8192
N_META = 16
CHUNK = 128
PAD_ROWS = CHUNK - N_META
R = PAD_ROWS + N_META + SEQ
N_CHUNKS = R // CHUNK
POOL_WINDOWS = (2, 4, 8, 16)
POOL_GROUP = D_MODEL // 4
D_INNER = 2 * D_MODEL
HEAD_DIM = 64
N_HEADS = D_INNER // HEAD_DIM
N_STATE = 128
N_GROUPS = 8
HEADS_PER_GROUP = N_HEADS // N_GROUPS
GROUP_W = HEADS_PER_GROUP * HEAD_DIM
CONV_W = 4
HALO = 16
POOL_HALO = 2 * max(POOL_WINDOWS)
N_EXPERTS = 32
TOP_K = 4
D_EXPERT = 7 * D_MODEL // 16
LIMIT = 7.0
ALPHA = 1.702
EPS = 1e-5
LOG2E = 1.4426950408889634
N_TOK = N_META + SEQ
N_PAIRS = N_TOK * TOP_K
MOE_TM = 256
N_TILES = -(-N_PAIRS // MOE_TM) + N_EXPERTS
N_SLOTS = N_TILES * MOE_TM
MOE_CAP = 5
GROUP_ROWS = MOE_CAP * MOE_TM
MAX_GROUPS = (N_TILES - 1 + N_EXPERTS * (MOE_CAP - 1)) // MOE_CAP
XS_ROWS = N_SLOTS + GROUP_ROWS
COL_Z = D_MODEL
COL_X = COL_Z + D_INNER
COL_B = COL_X + D_INNER
COL_C = COL_B + N_GROUPS * N_STATE
COL_DT = COL_C + N_GROUPS * N_STATE
COL_G = COL_DT + N_HEADS
SEG1_COLS = COL_DT
LANES = 128
V7X_VMEM_LIMIT = 56 * 1024 * 1024
DMA_ISSUE_UNROLL = 8


def _params(sem, vmem=V7X_VMEM_LIMIT):
    return pltpu.CompilerParams(dimension_semantics=sem, vmem_limit_bytes=vmem)


def _sigmoid(v):
    return 1.0 / (1.0 + jnp.exp(-v))


def _sigmoid_tanh(v):
    return 0.5 * jnp.tanh(0.5 * v) + 0.5


def _x_rows_spec(tm, tn, col_of):
    return pl.BlockSpec((pl.Element(tm), pl.Element(tn)),
                        lambda i, *rest: (pl.multiple_of(jnp.maximum(i * tm - CHUNK, 0), CHUNK), col_of(i, *rest)))


def _rms_kernel(x_ref, head_ref, w_ref, o_ref):
    i = pl.program_id(0)
    tm = o_ref.shape[0]

    def norm(h):
        ms = jnp.mean(h * h, axis=-1, keepdims=True)
        return (h * lax.rsqrt(ms + EPS) * w_ref[...]).astype(o_ref.dtype)

    @pl.when(i == 0)
    def _():
        o_ref[0:CHUNK, :] = norm(head_ref[...])
        o_ref[CHUNK:tm, :] = norm(x_ref[0:tm - CHUNK, :])

    @pl.when(i > 0)
    def _():
        o_ref[...] = norm(x_ref[...])


def _rms_norm_rows(x2, head, w, tm=640):
    d = x2.shape[1]
    return pl.pallas_call(
        _rms_kernel,
        grid=(R // tm,),
        in_specs=[_x_rows_spec(tm, d, lambda i: 0), pl.BlockSpec((CHUNK, d), lambda i: (0, 0)),
                  pl.BlockSpec((1, d), lambda i: (0, 0))],
        out_specs=pl.BlockSpec((tm, d), lambda i: (i, 0)),
        out_shape=jax.ShapeDtypeStruct((R, d), bf16),
        compiler_params=_params(("parallel",)),
        name="rms_mix",
    )(x2, head, w.reshape(1, d))


def _mm_kernel(*refs, n_extra, epilogue, nk):
    a_ref, w_ref = refs[0], refs[1]
    extra = refs[2:2 + n_extra]
    o_ref = refs[2 + n_extra]
    acc_ref = refs[3 + n_extra]
    k = pl.program_id(2)

    @pl.when(k == 0)
    def _():
        acc_ref[...] = jnp.dot(a_ref[...], w_ref[...].astype(bf16), preferred_element_type=f32)

    @pl.when(k > 0)
    def _():
        acc_ref[...] += jnp.dot(a_ref[...], w_ref[...].astype(bf16), preferred_element_type=f32)

    @pl.when(k == nk - 1)
    def _():
        epilogue(o_ref, acc_ref, *extra)


def _store(fn):
    def epilogue(o_ref, acc_ref, *extra):
        o_ref[...] = fn(acc_ref[...], *[e[...] for e in extra]).astype(o_ref.dtype)

    return epilogue


def _matmul(a, w, *, col0, n, tm, tn, tk, out_dtype, epilogue, extras=(), extra_specs=(), name):
    m, kdim = a.shape
    nk = kdim // tk
    if col0 % tn == 0:
        w_spec = pl.BlockSpec((tk, tn), lambda i, j, k: (k, j + col0 // tn))
    else:
        w_spec = pl.BlockSpec((pl.Element(tk), pl.Element(tn)), lambda i, j, k: (k * tk, pl.multiple_of(col0 + j * tn, LANES)))
    return pl.pallas_call(
        functools.partial(_mm_kernel, n_extra=len(extras), epilogue=epilogue, nk=nk),
        grid=(m // tm, n // tn, nk),
        in_specs=[pl.BlockSpec((tm, tk), lambda i, j, k: (i, k)), w_spec, *extra_specs],
        out_specs=pl.BlockSpec((tm, tn), lambda i, j, k: (i, j)),
        out_shape=jax.ShapeDtypeStruct((m, n), out_dtype),
        scratch_shapes=[pltpu.VMEM((tm, tn), f32)],
        compiler_params=_params(("parallel", "parallel", "arbitrary")),
        name=name,
    )(a, w, *extras)


def _softplus_bias(acc, bias):
    v = acc + bias
    return jnp.maximum(v, 0.0) + jnp.log1p(jnp.exp(-jnp.abs(v)))


def _in_proj_main(u, w_in):
    return _matmul(u, w_in, col0=0, n=SEG1_COLS, tm=1664, tn=1024, tk=2048, out_dtype=bf16,
                   epilogue=_store(lambda acc: acc), name="in_proj_main")


def _in_proj_dt(u, w_in, dt_bias):
    return _matmul(u, w_in, col0=COL_DT, n=N_HEADS, tm=1664, tn=N_HEADS, tk=1024, out_dtype=f32,
                   epilogue=_store(_softplus_bias), extras=(dt_bias.reshape(1, N_HEADS),),
                   extra_specs=(pl.BlockSpec((1, N_HEADS), lambda i, j, k: (0, 0)),), name="in_proj_dt")


def _in_proj_gates(u, w_in):
    return _matmul(u, w_in, col0=COL_G, n=2 * D_MODEL, tm=1664, tn=1024, tk=2048, out_dtype=bf16,
                   epilogue=_store(_sigmoid_tanh), name="in_proj_gates")


def _ssd_out_merge(vw, w_ssd_out, ssq, ma, gates, tm=1664, tn=1024):
    gb_off = D_MODEL // tn

    def merge(acc, ssq_t, ma_t, gb_t):
        rs = lax.rsqrt(ssq_t[:, :1] * (1.0 / D_INNER) + EPS)
        return ma_t.astype(f32) + gb_t.astype(f32) * (acc * rs)

    return _matmul(vw, w_ssd_out, col0=0, n=D_MODEL, tm=tm, tn=tn, tk=1024, out_dtype=bf16, epilogue=_store(merge),
                   extras=(ssq, ma, gates),
                   extra_specs=(pl.BlockSpec((tm, LANES), lambda i, j, k: (i, 0)),
                                pl.BlockSpec((tm, tn), lambda i, j, k: (i, j)),
                                pl.BlockSpec((tm, tn), lambda i, j, k: (i, j + gb_off))),
                   name="ssd_out_merge")


def _out_proj(merged, w_out, head, x2, tm=1664, tn=1024):
    def add_residual(o_ref, acc_ref, head_ref, x_ref):
        i = pl.program_id(0)

        @pl.when(i == 0)
        def _():
            o_ref[0:CHUNK, :] = head_ref[...] + acc_ref[0:CHUNK, :]
            o_ref[CHUNK:tm, :] = x_ref[0:tm - CHUNK, :] + acc_ref[CHUNK:tm, :]

        @pl.when(i > 0)
        def _():
            o_ref[...] = x_ref[...] + acc_ref[...]

    return _matmul(merged, w_out, col0=0, n=D_MODEL, tm=tm, tn=tn, tk=1024, out_dtype=f32,
                   epilogue=add_residual, extras=(head, x2),
                   extra_specs=(pl.BlockSpec((CHUNK, tn), lambda i, j, k: (0, j)),
                                _x_rows_spec(tm, tn, lambda i, j, k: pl.multiple_of(j * tn, tn))),
                   name="out_proj_residual")


def _pool_kernel(x_ref, halo_ref, w_ref, scale_ref, ga_ref, o_ref, a_ref, b_ref, pooled_ref, *, tm):
    assert POOL_WINDOWS == (2, 4, 8, 16)
    gi = pl.program_id(0)
    i = pl.program_id(1)
    n = POOL_HALO + tm
    halo = halo_ref[...].astype(f32)
    a_ref[0:POOL_HALO, :] = jnp.where(i == 0, 0.0, halo)
    a_ref[POOL_HALO:n, :] = x_ref[...].astype(f32)
    t = i * tm + lax.broadcasted_iota(i32, (tm, 1), 0) - PAD_ROWS

    def pair_sum(src_ref, shift, start):
        return src_ref[start:n, :] + src_ref[start - shift:n - shift, :]

    def finish(s, win):
        cnt = jnp.clip(t + 1, 1, win).astype(f32)
        pooled_ref[...] = (s / cnt - x_ref[...].astype(f32)).astype(bf16)

    @pl.when(gi == 0)
    def _():
        finish(pair_sum(a_ref, 1, POOL_HALO), 2)

    @pl.when(gi == 1)
    def _():
        b_ref[8:n, :] = pair_sum(a_ref, 1, 8)
        finish(pair_sum(b_ref, 2, POOL_HALO), 4)

    @pl.when(gi == 2)
    def _():
        b_ref[8:n, :] = pair_sum(a_ref, 1, 8)
        a_ref[16:n, :] = pair_sum(b_ref, 2, 16)
        finish(pair_sum(a_ref, 4, POOL_HALO), 8)

    @pl.when(gi == 3)
    def _():
        b_ref[8:n, :] = pair_sum(a_ref, 1, 8)
        a_ref[16:n, :] = pair_sum(b_ref, 2, 16)
        b_ref[24:n, :] = pair_sum(a_ref, 4, 24)
        finish(pair_sum(b_ref, 8, POOL_HALO), 16)

    acc = jnp.dot(pooled_ref[...], w_ref[0].astype(bf16), preferred_element_type=f32)
    o_ref[...] = (ga_ref[...].astype(f32) * (acc * scale_ref[...])).astype(o_ref.dtype)


def _pool_branch(proj, gates, pool_w, pool_scale, tm=1664):
    gw = POOL_GROUP
    return pl.pallas_call(
        functools.partial(_pool_kernel, tm=tm),
        grid=(4, R // tm),
        in_specs=[
            pl.BlockSpec((tm, gw), lambda g, i: (i, g)),
            pl.BlockSpec((POOL_HALO, gw), lambda g, i: (jnp.maximum(i * (tm // POOL_HALO) - 1, 0), g)),
            pl.BlockSpec((1, gw, gw), lambda g, i: (g, 0, 0)),
            pl.BlockSpec((1, gw), lambda g, i: (0, g)),
            pl.BlockSpec((tm, gw), lambda g, i: (i, g)),
        ],
        out_specs=pl.BlockSpec((tm, gw), lambda g, i: (i, g)),
        out_shape=jax.ShapeDtypeStruct((R, D_MODEL), bf16),
        scratch_shapes=[pltpu.VMEM((POOL_HALO + tm, gw), f32), pltpu.VMEM((POOL_HALO + tm, gw), f32),
                        pltpu.VMEM((tm, gw), bf16)],
        compiler_params=_params(("parallel", "arbitrary")),
        name="pool_branch",
    )(proj, proj, pool_w, pool_scale.reshape(1, D_MODEL), gates)


def _fill_ext(cur_ref, halo_ref, ext_ref, c):
    halo = halo_ref[...]
    ext_ref[0:HALO, :] = jnp.where(c == 0, jnp.zeros_like(halo), halo)
    ext_ref[HALO:HALO + CHUNK, :] = cur_ref[...]


def _conv_silu(ext_ref, sel, w_ref, b_ref, lanes):
    taps = jnp.dot(sel, ext_ref[:, lanes], preferred_element_type=f32)
    out = b_ref[:, lanes]
    for k in range(CONV_W):
        out = out + w_ref[k:k + 1, lanes] * taps[k * CHUNK:(k + 1) * CHUNK]
    return out * _sigmoid(out)


def _ssd_kernel(x_ref, xh_ref, bc_ref, bch_ref, z_ref, dt_ref, dtt_ref, alog_ref, alogt_ref,
                wx_ref, wbc_ref, bx_ref, bbc_ref, dskip_ref, nw_ref,
                vw_ref, ssq_ref,
                state_ref, extx_ref, extbc_ref):
    c = pl.program_id(0)

    @pl.when(c == 0)
    def _():
        state_ref[...] = jnp.zeros_like(state_ref)

    _fill_ext(x_ref, xh_ref, extx_ref, c)
    _fill_ext(bc_ref, bch_ref, extbc_ref, c)

    rows = c * CHUNK + lax.broadcasted_iota(i32, (CHUNK, 1), 0)
    row_ok = rows >= PAD_ROWS
    cols = c * CHUNK + lax.broadcasted_iota(i32, (1, CHUNK), 1)
    col_ok = cols >= PAD_ROWS
    li = lax.broadcasted_iota(i32, (CHUNK, CHUNK), 0)
    si = lax.broadcasted_iota(i32, (CHUNK, CHUNK), 1)
    causal = li >= si
    tril = causal.astype(f32)
    triu = (li <= si).astype(f32)
    lane = lax.broadcasted_iota(i32, (CHUNK, 2 * HEAD_DIM), 1)
    first = lane < HEAD_DIM
    first1 = lax.broadcasted_iota(i32, (1, 2 * HEAD_DIM), 1) < HEAD_DIM
    sr = lax.broadcasted_iota(i32, (CONV_W * CHUNK, HALO + CHUNK), 0)
    se = lax.broadcasted_iota(i32, (CONV_W * CHUNK, HALO + CHUNK), 1)
    sel = jnp.where(se == (sr % CHUNK) + (sr // CHUNK) + HALO - (CONV_W - 1), 1.0, 0.0).astype(bf16)
    hh = lax.broadcasted_iota(i32, (HEADS_PER_GROUP, GROUP_W), 0)
    hl = lax.broadcasted_iota(i32, (HEADS_PER_GROUP, GROUP_W), 1)
    head_lanes = jnp.where(hl // HEAD_DIM == hh, 1.0, 0.0).astype(bf16)

    def group(g, ssq):
        lo = pl.multiple_of(g * GROUP_W, GROUP_W)
        blo = pl.multiple_of(g * N_STATE, N_STATE)
        xs = _conv_silu(extx_ref, sel, wx_ref, bx_ref, pl.ds(lo, GROUP_W))
        bm = _conv_silu(extbc_ref, sel, wbc_ref, bbc_ref, pl.ds(blo, N_STATE))
        cm = _conv_silu(extbc_ref, sel, wbc_ref, bbc_ref, pl.ds(N_GROUPS * N_STATE + blo, N_STATE))

        dt = jnp.where(row_ok, dt_ref[g], 0.0)
        dtt = jnp.where(col_ok, dtt_ref[g], 0.0)
        a_dt = dt * (-jnp.exp(alog_ref[g]))
        a_dtt = dtt * (-jnp.exp(alogt_ref[g]))
        a_cs = jnp.dot(tril, a_dt, preferred_element_type=f32, precision=lax.Precision.HIGHEST)
        a_cst = jnp.dot(a_dtt, triu, preferred_element_type=f32, precision=lax.Precision.HIGHEST)
        last = a_cs[CHUNK - 1:CHUNK, :]
        e_cs = jnp.exp(a_cs)
        e_end = jnp.exp(last - a_cs)
        e_last = jnp.exp(last)
        a2 = a_cs * LOG2E
        a2t = a_cst * LOG2E

        bmb = bm.astype(bf16)
        cmb = cm.astype(bf16)
        cb = lax.dot_general(cmb, bmb, (((1,), (1,)), ((), ())), preferred_element_type=f32)
        state = state_ref[g]
        yoff = jnp.dot(cmb, state.astype(bf16), preferred_element_type=f32)

        scales = jnp.concatenate([dt, e_cs, e_end], axis=0).astype(bf16)
        spread = jnp.dot(scales, head_lanes, preferred_element_type=f32)
        xdt_g = xs * spread[0:CHUNK]
        xd2 = (xdt_g * spread[2 * CHUNK:3 * CHUNK]).astype(bf16)
        cd = []
        for q in range(HEADS_PER_GROUP // 2):
            ha, hb = 2 * q, 2 * q + 1
            sl = slice(q * 2 * HEAD_DIM, (q + 1) * 2 * HEAD_DIM)
            gl = pl.ds(pl.multiple_of(lo + q * 2 * HEAD_DIM, 2 * HEAD_DIM), 2 * HEAD_DIM)
            xq = xs[:, sl]
            xdtb = xdt_g[:, sl].astype(bf16)
            wa = cb * jnp.exp2(jnp.where(causal, a2[:, ha:ha + 1] - a2t[ha:ha + 1, :], -1e30))
            wb = cb * jnp.exp2(jnp.where(causal, a2[:, hb:hb + 1] - a2t[hb:hb + 1, :], -1e30))
            ya = jnp.dot(wa.astype(bf16), xdtb, preferred_element_type=f32)
            yb = jnp.dot(wb.astype(bf16), xdtb, preferred_element_type=f32)
            y = jnp.where(first, ya, yb) + yoff[:, sl] * spread[CHUNK:2 * CHUNK, sl] + dskip_ref[:, gl] * xq
            cd.append(jnp.where(first1, e_last[:, ha:ha + 1], e_last[:, hb:hb + 1]))
            zq = z_ref[:, gl].astype(f32)
            v = y * (zq * _sigmoid(zq))
            ssq = ssq + v * v
            vw_ref[:, gl] = (v * nw_ref[:, gl]).astype(vw_ref.dtype)

        bt = jnp.transpose(bm).astype(bf16)
        s_new = jnp.dot(bt, xd2, preferred_element_type=f32)
        state_ref[g] = state * jnp.concatenate(cd, axis=1) + s_new
        return ssq

    ssq = lax.fori_loop(0, N_GROUPS, group, jnp.zeros((CHUNK, 2 * HEAD_DIM), f32))
    ssq_ref[...] = jnp.broadcast_to(jnp.sum(ssq, axis=1, keepdims=True), (CHUNK, LANES))


def _ssd_branch(proj, dt, conv_w, conv_b, a_log, d_skip, ssd_norm_w):
    bc_w = 2 * N_GROUPS * N_STATE
    dt_g = dt.reshape(R, N_GROUPS, HEADS_PER_GROUP).transpose(1, 0, 2)
    dt_gt = dt_g.transpose(0, 2, 1)
    alog = a_log.reshape(N_GROUPS, 1, HEADS_PER_GROUP)
    alogt = a_log.reshape(N_GROUPS, HEADS_PER_GROUP, 1)
    conv_b2 = conv_b.reshape(1, -1)
    dskip_e = jnp.repeat(d_skip, HEAD_DIM).reshape(1, D_INNER)

    def cur(col, width):
        return pl.BlockSpec((pl.Element(CHUNK), pl.Element(width)),
                            lambda c: (pl.multiple_of(c * CHUNK, CHUNK), col))

    def halo(col, width):
        return pl.BlockSpec((pl.Element(HALO), pl.Element(width)),
                            lambda c: (pl.multiple_of(jnp.maximum(c * CHUNK - HALO, 0), HALO), col))

    full = lambda shape: pl.BlockSpec(shape, lambda c: (0,) * len(shape))
    in_specs = [
        cur(COL_X, D_INNER), halo(COL_X, D_INNER),
        cur(COL_B, bc_w), halo(COL_B, bc_w),
        cur(COL_Z, D_INNER),
        pl.BlockSpec((N_GROUPS, CHUNK, HEADS_PER_GROUP), lambda c: (0, c, 0)),
        pl.BlockSpec((N_GROUPS, HEADS_PER_GROUP, CHUNK), lambda c: (0, 0, c)),
        full((N_GROUPS, 1, HEADS_PER_GROUP)),
        full((N_GROUPS, HEADS_PER_GROUP, 1)),
        pl.BlockSpec((CONV_W, D_INNER), lambda c: (0, 0)),
        pl.BlockSpec((CONV_W, bc_w), lambda c: (0, D_INNER // bc_w)),
        pl.BlockSpec((1, D_INNER), lambda c: (0, 0)),
        pl.BlockSpec((1, bc_w), lambda c: (0, D_INNER // bc_w)),
        full((1, D_INNER)),
        full((1, D_INNER)),
    ]
    return pl.pallas_call(
        _ssd_kernel,
        grid=(N_CHUNKS,),
        in_specs=in_specs,
        out_specs=[
            pl.BlockSpec((CHUNK, D_INNER), lambda c: (c, 0)),
            pl.BlockSpec((CHUNK, LANES), lambda c: (c, 0)),
        ],
        out_shape=[
            jax.ShapeDtypeStruct((R, D_INNER), bf16),
            jax.ShapeDtypeStruct((R, LANES), f32),
        ],
        scratch_shapes=[
            pltpu.VMEM((N_GROUPS, N_STATE, GROUP_W), f32),
            pltpu.VMEM((HALO + CHUNK, D_INNER), bf16),
            pltpu.VMEM((HALO + CHUNK, bc_w), bf16),
        ],
        compiler_params=_params(("arbitrary",)),
        name="ssd_mixer",
    )(proj, proj, proj, proj, proj, dt_g, dt_gt, alog, alogt,
      conv_w, conv_w, conv_b2, conv_b2, dskip_e, ssd_norm_w.reshape(1, D_INNER))


def _router_kernel(h_ref, nw_ref, rw_ref, rb_ref, u_ref, idx_ref, gate_ref, *, tm):
    i = pl.program_id(0)
    h = h_ref[...]
    ms = jnp.mean(h * h, axis=-1, keepdims=True)
    u = h * lax.rsqrt(ms + EPS) * nw_ref[...]
    rows = i * tm + lax.broadcasted_iota(i32, (tm, 1), 0)
    u = jnp.where(rows >= PAD_ROWS, u, 0.0)
    u_ref[...] = u
    logits = jnp.dot(u, rw_ref[...], preferred_element_type=f32, precision=lax.Precision.HIGHEST) + rb_ref[...]
    e_iota = lax.broadcasted_iota(i32, (tm, N_EXPERTS), 1)
    lane = lax.broadcasted_iota(i32, (tm, LANES), 1)
    idx_out = jnp.zeros((tm, LANES), i32)
    val_out = jnp.zeros((tm, LANES), f32)
    vals = []
    for k in range(TOP_K):
        m = jnp.max(logits, axis=1, keepdims=True)
        sel = jnp.min(jnp.where(logits == m, e_iota, N_EXPERTS), axis=1, keepdims=True)
        vals.append(m)
        idx_out = jnp.where(lane == k, sel, idx_out)
        logits = jnp.where(e_iota == sel, -jnp.inf, logits)
    exps = [jnp.exp(v - vals[0]) for v in vals]
    denom = exps[0] + exps[1] + exps[2] + exps[3]
    for k in range(TOP_K):
        val_out = jnp.where(lane == k, exps[k] / denom, val_out)
    idx_ref[...] = idx_out
    gate_ref[...] = val_out


def _router(h1, norm_w, router_w, router_b, tm=320):
    return pl.pallas_call(
        functools.partial(_router_kernel, tm=tm),
        grid=(R // tm,),
        in_specs=[
            pl.BlockSpec((tm, D_MODEL), lambda i: (i, 0)),
            pl.BlockSpec((1, D_MODEL), lambda i: (0, 0)),
            pl.BlockSpec((D_MODEL, N_EXPERTS), lambda i: (0, 0)),
            pl.BlockSpec((1, N_EXPERTS), lambda i: (0, 0)),
        ],
        out_specs=[
            pl.BlockSpec((tm, D_MODEL), lambda i: (i, 0)),
            pl.BlockSpec((tm, LANES), lambda i: (i, 0)),
            pl.BlockSpec((tm, LANES), lambda i: (i, 0)),
        ],
        out_shape=[
            jax.ShapeDtypeStruct((R, D_MODEL), f32),
            jax.ShapeDtypeStruct((R, LANES), i32),
            jax.ShapeDtypeStruct((R, LANES), f32),
        ],
        compiler_params=_params(("parallel",)),
        name="router",
    )(h1, norm_w.reshape(1, D_MODEL), router_w, router_b.reshape(1, N_EXPERTS))


def _gather_x_kernel(pos_ref, nu_ref, nv_ref, src_ref, o_ref, idx_ref, buf_ref, sem, *, tm):
    i = pl.program_id(0)
    n_used = nu_ref[0]

    @pl.when(i == 0)
    def _():
        def invert(b, carry):
            for u in range(DMA_ISSUE_UNROLL):
                p = b * DMA_ISSUE_UNROLL + u
                idx_ref[pos_ref[p]] = lax.shift_right_logical(p, 2) + PAD_ROWS
            return carry

        lax.fori_loop(0, N_PAIRS // DMA_ISSUE_UNROLL, invert, 0)

    def row_copy(src_row, slot, r):
        return pltpu.make_async_copy(src_ref.at[pl.ds(src_row, 1)], buf_ref.at[slot, pl.ds(r, 1)], sem.at[slot])

    def issue_tile(step):
        slot = step % 2
        n_blocks = nv_ref[step] // DMA_ISSUE_UNROLL

        def issue_block(b, carry):
            for u in range(DMA_ISSUE_UNROLL):
                r = b * DMA_ISSUE_UNROLL + u
                row_copy(idx_ref[step * tm + r], slot, r).start()
            return carry

        def issue(r, carry):
            row_copy(idx_ref[step * tm + r], slot, r).start()
            return carry

        lax.fori_loop(0, n_blocks, issue_block, 0)
        lax.fori_loop(n_blocks * DMA_ISSUE_UNROLL, nv_ref[step], issue, 0)

    @pl.when(i == 0)
    def _():
        buf_ref[...] = jnp.zeros_like(buf_ref)
        issue_tile(0)

    @pl.when(i + 1 < n_used)
    def _():
        issue_tile(i + 1)

    @pl.when(i < n_used)
    def _():
        slot = i % 2

        @pl.when(nv_ref[i] == tm)
        def _():
            pltpu.make_async_copy(src_ref.at[pl.ds(0, tm)], buf_ref.at[slot], sem.at[slot]).wait()

        @pl.when(nv_ref[i] < tm)
        def _():
            def drain(r, carry):
                row_copy(0, slot, 0).wait()
                return carry

            lax.fori_loop(0, nv_ref[i], drain, 0)

        o_ref[...] = buf_ref[slot].astype(o_ref.dtype)

    @pl.when(i >= n_used)
    def _():
        o_ref[...] = jnp.zeros_like(o_ref)


def _gather_x(pos_flat, n_used, tile_rows, u2, tm=MOE_TM):
    assert TOP_K == 4
    return pl.pallas_call(
        functools.partial(_gather_x_kernel, tm=tm),
        grid_spec=pltpu.PrefetchScalarGridSpec(
            num_scalar_prefetch=3,
            grid=(XS_ROWS // tm,),
            in_specs=[pl.BlockSpec(memory_space=pl.ANY)],
            out_specs=pl.BlockSpec((tm, D_MODEL), lambda i, idx, nu, nv: (i, 0)),
            scratch_shapes=[pltpu.SMEM((XS_ROWS,), i32), pltpu.VMEM((2, tm, D_MODEL), f32),
                            pltpu.SemaphoreType.DMA((2,))],
        ),
        out_shape=jax.ShapeDtypeStruct((XS_ROWS, D_MODEL), bf16),
        compiler_params=_params(("arbitrary",)),
        name="gather_x",
    )(pos_flat, n_used, tile_rows, u2)


def _grouped_kernel(ge_ref, gt_ref, gn_ref, ng_ref, nu_ref, x_ref, *refs, n_w, tn, nj, ocols, tile_fn):
    del ge_ref
    w_refs, b_refs, o_ref = refs[:n_w], refs[n_w:2 * n_w], refs[2 * n_w]
    wb_ref, obuf_ref, zbuf_ref, sem, zsem = refs[2 * n_w + 1:]
    g = pl.program_id(0)
    j = pl.program_id(1)
    active = g < ng_ref[0]
    gc = jnp.minimum(g, ng_ref[0] - 1)
    n_tiles = jnp.where(active, gn_ref[gc], 0)
    tile0 = gt_ref[gc]
    col0 = pl.multiple_of(j * ocols, ocols)
    total_tiles = o_ref.shape[0] // MOE_TM

    @pl.when(jnp.logical_and(g == 0, j == 0))
    def _():
        zbuf_ref[...] = jnp.zeros_like(zbuf_ref)

        def zero_copy(t, jc):
            rows = pl.ds(pl.multiple_of(t * MOE_TM, MOE_TM), MOE_TM)
            return pltpu.make_async_copy(zbuf_ref, o_ref.at[rows, pl.ds(jc * ocols, ocols)], zsem)

        def z_start(t, carry):
            for jc in range(nj):
                zero_copy(t, jc).start()
            return carry

        def z_wait(t, carry):
            for jc in range(nj):
                zero_copy(t, jc).wait()
            return carry

        lax.fori_loop(nu_ref[0], total_tiles, z_start, 0)
        lax.fori_loop(nu_ref[0], total_tiles, z_wait, 0)

    @pl.when(active)
    def _():
        for k in range(n_w):
            wb_ref[:, k * tn:(k + 1) * tn] = w_refs[k][0].astype(bf16)

    step = g * nj + j
    slot = step % 2

    def out_copy(slot_, n, tile):
        rows = pl.ds(pl.multiple_of(tile * MOE_TM, MOE_TM), n * MOE_TM)
        return pltpu.make_async_copy(obuf_ref.at[slot_, pl.ds(0, n * MOE_TM)], o_ref.at[rows, pl.ds(col0, ocols)],
                                     sem.at[slot_])

    def wait_step(s):
        gs = s // nj
        sent = jnp.where(jnp.logical_and(s >= 0, gs < ng_ref[0]), gn_ref[jnp.clip(gs, 0, MAX_GROUPS - 1)], 0)
        for n in range(1, MOE_CAP + 1):

            @pl.when(sent == n)
            def _(n=n):
                out_copy(s % 2, n, 0).wait()

    wait_step(step - 2)
    for n in range(1, MOE_CAP + 1):

        @pl.when(n_tiles == n)
        def _(n=n):
            acc = jnp.dot(x_ref[0:n * MOE_TM, :], wb_ref[...], preferred_element_type=f32)
            obuf_ref[slot, pl.ds(0, n * MOE_TM)] = tile_fn(acc, [b[0] for b in b_refs])
            out_copy(slot, n, tile0).start()

    @pl.when(step == MAX_GROUPS * nj - 1)
    def _():
        wait_step(step - 1)
        wait_step(step)


def _grouped_call(groups, n_used, x, ws, bs, *, k_dim, tn, nj, w_col_offsets, ocols, out_shape, tile_fn, name):
    g_expert, g_tile0, g_ntiles, n_groups = groups
    n_w = len(ws)

    def gi(g, ng):
        return jnp.minimum(g, ng[0] - 1)

    def x_map(g, j, ge, gt, gn, ng, nu):
        return (pl.multiple_of(gt[gi(g, ng)] * MOE_TM, MOE_TM), 0)

    def w_map(off):
        return lambda g, j, ge, gt, gn, ng, nu: (ge[gi(g, ng)], 0, off + jnp.where(g < ng[0], j, nj - 1))

    return pl.pallas_call(
        functools.partial(_grouped_kernel, n_w=n_w, tn=tn, nj=nj, ocols=ocols, tile_fn=tile_fn),
        grid_spec=pltpu.PrefetchScalarGridSpec(
            num_scalar_prefetch=5,
            grid=(MAX_GROUPS, nj),
            in_specs=[
                pl.BlockSpec((pl.Element(GROUP_ROWS), pl.Element(k_dim)), x_map),
                *[pl.BlockSpec((1, k_dim, tn), w_map(off)) for off in w_col_offsets],
                *[pl.BlockSpec((1, 1, tn), w_map(off)) for off in w_col_offsets],
            ],
            out_specs=pl.BlockSpec(memory_space=pl.ANY),
            scratch_shapes=[
                pltpu.VMEM((k_dim, n_w * tn), bf16),
                pltpu.VMEM((2, GROUP_ROWS, ocols), out_shape.dtype),
                pltpu.VMEM((MOE_TM, ocols), out_shape.dtype),
                pltpu.SemaphoreType.DMA((2,)),
                pltpu.SemaphoreType.DMA(()),
            ],
        ),
        out_shape=out_shape,
        compiler_params=_params(("arbitrary", "arbitrary")),
        name=name,
    )(g_expert, g_tile0, g_ntiles, n_groups, n_used, x, *ws, *bs)


def _moe_up(groups, n_used, xs, w_up, b_up, tn=256):
    nj = D_EXPERT // tn

    def tile_fn(acc, biases):
        gate = jnp.minimum(acc[:, :tn] + biases[0], LIMIT)
        up = jnp.clip(acc[:, tn:] + biases[1], -LIMIT, LIMIT)
        return ((up + 1.0) * (gate * _sigmoid(ALPHA * gate))).astype(bf16)

    b3 = b_up.reshape(N_EXPERTS, 1, -1)
    return _grouped_call(groups, n_used, xs, (w_up, w_up), (b3, b3), k_dim=D_MODEL, tn=tn, nj=nj,
                         w_col_offsets=(0, nj), ocols=tn,
                         out_shape=jax.ShapeDtypeStruct((XS_ROWS, D_EXPERT), bf16), tile_fn=tile_fn, name="moe_up")


def _moe_down(groups, n_used, act, w_down, b_down, tn=1024):
    return _grouped_call(groups, n_used, act, (w_down,), (b_down.reshape(N_EXPERTS, 1, -1),), k_dim=D_EXPERT,
                         tn=tn, nj=D_MODEL // tn, w_col_offsets=(0,), ocols=tn,
                         out_shape=jax.ShapeDtypeStruct((N_SLOTS, D_MODEL), f32),
                         tile_fn=lambda acc, biases: acc + biases[0], name="moe_down")


def _combine_kernel(pos_ref, h_ref, g_ref, nw_ref, y_ref, o_ref, buf_ref, sem, *, tm):
    i = pl.program_id(0)
    n = pl.num_programs(0)

    def row_copy(src_row, slot, k, r):
        return pltpu.make_async_copy(y_ref.at[pl.ds(src_row, 1)], buf_ref.at[slot, k, pl.ds(r, 1)], sem.at[slot])

    def issue_tile(step):
        slot = step % 2
        for k in range(TOP_K):

            def issue(r, carry, k=k):
                row_copy(pos_ref[k * SEQ + step * tm + r], slot, k, r).start()
                return carry

            lax.fori_loop(0, tm, issue, 0, unroll=DMA_ISSUE_UNROLL)

    @pl.when(i == 0)
    def _():
        issue_tile(0)

    @pl.when(i + 1 < n)
    def _():
        issue_tile(i + 1)

    slot = i % 2
    for k in range(TOP_K):
        pltpu.make_async_copy(y_ref.at[pl.ds(0, tm)], buf_ref.at[slot, k], sem.at[slot]).wait()

    g = g_ref[...]
    h = h_ref[...]
    for k in range(TOP_K):
        h = h + g[:, k:k + 1] * buf_ref[slot, k]
    ms = jnp.mean(h * h, axis=-1, keepdims=True)
    o_ref[...] = h * lax.rsqrt(ms + EPS) * nw_ref[...]


def _combine(pos_kt, h1, y, gates, norm_w, tm=128):
    nb = SEQ // tm
    off = (PAD_ROWS + N_META) // tm
    return pl.pallas_call(
        functools.partial(_combine_kernel, tm=tm),
        grid_spec=pltpu.PrefetchScalarGridSpec(
            num_scalar_prefetch=1,
            grid=(nb,),
            in_specs=[
                pl.BlockSpec((tm, D_MODEL), lambda i, pos: (i + off, 0)),
                pl.BlockSpec((tm, LANES), lambda i, pos: (i + off, 0)),
                pl.BlockSpec((1, D_MODEL), lambda i, pos: (0, 0)),
                pl.BlockSpec(memory_space=pl.ANY),
            ],
            out_specs=pl.BlockSpec((tm, D_MODEL), lambda i, pos: (i, 0)),
            scratch_shapes=[pltpu.VMEM((2, TOP_K, tm, D_MODEL), f32), pltpu.SemaphoreType.DMA((2,))],
        ),
        out_shape=jax.ShapeDtypeStruct((SEQ, D_MODEL), f32),
        compiler_params=_params(("arbitrary",)),
        name="combine_final_norm",
    )(pos_kt, h1, gates, norm_w.reshape(1, D_MODEL), y)


def _rank_kernel(idx_ref, rank_ref, cnt_ref, carry_ref, *, tm):
    i = pl.program_id(0)

    @pl.when(i == 0)
    def _():
        carry_ref[...] = jnp.zeros_like(carry_ref)

    idx = idx_ref[...]
    rows = i * tm + lax.broadcasted_iota(i32, (tm, 1), 0)
    valid = rows >= PAD_ROWS
    e_iota = lax.broadcasted_iota(i32, (tm, N_EXPERTS), 1)
    hits = [jnp.logical_and(idx[:, k:k + 1] == e_iota, valid) for k in range(TOP_K)]
    onehot = sum(jnp.where(h, 1.0, 0.0) for h in hits)
    earlier = lax.broadcasted_iota(i32, (tm, tm), 0) > lax.broadcasted_iota(i32, (tm, tm), 1)
    before = jnp.dot(jnp.where(earlier, 1.0, 0.0).astype(bf16), onehot.astype(bf16),
                     preferred_element_type=f32) + carry_ref[...]
    lane = lax.broadcasted_iota(i32, (tm, LANES), 1)
    out = jnp.zeros((tm, LANES), i32)
    for k in range(TOP_K):
        rk = jnp.sum(jnp.where(hits[k], before, 0.0), axis=1, keepdims=True)
        out = jnp.where(lane == k, rk.astype(i32), out)
    rank_ref[...] = out
    carry_ref[...] += jnp.sum(onehot, axis=0, keepdims=True)
    cnt_ref[...] = carry_ref[...]


def _rank_pairs(top_idx, tm=320):
    return pl.pallas_call(
        functools.partial(_rank_kernel, tm=tm),
        grid=(R // tm,),
        in_specs=[pl.BlockSpec((tm, LANES), lambda i: (i, 0))],
        out_specs=[pl.BlockSpec((tm, LANES), lambda i: (i, 0)), pl.BlockSpec((1, N_EXPERTS), lambda i: (0, 0))],
        out_shape=[jax.ShapeDtypeStruct((R, LANES), i32), jax.ShapeDtypeStruct((1, N_EXPERTS), f32)],
        scratch_shapes=[pltpu.VMEM((1, N_EXPERTS), f32)],
        compiler_params=_params(("arbitrary",)),
        name="rank_pairs",
    )(top_idx)


def _routing_tables(top_idx, rank, counts):
    counts = counts.reshape(N_EXPERTS).astype(i32)
    tiles = (counts + MOE_TM - 1) // MOE_TM
    tile_end = jnp.cumsum(tiles)
    tile_start = tile_end - tiles
    e_ids = jnp.arange(N_EXPERTS, dtype=i32)
    tok_e = top_idx[PAD_ROWS:, :TOP_K]
    tok_start = jnp.sum(jnp.where(tok_e[..., None] == e_ids, tile_start * MOE_TM, 0), axis=-1)
    pos = tok_start + rank[PAD_ROWS:, :TOP_K]
    n_used = tile_end[-1:].astype(i32)
    tile_ids = jnp.arange(XS_ROWS // MOE_TM, dtype=i32)
    tile_e = jnp.minimum(jnp.sum(tile_end[None, :] <= tile_ids[:, None], axis=1), N_EXPERTS - 1)
    tile_rows = jnp.clip(counts[tile_e] - (tile_ids - tile_start[tile_e]) * MOE_TM, 0, MOE_TM).astype(i32)
    tile_rows = jnp.where(tile_ids < n_used[0], tile_rows, 0)
    groups = (tiles + MOE_CAP - 1) // MOE_CAP
    group_end = jnp.cumsum(groups)
    g_ids = jnp.arange(MAX_GROUPS, dtype=i32)
    g_expert = jnp.minimum(jnp.sum(group_end[None, :] <= g_ids[:, None], axis=1), N_EXPERTS - 1).astype(i32)
    g_local = g_ids - (group_end - groups)[g_expert]
    g_tile0 = (tile_start[g_expert] + g_local * MOE_CAP).astype(i32)
    g_ntiles = jnp.clip(tiles[g_expert] - g_local * MOE_CAP, 0, MOE_CAP).astype(i32)
    n_groups = group_end[-1:].astype(i32)
    return pos, n_used, tile_rows, (g_expert, g_tile0, g_ntiles, n_groups)


def kernel(x, meta_tokens, norm_mix_w, w_in, conv_w, conv_b, dt_bias, a_log, d_skip, ssd_norm_w, w_ssd_out,
           pool_w, pool_scale, w_out, norm_ffn_w, router_w, router_b, w_up, b_up, w_down, b_down, norm_final_w):
    assert x.shape == (1, SEQ, D_MODEL) and norm_mix_w.shape[0] == 1
    x2 = x[0]
    head = jnp.concatenate([jnp.zeros((PAD_ROWS, D_MODEL), f32), meta_tokens.astype(f32)], axis=0)

    u = _rms_norm_rows(x2, head, norm_mix_w[0])
    proj = _in_proj_main(u, w_in[0])
    dt = _in_proj_dt(u, w_in[0], dt_bias[0])
    gates = _in_proj_gates(u, w_in[0])
    ma = _pool_branch(proj, gates, pool_w[0], pool_scale[0])
    vw, ssq = _ssd_branch(proj, dt, conv_w[0], conv_b[0], a_log[0], d_skip[0], ssd_norm_w[0])
    merged = _ssd_out_merge(vw, w_ssd_out[0], ssq, ma, gates)
    h1 = _out_proj(merged, w_out[0], head, x2)

    u2, top_idx, top_gate = _router(h1, norm_ffn_w[0], router_w[0], router_b[0])
    rank, counts = _rank_pairs(top_idx)
    pos, n_used, tile_rows, groups = _routing_tables(top_idx, rank, counts)
    xs = _gather_x(pos.reshape(-1), n_used, tile_rows, u2)
    act = _moe_up(groups, n_used, xs, w_up[0], b_up[0])
    y = _moe_down(groups, n_used, act, w_down[0], b_down[0])
    pos_kt = pos[N_META:].T.reshape(-1)
    out = _combine(pos_kt, h1, y, top_gate, norm_final_w)
    return out[None]
```

```python
import functools

import jax
import jax.numpy as jnp
from jax import lax
from jax.experimental import pallas as pl
from jax.experimental.pallas import tpu as pltpu

f32 = jnp.float32
bf16 = jnp.bfloat16
i32 = jnp.int32

D_MODEL = 4096
SEQ = 8192
N_META = 16
CHUNK = 128
PAD_ROWS = CHUNK - N_META
R = PAD_ROWS + N_META + SEQ
N_CHUNKS = R // CHUNK
POOL_WINDOWS = (2, 4, 8, 16)
POOL_GROUP = D_MODEL // 4
D_INNER = 2 * D_MODEL
HEAD_DIM = 64
N_HEADS = D_INNER // HEAD_DIM
N_STATE = 128
N_GROUPS = 8
HEADS_PER_GROUP = N_HEADS // N_GROUPS
GROUP_W = HEADS_PER_GROUP * HEAD_DIM
CONV_W = 4
HALO = 16
POOL_HALO = 2 * max(POOL_WINDOWS)
N_EXPERTS = 32
TOP_K = 4
D_EXPERT = 7 * D_MODEL // 16
LIMIT = 7.0
ALPHA = 1.702
EPS = 1e-5
LOG2E = 1.4426950408889634
N_TOK = N_META + SEQ
N_PAIRS = N_TOK * TOP_K
MOE_TM = 256
N_TILES = -(-N_PAIRS // MOE_TM) + N_EXPERTS
N_SLOTS = N_TILES * MOE_TM
MOE_CAP = 6
GROUP_ROWS = MOE_CAP * MOE_TM
MAX_GROUPS = (N_TILES - 1 + N_EXPERTS * (MOE_CAP - 1)) // MOE_CAP
XS_ROWS = N_SLOTS + GROUP_ROWS
COL_Z = D_MODEL
COL_X = COL_Z + D_INNER
COL_B = COL_X + D_INNER
COL_C = COL_B + N_GROUPS * N_STATE
COL_DT = COL_C + N_GROUPS * N_STATE
COL_G = COL_DT + N_HEADS
SEG1_COLS = COL_DT
LANES = 128
V7X_VMEM_LIMIT = 56 * 1024 * 1024
DMA_ISSUE_UNROLL = 8


def _params(sem, vmem=V7X_VMEM_LIMIT):
    return pltpu.CompilerParams(dimension_semantics=sem, vmem_limit_bytes=vmem)


def _sigmoid(v):
    return 1.0 / (1.0 + jnp.exp(-v))


def _sigmoid_tanh(v):
    return 0.5 * jnp.tanh(0.5 * v) + 0.5


def _x_rows_spec(tm, tn, col_of):
    return pl.BlockSpec((pl.Element(tm), pl.Element(tn)),
                        lambda i, *rest: (pl.multiple_of(jnp.maximum(i * tm - CHUNK, 0), CHUNK), col_of(i, *rest)))


def _rms_kernel(x_ref, head_ref, w_ref, o_ref):
    i = pl.program_id(0)
    tm = o_ref.shape[0]

    def norm(h):
        ms = jnp.mean(h * h, axis=-1, keepdims=True)
        return (h * lax.rsqrt(ms + EPS) * w_ref[...]).astype(o_ref.dtype)

    @pl.when(i == 0)
    def _():
        o_ref[0:CHUNK, :] = norm(head_ref[...])
        o_ref[CHUNK:tm, :] = norm(x_ref[0:tm - CHUNK, :])

    @pl.when(i > 0)
    def _():
        o_ref[...] = norm(x_ref[...])


def _rms_norm_rows(x2, head, w, tm=640):
    d = x2.shape[1]
    return pl.pallas_call(
        _rms_kernel,
        grid=(R // tm,),
        in_specs=[_x_rows_spec(tm, d, lambda i: 0), pl.BlockSpec((CHUNK, d), lambda i: (0, 0)),
                  pl.BlockSpec((1, d), lambda i: (0, 0))],
        out_specs=pl.BlockSpec((tm, d), lambda i: (i, 0)),
        out_shape=jax.ShapeDtypeStruct((R, d), bf16),
        compiler_params=_params(("parallel",)),
        name="rms_mix",
    )(x2, head, w.reshape(1, d))


def _mm_kernel(*refs, n_extra, epilogue, nk):
    a_ref, w_ref = refs[0], refs[1]
    extra = refs[2:2 + n_extra]
    o_ref = refs[2 + n_extra]
    acc_ref = refs[3 + n_extra]
    k = pl.program_id(2)

    @pl.when(k == 0)
    def _():
        acc_ref[...] = jnp.dot(a_ref[...], w_ref[...].astype(bf16), preferred_element_type=f32)

    @pl.when(k > 0)
    def _():
        acc_ref[...] += jnp.dot(a_ref[...], w_ref[...].astype(bf16), preferred_element_type=f32)

    @pl.when(k == nk - 1)
    def _():
        epilogue(o_ref, acc_ref, *extra)


def _store(fn):
    def epilogue(o_ref, acc_ref, *extra):
        o_ref[...] = fn(acc_ref[...], *[e[...] for e in extra]).astype(o_ref.dtype)

    return epilogue


def _matmul(a, w, *, col0, n, tm, tn, tk, out_dtype, epilogue, extras=(), extra_specs=(), name):
    m, kdim = a.shape
    nk = kdim // tk
    if col0 % tn == 0:
        w_spec = pl.BlockSpec((tk, tn), lambda i, j, k: (k, j + col0 // tn))
    else:
        w_spec = pl.BlockSpec((pl.Element(tk), pl.Element(tn)), lambda i, j, k: (k * tk, pl.multiple_of(col0 + j * tn, LANES)))
    return pl.pallas_call(
        functools.partial(_mm_kernel, n_extra=len(extras), epilogue=epilogue, nk=nk),
        grid=(m // tm, n // tn, nk),
        in_specs=[pl.BlockSpec((tm, tk), lambda i, j, k: (i, k)), w_spec, *extra_specs],
        out_specs=pl.BlockSpec((tm, tn), lambda i, j, k: (i, j)),
        out_shape=jax.ShapeDtypeStruct((m, n), out_dtype),
        scratch_shapes=[pltpu.VMEM((tm, tn), f32)],
        compiler_params=_params(("parallel", "parallel", "arbitrary")),
        name=name,
    )(a, w, *extras)


def _softplus_bias(acc, bias):
    v = acc + bias
    return jnp.maximum(v, 0.0) + jnp.log1p(jnp.exp(-jnp.abs(v)))


def _in_proj_main(u, w_in):
    return _matmul(u, w_in, col0=0, n=SEG1_COLS, tm=1664, tn=1024, tk=2048, out_dtype=bf16,
                   epilogue=_store(lambda acc: acc), name="in_proj_main")


def _in_proj_dt(u, w_in, dt_bias):
    return _matmul(u, w_in, col0=COL_DT, n=N_HEADS, tm=1664, tn=N_HEADS, tk=1024, out_dtype=f32,
                   epilogue=_store(_softplus_bias), extras=(dt_bias.reshape(1, N_HEADS),),
                   extra_specs=(pl.BlockSpec((1, N_HEADS), lambda i, j, k: (0, 0)),), name="in_proj_dt")


def _in_proj_gates(u, w_in):
    return _matmul(u, w_in, col0=COL_G, n=2 * D_MODEL, tm=1664, tn=1024, tk=2048, out_dtype=bf16,
                   epilogue=_store(_sigmoid_tanh), name="in_proj_gates")


def _ssd_out_merge(vw, w_ssd_out, ssq, ma, gates, tm=1664, tn=1024):
    gb_off = D_MODEL // tn

    def merge(acc, ssq_t, ma_t, gb_t):
        rs = lax.rsqrt(ssq_t[:, :1] * (1.0 / D_INNER) + EPS)
        return ma_t.astype(f32) + gb_t.astype(f32) * (acc * rs)

    return _matmul(vw, w_ssd_out, col0=0, n=D_MODEL, tm=tm, tn=tn, tk=1024, out_dtype=bf16, epilogue=_store(merge),
                   extras=(ssq, ma, gates),
                   extra_specs=(pl.BlockSpec((tm, LANES), lambda i, j, k: (i, 0)),
                                pl.BlockSpec((tm, tn), lambda i, j, k: (i, j)),
                                pl.BlockSpec((tm, tn), lambda i, j, k: (i, j + gb_off))),
                   name="ssd_out_merge")


def _out_proj(merged, w_out, head, x2, tm=1664, tn=1024):
    def add_residual(o_ref, acc_ref, head_ref, x_ref):
        i = pl.program_id(0)

        @pl.when(i == 0)
        def _():
            o_ref[0:CHUNK, :] = head_ref[...] + acc_ref[0:CHUNK, :]
            o_ref[CHUNK:tm, :] = x_ref[0:tm - CHUNK, :] + acc_ref[CHUNK:tm, :]

        @pl.when(i > 0)
        def _():
            o_ref[...] = x_ref[...] + acc_ref[...]

    return _matmul(merged, w_out, col0=0, n=D_MODEL, tm=tm, tn=tn, tk=1024, out_dtype=f32,
                   epilogue=add_residual, extras=(head, x2),
                   extra_specs=(pl.BlockSpec((CHUNK, tn), lambda i, j, k: (0, j)),
                                _x_rows_spec(tm, tn, lambda i, j, k: pl.multiple_of(j * tn, tn))),
                   name="out_proj_residual")


def _pool_kernel(x_ref, halo_ref, w_ref, scale_ref, ga_ref, o_ref, a_ref, b_ref, pooled_ref, *, tm):
    assert POOL_WINDOWS == (2, 4, 8, 16)
    gi = pl.program_id(0)
    i = pl.program_id(1)
    n = POOL_HALO + tm
    halo = halo_ref[...].astype(f32)
    a_ref[0:POOL_HALO, :] = jnp.where(i == 0, 0.0, halo)
    a_ref[POOL_HALO:n, :] = x_ref[...].astype(f32)
    t = i * tm + lax.broadcasted_iota(i32, (tm, 1), 0) - PAD_ROWS

    def pair_sum(src_ref, shift, start):
        return src_ref[start:n, :] + src_ref[start - shift:n - shift, :]

    def finish(s, win):
        cnt = jnp.clip(t + 1, 1, win).astype(f32)
        pooled_ref[...] = (s / cnt - x_ref[...].astype(f32)).astype(bf16)

    @pl.when(gi == 0)
    def _():
        finish(pair_sum(a_ref, 1, POOL_HALO), 2)

    @pl.when(gi == 1)
    def _():
        b_ref[8:n, :] = pair_sum(a_ref, 1, 8)
        finish(pair_sum(b_ref, 2, POOL_HALO), 4)

    @pl.when(gi == 2)
    def _():
        b_ref[8:n, :] = pair_sum(a_ref, 1, 8)
        a_ref[16:n, :] = pair_sum(b_ref, 2, 16)
        finish(pair_sum(a_ref, 4, POOL_HALO), 8)

    @pl.when(gi == 3)
    def _():
        b_ref[8:n, :] = pair_sum(a_ref, 1, 8)
        a_ref[16:n, :] = pair_sum(b_ref, 2, 16)
        b_ref[24:n, :] = pair_sum(a_ref, 4, 24)
        finish(pair_sum(b_ref, 8, POOL_HALO), 16)

    acc = jnp.dot(pooled_ref[...], w_ref[0].astype(bf16), preferred_element_type=f32)
    o_ref[...] = (ga_ref[...].astype(f32) * (acc * scale_ref[...])).astype(o_ref.dtype)


def _pool_branch(proj, gates, pool_w, pool_scale, tm=1664):
    gw = POOL_GROUP
    return pl.pallas_call(
        functools.partial(_pool_kernel, tm=tm),
        grid=(4, R // tm),
        in_specs=[
            pl.BlockSpec((tm, gw), lambda g, i: (i, g)),
            pl.BlockSpec((POOL_HALO, gw), lambda g, i: (jnp.maximum(i * (tm // POOL_HALO) - 1, 0), g)),
            pl.BlockSpec((1, gw, gw), lambda g, i: (g, 0, 0)),
            pl.BlockSpec((1, gw), lambda g, i: (0, g)),
            pl.BlockSpec((tm, gw), lambda g, i: (i, g)),
        ],
        out_specs=pl.BlockSpec((tm, gw), lambda g, i: (i, g)),
        out_shape=jax.ShapeDtypeStruct((R, D_MODEL), bf16),
        scratch_shapes=[pltpu.VMEM((POOL_HALO + tm, gw), f32), pltpu.VMEM((POOL_HALO + tm, gw), f32),
                        pltpu.VMEM((tm, gw), bf16)],
        compiler_params=_params(("parallel", "arbitrary")),
        name="pool_branch",
    )(proj, proj, pool_w, pool_scale.reshape(1, D_MODEL), gates)


def _fill_ext(cur_ref, halo_ref, ext_ref, c):
    halo = halo_ref[...]
    ext_ref[0:HALO, :] = jnp.where(c == 0, jnp.zeros_like(halo), halo)
    ext_ref[HALO:HALO + CHUNK, :] = cur_ref[...]


def _conv_silu(ext_ref, sel, w_ref, b_ref, lanes):
    taps = jnp.dot(sel, ext_ref[:, lanes], preferred_element_type=f32)
    out = b_ref[:, lanes]
    for k in range(CONV_W):
        out = out + w_ref[k:k + 1, lanes] * taps[k * CHUNK:(k + 1) * CHUNK]
    return out * _sigmoid(out)


def _ssd_kernel(x_ref, xh_ref, bc_ref, bch_ref, z_ref, dt_ref, dtt_ref, alog_ref, alogt_ref,
                wx_ref, wbc_ref, bx_ref, bbc_ref, dskip_ref, nw_ref,
                vw_ref, ssq_ref,
                state_ref, extx_ref, extbc_ref):
    c = pl.program_id(0)

    @pl.when(c == 0)
    def _():
        state_ref[...] = jnp.zeros_like(state_ref)

    _fill_ext(x_ref, xh_ref, extx_ref, c)
    _fill_ext(bc_ref, bch_ref, extbc_ref, c)

    rows = c * CHUNK + lax.broadcasted_iota(i32, (CHUNK, 1), 0)
    row_ok = rows >= PAD_ROWS
    cols = c * CHUNK + lax.broadcasted_iota(i32, (1, CHUNK), 1)
    col_ok = cols >= PAD_ROWS
    li = lax.broadcasted_iota(i32, (CHUNK, CHUNK), 0)
    si = lax.broadcasted_iota(i32, (CHUNK, CHUNK), 1)
    causal = li >= si
    tril = causal.astype(f32)
    triu = (li <= si).astype(f32)
    lane = lax.broadcasted_iota(i32, (CHUNK, 2 * HEAD_DIM), 1)
    first = lane < HEAD_DIM
    first1 = lax.broadcasted_iota(i32, (1, 2 * HEAD_DIM), 1) < HEAD_DIM
    sr = lax.broadcasted_iota(i32, (CONV_W * CHUNK, HALO + CHUNK), 0)
    se = lax.broadcasted_iota(i32, (CONV_W * CHUNK, HALO + CHUNK), 1)
    sel = jnp.where(se == (sr % CHUNK) + (sr // CHUNK) + HALO - (CONV_W - 1), 1.0, 0.0).astype(bf16)
    hh = lax.broadcasted_iota(i32, (HEADS_PER_GROUP, GROUP_W), 0)
    hl = lax.broadcasted_iota(i32, (HEADS_PER_GROUP, GROUP_W), 1)
    head_lanes = jnp.where(hl // HEAD_DIM == hh, 1.0, 0.0).astype(bf16)

    def group(g, ssq):
        lo = pl.multiple_of(g * GROUP_W, GROUP_W)
        blo = pl.multiple_of(g * N_STATE, N_STATE)
        xs = _conv_silu(extx_ref, sel, wx_ref, bx_ref, pl.ds(lo, GROUP_W))
        bm = _conv_silu(extbc_ref, sel, wbc_ref, bbc_ref, pl.ds(blo, N_STATE))
        cm = _conv_silu(extbc_ref, sel, wbc_ref, bbc_ref, pl.ds(N_GROUPS * N_STATE + blo, N_STATE))

        dt = jnp.where(row_ok, dt_ref[g], 0.0)
        dtt = jnp.where(col_ok, dtt_ref[g], 0.0)
        a_dt = dt * (-jnp.exp(alog_ref[g]))
        a_dtt = dtt * (-jnp.exp(alogt_ref[g]))
        a_cs = jnp.dot(tril, a_dt, preferred_element_type=f32, precision=lax.Precision.HIGHEST)
        a_cst = jnp.dot(a_dtt, triu, preferred_element_type=f32, precision=lax.Precision.HIGHEST)
        last = a_cs[CHUNK - 1:CHUNK, :]
        e_cs = jnp.exp(a_cs)
        e_end = jnp.exp(last - a_cs)
        e_last = jnp.exp(last)
        a2 = a_cs * LOG2E
        a2t = a_cst * LOG2E

        bmb = bm.astype(bf16)
        cmb = cm.astype(bf16)
        cb = lax.dot_general(cmb, bmb, (((1,), (1,)), ((), ())), preferred_element_type=f32)
        state = state_ref[g]
        yoff = jnp.dot(cmb, state.astype(bf16), preferred_element_type=f32)

        scales = jnp.concatenate([dt, e_cs, e_end], axis=0).astype(bf16)
        spread = jnp.dot(scales, head_lanes, preferred_element_type=f32)
        xdt_g = xs * spread[0:CHUNK]
        xd2 = (xdt_g * spread[2 * CHUNK:3 * CHUNK]).astype(bf16)
        cd = []
        for q in range(HEADS_PER_GROUP // 2):
            ha, hb = 2 * q, 2 * q + 1
            sl = slice(q * 2 * HEAD_DIM, (q + 1) * 2 * HEAD_DIM)
            gl = pl.ds(pl.multiple_of(lo + q * 2 * HEAD_DIM, 2 * HEAD_DIM), 2 * HEAD_DIM)
            xq = xs[:, sl]
            xdtb = xdt_g[:, sl].astype(bf16)
            wa = cb * jnp.exp2(jnp.where(causal, a2[:, ha:ha + 1] - a2t[ha:ha + 1, :], -1e30))
            wb = cb * jnp.exp2(jnp.where(causal, a2[:, hb:hb + 1] - a2t[hb:hb + 1, :], -1e30))
            ya = jnp.dot(wa.astype(bf16), xdtb, preferred_element_type=f32)
            yb = jnp.dot(wb.astype(bf16), xdtb, preferred_element_type=f32)
            y = jnp.where(first, ya, yb) + yoff[:, sl] * spread[CHUNK:2 * CHUNK, sl] + dskip_ref[:, gl] * xq
            cd.append(jnp.where(first1, e_last[:, ha:ha + 1], e_last[:, hb:hb + 1]))
            zq = z_ref[:, gl].astype(f32)
            v = y * (zq * _sigmoid(zq))
            ssq = ssq + v * v
            vw_ref[:, gl] = (v * nw_ref[:, gl]).astype(vw_ref.dtype)

        bt = jnp.transpose(bm).astype(bf16)
        s_new = jnp.dot(bt, xd2, preferred_element_type=f32)
        state_ref[g] = state * jnp.concatenate(cd, axis=1) + s_new
        return ssq

    ssq = lax.fori_loop(0, N_GROUPS, group, jnp.zeros((CHUNK, 2 * HEAD_DIM), f32))
    ssq_ref[...] = jnp.broadcast_to(jnp.sum(ssq, axis=1, keepdims=True), (CHUNK, LANES))


def _ssd_branch(proj, dt, conv_w, conv_b, a_log, d_skip, ssd_norm_w):
    bc_w = 2 * N_GROUPS * N_STATE
    dt_g = dt.reshape(R, N_GROUPS, HEADS_PER_GROUP).transpose(1, 0, 2)
    dt_gt = dt_g.transpose(0, 2, 1)
    alog = a_log.reshape(N_GROUPS, 1, HEADS_PER_GROUP)
    alogt = a_log.reshape(N_GROUPS, HEADS_PER_GROUP, 1)
    conv_b2 = conv_b.reshape(1, -1)
    dskip_e = jnp.repeat(d_skip, HEAD_DIM).reshape(1, D_INNER)

    def cur(col, width):
        return pl.BlockSpec((pl.Element(CHUNK), pl.Element(width)),
                            lambda c: (pl.multiple_of(c * CHUNK, CHUNK), col))

    def halo(col, width):
        return pl.BlockSpec((pl.Element(HALO), pl.Element(width)),
                            lambda c: (pl.multiple_of(jnp.maximum(c * CHUNK - HALO, 0), HALO), col))

    full = lambda shape: pl.BlockSpec(shape, lambda c: (0,) * len(shape))
    in_specs = [
        cur(COL_X, D_INNER), halo(COL_X, D_INNER),
        cur(COL_B, bc_w), halo(COL_B, bc_w),
        cur(COL_Z, D_INNER),
        pl.BlockSpec((N_GROUPS, CHUNK, HEADS_PER_GROUP), lambda c: (0, c, 0)),
        pl.BlockSpec((N_GROUPS, HEADS_PER_GROUP, CHUNK), lambda c: (0, 0, c)),
        full((N_GROUPS, 1, HEADS_PER_GROUP)),
        full((N_GROUPS, HEADS_PER_GROUP, 1)),
        pl.BlockSpec((CONV_W, D_INNER), lambda c: (0, 0)),
        pl.BlockSpec((CONV_W, bc_w), lambda c: (0, D_INNER // bc_w)),
        pl.BlockSpec((1, D_INNER), lambda c: (0, 0)),
        pl.BlockSpec((1, bc_w), lambda c: (0, D_INNER // bc_w)),
        full((1, D_INNER)),
        full((1, D_INNER)),
    ]
    return pl.pallas_call(
        _ssd_kernel,
        grid=(N_CHUNKS,),
        in_specs=in_specs,
        out_specs=[
            pl.BlockSpec((CHUNK, D_INNER), lambda c: (c, 0)),
            pl.BlockSpec((CHUNK, LANES), lambda c: (c, 0)),
        ],
        out_shape=[
            jax.ShapeDtypeStruct((R, D_INNER), bf16),
            jax.ShapeDtypeStruct((R, LANES), f32),
        ],
        scratch_shapes=[
            pltpu.VMEM((N_GROUPS, N_STATE, GROUP_W), f32),
            pltpu.VMEM((HALO + CHUNK, D_INNER), bf16),
            pltpu.VMEM((HALO + CHUNK, bc_w), bf16),
        ],
        compiler_params=_params(("arbitrary",)),
        name="ssd_mixer",
    )(proj, proj, proj, proj, proj, dt_g, dt_gt, alog, alogt,
      conv_w, conv_w, conv_b2, conv_b2, dskip_e, ssd_norm_w.reshape(1, D_INNER))


def _router_kernel(h_ref, nw_ref, rw_ref, rb_ref, u_ref, idx_ref, gate_ref, *, tm):
    i = pl.program_id(0)
    h = h_ref[...]
    ms = jnp.mean(h * h, axis=-1, keepdims=True)
    u = h * lax.rsqrt(ms + EPS) * nw_ref[...]
    rows = i * tm + lax.broadcasted_iota(i32, (tm, 1), 0)
    u = jnp.where(rows >= PAD_ROWS, u, 0.0)
    u_ref[...] = u
    logits = jnp.dot(u, rw_ref[...], preferred_element_type=f32, precision=lax.Precision.HIGHEST) + rb_ref[...]
    e_iota = lax.broadcasted_iota(i32, (tm, N_EXPERTS), 1)
    lane = lax.broadcasted_iota(i32, (tm, LANES), 1)
    idx_out = jnp.zeros((tm, LANES), i32)
    val_out = jnp.zeros((tm, LANES), f32)
    vals = []
    for k in range(TOP_K):
        m = jnp.max(logits, axis=1, keepdims=True)
        sel = jnp.min(jnp.where(logits == m, e_iota, N_EXPERTS), axis=1, keepdims=True)
        vals.append(m)
        idx_out = jnp.where(lane == k, sel, idx_out)
        logits = jnp.where(e_iota == sel, -jnp.inf, logits)
    exps = [jnp.exp(v - vals[0]) for v in vals]
    denom = exps[0] + exps[1] + exps[2] + exps[3]
    for k in range(TOP_K):
        val_out = jnp.where(lane == k, exps[k] / denom, val_out)
    idx_ref[...] = idx_out
    gate_ref[...] = val_out


def _router(h1, norm_w, router_w, router_b, tm=320):
    return pl.pallas_call(
        functools.partial(_router_kernel, tm=tm),
        grid=(R // tm,),
        in_specs=[
            pl.BlockSpec((tm, D_MODEL), lambda i: (i, 0)),
            pl.BlockSpec((1, D_MODEL), lambda i: (0, 0)),
            pl.BlockSpec((D_MODEL, N_EXPERTS), lambda i: (0, 0)),
            pl.BlockSpec((1, N_EXPERTS), lambda i: (0, 0)),
        ],
        out_specs=[
            pl.BlockSpec((tm, D_MODEL), lambda i: (i, 0)),
            pl.BlockSpec((tm, LANES), lambda i: (i, 0)),
            pl.BlockSpec((tm, LANES), lambda i: (i, 0)),
        ],
        out_shape=[
            jax.ShapeDtypeStruct((R, D_MODEL), f32),
            jax.ShapeDtypeStruct((R, LANES), i32),
            jax.ShapeDtypeStruct((R, LANES), f32),
        ],
        compiler_params=_params(("parallel",)),
        name="router",
    )(h1, norm_w.reshape(1, D_MODEL), router_w, router_b.reshape(1, N_EXPERTS))


def _gather_x_kernel(pos_ref, nu_ref, nv_ref, src_ref, o_ref, idx_ref, buf_ref, sem, *, tm):
    i = pl.program_id(0)
    n_used = nu_ref[0]

    @pl.when(i == 0)
    def _():
        def invert(b, carry):
            for u in range(DMA_ISSUE_UNROLL):
                p = b * DMA_ISSUE_UNROLL + u
                idx_ref[pos_ref[p]] = lax.shift_right_logical(p, 2) + PAD_ROWS
            return carry

        lax.fori_loop(0, N_PAIRS // DMA_ISSUE_UNROLL, invert, 0)

    def row_copy(src_row, slot, r):
        return pltpu.make_async_copy(src_ref.at[pl.ds(src_row, 1)], buf_ref.at[slot, pl.ds(r, 1)], sem.at[slot])

    def issue_tile(step):
        slot = step % 2
        n_blocks = nv_ref[step] // DMA_ISSUE_UNROLL

        def issue_block(b, carry):
            for u in range(DMA_ISSUE_UNROLL):
                r = b * DMA_ISSUE_UNROLL + u
                row_copy(idx_ref[step * tm + r], slot, r).start()
            return carry

        def issue(r, carry):
            row_copy(idx_ref[step * tm + r], slot, r).start()
            return carry

        lax.fori_loop(0, n_blocks, issue_block, 0)
        lax.fori_loop(n_blocks * DMA_ISSUE_UNROLL, nv_ref[step], issue, 0)

    @pl.when(i == 0)
    def _():
        buf_ref[...] = jnp.zeros_like(buf_ref)
        issue_tile(0)

    @pl.when(i + 1 < n_used)
    def _():
        issue_tile(i + 1)

    @pl.when(i < n_used)
    def _():
        slot = i % 2

        @pl.when(nv_ref[i] == tm)
        def _():
            pltpu.make_async_copy(src_ref.at[pl.ds(0, tm)], buf_ref.at[slot], sem.at[slot]).wait()

        @pl.when(nv_ref[i] < tm)
        def _():
            def drain(r, carry):
                row_copy(0, slot, 0).wait()
                return carry

            lax.fori_loop(0, nv_ref[i], drain, 0)

        o_ref[...] = buf_ref[slot].astype(o_ref.dtype)

    @pl.when(i >= n_used)
    def _():
        o_ref[...] = jnp.zeros_like(o_ref)


def _gather_x(pos_flat, n_used, tile_rows, u2, tm=MOE_TM):
    assert TOP_K == 4
    return pl.pallas_call(
        functools.partial(_gather_x_kernel, tm=tm),
        grid_spec=pltpu.PrefetchScalarGridSpec(
            num_scalar_prefetch=3,
            grid=(XS_ROWS // tm,),
            in_specs=[pl.BlockSpec(memory_space=pl.ANY)],
            out_specs=pl.BlockSpec((tm, D_MODEL), lambda i, idx, nu, nv: (i, 0)),
            scratch_shapes=[pltpu.SMEM((XS_ROWS,), i32), pltpu.VMEM((2, tm, D_MODEL), f32),
                            pltpu.SemaphoreType.DMA((2,))],
        ),
        out_shape=jax.ShapeDtypeStruct((XS_ROWS, D_MODEL), bf16),
        compiler_params=_params(("arbitrary",)),
        name="gather_x",
    )(pos_flat, n_used, tile_rows, u2)


def _grouped_kernel(ge_ref, gt_ref, gn_ref, ng_ref, nu_ref, x_ref, *refs, n_w, tn, nj, ocols, tile_fn):
    del ge_ref
    w_refs, b_refs, o_ref = refs[:n_w], refs[n_w:2 * n_w], refs[2 * n_w]
    wb_ref, obuf_ref, zbuf_ref, sem, zsem = refs[2 * n_w + 1:]
    g = pl.program_id(0)
    j = pl.program_id(1)
    active = g < ng_ref[0]
    gc = jnp.minimum(g, ng_ref[0] - 1)
    n_tiles = jnp.where(active, gn_ref[gc], 0)
    tile0 = gt_ref[gc]
    col0 = pl.multiple_of(j * ocols, ocols)
    total_tiles = o_ref.shape[0] // MOE_TM

    @pl.when(jnp.logical_and(g == 0, j == 0))
    def _():
        zbuf_ref[...] = jnp.zeros_like(zbuf_ref)

        def zero_copy(t, jc):
            rows = pl.ds(pl.multiple_of(t * MOE_TM, MOE_TM), MOE_TM)
            return pltpu.make_async_copy(zbuf_ref, o_ref.at[rows, pl.ds(jc * ocols, ocols)], zsem)

        def z_start(t, carry):
            for jc in range(nj):
                zero_copy(t, jc).start()
            return carry

        def z_wait(t, carry):
            for jc in range(nj):
                zero_copy(t, jc).wait()
            return carry

        lax.fori_loop(nu_ref[0], total_tiles, z_start, 0)
        lax.fori_loop(nu_ref[0], total_tiles, z_wait, 0)

    @pl.when(active)
    def _():
        for k in range(n_w):
            wb_ref[:, k * tn:(k + 1) * tn] = w_refs[k][0].astype(bf16)

    step = g * nj + j
    slot = step % 2

    def out_copy(slot_, n, tile):
        rows = pl.ds(pl.multiple_of(tile * MOE_TM, MOE_TM), n * MOE_TM)
        return pltpu.make_async_copy(obuf_ref.at[slot_, pl.ds(0, n * MOE_TM)], o_ref.at[rows, pl.ds(col0, ocols)],
                                     sem.at[slot_])

    def wait_step(s):
        gs = s // nj
        sent = jnp.where(jnp.logical_and(s >= 0, gs < ng_ref[0]), gn_ref[jnp.clip(gs, 0, MAX_GROUPS - 1)], 0)
        for n in range(1, MOE_CAP + 1):

            @pl.when(sent == n)
            def _(n=n):
                out_copy(s % 2, n, 0).wait()

    wait_step(step - 2)
    for n in range(1, MOE_CAP + 1):

        @pl.when(n_tiles == n)
        def _(n=n):
            acc = jnp.dot(x_ref[0:n * MOE_TM, :], wb_ref[...], preferred_element_type=f32)
            obuf_ref[slot, pl.ds(0, n * MOE_TM)] = tile_fn(acc, [b[0] for b in b_refs])
            out_copy(slot, n, tile0).start()

    @pl.when(step == MAX_GROUPS * nj - 1)
    def _():
        wait_step(step - 1)
        wait_step(step)


def _grouped_call(groups, n_used, x, ws, bs, *, k_dim, tn, nj, w_col_offsets, ocols, out_shape, tile_fn, name):
    g_expert, g_tile0, g_ntiles, n_groups = groups
    n_w = len(ws)

    def gi(g, ng):
        return jnp.minimum(g, ng[0] - 1)

    def x_map(g, j, ge, gt, gn, ng, nu):
        return (pl.multiple_of(gt[gi(g, ng)] * MOE_TM, MOE_TM), 0)

    def w_map(off):
        return lambda g, j, ge, gt, gn, ng, nu: (ge[gi(g, ng)], 0, off + jnp.where(g < ng[0], j, nj - 1))

    return pl.pallas_call(
        functools.partial(_grouped_kernel, n_w=n_w, tn=tn, nj=nj, ocols=ocols, tile_fn=tile_fn),
        grid_spec=pltpu.PrefetchScalarGridSpec(
            num_scalar_prefetch=5,
            grid=(MAX_GROUPS, nj),
            in_specs=[
                pl.BlockSpec((pl.Element(GROUP_ROWS), pl.Element(k_dim)), x_map),
                *[pl.BlockSpec((1, k_dim, tn), w_map(off)) for off in w_col_offsets],
                *[pl.BlockSpec((1, 1, tn), w_map(off)) for off in w_col_offsets],
            ],
            out_specs=pl.BlockSpec(memory_space=pl.ANY),
            scratch_shapes=[
                pltpu.VMEM((k_dim, n_w * tn), bf16),
                pltpu.VMEM((2, GROUP_ROWS, ocols), out_shape.dtype),
                pltpu.VMEM((MOE_TM, ocols), out_shape.dtype),
                pltpu.SemaphoreType.DMA((2,)),
                pltpu.SemaphoreType.DMA(()),
            ],
        ),
        out_shape=out_shape,
        compiler_params=_params(("arbitrary", "arbitrary")),
        name=name,
    )(g_expert, g_tile0, g_ntiles, n_groups, n_used, x, *ws, *bs)


def _moe_up(groups, n_used, xs, w_up, b_up, tn=256):
    nj = D_EXPERT // tn

    def tile_fn(acc, biases):
        gate = jnp.minimum(acc[:, :tn] + biases[0], LIMIT)
        up = jnp.clip(acc[:, tn:] + biases[1], -LIMIT, LIMIT)
        return ((up + 1.0) * (gate * _sigmoid(ALPHA * gate))).astype(bf16)

    b3 = b_up.reshape(N_EXPERTS, 1, -1)
    return _grouped_call(groups, n_used, xs, (w_up, w_up), (b3, b3), k_dim=D_MODEL, tn=tn, nj=nj,
                         w_col_offsets=(0, nj), ocols=tn,
                         out_shape=jax.ShapeDtypeStruct((XS_ROWS, D_EXPERT), bf16), tile_fn=tile_fn, name="moe_up")


def _moe_down(groups, n_used, act, w_down, b_down, tn=1024):
    return _grouped_call(groups, n_used, act, (w_down,), (b_down.reshape(N_EXPERTS, 1, -1),), k_dim=D_EXPERT,
                         tn=tn, nj=D_MODEL // tn, w_col_offsets=(0,), ocols=tn,
                         out_shape=jax.ShapeDtypeStruct((N_SLOTS, D_MODEL), f32),
                         tile_fn=lambda acc, biases: acc + biases[0], name="moe_down")


def _combine_kernel(pos_ref, h_ref, g_ref, nw_ref, y_ref, o_ref, buf_ref, sem, *, tm):
    i = pl.program_id(0)
    n = pl.num_programs(0)

    def row_copy(src_row, slot, k, r):
        return pltpu.make_async_copy(y_ref.at[pl.ds(src_row, 1)], buf_ref.at[slot, k, pl.ds(r, 1)], sem.at[slot])

    def issue_tile(step):
        slot = step % 2
        for k in range(TOP_K):

            def issue(r, carry, k=k):
                row_copy(pos_ref[k * SEQ + step * tm + r], slot, k, r).start()
                return carry

            lax.fori_loop(0, tm, issue, 0, unroll=DMA_ISSUE_UNROLL)

    @pl.when(i == 0)
    def _():
        issue_tile(0)

    @pl.when(i + 1 < n)
    def _():
        issue_tile(i + 1)

    slot = i % 2
    for k in range(TOP_K):
        pltpu.make_async_copy(y_ref.at[pl.ds(0, tm)], buf_ref.at[slot, k], sem.at[slot]).wait()

    g = g_ref[...]
    h = h_ref[...]
    for k in range(TOP_K):
        h = h + g[:, k:k + 1] * buf_ref[slot, k]
    ms = jnp.mean(h * h, axis=-1, keepdims=True)
    o_ref[...] = h * lax.rsqrt(ms + EPS) * nw_ref[...]


def _combine(pos_kt, h1, y, gates, norm_w, tm=128):
    nb = SEQ // tm
    off = (PAD_ROWS + N_META) // tm
    return pl.pallas_call(
        functools.partial(_combine_kernel, tm=tm),
        grid_spec=pltpu.PrefetchScalarGridSpec(
            num_scalar_prefetch=1,
            grid=(nb,),
            in_specs=[
                pl.BlockSpec((tm, D_MODEL), lambda i, pos: (i + off, 0)),
                pl.BlockSpec((tm, LANES), lambda i, pos: (i + off, 0)),
                pl.BlockSpec((1, D_MODEL), lambda i, pos: (0, 0)),
                pl.BlockSpec(memory_space=pl.ANY),
            ],
            out_specs=pl.BlockSpec((tm, D_MODEL), lambda i, pos: (i, 0)),
            scratch_shapes=[pltpu.VMEM((2, TOP_K, tm, D_MODEL), f32), pltpu.SemaphoreType.DMA((2,))],
        ),
        out_shape=jax.ShapeDtypeStruct((SEQ, D_MODEL), f32),
        compiler_params=_params(("arbitrary",)),
        name="combine_final_norm",
    )(pos_kt, h1, gates, norm_w.reshape(1, D_MODEL), y)


def _rank_kernel(idx_ref, rank_ref, cnt_ref, carry_ref, *, tm):
    i = pl.program_id(0)

    @pl.when(i == 0)
    def _():
        carry_ref[...] = jnp.zeros_like(carry_ref)

    idx = idx_ref[...]
    rows = i * tm + lax.broadcasted_iota(i32, (tm, 1), 0)
    valid = rows >= PAD_ROWS
    e_iota = lax.broadcasted_iota(i32, (tm, N_EXPERTS), 1)
    hits = [jnp.logical_and(idx[:, k:k + 1] == e_iota, valid) for k in range(TOP_K)]
    onehot = sum(jnp.where(h, 1.0, 0.0) for h in hits)
    earlier = lax.broadcasted_iota(i32, (tm, tm), 0) > lax.broadcasted_iota(i32, (tm, tm), 1)
    before = jnp.dot(jnp.where(earlier, 1.0, 0.0).astype(bf16), onehot.astype(bf16),
                     preferred_element_type=f32) + carry_ref[...]
    lane = lax.broadcasted_iota(i32, (tm, LANES), 1)
    out = jnp.zeros((tm, LANES), i32)
    for k in range(TOP_K):
        rk = jnp.sum(jnp.where(hits[k], before, 0.0), axis=1, keepdims=True)
        out = jnp.where(lane == k, rk.astype(i32), out)
    rank_ref[...] = out
    carry_ref[...] += jnp.sum(onehot, axis=0, keepdims=True)
    cnt_ref[...] = carry_ref[...]


def _rank_pairs(top_idx, tm=320):
    return pl.pallas_call(
        functools.partial(_rank_kernel, tm=tm),
        grid=(R // tm,),
        in_specs=[pl.BlockSpec((tm, LANES), lambda i: (i, 0))],
        out_specs=[pl.BlockSpec((tm, LANES), lambda i: (i, 0)), pl.BlockSpec((1, N_EXPERTS), lambda i: (0, 0))],
        out_shape=[jax.ShapeDtypeStruct((R, LANES), i32), jax.ShapeDtypeStruct((1, N_EXPERTS), f32)],
        scratch_shapes=[pltpu.VMEM((1, N_EXPERTS), f32)],
        compiler_params=_params(("arbitrary",)),
        name="rank_pairs",
    )(top_idx)


def _routing_tables(top_idx, rank, counts):
    counts = counts.reshape(N_EXPERTS).astype(i32)
    tiles = (counts + MOE_TM - 1) // MOE_TM
    tile_end = jnp.cumsum(tiles)
    tile_start = tile_end - tiles
    e_ids = jnp.arange(N_EXPERTS, dtype=i32)
    tok_e = top_idx[PAD_ROWS:, :TOP_K]
    tok_start = jnp.sum(jnp.where(tok_e[..., None] == e_ids, tile_start * MOE_TM, 0), axis=-1)
    pos = tok_start + rank[PAD_ROWS:, :TOP_K]
    n_used = tile_end[-1:].astype(i32)
    tile_ids = jnp.arange(XS_ROWS // MOE_TM, dtype=i32)
    tile_e = jnp.minimum(jnp.sum(tile_end[None, :] <= tile_ids[:, None], axis=1), N_EXPERTS - 1)
    tile_rows = jnp.clip(counts[tile_e] - (tile_ids - tile_start[tile_e]) * MOE_TM, 0, MOE_TM).astype(i32)
    tile_rows = jnp.where(tile_ids < n_used[0], tile_rows, 0)
    groups = (tiles + MOE_CAP - 1) // MOE_CAP
    group_end = jnp.cumsum(groups)
    g_ids = jnp.arange(MAX_GROUPS, dtype=i32)
    g_expert = jnp.minimum(jnp.sum(group_end[None, :] <= g_ids[:, None], axis=1), N_EXPERTS - 1).astype(i32)
    g_local = g_ids - (group_end - groups)[g_expert]
    g_tile0 = (tile_start[g_expert] + g_local * MOE_CAP).astype(i32)
    g_ntiles = jnp.clip(tiles[g_expert] - g_local * MOE_CAP, 0, MOE_CAP).astype(i32)
    n_groups = group_end[-1:].astype(i32)
    return pos, n_used, tile_rows, (g_expert, g_tile0, g_ntiles, n_groups)


def kernel(x, meta_tokens, norm_mix_w, w_in, conv_w, conv_b, dt_bias, a_log, d_skip, ssd_norm_w, w_ssd_out,
           pool_w, pool_scale, w_out, norm_ffn_w, router_w, router_b, w_up, b_up, w_down, b_down, norm_final_w):
    assert x.shape == (1, SEQ, D_MODEL) and norm_mix_w.shape[0] == 1
    x2 = x[0]
    head = jnp.concatenate([jnp.zeros((PAD_ROWS, D_MODEL), f32), meta_tokens.astype(f32)], axis=0)

    u = _rms_norm_rows(x2, head, norm_mix_w[0])
    proj = _in_proj_main(u, w_in[0])
    dt = _in_proj_dt(u, w_in[0], dt_bias[0])
    gates = _in_proj_gates(u, w_in[0])
    ma = _pool_branch(proj, gates, pool_w[0], pool_scale[0])
    vw, ssq = _ssd_branch(proj, dt, conv_w[0], conv_b[0], a_log[0], d_skip[0], ssd_norm_w[0])
    merged = _ssd_out_merge(vw, w_ssd_out[0], ssq, ma, gates)
    h1 = _out_proj(merged, w_out[0], head, x2)

    u2, top_idx, top_gate = _router(h1, norm_ffn_w[0], router_w[0], router_b[0])
    rank, counts = _rank_pairs(top_idx)
    pos, n_used, tile_rows, groups = _routing_tables(top_idx, rank, counts)
    xs = _gather_x(pos.reshape(-1), n_used, tile_rows, u2)
    act = _moe_up(groups, n_used, xs, w_up[0], b_up[0])
    y = _moe_down(groups, n_used, act, w_down[0], b_down[0])
    pos_kt = pos[N_META:].T.reshape(-1)
    out = _combine(pos_kt, h1, y, top_gate, norm_final_w)
    return out[None]
```

```python
import functools

import jax
import jax.numpy as jnp
from jax import lax
from jax.experimental import pallas as pl
from jax.experimental.pallas import tpu as pltpu

f32 = jnp.float32
bf16 = jnp.bfloat16
i32 = jnp.int32

D_MODEL = 4096
SEQ = 8192
N_META = 16
CHUNK = 128
PAD_ROWS = CHUNK - N_META
R = PAD_ROWS + N_META + SEQ
N_CHUNKS = R // CHUNK
POOL_WINDOWS = (2, 4, 8, 16)
POOL_GROUP = D_MODEL // 4
D_INNER = 2 * D_MODEL
HEAD_DIM = 64
N_HEADS = D_INNER // HEAD_DIM
N_STATE = 128
N_GROUPS = 8
HEADS_PER_GROUP = N_HEADS // N_GROUPS
GROUP_W = HEADS_PER_GROUP * HEAD_DIM
CONV_W = 4
HALO = 16
POOL_HALO = 2 * max(POOL_WINDOWS)
N_EXPERTS = 32
TOP_K = 4
D_EXPERT = 7 * D_MODEL // 16
LIMIT = 7.0
ALPHA = 1.702
EPS = 1e-5
LOG2E = 1.4426950408889634
N_TOK = N_META + SEQ
N_PAIRS = N_TOK * TOP_K
MOE_TM = 256
N_TILES = -(-N_PAIRS // MOE_TM) + N_EXPERTS
N_SLOTS = N_TILES * MOE_TM
MOE_CAP = 6
GROUP_ROWS = MOE_CAP * MOE_TM
MAX_GROUPS = (N_TILES - 1 + N_EXPERTS * (MOE_CAP - 1)) // MOE_CAP
XS_ROWS = N_SLOTS + GROUP_ROWS
COL_Z = D_MODEL
COL_X = COL_Z + D_INNER
COL_B = COL_X + D_INNER
COL_C = COL_B + N_GROUPS * N_STATE
COL_DT = COL_C + N_GROUPS * N_STATE
COL_G = COL_DT + N_HEADS
SEG1_COLS = COL_DT
LANES = 128
V7X_VMEM_LIMIT = 56 * 1024 * 1024
DMA_ISSUE_UNROLL = 8


def _params(sem, vmem=V7X_VMEM_LIMIT):
    return pltpu.CompilerParams(dimension_semantics=sem, vmem_limit_bytes=vmem)


def _sigmoid(v):
    return 1.0 / (1.0 + jnp.exp(-v))


def _sigmoid_tanh(v):
    return 0.5 * jnp.tanh(0.5 * v) + 0.5


def _x_rows_spec(tm, tn, col_of):
    return pl.BlockSpec((pl.Element(tm), pl.Element(tn)),
                        lambda i, *rest: (pl.multiple_of(jnp.maximum(i * tm - CHUNK, 0), CHUNK), col_of(i, *rest)))


def _rms_kernel(x_ref, head_ref, w_ref, o_ref):
    i = pl.program_id(0)
    tm = o_ref.shape[0]

    def norm(h):
        ms = jnp.mean(h * h, axis=-1, keepdims=True)
        return (h * lax.rsqrt(ms + EPS) * w_ref[...]).astype(o_ref.dtype)

    @pl.when(i == 0)
    def _():
        o_ref[0:CHUNK, :] = norm(head_ref[...])
        o_ref[CHUNK:tm, :] = norm(x_ref[0:tm - CHUNK, :])

    @pl.when(i > 0)
    def _():
        o_ref[...] = norm(x_ref[...])


def _rms_norm_rows(x2, head, w, tm=640):
    d = x2.shape[1]
    return pl.pallas_call(
        _rms_kernel,
        grid=(R // tm,),
        in_specs=[_x_rows_spec(tm, d, lambda i: 0), pl.BlockSpec((CHUNK, d), lambda i: (0, 0)),
                  pl.BlockSpec((1, d), lambda i: (0, 0))],
        out_specs=pl.BlockSpec((tm, d), lambda i: (i, 0)),
        out_shape=jax.ShapeDtypeStruct((R, d), bf16),
        compiler_params=_params(("parallel",)),
        name="rms_mix",
    )(x2, head, w.reshape(1, d))


def _mm_kernel(*refs, n_extra, epilogue, nk):
    a_ref, w_ref = refs[0], refs[1]
    extra = refs[2:2 + n_extra]
    o_ref = refs[2 + n_extra]
    acc_ref = refs[3 + n_extra]
    k = pl.program_id(2)

    @pl.when(k == 0)
    def _():
        acc_ref[...] = jnp.dot(a_ref[...], w_ref[...].astype(bf16), preferred_element_type=f32)

    @pl.when(k > 0)
    def _():
        acc_ref[...] += jnp.dot(a_ref[...], w_ref[...].astype(bf16), preferred_element_type=f32)

    @pl.when(k == nk - 1)
    def _():
        epilogue(o_ref, acc_ref, *extra)


def _store(fn):
    def epilogue(o_ref, acc_ref, *extra):
        o_ref[...] = fn(acc_ref[...], *[e[...] for e in extra]).astype(o_ref.dtype)

    return epilogue


def _matmul(a, w, *, col0, n, tm, tn, tk, out_dtype, epilogue, extras=(), extra_specs=(), name):
    m, kdim = a.shape
    nk = kdim // tk
    if col0 % tn == 0:
        w_spec = pl.BlockSpec((tk, tn), lambda i, j, k: (k, j + col0 // tn))
    else:
        w_spec = pl.BlockSpec((pl.Element(tk), pl.Element(tn)), lambda i, j, k: (k * tk, pl.multiple_of(col0 + j * tn, LANES)))
    return pl.pallas_call(
        functools.partial(_mm_kernel, n_extra=len(extras), epilogue=epilogue, nk=nk),
        grid=(m // tm, n // tn, nk),
        in_specs=[pl.BlockSpec((tm, tk), lambda i, j, k: (i, k)), w_spec, *extra_specs],
        out_specs=pl.BlockSpec((tm, tn), lambda i, j, k: (i, j)),
        out_shape=jax.ShapeDtypeStruct((m, n), out_dtype),
        scratch_shapes=[pltpu.VMEM((tm, tn), f32)],
        compiler_params=_params(("parallel", "parallel", "arbitrary")),
        name=name,
    )(a, w, *extras)


def _softplus_bias(acc, bias):
    v = acc + bias
    return jnp.maximum(v, 0.0) + jnp.log1p(jnp.exp(-jnp.abs(v)))


def _in_proj_main(u, w_in):
    return _matmul(u, w_in, col0=0, n=SEG1_COLS, tm=1664, tn=1024, tk=2048, out_dtype=bf16,
                   epilogue=_store(lambda acc: acc), name="in_proj_main")


def _in_proj_dt(u, w_in, dt_bias):
    return _matmul(u, w_in, col0=COL_DT, n=N_HEADS, tm=1664, tn=N_HEADS, tk=1024, out_dtype=f32,
                   epilogue=_store(_softplus_bias), extras=(dt_bias.reshape(1, N_HEADS),),
                   extra_specs=(pl.BlockSpec((1, N_HEADS), lambda i, j, k: (0, 0)),), name="in_proj_dt")


def _in_proj_gates(u, w_in):
    return _matmul(u, w_in, col0=COL_G, n=2 * D_MODEL, tm=1664, tn=1024, tk=2048, out_dtype=bf16,
                   epilogue=_store(_sigmoid_tanh), name="in_proj_gates")


def _ssd_out_merge(vw, w_ssd_out, ssq, ma, gates, tm=1664, tn=1024):
    gb_off = D_MODEL // tn

    def merge(acc, ssq_t, ma_t, gb_t):
        rs = lax.rsqrt(ssq_t[:, :1] * (1.0 / D_INNER) + EPS)
        return ma_t.astype(f32) + gb_t.astype(f32) * (acc * rs)

    return _matmul(vw, w_ssd_out, col0=0, n=D_MODEL, tm=tm, tn=tn, tk=1024, out_dtype=bf16, epilogue=_store(merge),
                   extras=(ssq, ma, gates),
                   extra_specs=(pl.BlockSpec((tm, LANES), lambda i, j, k: (i, 0)),
                                pl.BlockSpec((tm, tn), lambda i, j, k: (i, j)),
                                pl.BlockSpec((tm, tn), lambda i, j, k: (i, j + gb_off))),
                   name="ssd_out_merge")


def _out_proj(merged, w_out, head, x2, tm=1664, tn=1024):
    def add_residual(o_ref, acc_ref, head_ref, x_ref):
        i = pl.program_id(0)

        @pl.when(i == 0)
        def _():
            o_ref[0:CHUNK, :] = head_ref[...] + acc_ref[0:CHUNK, :]
            o_ref[CHUNK:tm, :] = x_ref[0:tm - CHUNK, :] + acc_ref[CHUNK:tm, :]

        @pl.when(i > 0)
        def _():
            o_ref[...] = x_ref[...] + acc_ref[...]

    return _matmul(merged, w_out, col0=0, n=D_MODEL, tm=tm, tn=tn, tk=1024, out_dtype=f32,
                   epilogue=add_residual, extras=(head, x2),
                   extra_specs=(pl.BlockSpec((CHUNK, tn), lambda i, j, k: (0, j)),
                                _x_rows_spec(tm, tn, lambda i, j, k: pl.multiple_of(j * tn, tn))),
                   name="out_proj_residual")


def _pool_kernel(x_ref, halo_ref, w_ref, scale_ref, ga_ref, o_ref, a_ref, b_ref, pooled_ref, *, tm):
    assert POOL_WINDOWS == (2, 4, 8, 16)
    gi = pl.program_id(0)
    i = pl.program_id(1)
    n = POOL_HALO + tm
    halo = halo_ref[...].astype(f32)
    a_ref[0:POOL_HALO, :] = jnp.where(i == 0, 0.0, halo)
    a_ref[POOL_HALO:n, :] = x_ref[...].astype(f32)
    t = i * tm + lax.broadcasted_iota(i32, (tm, 1), 0) - PAD_ROWS

    def pair_sum(src_ref, shift, start):
        return src_ref[start:n, :] + src_ref[start - shift:n - shift, :]

    def finish(s, win):
        cnt = jnp.clip(t + 1, 1, win).astype(f32)
        pooled_ref[...] = (s / cnt - x_ref[...].astype(f32)).astype(bf16)

    @pl.when(gi == 0)
    def _():
        finish(pair_sum(a_ref, 1, POOL_HALO), 2)

    @pl.when(gi == 1)
    def _():
        b_ref[8:n, :] = pair_sum(a_ref, 1, 8)
        finish(pair_sum(b_ref, 2, POOL_HALO), 4)

    @pl.when(gi == 2)
    def _():
        b_ref[8:n, :] = pair_sum(a_ref, 1, 8)
        a_ref[16:n, :] = pair_sum(b_ref, 2, 16)
        finish(pair_sum(a_ref, 4, POOL_HALO), 8)

    @pl.when(gi == 3)
    def _():
        b_ref[8:n, :] = pair_sum(a_ref, 1, 8)
        a_ref[16:n, :] = pair_sum(b_ref, 2, 16)
        b_ref[24:n, :] = pair_sum(a_ref, 4, 24)
        finish(pair_sum(b_ref, 8, POOL_HALO), 16)

    acc = jnp.dot(pooled_ref[...], w_ref[0].astype(bf16), preferred_element_type=f32)
    o_ref[...] = (ga_ref[...].astype(f32) * (acc * scale_ref[...])).astype(o_ref.dtype)


def _pool_branch(proj, gates, pool_w, pool_scale, tm=1664):
    gw = POOL_GROUP
    return pl.pallas_call(
        functools.partial(_pool_kernel, tm=tm),
        grid=(4, R // tm),
        in_specs=[
            pl.BlockSpec((tm, gw), lambda g, i: (i, g)),
            pl.BlockSpec((POOL_HALO, gw), lambda g, i: (jnp.maximum(i * (tm // POOL_HALO) - 1, 0), g)),
            pl.BlockSpec((1, gw, gw), lambda g, i: (g, 0, 0)),
            pl.BlockSpec((1, gw), lambda g, i: (0, g)),
            pl.BlockSpec((tm, gw), lambda g, i: (i, g)),
        ],
        out_specs=pl.BlockSpec((tm, gw), lambda g, i: (i, g)),
        out_shape=jax.ShapeDtypeStruct((R, D_MODEL), bf16),
        scratch_shapes=[pltpu.VMEM((POOL_HALO + tm, gw), f32), pltpu.VMEM((POOL_HALO + tm, gw), f32),
                        pltpu.VMEM((tm, gw), bf16)],
        compiler_params=_params(("parallel", "arbitrary")),
        name="pool_branch",
    )(proj, proj, pool_w, pool_scale.reshape(1, D_MODEL), gates)


def _fill_ext(cur_ref, halo_ref, ext_ref, c):
    halo = halo_ref[...]
    ext_ref[0:HALO, :] = jnp.where(c == 0, jnp.zeros_like(halo), halo)
    ext_ref[HALO:HALO + CHUNK, :] = cur_ref[...]


def _conv_silu(ext_ref, sel, w_ref, b_ref, lanes):
    taps = jnp.dot(sel, ext_ref[:, lanes], preferred_element_type=f32)
    out = b_ref[:, lanes]
    for k in range(CONV_W):
        out = out + w_ref[k:k + 1, lanes] * taps[k * CHUNK:(k + 1) * CHUNK]
    return out * _sigmoid(out)


def _ssd_kernel(x_ref, xh_ref, bc_ref, bch_ref, z_ref, dt_ref, dtt_ref, alog_ref, alogt_ref,
                wx_ref, wbc_ref, bx_ref, bbc_ref, dskip_ref, nw_ref,
                vw_ref, ssq_ref,
                state_ref, extx_ref, extbc_ref):
    c = pl.program_id(0)

    @pl.when(c == 0)
    def _():
        state_ref[...] = jnp.zeros_like(state_ref)

    _fill_ext(x_ref, xh_ref, extx_ref, c)
    _fill_ext(bc_ref, bch_ref, extbc_ref, c)

    rows = c * CHUNK + lax.broadcasted_iota(i32, (CHUNK, 1), 0)
    row_ok = rows >= PAD_ROWS
    cols = c * CHUNK + lax.broadcasted_iota(i32, (1, CHUNK), 1)
    col_ok = cols >= PAD_ROWS
    li = lax.broadcasted_iota(i32, (CHUNK, CHUNK), 0)
    si = lax.broadcasted_iota(i32, (CHUNK, CHUNK), 1)
    causal = li >= si
    tril = causal.astype(f32)
    triu = (li <= si).astype(f32)
    lane = lax.broadcasted_iota(i32, (CHUNK, 2 * HEAD_DIM), 1)
    first = lane < HEAD_DIM
    first1 = lax.broadcasted_iota(i32, (1, 2 * HEAD_DIM), 1) < HEAD_DIM
    sr = lax.broadcasted_iota(i32, (CONV_W * CHUNK, HALO + CHUNK), 0)
    se = lax.broadcasted_iota(i32, (CONV_W * CHUNK, HALO + CHUNK), 1)
    sel = jnp.where(se == (sr % CHUNK) + (sr // CHUNK) + HALO - (CONV_W - 1), 1.0, 0.0).astype(bf16)
    hh = lax.broadcasted_iota(i32, (HEADS_PER_GROUP, GROUP_W), 0)
    hl = lax.broadcasted_iota(i32, (HEADS_PER_GROUP, GROUP_W), 1)
    head_lanes = jnp.where(hl // HEAD_DIM == hh, 1.0, 0.0).astype(bf16)

    def group(g, ssq):
        lo = pl.multiple_of(g * GROUP_W, GROUP_W)
        blo = pl.multiple_of(g * N_STATE, N_STATE)
        xs = _conv_silu(extx_ref, sel, wx_ref, bx_ref, pl.ds(lo, GROUP_W))
        bm = _conv_silu(extbc_ref, sel, wbc_ref, bbc_ref, pl.ds(blo, N_STATE))
        cm = _conv_silu(extbc_ref, sel, wbc_ref, bbc_ref, pl.ds(N_GROUPS * N_STATE + blo, N_STATE))

        dt = jnp.where(row_ok, dt_ref[g], 0.0)
        dtt = jnp.where(col_ok, dtt_ref[g], 0.0)
        a_dt = dt * (-jnp.exp(alog_ref[g]))
        a_dtt = dtt * (-jnp.exp(alogt_ref[g]))
        a_cs = jnp.dot(tril, a_dt, preferred_element_type=f32, precision=lax.Precision.HIGHEST)
        a_cst = jnp.dot(a_dtt, triu, preferred_element_type=f32, precision=lax.Precision.HIGHEST)
        last = a_cs[CHUNK - 1:CHUNK, :]
        e_cs = jnp.exp(a_cs)
        e_end = jnp.exp(last - a_cs)
        e_last = jnp.exp(last)
        a2 = a_cs * LOG2E
        a2t = a_cst * LOG2E

        bmb = bm.astype(bf16)
        cmb = cm.astype(bf16)
        cb = lax.dot_general(cmb, bmb, (((1,), (1,)), ((), ())), preferred_element_type=f32)
        state = state_ref[g]
        yoff = jnp.dot(cmb, state.astype(bf16), preferred_element_type=f32)

        scales = jnp.concatenate([dt, e_cs, e_end], axis=0).astype(bf16)
        spread = jnp.dot(scales, head_lanes, preferred_element_type=f32)
        xdt_g = xs * spread[0:CHUNK]
        xd2 = (xdt_g * spread[2 * CHUNK:3 * CHUNK]).astype(bf16)
        cd = []
        for q in range(HEADS_PER_GROUP // 2):
            ha, hb = 2 * q, 2 * q + 1
            sl = slice(q * 2 * HEAD_DIM, (q + 1) * 2 * HEAD_DIM)
            gl = pl.ds(pl.multiple_of(lo + q * 2 * HEAD_DIM, 2 * HEAD_DIM), 2 * HEAD_DIM)
            xq = xs[:, sl]
            xdtb = xdt_g[:, sl].astype(bf16)
            wa = cb * jnp.exp2(jnp.where(causal, a2[:, ha:ha + 1] - a2t[ha:ha + 1, :], -1e30))
            wb = cb * jnp.exp2(jnp.where(causal, a2[:, hb:hb + 1] - a2t[hb:hb + 1, :], -1e30))
            ya = jnp.dot(wa.astype(bf16), xdtb, preferred_element_type=f32)
            yb = jnp.dot(wb.astype(bf16), xdtb, preferred_element_type=f32)
            y = jnp.where(first, ya, yb) + yoff[:, sl] * spread[CHUNK:2 * CHUNK, sl] + dskip_ref[:, gl] * xq
            cd.append(jnp.where(first1, e_last[:, ha:ha + 1], e_last[:, hb:hb + 1]))
            zq = z_ref[:, gl].astype(f32)
            v = y * (zq * _sigmoid(zq))
            ssq = ssq + v * v
            vw_ref[:, gl] = (v * nw_ref[:, gl]).astype(vw_ref.dtype)

        bt = jnp.transpose(bm).astype(bf16)
        s_new = jnp.dot(bt, xd2, preferred_element_type=f32)
        state_ref[g] = state * jnp.concatenate(cd, axis=1) + s_new
        return ssq

    ssq = lax.fori_loop(0, N_GROUPS, group, jnp.zeros((CHUNK, 2 * HEAD_DIM), f32))
    ssq_ref[...] = jnp.broadcast_to(jnp.sum(ssq, axis=1, keepdims=True), (CHUNK, LANES))


def _ssd_branch(proj, dt, conv_w, conv_b, a_log, d_skip, ssd_norm_w):
    bc_w = 2 * N_GROUPS * N_STATE
    dt_g = dt.reshape(R, N_GROUPS, HEADS_PER_GROUP).transpose(1, 0, 2)
    dt_gt = dt_g.transpose(0, 2, 1)
    alog = a_log.reshape(N_GROUPS, 1, HEADS_PER_GROUP)
    alogt = a_log.reshape(N_GROUPS, HEADS_PER_GROUP, 1)
    conv_b2 = conv_b.reshape(1, -1)
    dskip_e = jnp.repeat(d_skip, HEAD_DIM).reshape(1, D_INNER)

    def cur(col, width):
        return pl.BlockSpec((pl.Element(CHUNK), pl.Element(width)),
                            lambda c: (pl.multiple_of(c * CHUNK, CHUNK), col))

    def halo(col, width):
        return pl.BlockSpec((pl.Element(HALO), pl.Element(width)),
                            lambda c: (pl.multiple_of(jnp.maximum(c * CHUNK - HALO, 0), HALO), col))

    full = lambda shape: pl.BlockSpec(shape, lambda c: (0,) * len(shape))
    in_specs = [
        cur(COL_X, D_INNER), halo(COL_X, D_INNER),
        cur(COL_B, bc_w), halo(COL_B, bc_w),
        cur(COL_Z, D_INNER),
        pl.BlockSpec((N_GROUPS, CHUNK, HEADS_PER_GROUP), lambda c: (0, c, 0)),
        pl.BlockSpec((N_GROUPS, HEADS_PER_GROUP, CHUNK), lambda c: (0, 0, c)),
        full((N_GROUPS, 1, HEADS_PER_GROUP)),
        full((N_GROUPS, HEADS_PER_GROUP, 1)),
        pl.BlockSpec((CONV_W, D_INNER), lambda c: (0, 0)),
        pl.BlockSpec((CONV_W, bc_w), lambda c: (0, D_INNER // bc_w)),
        pl.BlockSpec((1, D_INNER), lambda c: (0, 0)),
        pl.BlockSpec((1, bc_w), lambda c: (0, D_INNER // bc_w)),
        full((1, D_INNER)),
        full((1, D_INNER)),
    ]
    return pl.pallas_call(
        _ssd_kernel,
        grid=(N_CHUNKS,),
        in_specs=in_specs,
        out_specs=[
            pl.BlockSpec((CHUNK, D_INNER), lambda c: (c, 0)),
            pl.BlockSpec((CHUNK, LANES), lambda c: (c, 0)),
        ],
        out_shape=[
            jax.ShapeDtypeStruct((R, D_INNER), bf16),
            jax.ShapeDtypeStruct((R, LANES), f32),
        ],
        scratch_shapes=[
            pltpu.VMEM((N_GROUPS, N_STATE, GROUP_W), f32),
            pltpu.VMEM((HALO + CHUNK, D_INNER), bf16),
            pltpu.VMEM((HALO + CHUNK, bc_w), bf16),
        ],
        compiler_params=_params(("arbitrary",)),
        name="ssd_mixer",
    )(proj, proj, proj, proj, proj, dt_g, dt_gt, alog, alogt,
      conv_w, conv_w, conv_b2, conv_b2, dskip_e, ssd_norm_w.reshape(1, D_INNER))


def _router_kernel(h_ref, nw_ref, rw_ref, rb_ref, u_ref, idx_ref, gate_ref, *, tm):
    i = pl.program_id(0)
    h = h_ref[...]
    ms = jnp.mean(h * h, axis=-1, keepdims=True)
    u = h * lax.rsqrt(ms + EPS) * nw_ref[...]
    rows = i * tm + lax.broadcasted_iota(i32, (tm, 1), 0)
    u = jnp.where(rows >= PAD_ROWS, u, 0.0)
    u_ref[...] = u
    logits = jnp.dot(u, rw_ref[...], preferred_element_type=f32, precision=lax.Precision.HIGHEST) + rb_ref[...]
    e_iota = lax.broadcasted_iota(i32, (tm, N_EXPERTS), 1)
    lane = lax.broadcasted_iota(i32, (tm, LANES), 1)
    idx_out = jnp.zeros((tm, LANES), i32)
    val_out = jnp.zeros((tm, LANES), f32)
    vals = []
    for k in range(TOP_K):
        m = jnp.max(logits, axis=1, keepdims=True)
        sel = jnp.min(jnp.where(logits == m, e_iota, N_EXPERTS), axis=1, keepdims=True)
        vals.append(m)
        idx_out = jnp.where(lane == k, sel, idx_out)
        logits = jnp.where(e_iota == sel, -jnp.inf, logits)
    exps = [jnp.exp(v - vals[0]) for v in vals]
    denom = exps[0] + exps[1] + exps[2] + exps[3]
    for k in range(TOP_K):
        val_out = jnp.where(lane == k, exps[k] / denom, val_out)
    idx_ref[...] = idx_out
    gate_ref[...] = val_out


def _router(h1, norm_w, router_w, router_b, tm=320):
    return pl.pallas_call(
        functools.partial(_router_kernel, tm=tm),
        grid=(R // tm,),
        in_specs=[
            pl.BlockSpec((tm, D_MODEL), lambda i: (i, 0)),
            pl.BlockSpec((1, D_MODEL), lambda i: (0, 0)),
            pl.BlockSpec((D_MODEL, N_EXPERTS), lambda i: (0, 0)),
            pl.BlockSpec((1, N_EXPERTS), lambda i: (0, 0)),
        ],
        out_specs=[
            pl.BlockSpec((tm, D_MODEL), lambda i: (i, 0)),
            pl.BlockSpec((tm, LANES), lambda i: (i, 0)),
            pl.BlockSpec((tm, LANES), lambda i: (i, 0)),
        ],
        out_shape=[
            jax.ShapeDtypeStruct((R, D_MODEL), f32),
            jax.ShapeDtypeStruct((R, LANES), i32),
            jax.ShapeDtypeStruct((R, LANES), f32),
        ],
        compiler_params=_params(("parallel",)),
        name="router",
    )(h1, norm_w.reshape(1, D_MODEL), router_w, router_b.reshape(1, N_EXPERTS))


def _gather_x_kernel(pos_ref, nu_ref, nv_ref, src_ref, o_ref, idx_ref, buf_ref, sem, *, tm):
    i = pl.program_id(0)
    n_used = nu_ref[0]

    @pl.when(i == 0)
    def _():
        def invert(b, carry):
            for u in range(DMA_ISSUE_UNROLL):
                p = b * DMA_ISSUE_UNROLL + u
                idx_ref[pos_ref[p]] = lax.shift_right_logical(p, 2) + PAD_ROWS
            return carry

        lax.fori_loop(0, N_PAIRS // DMA_ISSUE_UNROLL, invert, 0)

    def row_copy(src_row, slot, r):
        return pltpu.make_async_copy(src_ref.at[pl.ds(src_row, 1)], buf_ref.at[slot, pl.ds(r, 1)], sem.at[slot])

    def issue_tile(step):
        slot = step % 2
        n_blocks = nv_ref[step] // DMA_ISSUE_UNROLL

        def issue_block(b, carry):
            for u in range(DMA_ISSUE_UNROLL):
                r = b * DMA_ISSUE_UNROLL + u
                row_copy(idx_ref[step * tm + r], slot, r).start()
            return carry

        def issue(r, carry):
            row_copy(idx_ref[step * tm + r], slot, r).start()
            return carry

        lax.fori_loop(0, n_blocks, issue_block, 0)
        lax.fori_loop(n_blocks * DMA_ISSUE_UNROLL, nv_ref[step], issue, 0)

    @pl.when(i == 0)
    def _():
        buf_ref[...] = jnp.zeros_like(buf_ref)
        issue_tile(0)

    @pl.when(i + 1 < n_used)
    def _():
        issue_tile(i + 1)

    @pl.when(i < n_used)
    def _():
        slot = i % 2

        @pl.when(nv_ref[i] == tm)
        def _():
            pltpu.make_async_copy(src_ref.at[pl.ds(0, tm)], buf_ref.at[slot], sem.at[slot]).wait()

        @pl.when(nv_ref[i] < tm)
        def _():
            def drain(r, carry):
                row_copy(0, slot, 0).wait()
                return carry

            lax.fori_loop(0, nv_ref[i], drain, 0)

        o_ref[...] = buf_ref[slot].astype(o_ref.dtype)

    @pl.when(i >= n_used)
    def _():
        o_ref[...] = jnp.zeros_like(o_ref)


def _gather_x(pos_flat, n_used, tile_rows, u2, tm=MOE_TM):
    assert TOP_K == 4
    return pl.pallas_call(
        functools.partial(_gather_x_kernel, tm=tm),
        grid_spec=pltpu.PrefetchScalarGridSpec(
            num_scalar_prefetch=3,
            grid=(XS_ROWS // tm,),
            in_specs=[pl.BlockSpec(memory_space=pl.ANY)],
            out_specs=pl.BlockSpec((tm, D_MODEL), lambda i, idx, nu, nv: (i, 0)),
            scratch_shapes=[pltpu.SMEM((XS_ROWS,), i32), pltpu.VMEM((2, tm, D_MODEL), f32),
                            pltpu.SemaphoreType.DMA((2,))],
        ),
        out_shape=jax.ShapeDtypeStruct((XS_ROWS, D_MODEL), bf16),
        compiler_params=_params(("arbitrary",)),
        name="gather_x",
    )(pos_flat, n_used, tile_rows, u2)


def _grouped_kernel(ge_ref, gt_ref, gn_ref, ng_ref, nu_ref, x_ref, *refs, n_w, tn, nj, ocols, w_col_offsets, tile_fn):
    w_hbm, b_refs, o_ref = refs[:n_w], refs[n_w:2 * n_w], refs[2 * n_w]
    wbuf_ref, wb_ref, obuf_ref, zbuf_ref, wsem, sem, zsem = refs[2 * n_w + 1:]
    g = pl.program_id(0)
    n_groups = ng_ref[0]
    active = g < n_groups
    gc = jnp.minimum(g, n_groups - 1)
    n_tiles = jnp.where(active, gn_ref[gc], 0)
    tile0 = gt_ref[gc]
    total_tiles = o_ref.shape[0] // MOE_TM

    def weight_copies(grp, jj, wslot):
        e = ge_ref[grp]
        return [pltpu.make_async_copy(
            w_hbm[k].at[e, :, pl.ds(pl.multiple_of((w_col_offsets[k] + jj) * tn, tn), tn)],
            wbuf_ref.at[wslot, k], wsem.at[wslot]) for k in range(n_w)]

    @pl.when(g == 0)
    def _():
        for cp in weight_copies(0, 0, 0):
            cp.start()
        zbuf_ref[...] = jnp.zeros_like(zbuf_ref)

        def zero_copy(t, jc):
            rows = pl.ds(pl.multiple_of(t * MOE_TM, MOE_TM), MOE_TM)
            return pltpu.make_async_copy(zbuf_ref, o_ref.at[rows, pl.ds(jc * ocols, ocols)], zsem)

        def z_start(t, carry):
            for jc in range(nj):
                zero_copy(t, jc).start()
            return carry

        def z_wait(t, carry):
            for jc in range(nj):
                zero_copy(t, jc).wait()
            return carry

        lax.fori_loop(nu_ref[0], total_tiles, z_start, 0)
        lax.fori_loop(nu_ref[0], total_tiles, z_wait, 0)

    def out_copy(slot, n, tile, col):
        rows = pl.ds(pl.multiple_of(tile * MOE_TM, MOE_TM), n * MOE_TM)
        return pltpu.make_async_copy(obuf_ref.at[slot, pl.ds(0, n * MOE_TM)], o_ref.at[rows, pl.ds(col, ocols)],
                                     sem.at[slot])

    def wait_step(s):
        gs = s // nj
        sent = jnp.where(jnp.logical_and(s >= 0, gs < n_groups), gn_ref[jnp.clip(gs, 0, MAX_GROUPS - 1)], 0)
        for n in range(1, MOE_CAP + 1):

            @pl.when(sent == n)
            def _(n=n):
                out_copy(s % 2, n, 0, 0).wait()

    def col_block(j, carry):
        step = g * nj + j
        slot = step % 2

        @pl.when(j + 1 < nj)
        def _():
            for cp in weight_copies(gc, j + 1, 1 - slot):
                cp.start()

        @pl.when(jnp.logical_and(j + 1 == nj, g + 1 < n_groups))
        def _():
            for cp in weight_copies(g + 1, 0, 1 - slot):
                cp.start()

        for cp in weight_copies(gc, j, slot):
            cp.wait()
        for k in range(n_w):
            wb_ref[:, k * tn:(k + 1) * tn] = wbuf_ref[slot, k].astype(bf16)

        wait_step(step - 2)
        col = pl.multiple_of(j * ocols, ocols)
        biases = [b_refs[k][0, :, pl.ds(pl.multiple_of((w_col_offsets[k] + j) * tn, tn), tn)] for k in range(n_w)]
        for n in range(1, MOE_CAP + 1):

            @pl.when(n_tiles == n)
            def _(n=n):
                acc = jnp.dot(x_ref[0:n * MOE_TM, :], wb_ref[...], preferred_element_type=f32)
                obuf_ref[slot, pl.ds(0, n * MOE_TM)] = tile_fn(acc, biases)
                out_copy(slot, n, tile0, col).start()

        return carry

    @pl.when(active)
    def _():
        lax.fori_loop(0, nj, col_block, 0)

    @pl.when(g == MAX_GROUPS - 1)
    def _():
        last = n_groups * nj - 1
        wait_step(last - 1)
        wait_step(last)


def _grouped_call(groups, n_used, x, ws, bs, *, k_dim, tn, nj, w_col_offsets, ocols, out_shape, tile_fn, name):
    g_expert, g_tile0, g_ntiles, n_groups = groups
    n_w = len(ws)

    def gi(g, ng):
        return jnp.minimum(g, ng[0] - 1)

    def x_map(g, ge, gt, gn, ng, nu):
        return (pl.multiple_of(gt[gi(g, ng)] * MOE_TM, MOE_TM), 0)

    def b_map(g, ge, gt, gn, ng, nu):
        return (ge[gi(g, ng)], 0, 0)

    return pl.pallas_call(
        functools.partial(_grouped_kernel, n_w=n_w, tn=tn, nj=nj, ocols=ocols, w_col_offsets=w_col_offsets,
                          tile_fn=tile_fn),
        grid_spec=pltpu.PrefetchScalarGridSpec(
            num_scalar_prefetch=5,
            grid=(MAX_GROUPS,),
            in_specs=[
                pl.BlockSpec((pl.Element(GROUP_ROWS), pl.Element(k_dim)), x_map),
                *[pl.BlockSpec(memory_space=pl.ANY) for _ in ws],
                *[pl.BlockSpec((1, 1, b.shape[2]), b_map) for b in bs],
            ],
            out_specs=pl.BlockSpec(memory_space=pl.ANY),
            scratch_shapes=[
                pltpu.VMEM((2, n_w, k_dim, tn), f32),
                pltpu.VMEM((k_dim, n_w * tn), bf16),
                pltpu.VMEM((2, GROUP_ROWS, ocols), out_shape.dtype),
                pltpu.VMEM((MOE_TM, ocols), out_shape.dtype),
                pltpu.SemaphoreType.DMA((2,)),
                pltpu.SemaphoreType.DMA((2,)),
                pltpu.SemaphoreType.DMA(()),
            ],
        ),
        out_shape=out_shape,
        compiler_params=_params(("arbitrary",)),
        name=name,
    )(g_expert, g_tile0, g_ntiles, n_groups, n_used, x, *ws, *bs)


def _moe_up(groups, n_used, xs, w_up, b_up, tn=256):
    nj = D_EXPERT // tn

    def tile_fn(acc, biases):
        gate = jnp.minimum(acc[:, :tn] + biases[0], LIMIT)
        up = jnp.clip(acc[:, tn:] + biases[1], -LIMIT, LIMIT)
        return ((up + 1.0) * (gate * _sigmoid(ALPHA * gate))).astype(bf16)

    b3 = b_up.reshape(N_EXPERTS, 1, -1)
    return _grouped_call(groups, n_used, xs, (w_up, w_up), (b3, b3), k_dim=D_MODEL, tn=tn, nj=nj,
                         w_col_offsets=(0, nj), ocols=tn,
                         out_shape=jax.ShapeDtypeStruct((XS_ROWS, D_EXPERT), bf16), tile_fn=tile_fn, name="moe_up")


def _moe_down(groups, n_used, act, w_down, b_down, tn=1024):
    return _grouped_call(groups, n_used, act, (w_down,), (b_down.reshape(N_EXPERTS, 1, -1),), k_dim=D_EXPERT,
                         tn=tn, nj=D_MODEL // tn, w_col_offsets=(0,), ocols=tn,
                         out_shape=jax.ShapeDtypeStruct((N_SLOTS, D_MODEL), f32),
                         tile_fn=lambda acc, biases: acc + biases[0], name="moe_down")


def _combine_kernel(pos_ref, h_ref, g_ref, nw_ref, y_ref, o_ref, buf_ref, sem, *, tm):
    i = pl.program_id(0)
    n = pl.num_programs(0)

    def row_copy(src_row, slot, k, r):
        return pltpu.make_async_copy(y_ref.at[pl.ds(src_row, 1)], buf_ref.at[slot, k, pl.ds(r, 1)], sem.at[slot])

    def issue_tile(step):
        slot = step % 2
        for k in range(TOP_K):

            def issue(r, carry, k=k):
                row_copy(pos_ref[k * SEQ + step * tm + r], slot, k, r).start()
                return carry

            lax.fori_loop(0, tm, issue, 0, unroll=DMA_ISSUE_UNROLL)

    @pl.when(i == 0)
    def _():
        issue_tile(0)

    @pl.when(i + 1 < n)
    def _():
        issue_tile(i + 1)

    slot = i % 2
    for k in range(TOP_K):
        pltpu.make_async_copy(y_ref.at[pl.ds(0, tm)], buf_ref.at[slot, k], sem.at[slot]).wait()

    g = g_ref[...]
    h = h_ref[...]
    for k in range(TOP_K):
        h = h + g[:, k:k + 1] * buf_ref[slot, k]
    ms = jnp.mean(h * h, axis=-1, keepdims=True)
    o_ref[...] = h * lax.rsqrt(ms + EPS) * nw_ref[...]


def _combine(pos_kt, h1, y, gates, norm_w, tm=128):
    nb = SEQ // tm
    off = (PAD_ROWS + N_META) // tm
    return pl.pallas_call(
        functools.partial(_combine_kernel, tm=tm),
        grid_spec=pltpu.PrefetchScalarGridSpec(
            num_scalar_prefetch=1,
            grid=(nb,),
            in_specs=[
                pl.BlockSpec((tm, D_MODEL), lambda i, pos: (i + off, 0)),
                pl.BlockSpec((tm, LANES), lambda i, pos: (i + off, 0)),
                pl.BlockSpec((1, D_MODEL), lambda i, pos: (0, 0)),
                pl.BlockSpec(memory_space=pl.ANY),
            ],
            out_specs=pl.BlockSpec((tm, D_MODEL), lambda i, pos: (i, 0)),
            scratch_shapes=[pltpu.VMEM((2, TOP_K, tm, D_MODEL), f32), pltpu.SemaphoreType.DMA((2,))],
        ),
        out_shape=jax.ShapeDtypeStruct((SEQ, D_MODEL), f32),
        compiler_params=_params(("arbitrary",)),
        name="combine_final_norm",
    )(pos_kt, h1, gates, norm_w.reshape(1, D_MODEL), y)


def _rank_kernel(idx_ref, rank_ref, cnt_ref, carry_ref, *, tm):
    i = pl.program_id(0)

    @pl.when(i == 0)
    def _():
        carry_ref[...] = jnp.zeros_like(carry_ref)

    idx = idx_ref[...]
    rows = i * tm + lax.broadcasted_iota(i32, (tm, 1), 0)
    valid = rows >= PAD_ROWS
    e_iota = lax.broadcasted_iota(i32, (tm, N_EXPERTS), 1)
    hits = [jnp.logical_and(idx[:, k:k + 1] == e_iota, valid) for k in range(TOP_K)]
    onehot = sum(jnp.where(h, 1.0, 0.0) for h in hits)
    earlier = lax.broadcasted_iota(i32, (tm, tm), 0) > lax.broadcasted_iota(i32, (tm, tm), 1)
    before = jnp.dot(jnp.where(earlier, 1.0, 0.0).astype(bf16), onehot.astype(bf16),
                     preferred_element_type=f32) + carry_ref[...]
    lane = lax.broadcasted_iota(i32, (tm, LANES), 1)
    out = jnp.zeros((tm, LANES), i32)
    for k in range(TOP_K):
        rk = jnp.sum(jnp.where(hits[k], before, 0.0), axis=1, keepdims=True)
        out = jnp.where(lane == k, rk.astype(i32), out)
    rank_ref[...] = out
    carry_ref[...] += jnp.sum(onehot, axis=0, keepdims=True)
    cnt_ref[...] = carry_ref[...]


def _rank_pairs(top_idx, tm=320):
    return pl.pallas_call(
        functools.partial(_rank_kernel, tm=tm),
        grid=(R // tm,),
        in_specs=[pl.BlockSpec((tm, LANES), lambda i: (i, 0))],
        out_specs=[pl.BlockSpec((tm, LANES), lambda i: (i, 0)), pl.BlockSpec((1, N_EXPERTS), lambda i: (0, 0))],
        out_shape=[jax.ShapeDtypeStruct((R, LANES), i32), jax.ShapeDtypeStruct((1, N_EXPERTS), f32)],
        scratch_shapes=[pltpu.VMEM((1, N_EXPERTS), f32)],
        compiler_params=_params(("arbitrary",)),
        name="rank_pairs",
    )(top_idx)


def _routing_tables(top_idx, rank, counts):
    counts = counts.reshape(N_EXPERTS).astype(i32)
    tiles = (counts + MOE_TM - 1) // MOE_TM
    tile_end = jnp.cumsum(tiles)
    tile_start = tile_end - tiles
    e_ids = jnp.arange(N_EXPERTS, dtype=i32)
    tok_e = top_idx[PAD_ROWS:, :TOP_K]
    tok_start = jnp.sum(jnp.where(tok_e[..., None] == e_ids, tile_start * MOE_TM, 0), axis=-1)
    pos = tok_start + rank[PAD_ROWS:, :TOP_K]
    n_used = tile_end[-1:].astype(i32)
    tile_ids = jnp.arange(XS_ROWS // MOE_TM, dtype=i32)
    tile_e = jnp.minimum(jnp.sum(tile_end[None, :] <= tile_ids[:, None], axis=1), N_EXPERTS - 1)
    tile_rows = jnp.clip(counts[tile_e] - (tile_ids - tile_start[tile_e]) * MOE_TM, 0, MOE_TM).astype(i32)
    tile_rows = jnp.where(tile_ids < n_used[0], tile_rows, 0)
    groups = (tiles + MOE_CAP - 1) // MOE_CAP
    group_end = jnp.cumsum(groups)
    g_ids = jnp.arange(MAX_GROUPS, dtype=i32)
    g_expert = jnp.minimum(jnp.sum(group_end[None, :] <= g_ids[:, None], axis=1), N_EXPERTS - 1).astype(i32)
    g_local = g_ids - (group_end - groups)[g_expert]
    g_tile0 = (tile_start[g_expert] + g_local * MOE_CAP).astype(i32)
    g_ntiles = jnp.clip(tiles[g_expert] - g_local * MOE_CAP, 0, MOE_CAP).astype(i32)
    n_groups = group_end[-1:].astype(i32)
    return pos, n_used, tile_rows, (g_expert, g_tile0, g_ntiles, n_groups)


def kernel(x, meta_tokens, norm_mix_w, w_in, conv_w, conv_b, dt_bias, a_log, d_skip, ssd_norm_w, w_ssd_out,
           pool_w, pool_scale, w_out, norm_ffn_w, router_w, router_b, w_up, b_up, w_down, b_down, norm_final_w):
    assert x.shape == (1, SEQ, D_MODEL) and norm_mix_w.shape[0] == 1
    x2 = x[0]
    head = jnp.concatenate([jnp.zeros((PAD_ROWS, D_MODEL), f32), meta_tokens.astype(f32)], axis=0)

    u = _rms_norm_rows(x2, head, norm_mix_w[0])
    proj = _in_proj_main(u, w_in[0])
    dt = _in_proj_dt(u, w_in[0], dt_bias[0])
    gates = _in_proj_gates(u, w_in[0])
    ma = _pool_branch(proj, gates, pool_w[0], pool_scale[0])
    vw, ssq = _ssd_branch(proj, dt, conv_w[0], conv_b[0], a_log[0], d_skip[0], ssd_norm_w[0])
    merged = _ssd_out_merge(vw, w_ssd_out[0], ssq, ma, gates)
    h1 = _out_proj(merged, w_out[0], head, x2)

    u2, top_idx, top_gate = _router(h1, norm_ffn_w[0], router_w[0], router_b[0])
    rank, counts = _rank_pairs(top_idx)
    pos, n_used, tile_rows, groups = _routing_tables(top_idx, rank, counts)
    xs = _gather_x(pos.reshape(-1), n_used, tile_rows, u2)
    act = _moe_up(groups, n_used, xs, w_up[0], b_up[0])
    y = _moe_down(groups, n_used, act, w_down[0], b_down[0])
    pos_kt = pos[N_META:].T.reshape(-1)
    out = _combine(pos_kt, h1, y, top_gate, norm_final_w)
    return out[None]
```

```python
import functools

import jax
import jax.numpy as jnp
from jax import lax
from jax.experimental import pallas as pl
from jax.experimental.pallas import tpu as pltpu

f32 = jnp.float32
bf16 = jnp.bfloat16
i32 = jnp.int32

D_MODEL = 4096
SEQ = 8192
N_META = 16
CHUNK = 128
PAD_ROWS = CHUNK - N_META
R = PAD_ROWS + N_META + SEQ
N_CHUNKS = R // CHUNK
POOL_WINDOWS = (2, 4, 8, 16)
POOL_GROUP = D_MODEL // 4
D_INNER = 2 * D_MODEL
HEAD_DIM = 64
N_HEADS = D_INNER // HEAD_DIM
N_STATE = 128
N_GROUPS = 8
HEADS_PER_GROUP = N_HEADS // N_GROUPS
GROUP_W = HEADS_PER_GROUP * HEAD_DIM
CONV_W = 4
HALO = 16
POOL_HALO = 2 * max(POOL_WINDOWS)
N_EXPERTS = 32
TOP_K = 4
D_EXPERT = 7 * D_MODEL // 16
LIMIT = 7.0
ALPHA = 1.702
EPS = 1e-5
LOG2E = 1.4426950408889634
N_TOK = N_META + SEQ
N_PAIRS = N_TOK * TOP_K
MOE_TM = 256
N_TILES = -(-N_PAIRS // MOE_TM) + N_EXPERTS
N_SLOTS = N_TILES * MOE_TM
MOE_CAP = 6
MOE_SUB = 128
GROUP_ROWS = MOE_CAP * MOE_TM
MAX_GROUPS = (N_TILES - 1 + N_EXPERTS * (MOE_CAP - 1)) // MOE_CAP
XS_ROWS = N_SLOTS + GROUP_ROWS
COL_Z = D_MODEL
COL_X = COL_Z + D_INNER
COL_B = COL_X + D_INNER
COL_C = COL_B + N_GROUPS * N_STATE
COL_DT = COL_C + N_GROUPS * N_STATE
COL_G = COL_DT + N_HEADS
SEG1_COLS = COL_DT
LANES = 128
V7X_VMEM_LIMIT = 56 * 1024 * 1024
DMA_ISSUE_UNROLL = 8


def _params(sem, vmem=V7X_VMEM_LIMIT):
    return pltpu.CompilerParams(dimension_semantics=sem, vmem_limit_bytes=vmem)


def _sigmoid(v):
    return 1.0 / (1.0 + jnp.exp(-v))


def _sigmoid_tanh(v):
    return 0.5 * jnp.tanh(0.5 * v) + 0.5


def _x_rows_spec(tm, tn, col_of):
    return pl.BlockSpec((pl.Element(tm), pl.Element(tn)),
                        lambda i, *rest: (pl.multiple_of(jnp.maximum(i * tm - CHUNK, 0), CHUNK), col_of(i, *rest)))


def _rms_kernel(x_ref, head_ref, w_ref, o_ref):
    i = pl.program_id(0)
    tm = o_ref.shape[0]

    def norm(h):
        ms = jnp.mean(h * h, axis=-1, keepdims=True)
        return (h * lax.rsqrt(ms + EPS) * w_ref[...]).astype(o_ref.dtype)

    @pl.when(i == 0)
    def _():
        o_ref[0:CHUNK, :] = norm(head_ref[...])
        o_ref[CHUNK:tm, :] = norm(x_ref[0:tm - CHUNK, :])

    @pl.when(i > 0)
    def _():
        o_ref[...] = norm(x_ref[...])


def _rms_norm_rows(x2, head, w, tm=640):
    d = x2.shape[1]
    return pl.pallas_call(
        _rms_kernel,
        grid=(R // tm,),
        in_specs=[_x_rows_spec(tm, d, lambda i: 0), pl.BlockSpec((CHUNK, d), lambda i: (0, 0)),
                  pl.BlockSpec((1, d), lambda i: (0, 0))],
        out_specs=pl.BlockSpec((tm, d), lambda i: (i, 0)),
        out_shape=jax.ShapeDtypeStruct((R, d), bf16),
        compiler_params=_params(("parallel",)),
        name="rms_mix",
    )(x2, head, w.reshape(1, d))


def _mm_kernel(*refs, n_extra, epilogue, nk):
    a_ref, w_ref = refs[0], refs[1]
    extra = refs[2:2 + n_extra]
    o_ref = refs[2 + n_extra]
    acc_ref = refs[3 + n_extra]
    k = pl.program_id(2)

    @pl.when(k == 0)
    def _():
        acc_ref[...] = jnp.dot(a_ref[...], w_ref[...].astype(bf16), preferred_element_type=f32)

    @pl.when(k > 0)
    def _():
        acc_ref[...] += jnp.dot(a_ref[...], w_ref[...].astype(bf16), preferred_element_type=f32)

    @pl.when(k == nk - 1)
    def _():
        epilogue(o_ref, acc_ref, *extra)


def _store(fn):
    def epilogue(o_ref, acc_ref, *extra):
        o_ref[...] = fn(acc_ref[...], *[e[...] for e in extra]).astype(o_ref.dtype)

    return epilogue


def _matmul(a, w, *, col0, n, tm, tn, tk, out_dtype, epilogue, extras=(), extra_specs=(), name):
    m, kdim = a.shape
    nk = kdim // tk
    if col0 % tn == 0:
        w_spec = pl.BlockSpec((tk, tn), lambda i, j, k: (k, j + col0 // tn))
    else:
        w_spec = pl.BlockSpec((pl.Element(tk), pl.Element(tn)), lambda i, j, k: (k * tk, pl.multiple_of(col0 + j * tn, LANES)))
    return pl.pallas_call(
        functools.partial(_mm_kernel, n_extra=len(extras), epilogue=epilogue, nk=nk),
        grid=(m // tm, n // tn, nk),
        in_specs=[pl.BlockSpec((tm, tk), lambda i, j, k: (i, k)), w_spec, *extra_specs],
        out_specs=pl.BlockSpec((tm, tn), lambda i, j, k: (i, j)),
        out_shape=jax.ShapeDtypeStruct((m, n), out_dtype),
        scratch_shapes=[pltpu.VMEM((tm, tn), f32)],
        compiler_params=_params(("parallel", "parallel", "arbitrary")),
        name=name,
    )(a, w, *extras)


def _softplus_bias(acc, bias):
    v = acc + bias
    return jnp.maximum(v, 0.0) + jnp.log1p(jnp.exp(-jnp.abs(v)))


def _in_proj_main(u, w_in):
    return _matmul(u, w_in, col0=0, n=SEG1_COLS, tm=1664, tn=1024, tk=2048, out_dtype=bf16,
                   epilogue=_store(lambda acc: acc), name="in_proj_main")


def _in_proj_dt(u, w_in, dt_bias):
    return _matmul(u, w_in, col0=COL_DT, n=N_HEADS, tm=1664, tn=N_HEADS, tk=1024, out_dtype=f32,
                   epilogue=_store(_softplus_bias), extras=(dt_bias.reshape(1, N_HEADS),),
                   extra_specs=(pl.BlockSpec((1, N_HEADS), lambda i, j, k: (0, 0)),), name="in_proj_dt")


def _in_proj_gates(u, w_in):
    return _matmul(u, w_in, col0=COL_G, n=2 * D_MODEL, tm=1664, tn=1024, tk=2048, out_dtype=bf16,
                   epilogue=_store(_sigmoid_tanh), name="in_proj_gates")


def _ssd_out_merge(vw, w_ssd_out, ssq, ma, gates, tm=1664, tn=1024):
    gb_off = D_MODEL // tn

    def merge(acc, ssq_t, ma_t, gb_t):
        rs = lax.rsqrt(ssq_t[:, :1] * (1.0 / D_INNER) + EPS)
        return ma_t.astype(f32) + gb_t.astype(f32) * (acc * rs)

    return _matmul(vw, w_ssd_out, col0=0, n=D_MODEL, tm=tm, tn=tn, tk=1024, out_dtype=bf16, epilogue=_store(merge),
                   extras=(ssq, ma, gates),
                   extra_specs=(pl.BlockSpec((tm, LANES), lambda i, j, k: (i, 0)),
                                pl.BlockSpec((tm, tn), lambda i, j, k: (i, j)),
                                pl.BlockSpec((tm, tn), lambda i, j, k: (i, j + gb_off))),
                   name="ssd_out_merge")


def _out_proj(merged, w_out, head, x2, tm=1664, tn=1024):
    def add_residual(o_ref, acc_ref, head_ref, x_ref):
        i = pl.program_id(0)

        @pl.when(i == 0)
        def _():
            o_ref[0:CHUNK, :] = head_ref[...] + acc_ref[0:CHUNK, :]
            o_ref[CHUNK:tm, :] = x_ref[0:tm - CHUNK, :] + acc_ref[CHUNK:tm, :]

        @pl.when(i > 0)
        def _():
            o_ref[...] = x_ref[...] + acc_ref[...]

    return _matmul(merged, w_out, col0=0, n=D_MODEL, tm=tm, tn=tn, tk=1024, out_dtype=f32,
                   epilogue=add_residual, extras=(head, x2),
                   extra_specs=(pl.BlockSpec((CHUNK, tn), lambda i, j, k: (0, j)),
                                _x_rows_spec(tm, tn, lambda i, j, k: pl.multiple_of(j * tn, tn))),
                   name="out_proj_residual")


def _pool_kernel(x_ref, halo_ref, w_ref, scale_ref, ga_ref, o_ref, a_ref, b_ref, pooled_ref, *, tm):
    assert POOL_WINDOWS == (2, 4, 8, 16)
    gi = pl.program_id(0)
    i = pl.program_id(1)
    n = POOL_HALO + tm
    halo = halo_ref[...].astype(f32)
    a_ref[0:POOL_HALO, :] = jnp.where(i == 0, 0.0, halo)
    a_ref[POOL_HALO:n, :] = x_ref[...].astype(f32)
    t = i * tm + lax.broadcasted_iota(i32, (tm, 1), 0) - PAD_ROWS

    def pair_sum(src_ref, shift, start):
        return src_ref[start:n, :] + src_ref[start - shift:n - shift, :]

    def finish(s, win):
        cnt = jnp.clip(t + 1, 1, win).astype(f32)
        pooled_ref[...] = (s / cnt - x_ref[...].astype(f32)).astype(bf16)

    @pl.when(gi == 0)
    def _():
        finish(pair_sum(a_ref, 1, POOL_HALO), 2)

    @pl.when(gi == 1)
    def _():
        b_ref[8:n, :] = pair_sum(a_ref, 1, 8)
        finish(pair_sum(b_ref, 2, POOL_HALO), 4)

    @pl.when(gi == 2)
    def _():
        b_ref[8:n, :] = pair_sum(a_ref, 1, 8)
        a_ref[16:n, :] = pair_sum(b_ref, 2, 16)
        finish(pair_sum(a_ref, 4, POOL_HALO), 8)

    @pl.when(gi == 3)
    def _():
        b_ref[8:n, :] = pair_sum(a_ref, 1, 8)
        a_ref[16:n, :] = pair_sum(b_ref, 2, 16)
        b_ref[24:n, :] = pair_sum(a_ref, 4, 24)
        finish(pair_sum(b_ref, 8, POOL_HALO), 16)

    acc = jnp.dot(pooled_ref[...], w_ref[0].astype(bf16), preferred_element_type=f32)
    o_ref[...] = (ga_ref[...].astype(f32) * (acc * scale_ref[...])).astype(o_ref.dtype)


def _pool_branch(proj, gates, pool_w, pool_scale, tm=1664):
    gw = POOL_GROUP
    return pl.pallas_call(
        functools.partial(_pool_kernel, tm=tm),
        grid=(4, R // tm),
        in_specs=[
            pl.BlockSpec((tm, gw), lambda g, i: (i, g)),
            pl.BlockSpec((POOL_HALO, gw), lambda g, i: (jnp.maximum(i * (tm // POOL_HALO) - 1, 0), g)),
            pl.BlockSpec((1, gw, gw), lambda g, i: (g, 0, 0)),
            pl.BlockSpec((1, gw), lambda g, i: (0, g)),
            pl.BlockSpec((tm, gw), lambda g, i: (i, g)),
        ],
        out_specs=pl.BlockSpec((tm, gw), lambda g, i: (i, g)),
        out_shape=jax.ShapeDtypeStruct((R, D_MODEL), bf16),
        scratch_shapes=[pltpu.VMEM((POOL_HALO + tm, gw), f32), pltpu.VMEM((POOL_HALO + tm, gw), f32),
                        pltpu.VMEM((tm, gw), bf16)],
        compiler_params=_params(("parallel", "arbitrary")),
        name="pool_branch",
    )(proj, proj, pool_w, pool_scale.reshape(1, D_MODEL), gates)


def _fill_ext(cur_ref, halo_ref, ext_ref, c):
    halo = halo_ref[...]
    ext_ref[0:HALO, :] = jnp.where(c == 0, jnp.zeros_like(halo), halo)
    ext_ref[HALO:HALO + CHUNK, :] = cur_ref[...]


def _conv_silu(ext_ref, sel, w_ref, b_ref, lanes):
    taps = jnp.dot(sel, ext_ref[:, lanes], preferred_element_type=f32)
    out = b_ref[:, lanes]
    for k in range(CONV_W):
        out = out + w_ref[k:k + 1, lanes] * taps[k * CHUNK:(k + 1) * CHUNK]
    return out * _sigmoid(out)


def _ssd_kernel(x_ref, xh_ref, bc_ref, bch_ref, z_ref, dt_ref, dtt_ref, alog_ref, alogt_ref,
                wx_ref, wbc_ref, bx_ref, bbc_ref, dskip_ref, nw_ref,
                vw_ref, ssq_ref,
                state_ref, extx_ref, extbc_ref):
    c = pl.program_id(0)

    @pl.when(c == 0)
    def _():
        state_ref[...] = jnp.zeros_like(state_ref)

    _fill_ext(x_ref, xh_ref, extx_ref, c)
    _fill_ext(bc_ref, bch_ref, extbc_ref, c)

    rows = c * CHUNK + lax.broadcasted_iota(i32, (CHUNK, 1), 0)
    row_ok = rows >= PAD_ROWS
    cols = c * CHUNK + lax.broadcasted_iota(i32, (1, CHUNK), 1)
    col_ok = cols >= PAD_ROWS
    li = lax.broadcasted_iota(i32, (CHUNK, CHUNK), 0)
    si = lax.broadcasted_iota(i32, (CHUNK, CHUNK), 1)
    causal = li >= si
    tril = causal.astype(f32)
    triu = (li <= si).astype(f32)
    lane = lax.broadcasted_iota(i32, (CHUNK, 2 * HEAD_DIM), 1)
    first = lane < HEAD_DIM
    first1 = lax.broadcasted_iota(i32, (1, 2 * HEAD_DIM), 1) < HEAD_DIM
    sr = lax.broadcasted_iota(i32, (CONV_W * CHUNK, HALO + CHUNK), 0)
    se = lax.broadcasted_iota(i32, (CONV_W * CHUNK, HALO + CHUNK), 1)
    sel = jnp.where(se == (sr % CHUNK) + (sr // CHUNK) + HALO - (CONV_W - 1), 1.0, 0.0).astype(bf16)
    hh = lax.broadcasted_iota(i32, (HEADS_PER_GROUP, GROUP_W), 0)
    hl = lax.broadcasted_iota(i32, (HEADS_PER_GROUP, GROUP_W), 1)
    head_lanes = jnp.where(hl // HEAD_DIM == hh, 1.0, 0.0).astype(bf16)

    def group(g, ssq):
        lo = pl.multiple_of(g * GROUP_W, GROUP_W)
        blo = pl.multiple_of(g * N_STATE, N_STATE)
        xs = _conv_silu(extx_ref, sel, wx_ref, bx_ref, pl.ds(lo, GROUP_W))
        bm = _conv_silu(extbc_ref, sel, wbc_ref, bbc_ref, pl.ds(blo, N_STATE))
        cm = _conv_silu(extbc_ref, sel, wbc_ref, bbc_ref, pl.ds(N_GROUPS * N_STATE + blo, N_STATE))

        dt = jnp.where(row_ok, dt_ref[g], 0.0)
        dtt = jnp.where(col_ok, dtt_ref[g], 0.0)
        a_dt = dt * (-jnp.exp(alog_ref[g]))
        a_dtt = dtt * (-jnp.exp(alogt_ref[g]))
        a_cs = jnp.dot(tril, a_dt, preferred_element_type=f32, precision=lax.Precision.HIGHEST)
        a_cst = jnp.dot(a_dtt, triu, preferred_element_type=f32, precision=lax.Precision.HIGHEST)
        last = a_cs[CHUNK - 1:CHUNK, :]
        e_cs = jnp.exp(a_cs)
        e_end = jnp.exp(last - a_cs)
        e_last = jnp.exp(last)
        a2 = a_cs * LOG2E
        a2t = a_cst * LOG2E

        bmb = bm.astype(bf16)
        cmb = cm.astype(bf16)
        cb = lax.dot_general(cmb, bmb, (((1,), (1,)), ((), ())), preferred_element_type=f32)
        state = state_ref[g]
        yoff = jnp.dot(cmb, state.astype(bf16), preferred_element_type=f32)

        scales = jnp.concatenate([dt, e_cs, e_end], axis=0).astype(bf16)
        spread = jnp.dot(scales, head_lanes, preferred_element_type=f32)
        xdt_g = xs * spread[0:CHUNK]
        xd2 = (xdt_g * spread[2 * CHUNK:3 * CHUNK]).astype(bf16)
        cd = []
        for q in range(HEADS_PER_GROUP // 2):
            ha, hb = 2 * q, 2 * q + 1
            sl = slice(q * 2 * HEAD_DIM, (q + 1) * 2 * HEAD_DIM)
            gl = pl.ds(pl.multiple_of(lo + q * 2 * HEAD_DIM, 2 * HEAD_DIM), 2 * HEAD_DIM)
            xq = xs[:, sl]
            xdtb = xdt_g[:, sl].astype(bf16)
            wa = cb * jnp.exp2(jnp.where(causal, a2[:, ha:ha + 1] - a2t[ha:ha + 1, :], -1e30))
            wb = cb * jnp.exp2(jnp.where(causal, a2[:, hb:hb + 1] - a2t[hb:hb + 1, :], -1e30))
            ya = jnp.dot(wa.astype(bf16), xdtb, preferred_element_type=f32)
            yb = jnp.dot(wb.astype(bf16), xdtb, preferred_element_type=f32)
            y = jnp.where(first, ya, yb) + yoff[:, sl] * spread[CHUNK:2 * CHUNK, sl] + dskip_ref[:, gl] * xq
            cd.append(jnp.where(first1, e_last[:, ha:ha + 1], e_last[:, hb:hb + 1]))
            zq = z_ref[:, gl].astype(f32)
            v = y * (zq * _sigmoid(zq))
            ssq = ssq + v * v
            vw_ref[:, gl] = (v * nw_ref[:, gl]).astype(vw_ref.dtype)

        bt = jnp.transpose(bm).astype(bf16)
        s_new = jnp.dot(bt, xd2, preferred_element_type=f32)
        state_ref[g] = state * jnp.concatenate(cd, axis=1) + s_new
        return ssq

    ssq = lax.fori_loop(0, N_GROUPS, group, jnp.zeros((CHUNK, 2 * HEAD_DIM), f32))
    ssq_ref[...] = jnp.broadcast_to(jnp.sum(ssq, axis=1, keepdims=True), (CHUNK, LANES))


def _ssd_branch(proj, dt, conv_w, conv_b, a_log, d_skip, ssd_norm_w):
    bc_w = 2 * N_GROUPS * N_STATE
    dt_g = dt.reshape(R, N_GROUPS, HEADS_PER_GROUP).transpose(1, 0, 2)
    dt_gt = dt_g.transpose(0, 2, 1)
    alog = a_log.reshape(N_GROUPS, 1, HEADS_PER_GROUP)
    alogt = a_log.reshape(N_GROUPS, HEADS_PER_GROUP, 1)
    conv_b2 = conv_b.reshape(1, -1)
    dskip_e = jnp.repeat(d_skip, HEAD_DIM).reshape(1, D_INNER)

    def cur(col, width):
        return pl.BlockSpec((pl.Element(CHUNK), pl.Element(width)),
                            lambda c: (pl.multiple_of(c * CHUNK, CHUNK), col))

    def halo(col, width):
        return pl.BlockSpec((pl.Element(HALO), pl.Element(width)),
                            lambda c: (pl.multiple_of(jnp.maximum(c * CHUNK - HALO, 0), HALO), col))

    full = lambda shape: pl.BlockSpec(shape, lambda c: (0,) * len(shape))
    in_specs = [
        cur(COL_X, D_INNER), halo(COL_X, D_INNER),
        cur(COL_B, bc_w), halo(COL_B, bc_w),
        cur(COL_Z, D_INNER),
        pl.BlockSpec((N_GROUPS, CHUNK, HEADS_PER_GROUP), lambda c: (0, c, 0)),
        pl.BlockSpec((N_GROUPS, HEADS_PER_GROUP, CHUNK), lambda c: (0, 0, c)),
        full((N_GROUPS, 1, HEADS_PER_GROUP)),
        full((N_GROUPS, HEADS_PER_GROUP, 1)),
        pl.BlockSpec((CONV_W, D_INNER), lambda c: (0, 0)),
        pl.BlockSpec((CONV_W, bc_w), lambda c: (0, D_INNER // bc_w)),
        pl.BlockSpec((1, D_INNER), lambda c: (0, 0)),
        pl.BlockSpec((1, bc_w), lambda c: (0, D_INNER // bc_w)),
        full((1, D_INNER)),
        full((1, D_INNER)),
    ]
    return pl.pallas_call(
        _ssd_kernel,
        grid=(N_CHUNKS,),
        in_specs=in_specs,
        out_specs=[
            pl.BlockSpec((CHUNK, D_INNER), lambda c: (c, 0)),
            pl.BlockSpec((CHUNK, LANES), lambda c: (c, 0)),
        ],
        out_shape=[
            jax.ShapeDtypeStruct((R, D_INNER), bf16),
            jax.ShapeDtypeStruct((R, LANES), f32),
        ],
        scratch_shapes=[
            pltpu.VMEM((N_GROUPS, N_STATE, GROUP_W), f32),
            pltpu.VMEM((HALO + CHUNK, D_INNER), bf16),
            pltpu.VMEM((HALO + CHUNK, bc_w), bf16),
        ],
        compiler_params=_params(("arbitrary",)),
        name="ssd_mixer",
    )(proj, proj, proj, proj, proj, dt_g, dt_gt, alog, alogt,
      conv_w, conv_w, conv_b2, conv_b2, dskip_e, ssd_norm_w.reshape(1, D_INNER))


def _router_kernel(h_ref, nw_ref, rw_ref, rb_ref, u_ref, idx_ref, gate_ref, *, tm):
    i = pl.program_id(0)
    h = h_ref[...]
    ms = jnp.mean(h * h, axis=-1, keepdims=True)
    u = h * lax.rsqrt(ms + EPS) * nw_ref[...]
    rows = i * tm + lax.broadcasted_iota(i32, (tm, 1), 0)
    u = jnp.where(rows >= PAD_ROWS, u, 0.0)
    u_ref[...] = u
    logits = jnp.dot(u, rw_ref[...], preferred_element_type=f32, precision=lax.Precision.HIGHEST) + rb_ref[...]
    e_iota = lax.broadcasted_iota(i32, (tm, N_EXPERTS), 1)
    lane = lax.broadcasted_iota(i32, (tm, LANES), 1)
    idx_out = jnp.zeros((tm, LANES), i32)
    val_out = jnp.zeros((tm, LANES), f32)
    vals = []
    for k in range(TOP_K):
        m = jnp.max(logits, axis=1, keepdims=True)
        sel = jnp.min(jnp.where(logits == m, e_iota, N_EXPERTS), axis=1, keepdims=True)
        vals.append(m)
        idx_out = jnp.where(lane == k, sel, idx_out)
        logits = jnp.where(e_iota == sel, -jnp.inf, logits)
    exps = [jnp.exp(v - vals[0]) for v in vals]
    denom = exps[0] + exps[1] + exps[2] + exps[3]
    for k in range(TOP_K):
        val_out = jnp.where(lane == k, exps[k] / denom, val_out)
    idx_ref[...] = idx_out
    gate_ref[...] = val_out


def _router(h1, norm_w, router_w, router_b, tm=320):
    return pl.pallas_call(
        functools.partial(_router_kernel, tm=tm),
        grid=(R // tm,),
        in_specs=[
            pl.BlockSpec((tm, D_MODEL), lambda i: (i, 0)),
            pl.BlockSpec((1, D_MODEL), lambda i: (0, 0)),
            pl.BlockSpec((D_MODEL, N_EXPERTS), lambda i: (0, 0)),
            pl.BlockSpec((1, N_EXPERTS), lambda i: (0, 0)),
        ],
        out_specs=[
            pl.BlockSpec((tm, D_MODEL), lambda i: (i, 0)),
            pl.BlockSpec((tm, LANES), lambda i: (i, 0)),
            pl.BlockSpec((tm, LANES), lambda i: (i, 0)),
        ],
        out_shape=[
            jax.ShapeDtypeStruct((R, D_MODEL), f32),
            jax.ShapeDtypeStruct((R, LANES), i32),
            jax.ShapeDtypeStruct((R, LANES), f32),
        ],
        compiler_params=_params(("parallel",)),
        name="router",
    )(h1, norm_w.reshape(1, D_MODEL), router_w, router_b.reshape(1, N_EXPERTS))


def _gather_x_kernel(pos_ref, nu_ref, nv_ref, src_ref, o_ref, idx_ref, buf_ref, sem, *, tm):
    i = pl.program_id(0)
    n_used = nu_ref[0]

    @pl.when(i == 0)
    def _():
        def invert(b, carry):
            for u in range(DMA_ISSUE_UNROLL):
                p = b * DMA_ISSUE_UNROLL + u
                idx_ref[pos_ref[p]] = lax.shift_right_logical(p, 2) + PAD_ROWS
            return carry

        lax.fori_loop(0, N_PAIRS // DMA_ISSUE_UNROLL, invert, 0)

    def row_copy(src_row, slot, r):
        return pltpu.make_async_copy(src_ref.at[pl.ds(src_row, 1)], buf_ref.at[slot, pl.ds(r, 1)], sem.at[slot])

    def issue_tile(step):
        slot = step % 2
        n_blocks = nv_ref[step] // DMA_ISSUE_UNROLL

        def issue_block(b, carry):
            for u in range(DMA_ISSUE_UNROLL):
                r = b * DMA_ISSUE_UNROLL + u
                row_copy(idx_ref[step * tm + r], slot, r).start()
            return carry

        def issue(r, carry):
            row_copy(idx_ref[step * tm + r], slot, r).start()
            return carry

        lax.fori_loop(0, n_blocks, issue_block, 0)
        lax.fori_loop(n_blocks * DMA_ISSUE_UNROLL, nv_ref[step], issue, 0)

    @pl.when(i == 0)
    def _():
        buf_ref[...] = jnp.zeros_like(buf_ref)
        issue_tile(0)

    @pl.when(i + 1 < n_used)
    def _():
        issue_tile(i + 1)

    @pl.when(i < n_used)
    def _():
        slot = i % 2

        @pl.when(nv_ref[i] == tm)
        def _():
            pltpu.make_async_copy(src_ref.at[pl.ds(0, tm)], buf_ref.at[slot], sem.at[slot]).wait()

        @pl.when(nv_ref[i] < tm)
        def _():
            def drain(r, carry):
                row_copy(0, slot, 0).wait()
                return carry

            lax.fori_loop(0, nv_ref[i], drain, 0)

        o_ref[...] = buf_ref[slot].astype(o_ref.dtype)

    @pl.when(i >= n_used)
    def _():
        o_ref[...] = jnp.zeros_like(o_ref)


def _gather_x(pos_flat, n_used, tile_rows, u2, tm=MOE_TM):
    assert TOP_K == 4
    return pl.pallas_call(
        functools.partial(_gather_x_kernel, tm=tm),
        grid_spec=pltpu.PrefetchScalarGridSpec(
            num_scalar_prefetch=3,
            grid=(XS_ROWS // tm,),
            in_specs=[pl.BlockSpec(memory_space=pl.ANY)],
            out_specs=pl.BlockSpec((tm, D_MODEL), lambda i, idx, nu, nv: (i, 0)),
            scratch_shapes=[pltpu.SMEM((XS_ROWS,), i32), pltpu.VMEM((2, tm, D_MODEL), f32),
                            pltpu.SemaphoreType.DMA((2,))],
        ),
        out_shape=jax.ShapeDtypeStruct((XS_ROWS, D_MODEL), bf16),
        compiler_params=_params(("arbitrary",)),
        name="gather_x",
    )(pos_flat, n_used, tile_rows, u2)


def _grouped_kernel(ge_ref, gt_ref, gn_ref, gs_ref, ng_ref, nu_ref, x_ref, *refs, n_w, tn, nj, ocols, w_col_offsets,
                    tile_fn):
    w_hbm, b_refs, o_ref = refs[:n_w], refs[n_w:2 * n_w], refs[2 * n_w]
    wbuf_ref, wb_ref, obuf_ref, zbuf_ref, wsem, sem, zsem = refs[2 * n_w + 1:]
    g = pl.program_id(0)
    n_groups = ng_ref[0]
    active = g < n_groups
    gc = jnp.minimum(g, n_groups - 1)
    n_sub = jnp.where(active, gs_ref[gc], 0)
    tile0 = gt_ref[gc]
    total_tiles = o_ref.shape[0] // MOE_TM

    def weight_copies(grp, jj, wslot):
        e = ge_ref[grp]
        return [pltpu.make_async_copy(
            w_hbm[k].at[e, :, pl.ds(pl.multiple_of((w_col_offsets[k] + jj) * tn, tn), tn)],
            wbuf_ref.at[wslot, k], wsem.at[wslot]) for k in range(n_w)]

    @pl.when(g == 0)
    def _():
        for cp in weight_copies(0, 0, 0):
            cp.start()
        zbuf_ref[...] = jnp.zeros_like(zbuf_ref)

        def zero_copy(t, jc):
            rows = pl.ds(pl.multiple_of(t * MOE_TM, MOE_TM), MOE_TM)
            return pltpu.make_async_copy(zbuf_ref, o_ref.at[rows, pl.ds(jc * ocols, ocols)], zsem)

        def z_start(t, carry):
            for jc in range(nj):
                zero_copy(t, jc).start()
            return carry

        def z_wait(t, carry):
            for jc in range(nj):
                zero_copy(t, jc).wait()
            return carry

        lax.fori_loop(nu_ref[0], total_tiles, z_start, 0)
        lax.fori_loop(nu_ref[0], total_tiles, z_wait, 0)

    def out_copy(slot, n, tile, col):
        rows = pl.ds(pl.multiple_of(tile * MOE_TM, MOE_TM), n * MOE_TM)
        return pltpu.make_async_copy(obuf_ref.at[slot, pl.ds(0, n * MOE_TM)], o_ref.at[rows, pl.ds(col, ocols)],
                                     sem.at[slot])

    def wait_step(s):
        gs = s // nj
        sent = jnp.where(jnp.logical_and(s >= 0, gs < n_groups), gn_ref[jnp.clip(gs, 0, MAX_GROUPS - 1)], 0)
        for n in range(1, MOE_CAP + 1):

            @pl.when(sent == n)
            def _(n=n):
                out_copy(s % 2, n, 0, 0).wait()

    def col_block(j, carry):
        step = g * nj + j
        slot = step % 2

        @pl.when(j + 1 < nj)
        def _():
            for cp in weight_copies(gc, j + 1, 1 - slot):
                cp.start()

        @pl.when(jnp.logical_and(j + 1 == nj, g + 1 < n_groups))
        def _():
            for cp in weight_copies(g + 1, 0, 1 - slot):
                cp.start()

        for cp in weight_copies(gc, j, slot):
            cp.wait()
        for k in range(n_w):
            wb_ref[:, k * tn:(k + 1) * tn] = wbuf_ref[slot, k].astype(bf16)

        wait_step(step - 2)
        col = pl.multiple_of(j * ocols, ocols)
        biases = [b_refs[k][0, :, pl.ds(pl.multiple_of((w_col_offsets[k] + j) * tn, tn), tn)] for k in range(n_w)]
        for m in range(1, MOE_CAP * MOE_TM // MOE_SUB + 1):

            @pl.when(n_sub == m)
            def _(m=m):
                rows = m * MOE_SUB
                n = -(-rows // MOE_TM)
                acc = jnp.dot(x_ref[0:rows, :], wb_ref[...], preferred_element_type=f32)
                obuf_ref[slot, pl.ds(0, rows)] = tile_fn(acc, biases)
                if rows < n * MOE_TM:
                    obuf_ref[slot, pl.ds(rows, n * MOE_TM - rows)] = jnp.zeros((n * MOE_TM - rows, ocols), obuf_ref.dtype)
                out_copy(slot, n, tile0, col).start()

        return carry

    @pl.when(active)
    def _():
        lax.fori_loop(0, nj, col_block, 0)

    @pl.when(g == MAX_GROUPS - 1)
    def _():
        last = n_groups * nj - 1
        wait_step(last - 1)
        wait_step(last)


def _grouped_call(groups, n_used, x, ws, bs, *, k_dim, tn, nj, w_col_offsets, ocols, out_shape, tile_fn, name):
    g_expert, g_tile0, g_ntiles, g_nsub, n_groups = groups
    n_w = len(ws)

    def gi(g, ng):
        return jnp.minimum(g, ng[0] - 1)

    def x_map(g, ge, gt, gn, gs, ng, nu):
        return (pl.multiple_of(gt[gi(g, ng)] * MOE_TM, MOE_TM), 0)

    def b_map(g, ge, gt, gn, gs, ng, nu):
        return (ge[gi(g, ng)], 0, 0)

    return pl.pallas_call(
        functools.partial(_grouped_kernel, n_w=n_w, tn=tn, nj=nj, ocols=ocols, w_col_offsets=w_col_offsets,
                          tile_fn=tile_fn),
        grid_spec=pltpu.PrefetchScalarGridSpec(
            num_scalar_prefetch=6,
            grid=(MAX_GROUPS,),
            in_specs=[
                pl.BlockSpec((pl.Element(GROUP_ROWS), pl.Element(k_dim)), x_map),
                *[pl.BlockSpec(memory_space=pl.ANY) for _ in ws],
                *[pl.BlockSpec((1, 1, b.shape[2]), b_map) for b in bs],
            ],
            out_specs=pl.BlockSpec(memory_space=pl.ANY),
            scratch_shapes=[
                pltpu.VMEM((2, n_w, k_dim, tn), f32),
                pltpu.VMEM((k_dim, n_w * tn), bf16),
                pltpu.VMEM((2, GROUP_ROWS, ocols), out_shape.dtype),
                pltpu.VMEM((MOE_TM, ocols), out_shape.dtype),
                pltpu.SemaphoreType.DMA((2,)),
                pltpu.SemaphoreType.DMA((2,)),
                pltpu.SemaphoreType.DMA(()),
            ],
        ),
        out_shape=out_shape,
        compiler_params=_params(("arbitrary",)),
        name=name,
    )(g_expert, g_tile0, g_ntiles, g_nsub, n_groups, n_used, x, *ws, *bs)


def _moe_up(groups, n_used, xs, w_up, b_up, tn=256):
    nj = D_EXPERT // tn

    def tile_fn(acc, biases):
        gate = jnp.minimum(acc[:, :tn] + biases[0], LIMIT)
        up = jnp.clip(acc[:, tn:] + biases[1], -LIMIT, LIMIT)
        return ((up + 1.0) * (gate * _sigmoid(ALPHA * gate))).astype(bf16)

    b3 = b_up.reshape(N_EXPERTS, 1, -1)
    return _grouped_call(groups, n_used, xs, (w_up, w_up), (b3, b3), k_dim=D_MODEL, tn=tn, nj=nj,
                         w_col_offsets=(0, nj), ocols=tn,
                         out_shape=jax.ShapeDtypeStruct((XS_ROWS, D_EXPERT), bf16), tile_fn=tile_fn, name="moe_up")


def _moe_down(groups, n_used, act, w_down, b_down, tn=1024):
    return _grouped_call(groups, n_used, act, (w_down,), (b_down.reshape(N_EXPERTS, 1, -1),), k_dim=D_EXPERT,
                         tn=tn, nj=D_MODEL // tn, w_col_offsets=(0,), ocols=tn,
                         out_shape=jax.ShapeDtypeStruct((N_SLOTS, D_MODEL), f32),
                         tile_fn=lambda acc, biases: acc + biases[0], name="moe_down")


def _combine_kernel(pos_ref, h_ref, g_ref, nw_ref, y_ref, o_ref, buf_ref, sem, *, tm):
    i = pl.program_id(0)
    n = pl.num_programs(0)

    def row_copy(src_row, slot, k, r):
        return pltpu.make_async_copy(y_ref.at[pl.ds(src_row, 1)], buf_ref.at[slot, k, pl.ds(r, 1)], sem.at[slot])

    def issue_tile(step):
        slot = step % 2
        for k in range(TOP_K):

            def issue(r, carry, k=k):
                row_copy(pos_ref[k * SEQ + step * tm + r], slot, k, r).start()
                return carry

            lax.fori_loop(0, tm, issue, 0, unroll=DMA_ISSUE_UNROLL)

    @pl.when(i == 0)
    def _():
        issue_tile(0)

    @pl.when(i + 1 < n)
    def _():
        issue_tile(i + 1)

    slot = i % 2
    for k in range(TOP_K):
        pltpu.make_async_copy(y_ref.at[pl.ds(0, tm)], buf_ref.at[slot, k], sem.at[slot]).wait()

    g = g_ref[...]
    h = h_ref[...]
    for k in range(TOP_K):
        h = h + g[:, k:k + 1] * buf_ref[slot, k]
    ms = jnp.mean(h * h, axis=-1, keepdims=True)
    o_ref[...] = h * lax.rsqrt(ms + EPS) * nw_ref[...]


def _combine(pos_kt, h1, y, gates, norm_w, tm=128):
    nb = SEQ // tm
    off = (PAD_ROWS + N_META) // tm
    return pl.pallas_call(
        functools.partial(_combine_kernel, tm=tm),
        grid_spec=pltpu.PrefetchScalarGridSpec(
            num_scalar_prefetch=1,
            grid=(nb,),
            in_specs=[
                pl.BlockSpec((tm, D_MODEL), lambda i, pos: (i + off, 0)),
                pl.BlockSpec((tm, LANES), lambda i, pos: (i + off, 0)),
                pl.BlockSpec((1, D_MODEL), lambda i, pos: (0, 0)),
                pl.BlockSpec(memory_space=pl.ANY),
            ],
            out_specs=pl.BlockSpec((tm, D_MODEL), lambda i, pos: (i, 0)),
            scratch_shapes=[pltpu.VMEM((2, TOP_K, tm, D_MODEL), f32), pltpu.SemaphoreType.DMA((2,))],
        ),
        out_shape=jax.ShapeDtypeStruct((SEQ, D_MODEL), f32),
        compiler_params=_params(("arbitrary",)),
        name="combine_final_norm",
    )(pos_kt, h1, gates, norm_w.reshape(1, D_MODEL), y)


def _rank_kernel(idx_ref, rank_ref, cnt_ref, carry_ref, *, tm):
    i = pl.program_id(0)

    @pl.when(i == 0)
    def _():
        carry_ref[...] = jnp.zeros_like(carry_ref)

    idx = idx_ref[...]
    rows = i * tm + lax.broadcasted_iota(i32, (tm, 1), 0)
    valid = rows >= PAD_ROWS
    e_iota = lax.broadcasted_iota(i32, (tm, N_EXPERTS), 1)
    hits = [jnp.logical_and(idx[:, k:k + 1] == e_iota, valid) for k in range(TOP_K)]
    onehot = sum(jnp.where(h, 1.0, 0.0) for h in hits)
    earlier = lax.broadcasted_iota(i32, (tm, tm), 0) > lax.broadcasted_iota(i32, (tm, tm), 1)
    before = jnp.dot(jnp.where(earlier, 1.0, 0.0).astype(bf16), onehot.astype(bf16),
                     preferred_element_type=f32) + carry_ref[...]
    lane = lax.broadcasted_iota(i32, (tm, LANES), 1)
    out = jnp.zeros((tm, LANES), i32)
    for k in range(TOP_K):
        rk = jnp.sum(jnp.where(hits[k], before, 0.0), axis=1, keepdims=True)
        out = jnp.where(lane == k, rk.astype(i32), out)
    rank_ref[...] = out
    carry_ref[...] += jnp.sum(onehot, axis=0, keepdims=True)
    cnt_ref[...] = carry_ref[...]


def _rank_pairs(top_idx, tm=320):
    return pl.pallas_call(
        functools.partial(_rank_kernel, tm=tm),
        grid=(R // tm,),
        in_specs=[pl.BlockSpec((tm, LANES), lambda i: (i, 0))],
        out_specs=[pl.BlockSpec((tm, LANES), lambda i: (i, 0)), pl.BlockSpec((1, N_EXPERTS), lambda i: (0, 0))],
        out_shape=[jax.ShapeDtypeStruct((R, LANES), i32), jax.ShapeDtypeStruct((1, N_EXPERTS), f32)],
        scratch_shapes=[pltpu.VMEM((1, N_EXPERTS), f32)],
        compiler_params=_params(("arbitrary",)),
        name="rank_pairs",
    )(top_idx)


def _routing_tables(top_idx, rank, counts):
    counts = counts.reshape(N_EXPERTS).astype(i32)
    tiles = (counts + MOE_TM - 1) // MOE_TM
    tile_end = jnp.cumsum(tiles)
    tile_start = tile_end - tiles
    e_ids = jnp.arange(N_EXPERTS, dtype=i32)
    tok_e = top_idx[PAD_ROWS:, :TOP_K]
    tok_start = jnp.sum(jnp.where(tok_e[..., None] == e_ids, tile_start * MOE_TM, 0), axis=-1)
    pos = tok_start + rank[PAD_ROWS:, :TOP_K]
    n_used = tile_end[-1:].astype(i32)
    tile_ids = jnp.arange(XS_ROWS // MOE_TM, dtype=i32)
    tile_e = jnp.minimum(jnp.sum(tile_end[None, :] <= tile_ids[:, None], axis=1), N_EXPERTS - 1)
    tile_rows = jnp.clip(counts[tile_e] - (tile_ids - tile_start[tile_e]) * MOE_TM, 0, MOE_TM).astype(i32)
    tile_rows = jnp.where(tile_ids < n_used[0], tile_rows, 0)
    groups = (tiles + MOE_CAP - 1) // MOE_CAP
    group_end = jnp.cumsum(groups)
    g_ids = jnp.arange(MAX_GROUPS, dtype=i32)
    g_expert = jnp.minimum(jnp.sum(group_end[None, :] <= g_ids[:, None], axis=1), N_EXPERTS - 1).astype(i32)
    g_local = g_ids - (group_end - groups)[g_expert]
    g_tile0 = (tile_start[g_expert] + g_local * MOE_CAP).astype(i32)
    g_ntiles = jnp.clip(tiles[g_expert] - g_local * MOE_CAP, 0, MOE_CAP).astype(i32)
    g_rows = jnp.clip(counts[g_expert] - g_local * GROUP_ROWS, 0, GROUP_ROWS)
    g_nsub = ((g_rows + MOE_SUB - 1) // MOE_SUB).astype(i32)
    n_groups = group_end[-1:].astype(i32)
    return pos, n_used, tile_rows, (g_expert, g_tile0, g_ntiles, g_nsub, n_groups)


def kernel(x, meta_tokens, norm_mix_w, w_in, conv_w, conv_b, dt_bias, a_log, d_skip, ssd_norm_w, w_ssd_out,
           pool_w, pool_scale, w_out, norm_ffn_w, router_w, router_b, w_up, b_up, w_down, b_down, norm_final_w):
    assert x.shape == (1, SEQ, D_MODEL) and norm_mix_w.shape[0] == 1
    x2 = x[0]
    head = jnp.concatenate([jnp.zeros((PAD_ROWS, D_MODEL), f32), meta_tokens.astype(f32)], axis=0)

    u = _rms_norm_rows(x2, head, norm_mix_w[0])
    proj = _in_proj_main(u, w_in[0])
    dt = _in_proj_dt(u, w_in[0], dt_bias[0])
    gates = _in_proj_gates(u, w_in[0])
    ma = _pool_branch(proj, gates, pool_w[0], pool_scale[0])
    vw, ssq = _ssd_branch(proj, dt, conv_w[0], conv_b[0], a_log[0], d_skip[0], ssd_norm_w[0])
    merged = _ssd_out_merge(vw, w_ssd_out[0], ssq, ma, gates)
    h1 = _out_proj(merged, w_out[0], head, x2)

    u2, top_idx, top_gate = _router(h1, norm_ffn_w[0], router_w[0], router_b[0])
    rank, counts = _rank_pairs(top_idx)
    pos, n_used, tile_rows, groups = _routing_tables(top_idx, rank, counts)
    xs = _gather_x(pos.reshape(-1), n_used, tile_rows, u2)
    act = _moe_up(groups, n_used, xs, w_up[0], b_up[0])
    y = _moe_down(groups, n_used, act, w_down[0], b_down[0])
    pos_kt = pos[N_META:].T.reshape(-1)
    out = _combine(pos_kt, h1, y, top_gate, norm_final_w)
    return out[None]
```

```python
import functools

import jax
import jax.numpy as jnp
from jax import lax
from jax.experimental import pallas as pl
from jax.experimental.pallas import tpu as pltpu

f32 = jnp.float32
bf16 = jnp.bfloat16
i32 = jnp.int32

D_MODEL = 4096
SEQ = 8192
N_META = 16
CHUNK = 128
PAD_ROWS = CHUNK - N_META
R = PAD_ROWS + N_META + SEQ
N_CHUNKS = R // CHUNK
POOL_WINDOWS = (2, 4, 8, 16)
POOL_GROUP = D_MODEL // 4
D_INNER = 2 * D_MODEL
HEAD_DIM = 64
N_HEADS = D_INNER // HEAD_DIM
N_STATE = 128
N_GROUPS = 8
HEADS_PER_GROUP = N_HEADS // N_GROUPS
GROUP_W = HEADS_PER_GROUP * HEAD_DIM
CONV_W = 4
HALO = 16
POOL_HALO = 2 * max(POOL_WINDOWS)
N_EXPERTS = 32
TOP_K = 4
D_EXPERT = 7 * D_MODEL // 16
LIMIT = 7.0
ALPHA = 1.702
EPS = 1e-5
LOG2E = 1.4426950408889634
N_TOK = N_META + SEQ
N_PAIRS = N_TOK * TOP_K
MOE_TM = 256
N_TILES = -(-N_PAIRS // MOE_TM) + N_EXPERTS
N_SLOTS = N_TILES * MOE_TM
MOE_CAP = 6
MOE_SUB = 64
GROUP_ROWS = MOE_CAP * MOE_TM
MAX_GROUPS = (N_TILES - 1 + N_EXPERTS * (MOE_CAP - 1)) // MOE_CAP
XS_ROWS = N_SLOTS + GROUP_ROWS
COL_Z = D_MODEL
COL_X = COL_Z + D_INNER
COL_B = COL_X + D_INNER
COL_C = COL_B + N_GROUPS * N_STATE
COL_DT = COL_C + N_GROUPS * N_STATE
COL_G = COL_DT + N_HEADS
SEG1_COLS = COL_DT
LANES = 128
V7X_VMEM_LIMIT = 56 * 1024 * 1024
DMA_ISSUE_UNROLL = 8


def _params(sem, vmem=V7X_VMEM_LIMIT):
    return pltpu.CompilerParams(dimension_semantics=sem, vmem_limit_bytes=vmem)


def _sigmoid(v):
    return 1.0 / (1.0 + jnp.exp(-v))


def _sigmoid_tanh(v):
    return 0.5 * jnp.tanh(0.5 * v) + 0.5


def _x_rows_spec(tm, tn, col_of):
    return pl.BlockSpec((pl.Element(tm), pl.Element(tn)),
                        lambda i, *rest: (pl.multiple_of(jnp.maximum(i * tm - CHUNK, 0), CHUNK), col_of(i, *rest)))


def _rms_kernel(x_ref, head_ref, w_ref, o_ref):
    i = pl.program_id(0)
    tm = o_ref.shape[0]

    def norm(h):
        ms = jnp.mean(h * h, axis=-1, keepdims=True)
        return (h * lax.rsqrt(ms + EPS) * w_ref[...]).astype(o_ref.dtype)

    @pl.when(i == 0)
    def _():
        o_ref[0:CHUNK, :] = norm(head_ref[...])
        o_ref[CHUNK:tm, :] = norm(x_ref[0:tm - CHUNK, :])

    @pl.when(i > 0)
    def _():
        o_ref[...] = norm(x_ref[...])


def _rms_norm_rows(x2, head, w, tm=640):
    d = x2.shape[1]
    return pl.pallas_call(
        _rms_kernel,
        grid=(R // tm,),
        in_specs=[_x_rows_spec(tm, d, lambda i: 0), pl.BlockSpec((CHUNK, d), lambda i: (0, 0)),
                  pl.BlockSpec((1, d), lambda i: (0, 0))],
        out_specs=pl.BlockSpec((tm, d), lambda i: (i, 0)),
        out_shape=jax.ShapeDtypeStruct((R, d), bf16),
        compiler_params=_params(("parallel",)),
        name="rms_mix",
    )(x2, head, w.reshape(1, d))


def _mm_kernel(*refs, n_extra, epilogue, nk):
    a_ref, w_ref = refs[0], refs[1]
    extra = refs[2:2 + n_extra]
    o_ref = refs[2 + n_extra]
    acc_ref = refs[3 + n_extra]
    k = pl.program_id(2)

    @pl.when(k == 0)
    def _():
        acc_ref[...] = jnp.dot(a_ref[...], w_ref[...].astype(bf16), preferred_element_type=f32)

    @pl.when(k > 0)
    def _():
        acc_ref[...] += jnp.dot(a_ref[...], w_ref[...].astype(bf16), preferred_element_type=f32)

    @pl.when(k == nk - 1)
    def _():
        epilogue(o_ref, acc_ref, *extra)


def _store(fn):
    def epilogue(o_ref, acc_ref, *extra):
        o_ref[...] = fn(acc_ref[...], *[e[...] for e in extra]).astype(o_ref.dtype)

    return epilogue


def _matmul(a, w, *, col0, n, tm, tn, tk, out_dtype, epilogue, extras=(), extra_specs=(), name):
    m, kdim = a.shape
    nk = kdim // tk
    if col0 % tn == 0:
        w_spec = pl.BlockSpec((tk, tn), lambda i, j, k: (k, j + col0 // tn))
    else:
        w_spec = pl.BlockSpec((pl.Element(tk), pl.Element(tn)), lambda i, j, k: (k * tk, pl.multiple_of(col0 + j * tn, LANES)))
    return pl.pallas_call(
        functools.partial(_mm_kernel, n_extra=len(extras), epilogue=epilogue, nk=nk),
        grid=(m // tm, n // tn, nk),
        in_specs=[pl.BlockSpec((tm, tk), lambda i, j, k: (i, k)), w_spec, *extra_specs],
        out_specs=pl.BlockSpec((tm, tn), lambda i, j, k: (i, j)),
        out_shape=jax.ShapeDtypeStruct((m, n), out_dtype),
        scratch_shapes=[pltpu.VMEM((tm, tn), f32)],
        compiler_params=_params(("parallel", "parallel", "arbitrary")),
        name=name,
    )(a, w, *extras)


def _softplus_bias(acc, bias):
    v = acc + bias
    return jnp.maximum(v, 0.0) + jnp.log1p(jnp.exp(-jnp.abs(v)))


def _in_proj_main(u, w_in):
    return _matmul(u, w_in, col0=0, n=SEG1_COLS, tm=1664, tn=1024, tk=2048, out_dtype=bf16,
                   epilogue=_store(lambda acc: acc), name="in_proj_main")


def _in_proj_dt(u, w_in, dt_bias):
    return _matmul(u, w_in, col0=COL_DT, n=N_HEADS, tm=1664, tn=N_HEADS, tk=1024, out_dtype=f32,
                   epilogue=_store(_softplus_bias), extras=(dt_bias.reshape(1, N_HEADS),),
                   extra_specs=(pl.BlockSpec((1, N_HEADS), lambda i, j, k: (0, 0)),), name="in_proj_dt")


def _in_proj_gates(u, w_in):
    return _matmul(u, w_in, col0=COL_G, n=2 * D_MODEL, tm=1664, tn=1024, tk=2048, out_dtype=bf16,
                   epilogue=_store(_sigmoid_tanh), name="in_proj_gates")


def _ssd_out_merge(vw, w_ssd_out, ssq, ma, gates, tm=1664, tn=1024):
    gb_off = D_MODEL // tn

    def merge(acc, ssq_t, ma_t, gb_t):
        rs = lax.rsqrt(ssq_t[:, :1] * (1.0 / D_INNER) + EPS)
        return ma_t.astype(f32) + gb_t.astype(f32) * (acc * rs)

    return _matmul(vw, w_ssd_out, col0=0, n=D_MODEL, tm=tm, tn=tn, tk=1024, out_dtype=bf16, epilogue=_store(merge),
                   extras=(ssq, ma, gates),
                   extra_specs=(pl.BlockSpec((tm, LANES), lambda i, j, k: (i, 0)),
                                pl.BlockSpec((tm, tn), lambda i, j, k: (i, j)),
                                pl.BlockSpec((tm, tn), lambda i, j, k: (i, j + gb_off))),
                   name="ssd_out_merge")


def _out_proj(merged, w_out, head, x2, tm=1664, tn=1024):
    def add_residual(o_ref, acc_ref, head_ref, x_ref):
        i = pl.program_id(0)

        @pl.when(i == 0)
        def _():
            o_ref[0:CHUNK, :] = head_ref[...] + acc_ref[0:CHUNK, :]
            o_ref[CHUNK:tm, :] = x_ref[0:tm - CHUNK, :] + acc_ref[CHUNK:tm, :]

        @pl.when(i > 0)
        def _():
            o_ref[...] = x_ref[...] + acc_ref[...]

    return _matmul(merged, w_out, col0=0, n=D_MODEL, tm=tm, tn=tn, tk=1024, out_dtype=f32,
                   epilogue=add_residual, extras=(head, x2),
                   extra_specs=(pl.BlockSpec((CHUNK, tn), lambda i, j, k: (0, j)),
                                _x_rows_spec(tm, tn, lambda i, j, k: pl.multiple_of(j * tn, tn))),
                   name="out_proj_residual")


def _pool_kernel(x_ref, halo_ref, w_ref, scale_ref, ga_ref, o_ref, a_ref, b_ref, pooled_ref, *, tm):
    assert POOL_WINDOWS == (2, 4, 8, 16)
    gi = pl.program_id(0)
    i = pl.program_id(1)
    n = POOL_HALO + tm
    halo = halo_ref[...].astype(f32)
    a_ref[0:POOL_HALO, :] = jnp.where(i == 0, 0.0, halo)
    a_ref[POOL_HALO:n, :] = x_ref[...].astype(f32)
    t = i * tm + lax.broadcasted_iota(i32, (tm, 1), 0) - PAD_ROWS

    def pair_sum(src_ref, shift, start):
        return src_ref[start:n, :] + src_ref[start - shift:n - shift, :]

    def finish(s, win):
        cnt = jnp.clip(t + 1, 1, win).astype(f32)
        pooled_ref[...] = (s / cnt - x_ref[...].astype(f32)).astype(bf16)

    @pl.when(gi == 0)
    def _():
        finish(pair_sum(a_ref, 1, POOL_HALO), 2)

    @pl.when(gi == 1)
    def _():
        b_ref[8:n, :] = pair_sum(a_ref, 1, 8)
        finish(pair_sum(b_ref, 2, POOL_HALO), 4)

    @pl.when(gi == 2)
    def _():
        b_ref[8:n, :] = pair_sum(a_ref, 1, 8)
        a_ref[16:n, :] = pair_sum(b_ref, 2, 16)
        finish(pair_sum(a_ref, 4, POOL_HALO), 8)

    @pl.when(gi == 3)
    def _():
        b_ref[8:n, :] = pair_sum(a_ref, 1, 8)
        a_ref[16:n, :] = pair_sum(b_ref, 2, 16)
        b_ref[24:n, :] = pair_sum(a_ref, 4, 24)
        finish(pair_sum(b_ref, 8, POOL_HALO), 16)

    acc = jnp.dot(pooled_ref[...], w_ref[0].astype(bf16), preferred_element_type=f32)
    o_ref[...] = (ga_ref[...].astype(f32) * (acc * scale_ref[...])).astype(o_ref.dtype)


def _pool_branch(proj, gates, pool_w, pool_scale, tm=1664):
    gw = POOL_GROUP
    return pl.pallas_call(
        functools.partial(_pool_kernel, tm=tm),
        grid=(4, R // tm),
        in_specs=[
            pl.BlockSpec((tm, gw), lambda g, i: (i, g)),
            pl.BlockSpec((POOL_HALO, gw), lambda g, i: (jnp.maximum(i * (tm // POOL_HALO) - 1, 0), g)),
            pl.BlockSpec((1, gw, gw), lambda g, i: (g, 0, 0)),
            pl.BlockSpec((1, gw), lambda g, i: (0, g)),
            pl.BlockSpec((tm, gw), lambda g, i: (i, g)),
        ],
        out_specs=pl.BlockSpec((tm, gw), lambda g, i: (i, g)),
        out_shape=jax.ShapeDtypeStruct((R, D_MODEL), bf16),
        scratch_shapes=[pltpu.VMEM((POOL_HALO + tm, gw), f32), pltpu.VMEM((POOL_HALO + tm, gw), f32),
                        pltpu.VMEM((tm, gw), bf16)],
        compiler_params=_params(("parallel", "arbitrary")),
        name="pool_branch",
    )(proj, proj, pool_w, pool_scale.reshape(1, D_MODEL), gates)


def _fill_ext(cur_ref, halo_ref, ext_ref, c):
    halo = halo_ref[...]
    ext_ref[0:HALO, :] = jnp.where(c == 0, jnp.zeros_like(halo), halo)
    ext_ref[HALO:HALO + CHUNK, :] = cur_ref[...]


def _conv_silu(ext_ref, sel, w_ref, b_ref, lanes):
    taps = jnp.dot(sel, ext_ref[:, lanes], preferred_element_type=f32)
    out = b_ref[:, lanes]
    for k in range(CONV_W):
        out = out + w_ref[k:k + 1, lanes] * taps[k * CHUNK:(k + 1) * CHUNK]
    return out * _sigmoid(out)


def _ssd_kernel(x_ref, xh_ref, bc_ref, bch_ref, z_ref, dt_ref, dtt_ref, alog_ref, alogt_ref,
                wx_ref, wbc_ref, bx_ref, bbc_ref, dskip_ref, nw_ref,
                vw_ref, ssq_ref,
                state_ref, extx_ref, extbc_ref):
    c = pl.program_id(0)

    @pl.when(c == 0)
    def _():
        state_ref[...] = jnp.zeros_like(state_ref)

    _fill_ext(x_ref, xh_ref, extx_ref, c)
    _fill_ext(bc_ref, bch_ref, extbc_ref, c)

    rows = c * CHUNK + lax.broadcasted_iota(i32, (CHUNK, 1), 0)
    row_ok = rows >= PAD_ROWS
    cols = c * CHUNK + lax.broadcasted_iota(i32, (1, CHUNK), 1)
    col_ok = cols >= PAD_ROWS
    li = lax.broadcasted_iota(i32, (CHUNK, CHUNK), 0)
    si = lax.broadcasted_iota(i32, (CHUNK, CHUNK), 1)
    causal = li >= si
    tril = causal.astype(f32)
    triu = (li <= si).astype(f32)
    lane = lax.broadcasted_iota(i32, (CHUNK, 2 * HEAD_DIM), 1)
    first = lane < HEAD_DIM
    first1 = lax.broadcasted_iota(i32, (1, 2 * HEAD_DIM), 1) < HEAD_DIM
    sr = lax.broadcasted_iota(i32, (CONV_W * CHUNK, HALO + CHUNK), 0)
    se = lax.broadcasted_iota(i32, (CONV_W * CHUNK, HALO + CHUNK), 1)
    sel = jnp.where(se == (sr % CHUNK) + (sr // CHUNK) + HALO - (CONV_W - 1), 1.0, 0.0).astype(bf16)
    hh = lax.broadcasted_iota(i32, (HEADS_PER_GROUP, GROUP_W), 0)
    hl = lax.broadcasted_iota(i32, (HEADS_PER_GROUP, GROUP_W), 1)
    head_lanes = jnp.where(hl // HEAD_DIM == hh, 1.0, 0.0).astype(bf16)

    def group(g, ssq):
        lo = pl.multiple_of(g * GROUP_W, GROUP_W)
        blo = pl.multiple_of(g * N_STATE, N_STATE)
        xs = _conv_silu(extx_ref, sel, wx_ref, bx_ref, pl.ds(lo, GROUP_W))
        bm = _conv_silu(extbc_ref, sel, wbc_ref, bbc_ref, pl.ds(blo, N_STATE))
        cm = _conv_silu(extbc_ref, sel, wbc_ref, bbc_ref, pl.ds(N_GROUPS * N_STATE + blo, N_STATE))

        dt = jnp.where(row_ok, dt_ref[g], 0.0)
        dtt = jnp.where(col_ok, dtt_ref[g], 0.0)
        a_dt = dt * (-jnp.exp(alog_ref[g]))
        a_dtt = dtt * (-jnp.exp(alogt_ref[g]))
        a_cs = jnp.dot(tril, a_dt, preferred_element_type=f32, precision=lax.Precision.HIGHEST)
        a_cst = jnp.dot(a_dtt, triu, preferred_element_type=f32, precision=lax.Precision.HIGHEST)
        last = a_cs[CHUNK - 1:CHUNK, :]
        e_cs = jnp.exp(a_cs)
        e_end = jnp.exp(last - a_cs)
        e_last = jnp.exp(last)
        a2 = a_cs * LOG2E
        a2t = a_cst * LOG2E

        bmb = bm.astype(bf16)
        cmb = cm.astype(bf16)
        cb = lax.dot_general(cmb, bmb, (((1,), (1,)), ((), ())), preferred_element_type=f32)
        state = state_ref[g]
        yoff = jnp.dot(cmb, state.astype(bf16), preferred_element_type=f32)

        scales = jnp.concatenate([dt, e_cs, e_end], axis=0).astype(bf16)
        spread = jnp.dot(scales, head_lanes, preferred_element_type=f32)
        xdt_g = xs * spread[0:CHUNK]
        xd2 = (xdt_g * spread[2 * CHUNK:3 * CHUNK]).astype(bf16)
        cd = []
        for q in range(HEADS_PER_GROUP // 2):
            ha, hb = 2 * q, 2 * q + 1
            sl = slice(q * 2 * HEAD_DIM, (q + 1) * 2 * HEAD_DIM)
            gl = pl.ds(pl.multiple_of(lo + q * 2 * HEAD_DIM, 2 * HEAD_DIM), 2 * HEAD_DIM)
            xq = xs[:, sl]
            xdtb = xdt_g[:, sl].astype(bf16)
            wa = cb * jnp.exp2(jnp.where(causal, a2[:, ha:ha + 1] - a2t[ha:ha + 1, :], -1e30))
            wb = cb * jnp.exp2(jnp.where(causal, a2[:, hb:hb + 1] - a2t[hb:hb + 1, :], -1e30))
            ya = jnp.dot(wa.astype(bf16), xdtb, preferred_element_type=f32)
            yb = jnp.dot(wb.astype(bf16), xdtb, preferred_element_type=f32)
            y = jnp.where(first, ya, yb) + yoff[:, sl] * spread[CHUNK:2 * CHUNK, sl] + dskip_ref[:, gl] * xq
            cd.append(jnp.where(first1, e_last[:, ha:ha + 1], e_last[:, hb:hb + 1]))
            zq = z_ref[:, gl].astype(f32)
            v = y * (zq * _sigmoid(zq))
            ssq = ssq + v * v
            vw_ref[:, gl] = (v * nw_ref[:, gl]).astype(vw_ref.dtype)

        bt = jnp.transpose(bm).astype(bf16)
        s_new = jnp.dot(bt, xd2, preferred_element_type=f32)
        state_ref[g] = state * jnp.concatenate(cd, axis=1) + s_new
        return ssq

    ssq = lax.fori_loop(0, N_GROUPS, group, jnp.zeros((CHUNK, 2 * HEAD_DIM), f32))
    ssq_ref[...] = jnp.broadcast_to(jnp.sum(ssq, axis=1, keepdims=True), (CHUNK, LANES))


def _ssd_branch(proj, dt, conv_w, conv_b, a_log, d_skip, ssd_norm_w):
    bc_w = 2 * N_GROUPS * N_STATE
    dt_g = dt.reshape(R, N_GROUPS, HEADS_PER_GROUP).transpose(1, 0, 2)
    dt_gt = dt_g.transpose(0, 2, 1)
    alog = a_log.reshape(N_GROUPS, 1, HEADS_PER_GROUP)
    alogt = a_log.reshape(N_GROUPS, HEADS_PER_GROUP, 1)
    conv_b2 = conv_b.reshape(1, -1)
    dskip_e = jnp.repeat(d_skip, HEAD_DIM).reshape(1, D_INNER)

    def cur(col, width):
        return pl.BlockSpec((pl.Element(CHUNK), pl.Element(width)),
                            lambda c: (pl.multiple_of(c * CHUNK, CHUNK), col))

    def halo(col, width):
        return pl.BlockSpec((pl.Element(HALO), pl.Element(width)),
                            lambda c: (pl.multiple_of(jnp.maximum(c * CHUNK - HALO, 0), HALO), col))

    full = lambda shape: pl.BlockSpec(shape, lambda c: (0,) * len(shape))
    in_specs = [
        cur(COL_X, D_INNER), halo(COL_X, D_INNER),
        cur(COL_B, bc_w), halo(COL_B, bc_w),
        cur(COL_Z, D_INNER),
        pl.BlockSpec((N_GROUPS, CHUNK, HEADS_PER_GROUP), lambda c: (0, c, 0)),
        pl.BlockSpec((N_GROUPS, HEADS_PER_GROUP, CHUNK), lambda c: (0, 0, c)),
        full((N_GROUPS, 1, HEADS_PER_GROUP)),
        full((N_GROUPS, HEADS_PER_GROUP, 1)),
        pl.BlockSpec((CONV_W, D_INNER), lambda c: (0, 0)),
        pl.BlockSpec((CONV_W, bc_w), lambda c: (0, D_INNER // bc_w)),
        pl.BlockSpec((1, D_INNER), lambda c: (0, 0)),
        pl.BlockSpec((1, bc_w), lambda c: (0, D_INNER // bc_w)),
        full((1, D_INNER)),
        full((1, D_INNER)),
    ]
    return pl.pallas_call(
        _ssd_kernel,
        grid=(N_CHUNKS,),
        in_specs=in_specs,
        out_specs=[
            pl.BlockSpec((CHUNK, D_INNER), lambda c: (c, 0)),
            pl.BlockSpec((CHUNK, LANES), lambda c: (c, 0)),
        ],
        out_shape=[
            jax.ShapeDtypeStruct((R, D_INNER), bf16),
            jax.ShapeDtypeStruct((R, LANES), f32),
        ],
        scratch_shapes=[
            pltpu.VMEM((N_GROUPS, N_STATE, GROUP_W), f32),
            pltpu.VMEM((HALO + CHUNK, D_INNER), bf16),
            pltpu.VMEM((HALO + CHUNK, bc_w), bf16),
        ],
        compiler_params=_params(("arbitrary",)),
        name="ssd_mixer",
    )(proj, proj, proj, proj, proj, dt_g, dt_gt, alog, alogt,
      conv_w, conv_w, conv_b2, conv_b2, dskip_e, ssd_norm_w.reshape(1, D_INNER))


def _router_kernel(h_ref, nw_ref, rw_ref, rb_ref, u_ref, idx_ref, gate_ref, *, tm):
    i = pl.program_id(0)
    h = h_ref[...]
    ms = jnp.mean(h * h, axis=-1, keepdims=True)
    u = h * lax.rsqrt(ms + EPS) * nw_ref[...]
    rows = i * tm + lax.broadcasted_iota(i32, (tm, 1), 0)
    u = jnp.where(rows >= PAD_ROWS, u, 0.0)
    u_ref[...] = u
    logits = jnp.dot(u, rw_ref[...], preferred_element_type=f32, precision=lax.Precision.HIGHEST) + rb_ref[...]
    e_iota = lax.broadcasted_iota(i32, (tm, N_EXPERTS), 1)
    lane = lax.broadcasted_iota(i32, (tm, LANES), 1)
    idx_out = jnp.zeros((tm, LANES), i32)
    val_out = jnp.zeros((tm, LANES), f32)
    vals = []
    for k in range(TOP_K):
        m = jnp.max(logits, axis=1, keepdims=True)
        sel = jnp.min(jnp.where(logits == m, e_iota, N_EXPERTS), axis=1, keepdims=True)
        vals.append(m)
        idx_out = jnp.where(lane == k, sel, idx_out)
        logits = jnp.where(e_iota == sel, -jnp.inf, logits)
    exps = [jnp.exp(v - vals[0]) for v in vals]
    denom = exps[0] + exps[1] + exps[2] + exps[3]
    for k in range(TOP_K):
        val_out = jnp.where(lane == k, exps[k] / denom, val_out)
    idx_ref[...] = idx_out
    gate_ref[...] = val_out


def _router(h1, norm_w, router_w, router_b, tm=320):
    return pl.pallas_call(
        functools.partial(_router_kernel, tm=tm),
        grid=(R // tm,),
        in_specs=[
            pl.BlockSpec((tm, D_MODEL), lambda i: (i, 0)),
            pl.BlockSpec((1, D_MODEL), lambda i: (0, 0)),
            pl.BlockSpec((D_MODEL, N_EXPERTS), lambda i: (0, 0)),
            pl.BlockSpec((1, N_EXPERTS), lambda i: (0, 0)),
        ],
        out_specs=[
            pl.BlockSpec((tm, D_MODEL), lambda i: (i, 0)),
            pl.BlockSpec((tm, LANES), lambda i: (i, 0)),
            pl.BlockSpec((tm, LANES), lambda i: (i, 0)),
        ],
        out_shape=[
            jax.ShapeDtypeStruct((R, D_MODEL), f32),
            jax.ShapeDtypeStruct((R, LANES), i32),
            jax.ShapeDtypeStruct((R, LANES), f32),
        ],
        compiler_params=_params(("parallel",)),
        name="router",
    )(h1, norm_w.reshape(1, D_MODEL), router_w, router_b.reshape(1, N_EXPERTS))


def _gather_x_kernel(pos_ref, nu_ref, nv_ref, src_ref, o_ref, idx_ref, buf_ref, sem, *, tm):
    i = pl.program_id(0)
    n_used = nu_ref[0]

    @pl.when(i == 0)
    def _():
        def invert(b, carry):
            for u in range(DMA_ISSUE_UNROLL):
                p = b * DMA_ISSUE_UNROLL + u
                idx_ref[pos_ref[p]] = lax.shift_right_logical(p, 2) + PAD_ROWS
            return carry

        lax.fori_loop(0, N_PAIRS // DMA_ISSUE_UNROLL, invert, 0)

    def row_copy(src_row, slot, r):
        return pltpu.make_async_copy(src_ref.at[pl.ds(src_row, 1)], buf_ref.at[slot, pl.ds(r, 1)], sem.at[slot])

    def issue_tile(step):
        slot = step % 2
        n_blocks = nv_ref[step] // DMA_ISSUE_UNROLL

        def issue_block(b, carry):
            for u in range(DMA_ISSUE_UNROLL):
                r = b * DMA_ISSUE_UNROLL + u
                row_copy(idx_ref[step * tm + r], slot, r).start()
            return carry

        def issue(r, carry):
            row_copy(idx_ref[step * tm + r], slot, r).start()
            return carry

        lax.fori_loop(0, n_blocks, issue_block, 0)
        lax.fori_loop(n_blocks * DMA_ISSUE_UNROLL, nv_ref[step], issue, 0)

    @pl.when(i == 0)
    def _():
        buf_ref[...] = jnp.zeros_like(buf_ref)
        issue_tile(0)

    @pl.when(i + 1 < n_used)
    def _():
        issue_tile(i + 1)

    @pl.when(i < n_used)
    def _():
        slot = i % 2

        @pl.when(nv_ref[i] == tm)
        def _():
            pltpu.make_async_copy(src_ref.at[pl.ds(0, tm)], buf_ref.at[slot], sem.at[slot]).wait()

        @pl.when(nv_ref[i] < tm)
        def _():
            def drain(r, carry):
                row_copy(0, slot, 0).wait()
                return carry

            lax.fori_loop(0, nv_ref[i], drain, 0)

        o_ref[...] = buf_ref[slot].astype(o_ref.dtype)

    @pl.when(i >= n_used)
    def _():
        o_ref[...] = jnp.zeros_like(o_ref)


def _gather_x(pos_flat, n_used, tile_rows, u2, tm=MOE_TM):
    assert TOP_K == 4
    return pl.pallas_call(
        functools.partial(_gather_x_kernel, tm=tm),
        grid_spec=pltpu.PrefetchScalarGridSpec(
            num_scalar_prefetch=3,
            grid=(XS_ROWS // tm,),
            in_specs=[pl.BlockSpec(memory_space=pl.ANY)],
            out_specs=pl.BlockSpec((tm, D_MODEL), lambda i, idx, nu, nv: (i, 0)),
            scratch_shapes=[pltpu.SMEM((XS_ROWS,), i32), pltpu.VMEM((2, tm, D_MODEL), f32),
                            pltpu.SemaphoreType.DMA((2,))],
        ),
        out_shape=jax.ShapeDtypeStruct((XS_ROWS, D_MODEL), bf16),
        compiler_params=_params(("arbitrary",)),
        name="gather_x",
    )(pos_flat, n_used, tile_rows, u2)


def _grouped_kernel(ge_ref, gt_ref, gn_ref, gs_ref, ng_ref, nu_ref, x_ref, *refs, n_w, tn, nj, ocols, w_col_offsets,
                    tile_fn):
    w_hbm, b_refs, o_ref = refs[:n_w], refs[n_w:2 * n_w], refs[2 * n_w]
    wbuf_ref, wb_ref, obuf_ref, zbuf_ref, wsem, sem, zsem = refs[2 * n_w + 1:]
    g = pl.program_id(0)
    n_groups = ng_ref[0]
    active = g < n_groups
    gc = jnp.minimum(g, n_groups - 1)
    n_sub = jnp.where(active, gs_ref[gc], 0)
    tile0 = gt_ref[gc]
    total_tiles = o_ref.shape[0] // MOE_TM

    def weight_copies(grp, jj, wslot):
        e = ge_ref[grp]
        return [pltpu.make_async_copy(
            w_hbm[k].at[e, :, pl.ds(pl.multiple_of((w_col_offsets[k] + jj) * tn, tn), tn)],
            wbuf_ref.at[wslot, k], wsem.at[wslot]) for k in range(n_w)]

    @pl.when(g == 0)
    def _():
        for cp in weight_copies(0, 0, 0):
            cp.start()
        zbuf_ref[...] = jnp.zeros_like(zbuf_ref)

        def zero_copy(t, jc):
            rows = pl.ds(pl.multiple_of(t * MOE_TM, MOE_TM), MOE_TM)
            return pltpu.make_async_copy(zbuf_ref, o_ref.at[rows, pl.ds(jc * ocols, ocols)], zsem)

        def z_start(t, carry):
            for jc in range(nj):
                zero_copy(t, jc).start()
            return carry

        def z_wait(t, carry):
            for jc in range(nj):
                zero_copy(t, jc).wait()
            return carry

        lax.fori_loop(nu_ref[0], total_tiles, z_start, 0)
        lax.fori_loop(nu_ref[0], total_tiles, z_wait, 0)

    def out_copy(slot, n, tile, col):
        rows = pl.ds(pl.multiple_of(tile * MOE_TM, MOE_TM), n * MOE_TM)
        return pltpu.make_async_copy(obuf_ref.at[slot, pl.ds(0, n * MOE_TM)], o_ref.at[rows, pl.ds(col, ocols)],
                                     sem.at[slot])

    def wait_step(s):
        gs = s // nj
        sent = jnp.where(jnp.logical_and(s >= 0, gs < n_groups), gn_ref[jnp.clip(gs, 0, MAX_GROUPS - 1)], 0)
        for n in range(1, MOE_CAP + 1):

            @pl.when(sent == n)
            def _(n=n):
                out_copy(s % 2, n, 0, 0).wait()

    def col_block(j, carry):
        step = g * nj + j
        slot = step % 2

        @pl.when(j + 1 < nj)
        def _():
            for cp in weight_copies(gc, j + 1, 1 - slot):
                cp.start()

        @pl.when(jnp.logical_and(j + 1 == nj, g + 1 < n_groups))
        def _():
            for cp in weight_copies(g + 1, 0, 1 - slot):
                cp.start()

        for cp in weight_copies(gc, j, slot):
            cp.wait()
        for k in range(n_w):
            wb_ref[:, k * tn:(k + 1) * tn] = wbuf_ref[slot, k].astype(bf16)

        wait_step(step - 2)
        col = pl.multiple_of(j * ocols, ocols)
        biases = [b_refs[k][0, :, pl.ds(pl.multiple_of((w_col_offsets[k] + j) * tn, tn), tn)] for k in range(n_w)]
        for m in range(1, MOE_CAP * MOE_TM // MOE_SUB + 1):

            @pl.when(n_sub == m)
            def _(m=m):
                rows = m * MOE_SUB
                n = -(-rows // MOE_TM)
                acc = jnp.dot(x_ref[0:rows, :], wb_ref[...], preferred_element_type=f32)
                obuf_ref[slot, pl.ds(0, rows)] = tile_fn(acc, biases)
                if rows < n * MOE_TM:
                    obuf_ref[slot, pl.ds(rows, n * MOE_TM - rows)] = jnp.zeros((n * MOE_TM - rows, ocols), obuf_ref.dtype)
                out_copy(slot, n, tile0, col).start()

        return carry

    @pl.when(active)
    def _():
        lax.fori_loop(0, nj, col_block, 0)

    @pl.when(g == MAX_GROUPS - 1)
    def _():
        last = n_groups * nj - 1
        wait_step(last - 1)
        wait_step(last)


def _grouped_call(groups, n_used, x, ws, bs, *, k_dim, tn, nj, w_col_offsets, ocols, out_shape, tile_fn, name):
    g_expert, g_tile0, g_ntiles, g_nsub, n_groups = groups
    n_w = len(ws)

    def gi(g, ng):
        return jnp.minimum(g, ng[0] - 1)

    def x_map(g, ge, gt, gn, gs, ng, nu):
        return (pl.multiple_of(gt[gi(g, ng)] * MOE_TM, MOE_TM), 0)

    def b_map(g, ge, gt, gn, gs, ng, nu):
        return (ge[gi(g, ng)], 0, 0)

    return pl.pallas_call(
        functools.partial(_grouped_kernel, n_w=n_w, tn=tn, nj=nj, ocols=ocols, w_col_offsets=w_col_offsets,
                          tile_fn=tile_fn),
        grid_spec=pltpu.PrefetchScalarGridSpec(
            num_scalar_prefetch=6,
            grid=(MAX_GROUPS,),
            in_specs=[
                pl.BlockSpec((pl.Element(GROUP_ROWS), pl.Element(k_dim)), x_map),
                *[pl.BlockSpec(memory_space=pl.ANY) for _ in ws],
                *[pl.BlockSpec((1, 1, b.shape[2]), b_map) for b in bs],
            ],
            out_specs=pl.BlockSpec(memory_space=pl.ANY),
            scratch_shapes=[
                pltpu.VMEM((2, n_w, k_dim, tn), f32),
                pltpu.VMEM((k_dim, n_w * tn), bf16),
                pltpu.VMEM((2, GROUP_ROWS, ocols), out_shape.dtype),
                pltpu.VMEM((MOE_TM, ocols), out_shape.dtype),
                pltpu.SemaphoreType.DMA((2,)),
                pltpu.SemaphoreType.DMA((2,)),
                pltpu.SemaphoreType.DMA(()),
            ],
        ),
        out_shape=out_shape,
        compiler_params=_params(("arbitrary",)),
        name=name,
    )(g_expert, g_tile0, g_ntiles, g_nsub, n_groups, n_used, x, *ws, *bs)


def _moe_up(groups, n_used, xs, w_up, b_up, tn=256):
    nj = D_EXPERT // tn

    def tile_fn(acc, biases):
        gate = jnp.minimum(acc[:, :tn] + biases[0], LIMIT)
        up = jnp.clip(acc[:, tn:] + biases[1], -LIMIT, LIMIT)
        return ((up + 1.0) * (gate * _sigmoid(ALPHA * gate))).astype(bf16)

    b3 = b_up.reshape(N_EXPERTS, 1, -1)
    return _grouped_call(groups, n_used, xs, (w_up, w_up), (b3, b3), k_dim=D_MODEL, tn=tn, nj=nj,
                         w_col_offsets=(0, nj), ocols=tn,
                         out_shape=jax.ShapeDtypeStruct((XS_ROWS, D_EXPERT), bf16), tile_fn=tile_fn, name="moe_up")


def _moe_down(groups, n_used, act, w_down, b_down, tn=1024):
    return _grouped_call(groups, n_used, act, (w_down,), (b_down.reshape(N_EXPERTS, 1, -1),), k_dim=D_EXPERT,
                         tn=tn, nj=D_MODEL // tn, w_col_offsets=(0,), ocols=tn,
                         out_shape=jax.ShapeDtypeStruct((N_SLOTS, D_MODEL), f32),
                         tile_fn=lambda acc, biases: acc + biases[0], name="moe_down")


def _combine_kernel(pos_ref, h_ref, g_ref, nw_ref, y_ref, o_ref, buf_ref, sem, *, tm):
    i = pl.program_id(0)
    n = pl.num_programs(0)

    def row_copy(src_row, slot, k, r):
        return pltpu.make_async_copy(y_ref.at[pl.ds(src_row, 1)], buf_ref.at[slot, k, pl.ds(r, 1)], sem.at[slot])

    def issue_tile(step):
        slot = step % 2
        for k in range(TOP_K):

            def issue(r, carry, k=k):
                row_copy(pos_ref[k * SEQ + step * tm + r], slot, k, r).start()
                return carry

            lax.fori_loop(0, tm, issue, 0, unroll=DMA_ISSUE_UNROLL)

    @pl.when(i == 0)
    def _():
        issue_tile(0)

    @pl.when(i + 1 < n)
    def _():
        issue_tile(i + 1)

    slot = i % 2
    for k in range(TOP_K):
        pltpu.make_async_copy(y_ref.at[pl.ds(0, tm)], buf_ref.at[slot, k], sem.at[slot]).wait()

    g = g_ref[...]
    h = h_ref[...]
    for k in range(TOP_K):
        h = h + g[:, k:k + 1] * buf_ref[slot, k]
    ms = jnp.mean(h * h, axis=-1, keepdims=True)
    o_ref[...] = h * lax.rsqrt(ms + EPS) * nw_ref[...]


def _combine(pos_kt, h1, y, gates, norm_w, tm=128):
    nb = SEQ // tm
    off = (PAD_ROWS + N_META) // tm
    return pl.pallas_call(
        functools.partial(_combine_kernel, tm=tm),
        grid_spec=pltpu.PrefetchScalarGridSpec(
            num_scalar_prefetch=1,
            grid=(nb,),
            in_specs=[
                pl.BlockSpec((tm, D_MODEL), lambda i, pos: (i + off, 0)),
                pl.BlockSpec((tm, LANES), lambda i, pos: (i + off, 0)),
                pl.BlockSpec((1, D_MODEL), lambda i, pos: (0, 0)),
                pl.BlockSpec(memory_space=pl.ANY),
            ],
            out_specs=pl.BlockSpec((tm, D_MODEL), lambda i, pos: (i, 0)),
            scratch_shapes=[pltpu.VMEM((2, TOP_K, tm, D_MODEL), f32), pltpu.SemaphoreType.DMA((2,))],
        ),
        out_shape=jax.ShapeDtypeStruct((SEQ, D_MODEL), f32),
        compiler_params=_params(("arbitrary",)),
        name="combine_final_norm",
    )(pos_kt, h1, gates, norm_w.reshape(1, D_MODEL), y)


def _rank_kernel(idx_ref, rank_ref, cnt_ref, carry_ref, *, tm):
    i = pl.program_id(0)

    @pl.when(i == 0)
    def _():
        carry_ref[...] = jnp.zeros_like(carry_ref)

    idx = idx_ref[...]
    rows = i * tm + lax.broadcasted_iota(i32, (tm, 1), 0)
    valid = rows >= PAD_ROWS
    e_iota = lax.broadcasted_iota(i32, (tm, N_EXPERTS), 1)
    hits = [jnp.logical_and(idx[:, k:k + 1] == e_iota, valid) for k in range(TOP_K)]
    onehot = sum(jnp.where(h, 1.0, 0.0) for h in hits)
    earlier = lax.broadcasted_iota(i32, (tm, tm), 0) > lax.broadcasted_iota(i32, (tm, tm), 1)
    before = jnp.dot(jnp.where(earlier, 1.0, 0.0).astype(bf16), onehot.astype(bf16),
                     preferred_element_type=f32) + carry_ref[...]
    lane = lax.broadcasted_iota(i32, (tm, LANES), 1)
    out = jnp.zeros((tm, LANES), i32)
    for k in range(TOP_K):
        rk = jnp.sum(jnp.where(hits[k], before, 0.0), axis=1, keepdims=True)
        out = jnp.where(lane == k, rk.astype(i32), out)
    rank_ref[...] = out
    carry_ref[...] += jnp.sum(onehot, axis=0, keepdims=True)
    cnt_ref[...] = carry_ref[...]


def _rank_pairs(top_idx, tm=320):
    return pl.pallas_call(
        functools.partial(_rank_kernel, tm=tm),
        grid=(R // tm,),
        in_specs=[pl.BlockSpec((tm, LANES), lambda i: (i, 0))],
        out_specs=[pl.BlockSpec((tm, LANES), lambda i: (i, 0)), pl.BlockSpec((1, N_EXPERTS), lambda i: (0, 0))],
        out_shape=[jax.ShapeDtypeStruct((R, LANES), i32), jax.ShapeDtypeStruct((1, N_EXPERTS), f32)],
        scratch_shapes=[pltpu.VMEM((1, N_EXPERTS), f32)],
        compiler_params=_params(("arbitrary",)),
        name="rank_pairs",
    )(top_idx)


def _routing_tables(top_idx, rank, counts):
    counts = counts.reshape(N_EXPERTS).astype(i32)
    tiles = (counts + MOE_TM - 1) // MOE_TM
    tile_end = jnp.cumsum(tiles)
    tile_start = tile_end - tiles
    e_ids = jnp.arange(N_EXPERTS, dtype=i32)
    tok_e = top_idx[PAD_ROWS:, :TOP_K]
    tok_start = jnp.sum(jnp.where(tok_e[..., None] == e_ids, tile_start * MOE_TM, 0), axis=-1)
    pos = tok_start + rank[PAD_ROWS:, :TOP_K]
    n_used = tile_end[-1:].astype(i32)
    tile_ids = jnp.arange(XS_ROWS // MOE_TM, dtype=i32)
    tile_e = jnp.minimum(jnp.sum(tile_end[None, :] <= tile_ids[:, None], axis=1), N_EXPERTS - 1)
    tile_rows = jnp.clip(counts[tile_e] - (tile_ids - tile_start[tile_e]) * MOE_TM, 0, MOE_TM).astype(i32)
    tile_rows = jnp.where(tile_ids < n_used[0], tile_rows, 0)
    groups = (tiles + MOE_CAP - 1) // MOE_CAP
    group_end = jnp.cumsum(groups)
    g_ids = jnp.arange(MAX_GROUPS, dtype=i32)
    g_expert = jnp.minimum(jnp.sum(group_end[None, :] <= g_ids[:, None], axis=1), N_EXPERTS - 1).astype(i32)
    g_local = g_ids - (group_end - groups)[g_expert]
    g_tile0 = (tile_start[g_expert] + g_local * MOE_CAP).astype(i32)
    g_ntiles = jnp.clip(tiles[g_expert] - g_local * MOE_CAP, 0, MOE_CAP).astype(i32)
    g_rows = jnp.clip(counts[g_expert] - g_local * GROUP_ROWS, 0, GROUP_ROWS)
    g_nsub = ((g_rows + MOE_SUB - 1) // MOE_SUB).astype(i32)
    n_groups = group_end[-1:].astype(i32)
    return pos, n_used, tile_rows, (g_expert, g_tile0, g_ntiles, g_nsub, n_groups)


def kernel(x, meta_tokens, norm_mix_w, w_in, conv_w, conv_b, dt_bias, a_log, d_skip, ssd_norm_w, w_ssd_out,
           pool_w, pool_scale, w_out, norm_ffn_w, router_w, router_b, w_up, b_up, w_down, b_down, norm_final_w):
    assert x.shape == (1, SEQ, D_MODEL) and norm_mix_w.shape[0] == 1
    x2 = x[0]
    head = jnp.concatenate([jnp.zeros((PAD_ROWS, D_MODEL), f32), meta_tokens.astype(f32)], axis=0)

    u = _rms_norm_rows(x2, head, norm_mix_w[0])
    proj = _in_proj_main(u, w_in[0])
    dt = _in_proj_dt(u, w_in[0], dt_bias[0])
    gates = _in_proj_gates(u, w_in[0])
    ma = _pool_branch(proj, gates, pool_w[0], pool_scale[0])
    vw, ssq = _ssd_branch(proj, dt, conv_w[0], conv_b[0], a_log[0], d_skip[0], ssd_norm_w[0])
    merged = _ssd_out_merge(vw, w_ssd_out[0], ssq, ma, gates)
    h1 = _out_proj(merged, w_out[0], head, x2)

    u2, top_idx, top_gate = _router(h1, norm_ffn_w[0], router_w[0], router_b[0])
    rank, counts = _rank_pairs(top_idx)
    pos, n_used, tile_rows, groups = _routing_tables(top_idx, rank, counts)
    xs = _gather_x(pos.reshape(-1), n_used, tile_rows, u2)
    act = _moe_up(groups, n_used, xs, w_up[0], b_up[0])
    y = _moe_down(groups, n_used, act, w_down[0], b_down[0])
    pos_kt = pos[N_META:].T.reshape(-1)
    out = _combine(pos_kt, h1, y, top_gate, norm_final_w)
    return out[None]
```

```python
import functools

import jax
import jax.numpy as jnp
from jax import lax
from jax.experimental import pallas as pl
from jax.experimental.pallas import tpu as pltpu

f32 = jnp.float32
bf16 = jnp.bfloat16
i32 = jnp.int32

D_MODEL = 4096
SEQ = 8192
N_META = 16
CHUNK = 128
PAD_ROWS = CHUNK - N_META
R = PAD_ROWS + N_META + SEQ
N_CHUNKS = R // CHUNK
POOL_WINDOWS = (2, 4, 8, 16)
POOL_GROUP = D_MODEL // 4
D_INNER = 2 * D_MODEL
HEAD_DIM = 64
N_HEADS = D_INNER // HEAD_DIM
N_STATE = 128
N_GROUPS = 8
HEADS_PER_GROUP = N_HEADS // N_GROUPS
GROUP_W = HEADS_PER_GROUP * HEAD_DIM
CONV_W = 4
HALO = 16
POOL_HALO = 2 * max(POOL_WINDOWS)
N_EXPERTS = 32
TOP_K = 4
D_EXPERT = 7 * D_MODEL // 16
LIMIT = 7.0
ALPHA = 1.702
EPS = 1e-5
LOG2E = 1.4426950408889634
N_TOK = N_META + SEQ
N_PAIRS = N_TOK * TOP_K
MOE_TM = 256
N_TILES = -(-N_PAIRS // MOE_TM) + N_EXPERTS
N_SLOTS = N_TILES * MOE_TM
MOE_CAP = 6
MOE_SUB = 128
GROUP_ROWS = MOE_CAP * MOE_TM
MAX_GROUPS = (N_TILES - 1 + N_EXPERTS * (MOE_CAP - 1)) // MOE_CAP
XS_ROWS = N_SLOTS + GROUP_ROWS
COL_Z = D_MODEL
COL_X = COL_Z + D_INNER
COL_B = COL_X + D_INNER
COL_C = COL_B + N_GROUPS * N_STATE
COL_DT = COL_C + N_GROUPS * N_STATE
COL_G = COL_DT + N_HEADS
SEG1_COLS = COL_DT
LANES = 128
V7X_VMEM_LIMIT = 56 * 1024 * 1024
DMA_ISSUE_UNROLL = 8


def _params(sem, vmem=V7X_VMEM_LIMIT):
    return pltpu.CompilerParams(dimension_semantics=sem, vmem_limit_bytes=vmem)


def _sigmoid(v):
    return 1.0 / (1.0 + jnp.exp(-v))


def _sigmoid_tanh(v):
    return 0.5 * jnp.tanh(0.5 * v) + 0.5


def _x_rows_spec(tm, tn, col_of):
    return pl.BlockSpec((pl.Element(tm), pl.Element(tn)),
                        lambda i, *rest: (pl.multiple_of(jnp.maximum(i * tm - CHUNK, 0), CHUNK), col_of(i, *rest)))


def _rms_kernel(x_ref, head_ref, w_ref, o_ref):
    i = pl.program_id(0)
    tm = o_ref.shape[0]

    def norm(h):
        ms = jnp.mean(h * h, axis=-1, keepdims=True)
        return (h * lax.rsqrt(ms + EPS) * w_ref[...]).astype(o_ref.dtype)

    @pl.when(i == 0)
    def _():
        o_ref[0:CHUNK, :] = norm(head_ref[...])
        o_ref[CHUNK:tm, :] = norm(x_ref[0:tm - CHUNK, :])

    @pl.when(i > 0)
    def _():
        o_ref[...] = norm(x_ref[...])


def _rms_norm_rows(x2, head, w, tm=640):
    d = x2.shape[1]
    return pl.pallas_call(
        _rms_kernel,
        grid=(R // tm,),
        in_specs=[_x_rows_spec(tm, d, lambda i: 0), pl.BlockSpec((CHUNK, d), lambda i: (0, 0)),
                  pl.BlockSpec((1, d), lambda i: (0, 0))],
        out_specs=pl.BlockSpec((tm, d), lambda i: (i, 0)),
        out_shape=jax.ShapeDtypeStruct((R, d), bf16),
        compiler_params=_params(("parallel",)),
        name="rms_mix",
    )(x2, head, w.reshape(1, d))


def _mm_kernel(*refs, n_extra, epilogue, nk):
    a_ref, w_ref = refs[0], refs[1]
    extra = refs[2:2 + n_extra]
    o_ref = refs[2 + n_extra]
    acc_ref = refs[3 + n_extra]
    k = pl.program_id(2)

    def partial():
        return jnp.dot(a_ref[...], w_ref[...].astype(bf16), preferred_element_type=f32)

    if nk > 1:
        @pl.when(k == 0)
        def _():
            acc_ref[...] = partial()

    if nk > 2:
        @pl.when(jnp.logical_and(k > 0, k < nk - 1))
        def _():
            acc_ref[...] += partial()

    @pl.when(k == nk - 1)
    def _():
        epilogue(o_ref, partial() if nk == 1 else acc_ref[...] + partial(), *extra)


def _store(fn):
    def epilogue(o_ref, acc, *extra):
        o_ref[...] = fn(acc, *[e[...] for e in extra]).astype(o_ref.dtype)

    return epilogue


def _matmul(a, w, *, col0, n, tm, tn, tk, out_dtype, epilogue, extras=(), extra_specs=(), name):
    m, kdim = a.shape
    nk = kdim // tk
    if col0 % tn == 0:
        w_spec = pl.BlockSpec((tk, tn), lambda i, j, k: (k, j + col0 // tn))
    else:
        w_spec = pl.BlockSpec((pl.Element(tk), pl.Element(tn)), lambda i, j, k: (k * tk, pl.multiple_of(col0 + j * tn, LANES)))
    return pl.pallas_call(
        functools.partial(_mm_kernel, n_extra=len(extras), epilogue=epilogue, nk=nk),
        grid=(m // tm, n // tn, nk),
        in_specs=[pl.BlockSpec((tm, tk), lambda i, j, k: (i, k)), w_spec, *extra_specs],
        out_specs=pl.BlockSpec((tm, tn), lambda i, j, k: (i, j)),
        out_shape=jax.ShapeDtypeStruct((m, n), out_dtype),
        scratch_shapes=[pltpu.VMEM((tm, tn), f32)],
        compiler_params=_params(("parallel", "parallel", "arbitrary")),
        name=name,
    )(a, w, *extras)


def _softplus_bias(acc, bias):
    v = acc + bias
    return jnp.maximum(v, 0.0) + jnp.log1p(jnp.exp(-jnp.abs(v)))


def _in_proj_main(u, w_in):
    return _matmul(u, w_in, col0=0, n=SEG1_COLS, tm=1664, tn=1024, tk=2048, out_dtype=bf16,
                   epilogue=_store(lambda acc: acc), name="in_proj_main")


def _in_proj_dt(u, w_in, dt_bias):
    return _matmul(u, w_in, col0=COL_DT, n=N_HEADS, tm=1664, tn=N_HEADS, tk=1024, out_dtype=f32,
                   epilogue=_store(_softplus_bias), extras=(dt_bias.reshape(1, N_HEADS),),
                   extra_specs=(pl.BlockSpec((1, N_HEADS), lambda i, j, k: (0, 0)),), name="in_proj_dt")


def _in_proj_gates(u, w_in):
    return _matmul(u, w_in, col0=COL_G, n=2 * D_MODEL, tm=1664, tn=1024, tk=2048, out_dtype=bf16,
                   epilogue=_store(_sigmoid_tanh), name="in_proj_gates")


def _ssd_out_merge(vw, w_ssd_out, ssq, ma, gates, tm=1664, tn=1024):
    gb_off = D_MODEL // tn

    def merge(acc, ssq_t, ma_t, gb_t):
        rs = lax.rsqrt(ssq_t[:, :1] * (1.0 / D_INNER) + EPS)
        return ma_t.astype(f32) + gb_t.astype(f32) * (acc * rs)

    return _matmul(vw, w_ssd_out, col0=0, n=D_MODEL, tm=tm, tn=tn, tk=1024, out_dtype=bf16, epilogue=_store(merge),
                   extras=(ssq, ma, gates),
                   extra_specs=(pl.BlockSpec((tm, LANES), lambda i, j, k: (i, 0)),
                                pl.BlockSpec((tm, tn), lambda i, j, k: (i, j)),
                                pl.BlockSpec((tm, tn), lambda i, j, k: (i, j + gb_off))),
                   name="ssd_out_merge")


def _out_proj(merged, w_out, head, x2, tm=1664, tn=1024):
    def add_residual(o_ref, acc, head_ref, x_ref):
        i = pl.program_id(0)

        @pl.when(i == 0)
        def _():
            o_ref[0:CHUNK, :] = head_ref[...] + acc[0:CHUNK]
            o_ref[CHUNK:tm, :] = x_ref[0:tm - CHUNK, :] + acc[CHUNK:tm]

        @pl.when(i > 0)
        def _():
            o_ref[...] = x_ref[...] + acc

    return _matmul(merged, w_out, col0=0, n=D_MODEL, tm=tm, tn=tn, tk=1024, out_dtype=f32,
                   epilogue=add_residual, extras=(head, x2),
                   extra_specs=(pl.BlockSpec((CHUNK, tn), lambda i, j, k: (0, j)),
                                _x_rows_spec(tm, tn, lambda i, j, k: pl.multiple_of(j * tn, tn))),
                   name="out_proj_residual")


def _pool_kernel(x_ref, halo_ref, w_ref, scale_ref, ga_ref, o_ref, a_ref, b_ref, pooled_ref, *, tm):
    assert POOL_WINDOWS == (2, 4, 8, 16)
    gi = pl.program_id(0)
    i = pl.program_id(1)
    n = POOL_HALO + tm
    halo = halo_ref[...].astype(f32)
    a_ref[0:POOL_HALO, :] = jnp.where(i == 0, 0.0, halo)
    a_ref[POOL_HALO:n, :] = x_ref[...].astype(f32)
    t = i * tm + lax.broadcasted_iota(i32, (tm, 1), 0) - PAD_ROWS

    def pair_sum(src_ref, shift, start):
        return src_ref[start:n, :] + src_ref[start - shift:n - shift, :]

    def finish(s, win):
        cnt = jnp.clip(t + 1, 1, win).astype(f32)
        pooled_ref[...] = (s / cnt - x_ref[...].astype(f32)).astype(bf16)

    @pl.when(gi == 0)
    def _():
        finish(pair_sum(a_ref, 1, POOL_HALO), 2)

    @pl.when(gi == 1)
    def _():
        b_ref[8:n, :] = pair_sum(a_ref, 1, 8)
        finish(pair_sum(b_ref, 2, POOL_HALO), 4)

    @pl.when(gi == 2)
    def _():
        b_ref[8:n, :] = pair_sum(a_ref, 1, 8)
        a_ref[16:n, :] = pair_sum(b_ref, 2, 16)
        finish(pair_sum(a_ref, 4, POOL_HALO), 8)

    @pl.when(gi == 3)
    def _():
        b_ref[8:n, :] = pair_sum(a_ref, 1, 8)
        a_ref[16:n, :] = pair_sum(b_ref, 2, 16)
        b_ref[24:n, :] = pair_sum(a_ref, 4, 24)
        finish(pair_sum(b_ref, 8, POOL_HALO), 16)

    acc = jnp.dot(pooled_ref[...], w_ref[0].astype(bf16), preferred_element_type=f32)
    o_ref[...] = (ga_ref[...].astype(f32) * (acc * scale_ref[...])).astype(o_ref.dtype)


def _pool_branch(proj, gates, pool_w, pool_scale, tm=1664):
    gw = POOL_GROUP
    return pl.pallas_call(
        functools.partial(_pool_kernel, tm=tm),
        grid=(4, R // tm),
        in_specs=[
            pl.BlockSpec((tm, gw), lambda g, i: (i, g)),
            pl.BlockSpec((POOL_HALO, gw), lambda g, i: (jnp.maximum(i * (tm // POOL_HALO) - 1, 0), g)),
            pl.BlockSpec((1, gw, gw), lambda g, i: (g, 0, 0)),
            pl.BlockSpec((1, gw), lambda g, i: (0, g)),
            pl.BlockSpec((tm, gw), lambda g, i: (i, g)),
        ],
        out_specs=pl.BlockSpec((tm, gw), lambda g, i: (i, g)),
        out_shape=jax.ShapeDtypeStruct((R, D_MODEL), bf16),
        scratch_shapes=[pltpu.VMEM((POOL_HALO + tm, gw), f32), pltpu.VMEM((POOL_HALO + tm, gw), f32),
                        pltpu.VMEM((tm, gw), bf16)],
        compiler_params=_params(("parallel", "arbitrary")),
        name="pool_branch",
    )(proj, proj, pool_w, pool_scale.reshape(1, D_MODEL), gates)


def _fill_ext(cur_ref, halo_ref, ext_ref, c):
    halo = halo_ref[...]
    ext_ref[0:HALO, :] = jnp.where(c == 0, jnp.zeros_like(halo), halo)
    ext_ref[HALO:HALO + CHUNK, :] = cur_ref[...]


def _conv_silu(ext_ref, sel, w_ref, b_ref, lanes):
    taps = jnp.dot(sel, ext_ref[:, lanes], preferred_element_type=f32)
    out = b_ref[:, lanes]
    for k in range(CONV_W):
        out = out + w_ref[k:k + 1, lanes] * taps[k * CHUNK:(k + 1) * CHUNK]
    return out * _sigmoid(out)


def _ssd_kernel(x_ref, xh_ref, bc_ref, bch_ref, z_ref, dt_ref, dtt_ref, alog_ref, alogt_ref,
                wx_ref, wbc_ref, bx_ref, bbc_ref, dskip_ref, nw_ref,
                vw_ref, ssq_ref,
                state_ref, extx_ref, extbc_ref):
    c = pl.program_id(0)

    @pl.when(c == 0)
    def _():
        state_ref[...] = jnp.zeros_like(state_ref)

    _fill_ext(x_ref, xh_ref, extx_ref, c)
    _fill_ext(bc_ref, bch_ref, extbc_ref, c)

    rows = c * CHUNK + lax.broadcasted_iota(i32, (CHUNK, 1), 0)
    row_ok = rows >= PAD_ROWS
    cols = c * CHUNK + lax.broadcasted_iota(i32, (1, CHUNK), 1)
    col_ok = cols >= PAD_ROWS
    li = lax.broadcasted_iota(i32, (CHUNK, CHUNK), 0)
    si = lax.broadcasted_iota(i32, (CHUNK, CHUNK), 1)
    causal = li >= si
    tril = causal.astype(f32)
    triu = (li <= si).astype(f32)
    lane = lax.broadcasted_iota(i32, (CHUNK, 2 * HEAD_DIM), 1)
    first = lane < HEAD_DIM
    first1 = lax.broadcasted_iota(i32, (1, 2 * HEAD_DIM), 1) < HEAD_DIM
    sr = lax.broadcasted_iota(i32, (CONV_W * CHUNK, HALO + CHUNK), 0)
    se = lax.broadcasted_iota(i32, (CONV_W * CHUNK, HALO + CHUNK), 1)
    sel = jnp.where(se == (sr % CHUNK) + (sr // CHUNK) + HALO - (CONV_W - 1), 1.0, 0.0).astype(bf16)
    hh = lax.broadcasted_iota(i32, (HEADS_PER_GROUP, GROUP_W), 0)
    hl = lax.broadcasted_iota(i32, (HEADS_PER_GROUP, GROUP_W), 1)
    head_lanes = jnp.where(hl // HEAD_DIM == hh, 1.0, 0.0).astype(bf16)

    def group(g, ssq):
        lo = pl.multiple_of(g * GROUP_W, GROUP_W)
        blo = pl.multiple_of(g * N_STATE, N_STATE)
        xs = _conv_silu(extx_ref, sel, wx_ref, bx_ref, pl.ds(lo, GROUP_W))
        bm = _conv_silu(extbc_ref, sel, wbc_ref, bbc_ref, pl.ds(blo, N_STATE))
        cm = _conv_silu(extbc_ref, sel, wbc_ref, bbc_ref, pl.ds(N_GROUPS * N_STATE + blo, N_STATE))

        dt = jnp.where(row_ok, dt_ref[g], 0.0)
        dtt = jnp.where(col_ok, dtt_ref[g], 0.0)
        a_dt = dt * (-jnp.exp(alog_ref[g]))
        a_dtt = dtt * (-jnp.exp(alogt_ref[g]))
        a_cs = jnp.dot(tril, a_dt, preferred_element_type=f32, precision=lax.Precision.HIGHEST)
        a_cst = jnp.dot(a_dtt, triu, preferred_element_type=f32, precision=lax.Precision.HIGHEST)
        last = a_cs[CHUNK - 1:CHUNK, :]
        e_cs = jnp.exp(a_cs)
        e_end = jnp.exp(last - a_cs)
        e_last = jnp.exp(last)
        a2 = a_cs * LOG2E
        a2t = a_cst * LOG2E

        bmb = bm.astype(bf16)
        cmb = cm.astype(bf16)
        cb = lax.dot_general(cmb, bmb, (((1,), (1,)), ((), ())), preferred_element_type=f32)
        state = state_ref[g]
        yoff = jnp.dot(cmb, state.astype(bf16), preferred_element_type=f32)

        scales = jnp.concatenate([dt, e_cs, e_end], axis=0).astype(bf16)
        spread = jnp.dot(scales, head_lanes, preferred_element_type=f32)
        xdt_g = xs * spread[0:CHUNK]
        xd2 = (xdt_g * spread[2 * CHUNK:3 * CHUNK]).astype(bf16)
        cd = []
        for q in range(HEADS_PER_GROUP // 2):
            ha, hb = 2 * q, 2 * q + 1
            sl = slice(q * 2 * HEAD_DIM, (q + 1) * 2 * HEAD_DIM)
            gl = pl.ds(pl.multiple_of(lo + q * 2 * HEAD_DIM, 2 * HEAD_DIM), 2 * HEAD_DIM)
            xq = xs[:, sl]
            xdtb = xdt_g[:, sl].astype(bf16)
            wa = cb * jnp.exp2(jnp.where(causal, a2[:, ha:ha + 1] - a2t[ha:ha + 1, :], -1e30))
            wb = cb * jnp.exp2(jnp.where(causal, a2[:, hb:hb + 1] - a2t[hb:hb + 1, :], -1e30))
            ya = jnp.dot(wa.astype(bf16), xdtb, preferred_element_type=f32)
            yb = jnp.dot(wb.astype(bf16), xdtb, preferred_element_type=f32)
            y = jnp.where(first, ya, yb) + yoff[:, sl] * spread[CHUNK:2 * CHUNK, sl] + dskip_ref[:, gl] * xq
            cd.append(jnp.where(first1, e_last[:, ha:ha + 1], e_last[:, hb:hb + 1]))
            zq = z_ref[:, gl].astype(f32)
            v = y * (zq * _sigmoid(zq))
            ssq = ssq + v * v
            vw_ref[:, gl] = (v * nw_ref[:, gl]).astype(vw_ref.dtype)

        bt = jnp.transpose(bm).astype(bf16)
        s_new = jnp.dot(bt, xd2, preferred_element_type=f32)
        state_ref[g] = state * jnp.concatenate(cd, axis=1) + s_new
        return ssq

    ssq = lax.fori_loop(0, N_GROUPS, group, jnp.zeros((CHUNK, 2 * HEAD_DIM), f32))
    ssq_ref[...] = jnp.broadcast_to(jnp.sum(ssq, axis=1, keepdims=True), (CHUNK, LANES))


def _ssd_branch(proj, dt, conv_w, conv_b, a_log, d_skip, ssd_norm_w):
    bc_w = 2 * N_GROUPS * N_STATE
    dt_g = dt.reshape(R, N_GROUPS, HEADS_PER_GROUP).transpose(1, 0, 2)
    dt_gt = dt_g.transpose(0, 2, 1)
    alog = a_log.reshape(N_GROUPS, 1, HEADS_PER_GROUP)
    alogt = a_log.reshape(N_GROUPS, HEADS_PER_GROUP, 1)
    conv_b2 = conv_b.reshape(1, -1)
    dskip_e = jnp.repeat(d_skip, HEAD_DIM).reshape(1, D_INNER)

    def cur(col, width):
        return pl.BlockSpec((pl.Element(CHUNK), pl.Element(width)),
                            lambda c: (pl.multiple_of(c * CHUNK, CHUNK), col))

    def halo(col, width):
        return pl.BlockSpec((pl.Element(HALO), pl.Element(width)),
                            lambda c: (pl.multiple_of(jnp.maximum(c * CHUNK - HALO, 0), HALO), col))

    full = lambda shape: pl.BlockSpec(shape, lambda c: (0,) * len(shape))
    in_specs = [
        cur(COL_X, D_INNER), halo(COL_X, D_INNER),
        cur(COL_B, bc_w), halo(COL_B, bc_w),
        cur(COL_Z, D_INNER),
        pl.BlockSpec((N_GROUPS, CHUNK, HEADS_PER_GROUP), lambda c: (0, c, 0)),
        pl.BlockSpec((N_GROUPS, HEADS_PER_GROUP, CHUNK), lambda c: (0, 0, c)),
        full((N_GROUPS, 1, HEADS_PER_GROUP)),
        full((N_GROUPS, HEADS_PER_GROUP, 1)),
        pl.BlockSpec((CONV_W, D_INNER), lambda c: (0, 0)),
        pl.BlockSpec((CONV_W, bc_w), lambda c: (0, D_INNER // bc_w)),
        pl.BlockSpec((1, D_INNER), lambda c: (0, 0)),
        pl.BlockSpec((1, bc_w), lambda c: (0, D_INNER // bc_w)),
        full((1, D_INNER)),
        full((1, D_INNER)),
    ]
    return pl.pallas_call(
        _ssd_kernel,
        grid=(N_CHUNKS,),
        in_specs=in_specs,
        out_specs=[
            pl.BlockSpec((CHUNK, D_INNER), lambda c: (c, 0)),
            pl.BlockSpec((CHUNK, LANES), lambda c: (c, 0)),
        ],
        out_shape=[
            jax.ShapeDtypeStruct((R, D_INNER), bf16),
            jax.ShapeDtypeStruct((R, LANES), f32),
        ],
        scratch_shapes=[
            pltpu.VMEM((N_GROUPS, N_STATE, GROUP_W), f32),
            pltpu.VMEM((HALO + CHUNK, D_INNER), bf16),
            pltpu.VMEM((HALO + CHUNK, bc_w), bf16),
        ],
        compiler_params=_params(("arbitrary",)),
        name="ssd_mixer",
    )(proj, proj, proj, proj, proj, dt_g, dt_gt, alog, alogt,
      conv_w, conv_w, conv_b2, conv_b2, dskip_e, ssd_norm_w.reshape(1, D_INNER))


def _router_kernel(h_ref, nw_ref, rw_ref, rb_ref, u_ref, idx_ref, gate_ref, *, tm):
    i = pl.program_id(0)
    h = h_ref[...]
    ms = jnp.mean(h * h, axis=-1, keepdims=True)
    u = h * lax.rsqrt(ms + EPS) * nw_ref[...]
    rows = i * tm + lax.broadcasted_iota(i32, (tm, 1), 0)
    u = jnp.where(rows >= PAD_ROWS, u, 0.0)
    u_ref[...] = u
    logits = jnp.dot(u, rw_ref[...], preferred_element_type=f32, precision=lax.Precision.HIGHEST) + rb_ref[...]
    e_iota = lax.broadcasted_iota(i32, (tm, N_EXPERTS), 1)
    lane = lax.broadcasted_iota(i32, (tm, LANES), 1)
    idx_out = jnp.zeros((tm, LANES), i32)
    val_out = jnp.zeros((tm, LANES), f32)
    vals = []
    for k in range(TOP_K):
        m = jnp.max(logits, axis=1, keepdims=True)
        sel = jnp.min(jnp.where(logits == m, e_iota, N_EXPERTS), axis=1, keepdims=True)
        vals.append(m)
        idx_out = jnp.where(lane == k, sel, idx_out)
        logits = jnp.where(e_iota == sel, -jnp.inf, logits)
    exps = [jnp.exp(v - vals[0]) for v in vals]
    denom = exps[0] + exps[1] + exps[2] + exps[3]
    for k in range(TOP_K):
        val_out = jnp.where(lane == k, exps[k] / denom, val_out)
    idx_ref[...] = idx_out
    gate_ref[...] = val_out


def _router(h1, norm_w, router_w, router_b, tm=320):
    return pl.pallas_call(
        functools.partial(_router_kernel, tm=tm),
        grid=(R // tm,),
        in_specs=[
            pl.BlockSpec((tm, D_MODEL), lambda i: (i, 0)),
            pl.BlockSpec((1, D_MODEL), lambda i: (0, 0)),
            pl.BlockSpec((D_MODEL, N_EXPERTS), lambda i: (0, 0)),
            pl.BlockSpec((1, N_EXPERTS), lambda i: (0, 0)),
        ],
        out_specs=[
            pl.BlockSpec((tm, D_MODEL), lambda i: (i, 0)),
            pl.BlockSpec((tm, LANES), lambda i: (i, 0)),
            pl.BlockSpec((tm, LANES), lambda i: (i, 0)),
        ],
        out_shape=[
            jax.ShapeDtypeStruct((R, D_MODEL), f32),
            jax.ShapeDtypeStruct((R, LANES), i32),
            jax.ShapeDtypeStruct((R, LANES), f32),
        ],
        compiler_params=_params(("parallel",)),
        name="router",
    )(h1, norm_w.reshape(1, D_MODEL), router_w, router_b.reshape(1, N_EXPERTS))


def _gather_x_kernel(pos_ref, nu_ref, nv_ref, src_ref, o_ref, idx_ref, buf_ref, sem, *, tm):
    i = pl.program_id(0)
    n_used = nu_ref[0]

    @pl.when(i == 0)
    def _():
        def invert(b, carry):
            for u in range(DMA_ISSUE_UNROLL):
                p = b * DMA_ISSUE_UNROLL + u
                idx_ref[pos_ref[p]] = lax.shift_right_logical(p, 2) + PAD_ROWS
            return carry

        lax.fori_loop(0, N_PAIRS // DMA_ISSUE_UNROLL, invert, 0)

    def row_copy(src_row, slot, r):
        return pltpu.make_async_copy(src_ref.at[pl.ds(src_row, 1)], buf_ref.at[slot, pl.ds(r, 1)], sem.at[slot])

    def issue_tile(step):
        slot = step % 2
        n_blocks = nv_ref[step] // DMA_ISSUE_UNROLL

        def issue_block(b, carry):
            for u in range(DMA_ISSUE_UNROLL):
                r = b * DMA_ISSUE_UNROLL + u
                row_copy(idx_ref[step * tm + r], slot, r).start()
            return carry

        def issue(r, carry):
            row_copy(idx_ref[step * tm + r], slot, r).start()
            return carry

        lax.fori_loop(0, n_blocks, issue_block, 0)
        lax.fori_loop(n_blocks * DMA_ISSUE_UNROLL, nv_ref[step], issue, 0)

    @pl.when(i == 0)
    def _():
        buf_ref[...] = jnp.zeros_like(buf_ref)
        issue_tile(0)

    @pl.when(i + 1 < n_used)
    def _():
        issue_tile(i + 1)

    @pl.when(i < n_used)
    def _():
        slot = i % 2

        @pl.when(nv_ref[i] == tm)
        def _():
            pltpu.make_async_copy(src_ref.at[pl.ds(0, tm)], buf_ref.at[slot], sem.at[slot]).wait()

        @pl.when(nv_ref[i] < tm)
        def _():
            def drain(r, carry):
                row_copy(0, slot, 0).wait()
                return carry

            lax.fori_loop(0, nv_ref[i], drain, 0)

        o_ref[...] = buf_ref[slot].astype(o_ref.dtype)

    @pl.when(i >= n_used)
    def _():
        o_ref[...] = jnp.zeros_like(o_ref)


def _gather_x(pos_flat, n_used, tile_rows, u2, tm=MOE_TM):
    assert TOP_K == 4
    return pl.pallas_call(
        functools.partial(_gather_x_kernel, tm=tm),
        grid_spec=pltpu.PrefetchScalarGridSpec(
            num_scalar_prefetch=3,
            grid=(XS_ROWS // tm,),
            in_specs=[pl.BlockSpec(memory_space=pl.ANY)],
            out_specs=pl.BlockSpec((tm, D_MODEL), lambda i, idx, nu, nv: (i, 0)),
            scratch_shapes=[pltpu.SMEM((XS_ROWS,), i32), pltpu.VMEM((2, tm, D_MODEL), f32),
                            pltpu.SemaphoreType.DMA((2,))],
        ),
        out_shape=jax.ShapeDtypeStruct((XS_ROWS, D_MODEL), bf16),
        compiler_params=_params(("arbitrary",)),
        name="gather_x",
    )(pos_flat, n_used, tile_rows, u2)


def _grouped_kernel(ge_ref, gt_ref, gn_ref, gs_ref, ng_ref, nu_ref, x_ref, *refs, n_w, tn, nj, ocols, w_col_offsets,
                    tile_fn):
    w_hbm, b_refs, o_ref = refs[:n_w], refs[n_w:2 * n_w], refs[2 * n_w]
    wbuf_ref, wb_ref, obuf_ref, zbuf_ref, wsem, sem, zsem = refs[2 * n_w + 1:]
    g = pl.program_id(0)
    n_groups = ng_ref[0]
    active = g < n_groups
    gc = jnp.minimum(g, n_groups - 1)
    n_sub = jnp.where(active, gs_ref[gc], 0)
    tile0 = gt_ref[gc]
    total_tiles = o_ref.shape[0] // MOE_TM

    def weight_copies(grp, jj, wslot):
        e = ge_ref[grp]
        return [pltpu.make_async_copy(
            w_hbm[k].at[e, :, pl.ds(pl.multiple_of((w_col_offsets[k] + jj) * tn, tn), tn)],
            wbuf_ref.at[wslot, k], wsem.at[wslot]) for k in range(n_w)]

    @pl.when(g == 0)
    def _():
        for cp in weight_copies(0, 0, 0):
            cp.start()
        zbuf_ref[...] = jnp.zeros_like(zbuf_ref)

        def zero_copy(t, jc):
            rows = pl.ds(pl.multiple_of(t * MOE_TM, MOE_TM), MOE_TM)
            return pltpu.make_async_copy(zbuf_ref, o_ref.at[rows, pl.ds(jc * ocols, ocols)], zsem)

        def z_start(t, carry):
            for jc in range(nj):
                zero_copy(t, jc).start()
            return carry

        def z_wait(t, carry):
            for jc in range(nj):
                zero_copy(t, jc).wait()
            return carry

        lax.fori_loop(nu_ref[0], total_tiles, z_start, 0)
        lax.fori_loop(nu_ref[0], total_tiles, z_wait, 0)

    def out_copy(slot, n, tile, col):
        rows = pl.ds(pl.multiple_of(tile * MOE_TM, MOE_TM), n * MOE_TM)
        return pltpu.make_async_copy(obuf_ref.at[slot, pl.ds(0, n * MOE_TM)], o_ref.at[rows, pl.ds(col, ocols)],
                                     sem.at[slot])

    def wait_step(s):
        gs = s // nj
        sent = jnp.where(jnp.logical_and(s >= 0, gs < n_groups), gn_ref[jnp.clip(gs, 0, MAX_GROUPS - 1)], 0)
        for n in range(1, MOE_CAP + 1):

            @pl.when(sent == n)
            def _(n=n):
                out_copy(s % 2, n, 0, 0).wait()

    def col_block(j, carry):
        step = g * nj + j
        slot = step % 2

        @pl.when(j + 1 < nj)
        def _():
            for cp in weight_copies(gc, j + 1, 1 - slot):
                cp.start()

        @pl.when(jnp.logical_and(j + 1 == nj, g + 1 < n_groups))
        def _():
            for cp in weight_copies(g + 1, 0, 1 - slot):
                cp.start()

        for cp in weight_copies(gc, j, slot):
            cp.wait()
        for k in range(n_w):
            wb_ref[:, k * tn:(k + 1) * tn] = wbuf_ref[slot, k].astype(bf16)

        wait_step(step - 2)
        col = pl.multiple_of(j * ocols, ocols)
        biases = [b_refs[k][0, :, pl.ds(pl.multiple_of((w_col_offsets[k] + j) * tn, tn), tn)] for k in range(n_w)]
        for m in range(1, MOE_CAP * MOE_TM // MOE_SUB + 1):

            @pl.when(n_sub == m)
            def _(m=m):
                rows = m * MOE_SUB
                n = -(-rows // MOE_TM)
                acc = jnp.dot(x_ref[0:rows, :], wb_ref[...], preferred_element_type=f32)
                obuf_ref[slot, pl.ds(0, rows)] = tile_fn(acc, biases)
                if rows < n * MOE_TM:
                    obuf_ref[slot, pl.ds(rows, n * MOE_TM - rows)] = jnp.zeros((n * MOE_TM - rows, ocols), obuf_ref.dtype)
                out_copy(slot, n, tile0, col).start()

        return carry

    @pl.when(active)
    def _():
        lax.fori_loop(0, nj, col_block, 0)

    @pl.when(g == MAX_GROUPS - 1)
    def _():
        last = n_groups * nj - 1
        wait_step(last - 1)
        wait_step(last)


def _grouped_call(groups, n_used, x, ws, bs, *, k_dim, tn, nj, w_col_offsets, ocols, out_shape, tile_fn, name):
    g_expert, g_tile0, g_ntiles, g_nsub, n_groups = groups
    n_w = len(ws)

    def gi(g, ng):
        return jnp.minimum(g, ng[0] - 1)

    def x_map(g, ge, gt, gn, gs, ng, nu):
        return (pl.multiple_of(gt[gi(g, ng)] * MOE_TM, MOE_TM), 0)

    def b_map(g, ge, gt, gn, gs, ng, nu):
        return (ge[gi(g, ng)], 0, 0)

    return pl.pallas_call(
        functools.partial(_grouped_kernel, n_w=n_w, tn=tn, nj=nj, ocols=ocols, w_col_offsets=w_col_offsets,
                          tile_fn=tile_fn),
        grid_spec=pltpu.PrefetchScalarGridSpec(
            num_scalar_prefetch=6,
            grid=(MAX_GROUPS,),
            in_specs=[
                pl.BlockSpec((pl.Element(GROUP_ROWS), pl.Element(k_dim)), x_map),
                *[pl.BlockSpec(memory_space=pl.ANY) for _ in ws],
                *[pl.BlockSpec((1, 1, b.shape[2]), b_map) for b in bs],
            ],
            out_specs=pl.BlockSpec(memory_space=pl.ANY),
            scratch_shapes=[
                pltpu.VMEM((2, n_w, k_dim, tn), f32),
                pltpu.VMEM((k_dim, n_w * tn), bf16),
                pltpu.VMEM((2, GROUP_ROWS, ocols), out_shape.dtype),
                pltpu.VMEM((MOE_TM, ocols), out_shape.dtype),
                pltpu.SemaphoreType.DMA((2,)),
                pltpu.SemaphoreType.DMA((2,)),
                pltpu.SemaphoreType.DMA(()),
            ],
        ),
        out_shape=out_shape,
        compiler_params=_params(("arbitrary",)),
        name=name,
    )(g_expert, g_tile0, g_ntiles, g_nsub, n_groups, n_used, x, *ws, *bs)


def _moe_up(groups, n_used, xs, w_up, b_up, tn=256):
    nj = D_EXPERT // tn

    def tile_fn(acc, biases):
        gate = jnp.minimum(acc[:, :tn] + biases[0], LIMIT)
        up = jnp.clip(acc[:, tn:] + biases[1], -LIMIT, LIMIT)
        return ((up + 1.0) * (gate * _sigmoid(ALPHA * gate))).astype(bf16)

    b3 = b_up.reshape(N_EXPERTS, 1, -1)
    return _grouped_call(groups, n_used, xs, (w_up, w_up), (b3, b3), k_dim=D_MODEL, tn=tn, nj=nj,
                         w_col_offsets=(0, nj), ocols=tn,
                         out_shape=jax.ShapeDtypeStruct((XS_ROWS, D_EXPERT), bf16), tile_fn=tile_fn, name="moe_up")


def _moe_down(groups, n_used, act, w_down, b_down, tn=1024):
    return _grouped_call(groups, n_used, act, (w_down,), (b_down.reshape(N_EXPERTS, 1, -1),), k_dim=D_EXPERT,
                         tn=tn, nj=D_MODEL // tn, w_col_offsets=(0,), ocols=tn,
                         out_shape=jax.ShapeDtypeStruct((N_SLOTS, D_MODEL), f32),
                         tile_fn=lambda acc, biases: acc + biases[0], name="moe_down")


def _combine_kernel(pos_ref, h_ref, g_ref, nw_ref, y_ref, o_ref, buf_ref, sem, *, tm):
    i = pl.program_id(0)
    n = pl.num_programs(0)

    def row_copy(src_row, slot, k, r):
        return pltpu.make_async_copy(y_ref.at[pl.ds(src_row, 1)], buf_ref.at[slot, k, pl.ds(r, 1)], sem.at[slot])

    def issue_tile(step):
        slot = step % 2
        for k in range(TOP_K):

            def issue(r, carry, k=k):
                row_copy(pos_ref[k * SEQ + step * tm + r], slot, k, r).start()
                return carry

            lax.fori_loop(0, tm, issue, 0, unroll=DMA_ISSUE_UNROLL)

    @pl.when(i == 0)
    def _():
        issue_tile(0)

    @pl.when(i + 1 < n)
    def _():
        issue_tile(i + 1)

    slot = i % 2
    for k in range(TOP_K):
        pltpu.make_async_copy(y_ref.at[pl.ds(0, tm)], buf_ref.at[slot, k], sem.at[slot]).wait()

    g = g_ref[...]
    h = h_ref[...]
    for k in range(TOP_K):
        h = h + g[:, k:k + 1] * buf_ref[slot, k]
    ms = jnp.mean(h * h, axis=-1, keepdims=True)
    o_ref[...] = h * lax.rsqrt(ms + EPS) * nw_ref[...]


def _combine(pos_kt, h1, y, gates, norm_w, tm=128):
    nb = SEQ // tm
    off = (PAD_ROWS + N_META) // tm
    return pl.pallas_call(
        functools.partial(_combine_kernel, tm=tm),
        grid_spec=pltpu.PrefetchScalarGridSpec(
            num_scalar_prefetch=1,
            grid=(nb,),
            in_specs=[
                pl.BlockSpec((tm, D_MODEL), lambda i, pos: (i + off, 0)),
                pl.BlockSpec((tm, LANES), lambda i, pos: (i + off, 0)),
                pl.BlockSpec((1, D_MODEL), lambda i, pos: (0, 0)),
                pl.BlockSpec(memory_space=pl.ANY),
            ],
            out_specs=pl.BlockSpec((tm, D_MODEL), lambda i, pos: (i, 0)),
            scratch_shapes=[pltpu.VMEM((2, TOP_K, tm, D_MODEL), f32), pltpu.SemaphoreType.DMA((2,))],
        ),
        out_shape=jax.ShapeDtypeStruct((SEQ, D_MODEL), f32),
        compiler_params=_params(("arbitrary",)),
        name="combine_final_norm",
    )(pos_kt, h1, gates, norm_w.reshape(1, D_MODEL), y)


def _rank_kernel(idx_ref, rank_ref, cnt_ref, carry_ref, *, tm):
    i = pl.program_id(0)

    @pl.when(i == 0)
    def _():
        carry_ref[...] = jnp.zeros_like(carry_ref)

    idx = idx_ref[...]
    rows = i * tm + lax.broadcasted_iota(i32, (tm, 1), 0)
    valid = rows >= PAD_ROWS
    e_iota = lax.broadcasted_iota(i32, (tm, N_EXPERTS), 1)
    hits = [jnp.logical_and(idx[:, k:k + 1] == e_iota, valid) for k in range(TOP_K)]
    onehot = sum(jnp.where(h, 1.0, 0.0) for h in hits)
    earlier = lax.broadcasted_iota(i32, (tm, tm), 0) > lax.broadcasted_iota(i32, (tm, tm), 1)
    before = jnp.dot(jnp.where(earlier, 1.0, 0.0).astype(bf16), onehot.astype(bf16),
                     preferred_element_type=f32) + carry_ref[...]
    lane = lax.broadcasted_iota(i32, (tm, LANES), 1)
    out = jnp.zeros((tm, LANES), i32)
    for k in range(TOP_K):
        rk = jnp.sum(jnp.where(hits[k], before, 0.0), axis=1, keepdims=True)
        out = jnp.where(lane == k, rk.astype(i32), out)
    rank_ref[...] = out
    carry_ref[...] += jnp.sum(onehot, axis=0, keepdims=True)
    cnt_ref[...] = carry_ref[...]


def _rank_pairs(top_idx, tm=320):
    return pl.pallas_call(
        functools.partial(_rank_kernel, tm=tm),
        grid=(R // tm,),
        in_specs=[pl.BlockSpec((tm, LANES), lambda i: (i, 0))],
        out_specs=[pl.BlockSpec((tm, LANES), lambda i: (i, 0)), pl.BlockSpec((1, N_EXPERTS), lambda i: (0, 0))],
        out_shape=[jax.ShapeDtypeStruct((R, LANES), i32), jax.ShapeDtypeStruct((1, N_EXPERTS), f32)],
        scratch_shapes=[pltpu.VMEM((1, N_EXPERTS), f32)],
        compiler_params=_params(("arbitrary",)),
        name="rank_pairs",
    )(top_idx)


def _routing_tables(top_idx, rank, counts):
    counts = counts.reshape(N_EXPERTS).astype(i32)
    tiles = (counts + MOE_TM - 1) // MOE_TM
    tile_end = jnp.cumsum(tiles)
    tile_start = tile_end - tiles
    e_ids = jnp.arange(N_EXPERTS, dtype=i32)
    tok_e = top_idx[PAD_ROWS:, :TOP_K]
    tok_start = jnp.sum(jnp.where(tok_e[..., None] == e_ids, tile_start * MOE_TM, 0), axis=-1)
    pos = tok_start + rank[PAD_ROWS:, :TOP_K]
    n_used = tile_end[-1:].astype(i32)
    tile_ids = jnp.arange(XS_ROWS // MOE_TM, dtype=i32)
    tile_e = jnp.minimum(jnp.sum(tile_end[None, :] <= tile_ids[:, None], axis=1), N_EXPERTS - 1)
    tile_rows = jnp.clip(counts[tile_e] - (tile_ids - tile_start[tile_e]) * MOE_TM, 0, MOE_TM).astype(i32)
    tile_rows = jnp.where(tile_ids < n_used[0], tile_rows, 0)
    groups = (tiles + MOE_CAP - 1) // MOE_CAP
    group_end = jnp.cumsum(groups)
    g_ids = jnp.arange(MAX_GROUPS, dtype=i32)
    g_expert = jnp.minimum(jnp.sum(group_end[None, :] <= g_ids[:, None], axis=1), N_EXPERTS - 1).astype(i32)
    g_local = g_ids - (group_end - groups)[g_expert]
    g_tile0 = (tile_start[g_expert] + g_local * MOE_CAP).astype(i32)
    g_ntiles = jnp.clip(tiles[g_expert] - g_local * MOE_CAP, 0, MOE_CAP).astype(i32)
    g_rows = jnp.clip(counts[g_expert] - g_local * GROUP_ROWS, 0, GROUP_ROWS)
    g_nsub = ((g_rows + MOE_SUB - 1) // MOE_SUB).astype(i32)
    n_groups = group_end[-1:].astype(i32)
    return pos, n_used, tile_rows, (g_expert, g_tile0, g_ntiles, g_nsub, n_groups)


def kernel(x, meta_tokens, norm_mix_w, w_in, conv_w, conv_b, dt_bias, a_log, d_skip, ssd_norm_w, w_ssd_out,
           pool_w, pool_scale, w_out, norm_ffn_w, router_w, router_b, w_up, b_up, w_down, b_down, norm_final_w):
    assert x.shape == (1, SEQ, D_MODEL) and norm_mix_w.shape[0] == 1
    x2 = x[0]
    head = jnp.concatenate([jnp.zeros((PAD_ROWS, D_MODEL), f32), meta_tokens.astype(f32)], axis=0)

    u = _rms_norm_rows(x2, head, norm_mix_w[0])
    proj = _in_proj_main(u, w_in[0])
    dt = _in_proj_dt(u, w_in[0], dt_bias[0])
    gates = _in_proj_gates(u, w_in[0])
    ma = _pool_branch(proj, gates, pool_w[0], pool_scale[0])
    vw, ssq = _ssd_branch(proj, dt, conv_w[0], conv_b[0], a_log[0], d_skip[0], ssd_norm_w[0])
    merged = _ssd_out_merge(vw, w_ssd_out[0], ssq, ma, gates)
    h1 = _out_proj(merged, w_out[0], head, x2)

    u2, top_idx, top_gate = _router(h1, norm_ffn_w[0], router_w[0], router_b[0])
    rank, counts = _rank_pairs(top_idx)
    pos, n_used, tile_rows, groups = _routing_tables(top_idx, rank, counts)
    xs = _gather_x(pos.reshape(-1), n_used, tile_rows, u2)
    act = _moe_up(groups, n_used, xs, w_up[0], b_up[0])
    y = _moe_down(groups, n_used, act, w_down[0], b_down[0])
    pos_kt = pos[N_META:].T.reshape(-1)
    out = _combine(pos_kt, h1, y, top_gate, norm_final_w)
    return out[None]
```

```python
import functools

import jax
import jax.numpy as jnp
from jax import lax
from jax.experimental import pallas as pl
from jax.experimental.pallas import tpu as pltpu

f32 = jnp.float32
bf16 = jnp.bfloat16
i32 = jnp.int32

D_MODEL = 4096
SEQ = 8192
N_META = 16
CHUNK = 128
PAD_ROWS = CHUNK - N_META
R = PAD_ROWS + N_META + SEQ
N_CHUNKS = R // CHUNK
POOL_WINDOWS = (2, 4, 8, 16)
POOL_GROUP = D_MODEL // 4
D_INNER = 2 * D_MODEL
HEAD_DIM = 64
N_HEADS = D_INNER // HEAD_DIM
N_STATE = 128
N_GROUPS = 8
HEADS_PER_GROUP = N_HEADS // N_GROUPS
GROUP_W = HEADS_PER_GROUP * HEAD_DIM
CONV_W = 4
HALO = 16
POOL_HALO = 2 * max(POOL_WINDOWS)
N_EXPERTS = 32
TOP_K = 4
D_EXPERT = 7 * D_MODEL // 16
LIMIT = 7.0
ALPHA = 1.702
EPS = 1e-5
LOG2E = 1.4426950408889634
N_TOK = N_META + SEQ
N_PAIRS = N_TOK * TOP_K
MOE_TM = 256
N_TILES = -(-N_PAIRS // MOE_TM) + N_EXPERTS
N_SLOTS = N_TILES * MOE_TM
MOE_CAP = 6
MOE_SUB = 128
GROUP_ROWS = MOE_CAP * MOE_TM
MAX_GROUPS = (N_TILES - 1 + N_EXPERTS * (MOE_CAP - 1)) // MOE_CAP
XS_ROWS = N_SLOTS + GROUP_ROWS
COL_Z = D_MODEL
COL_X = COL_Z + D_INNER
COL_B = COL_X + D_INNER
COL_C = COL_B + N_GROUPS * N_STATE
COL_DT = COL_C + N_GROUPS * N_STATE
COL_G = COL_DT + N_HEADS
SEG1_COLS = COL_DT
LANES = 128
V7X_VMEM_LIMIT = 56 * 1024 * 1024
DMA_ISSUE_UNROLL = 8


def _params(sem, vmem=V7X_VMEM_LIMIT):
    return pltpu.CompilerParams(dimension_semantics=sem, vmem_limit_bytes=vmem)


def _sigmoid(v):
    return 1.0 / (1.0 + jnp.exp(-v))


def _sigmoid_tanh(v):
    return 0.5 * jnp.tanh(0.5 * v) + 0.5


def _x_rows_spec(tm, tn, col_of):
    return pl.BlockSpec((pl.Element(tm), pl.Element(tn)),
                        lambda i, *rest: (pl.multiple_of(jnp.maximum(i * tm - CHUNK, 0), CHUNK), col_of(i, *rest)))


def _rms_kernel(x_ref, head_ref, w_ref, o_ref):
    i = pl.program_id(0)
    tm = o_ref.shape[0]

    def norm(h):
        ms = jnp.mean(h * h, axis=-1, keepdims=True)
        return (h * lax.rsqrt(ms + EPS) * w_ref[...]).astype(o_ref.dtype)

    @pl.when(i == 0)
    def _():
        o_ref[0:CHUNK, :] = norm(head_ref[...])
        o_ref[CHUNK:tm, :] = norm(x_ref[0:tm - CHUNK, :])

    @pl.when(i > 0)
    def _():
        o_ref[...] = norm(x_ref[...])


def _rms_norm_rows(x2, head, w, tm=640):
    d = x2.shape[1]
    return pl.pallas_call(
        _rms_kernel,
        grid=(R // tm,),
        in_specs=[_x_rows_spec(tm, d, lambda i: 0), pl.BlockSpec((CHUNK, d), lambda i: (0, 0)),
                  pl.BlockSpec((1, d), lambda i: (0, 0))],
        out_specs=pl.BlockSpec((tm, d), lambda i: (i, 0)),
        out_shape=jax.ShapeDtypeStruct((R, d), bf16),
        compiler_params=_params(("parallel",)),
        name="rms_mix",
    )(x2, head, w.reshape(1, d))


def _mm_kernel(*refs, n_extra, epilogue, nk):
    a_ref, w_ref = refs[0], refs[1]
    extra = refs[2:2 + n_extra]
    o_ref = refs[2 + n_extra]
    acc_ref = refs[3 + n_extra]
    k = pl.program_id(2)

    def partial():
        return jnp.dot(a_ref[...], w_ref[...].astype(bf16), preferred_element_type=f32)

    if nk > 1:
        @pl.when(k == 0)
        def _():
            acc_ref[...] = partial()

    if nk > 2:
        @pl.when(jnp.logical_and(k > 0, k < nk - 1))
        def _():
            acc_ref[...] += partial()

    @pl.when(k == nk - 1)
    def _():
        epilogue(o_ref, partial() if nk == 1 else acc_ref[...] + partial(), *extra)


def _store(fn):
    def epilogue(o_ref, acc, *extra):
        o_ref[...] = fn(acc, *[e[...] for e in extra]).astype(o_ref.dtype)

    return epilogue


def _matmul(a, w, *, col0, n, tm, tn, tk, out_dtype, epilogue, extras=(), extra_specs=(), name):
    m, kdim = a.shape
    nk = kdim // tk
    a_mode = {"pipeline_mode": pl.Buffered(1)} if nk == 1 else {}
    if col0 % tn == 0:
        w_spec = pl.BlockSpec((tk, tn), lambda i, j, k: (k, j + col0 // tn))
    else:
        w_spec = pl.BlockSpec((pl.Element(tk), pl.Element(tn)),
                              lambda i, j, k: (k * tk, pl.multiple_of(col0 + j * tn, LANES)))
    return pl.pallas_call(
        functools.partial(_mm_kernel, n_extra=len(extras), epilogue=epilogue, nk=nk),
        grid=(m // tm, n // tn, nk),
        in_specs=[pl.BlockSpec((tm, tk), lambda i, j, k: (i, k), **a_mode), w_spec, *extra_specs],
        out_specs=pl.BlockSpec((tm, tn), lambda i, j, k: (i, j)),
        out_shape=jax.ShapeDtypeStruct((m, n), out_dtype),
        scratch_shapes=[pltpu.VMEM((tm, tn), f32)],
        compiler_params=_params(("parallel", "parallel", "arbitrary")),
        name=name,
    )(a, w, *extras)


def _softplus_bias(acc, bias):
    v = acc + bias
    return jnp.maximum(v, 0.0) + jnp.log1p(jnp.exp(-jnp.abs(v)))


def _in_proj_main(u, w_in):
    return _matmul(u, w_in, col0=0, n=SEG1_COLS, tm=1664, tn=512, tk=D_MODEL, out_dtype=bf16,
                   epilogue=_store(lambda acc: acc), name="in_proj_main")


def _in_proj_dt(u, w_in, dt_bias):
    return _matmul(u, w_in, col0=COL_DT, n=N_HEADS, tm=1664, tn=N_HEADS, tk=1024, out_dtype=f32,
                   epilogue=_store(_softplus_bias), extras=(dt_bias.reshape(1, N_HEADS),),
                   extra_specs=(pl.BlockSpec((1, N_HEADS), lambda i, j, k: (0, 0)),), name="in_proj_dt")


def _in_proj_gates(u, w_in):
    return _matmul(u, w_in, col0=COL_G, n=2 * D_MODEL, tm=1664, tn=512, tk=D_MODEL, out_dtype=bf16,
                   epilogue=_store(_sigmoid_tanh), name="in_proj_gates")


def _ssd_out_merge(vw, w_ssd_out, ssq, ma, gates, tm=1664, tn=1024):
    gb_off = D_MODEL // tn

    def merge(acc, ssq_t, ma_t, gb_t):
        rs = lax.rsqrt(ssq_t[:, :1] * (1.0 / D_INNER) + EPS)
        return ma_t.astype(f32) + gb_t.astype(f32) * (acc * rs)

    return _matmul(vw, w_ssd_out, col0=0, n=D_MODEL, tm=tm, tn=tn, tk=1024, out_dtype=bf16, epilogue=_store(merge),
                   extras=(ssq, ma, gates),
                   extra_specs=(pl.BlockSpec((tm, LANES), lambda i, j, k: (i, 0)),
                                pl.BlockSpec((tm, tn), lambda i, j, k: (i, j)),
                                pl.BlockSpec((tm, tn), lambda i, j, k: (i, j + gb_off))),
                   name="ssd_out_merge")


def _out_proj(merged, w_out, head, x2, tm=1664, tn=1024):
    def add_residual(o_ref, acc, head_ref, x_ref):
        i = pl.program_id(0)

        @pl.when(i == 0)
        def _():
            o_ref[0:CHUNK, :] = head_ref[...] + acc[0:CHUNK]
            o_ref[CHUNK:tm, :] = x_ref[0:tm - CHUNK, :] + acc[CHUNK:tm]

        @pl.when(i > 0)
        def _():
            o_ref[...] = x_ref[...] + acc

    return _matmul(merged, w_out, col0=0, n=D_MODEL, tm=tm, tn=tn, tk=1024, out_dtype=f32,
                   epilogue=add_residual, extras=(head, x2),
                   extra_specs=(pl.BlockSpec((CHUNK, tn), lambda i, j, k: (0, j)),
                                _x_rows_spec(tm, tn, lambda i, j, k: pl.multiple_of(j * tn, tn))),
                   name="out_proj_residual")


def _pool_kernel(x_ref, halo_ref, w_ref, scale_ref, ga_ref, o_ref, a_ref, b_ref, pooled_ref, *, tm):
    assert POOL_WINDOWS == (2, 4, 8, 16)
    gi = pl.program_id(0)
    i = pl.program_id(1)
    n = POOL_HALO + tm
    halo = halo_ref[...].astype(f32)
    a_ref[0:POOL_HALO, :] = jnp.where(i == 0, 0.0, halo)
    a_ref[POOL_HALO:n, :] = x_ref[...].astype(f32)
    t = i * tm + lax.broadcasted_iota(i32, (tm, 1), 0) - PAD_ROWS

    def pair_sum(src_ref, shift, start):
        return src_ref[start:n, :] + src_ref[start - shift:n - shift, :]

    def finish(s, win):
        cnt = jnp.clip(t + 1, 1, win).astype(f32)
        pooled_ref[...] = (s / cnt - x_ref[...].astype(f32)).astype(bf16)

    @pl.when(gi == 0)
    def _():
        finish(pair_sum(a_ref, 1, POOL_HALO), 2)

    @pl.when(gi == 1)
    def _():
        b_ref[8:n, :] = pair_sum(a_ref, 1, 8)
        finish(pair_sum(b_ref, 2, POOL_HALO), 4)

    @pl.when(gi == 2)
    def _():
        b_ref[8:n, :] = pair_sum(a_ref, 1, 8)
        a_ref[16:n, :] = pair_sum(b_ref, 2, 16)
        finish(pair_sum(a_ref, 4, POOL_HALO), 8)

    @pl.when(gi == 3)
    def _():
        b_ref[8:n, :] = pair_sum(a_ref, 1, 8)
        a_ref[16:n, :] = pair_sum(b_ref, 2, 16)
        b_ref[24:n, :] = pair_sum(a_ref, 4, 24)
        finish(pair_sum(b_ref, 8, POOL_HALO), 16)

    acc = jnp.dot(pooled_ref[...], w_ref[0].astype(bf16), preferred_element_type=f32)
    o_ref[...] = (ga_ref[...].astype(f32) * (acc * scale_ref[...])).astype(o_ref.dtype)


def _pool_branch(proj, gates, pool_w, pool_scale, tm=1664):
    gw = POOL_GROUP
    return pl.pallas_call(
        functools.partial(_pool_kernel, tm=tm),
        grid=(4, R // tm),
        in_specs=[
            pl.BlockSpec((tm, gw), lambda g, i: (i, g)),
            pl.BlockSpec((POOL_HALO, gw), lambda g, i: (jnp.maximum(i * (tm // POOL_HALO) - 1, 0), g)),
            pl.BlockSpec((1, gw, gw), lambda g, i: (g, 0, 0)),
            pl.BlockSpec((1, gw), lambda g, i: (0, g)),
            pl.BlockSpec((tm, gw), lambda g, i: (i, g)),
        ],
        out_specs=pl.BlockSpec((tm, gw), lambda g, i: (i, g)),
        out_shape=jax.ShapeDtypeStruct((R, D_MODEL), bf16),
        scratch_shapes=[pltpu.VMEM((POOL_HALO + tm, gw), f32), pltpu.VMEM((POOL_HALO + tm, gw), f32),
                        pltpu.VMEM((tm, gw), bf16)],
        compiler_params=_params(("parallel", "arbitrary")),
        name="pool_branch",
    )(proj, proj, pool_w, pool_scale.reshape(1, D_MODEL), gates)


def _fill_ext(cur_ref, halo_ref, ext_ref, c):
    halo = halo_ref[...]
    ext_ref[0:HALO, :] = jnp.where(c == 0, jnp.zeros_like(halo), halo)
    ext_ref[HALO:HALO + CHUNK, :] = cur_ref[...]


def _conv_silu(ext_ref, sel, w_ref, b_ref, lanes):
    taps = jnp.dot(sel, ext_ref[:, lanes], preferred_element_type=f32)
    out = b_ref[:, lanes]
    for k in range(CONV_W):
        out = out + w_ref[k:k + 1, lanes] * taps[k * CHUNK:(k + 1) * CHUNK]
    return out * _sigmoid(out)


def _ssd_kernel(x_ref, xh_ref, bc_ref, bch_ref, z_ref, dt_ref, dtt_ref, alog_ref, alogt_ref,
                wx_ref, wbc_ref, bx_ref, bbc_ref, dskip_ref, nw_ref,
                vw_ref, ssq_ref,
                state_ref, extx_ref, extbc_ref):
    c = pl.program_id(0)

    @pl.when(c == 0)
    def _():
        state_ref[...] = jnp.zeros_like(state_ref)

    _fill_ext(x_ref, xh_ref, extx_ref, c)
    _fill_ext(bc_ref, bch_ref, extbc_ref, c)

    rows = c * CHUNK + lax.broadcasted_iota(i32, (CHUNK, 1), 0)
    row_ok = rows >= PAD_ROWS
    cols = c * CHUNK + lax.broadcasted_iota(i32, (1, CHUNK), 1)
    col_ok = cols >= PAD_ROWS
    li = lax.broadcasted_iota(i32, (CHUNK, CHUNK), 0)
    si = lax.broadcasted_iota(i32, (CHUNK, CHUNK), 1)
    causal = li >= si
    tril = causal.astype(f32)
    triu = (li <= si).astype(f32)
    lane = lax.broadcasted_iota(i32, (CHUNK, 2 * HEAD_DIM), 1)
    first = lane < HEAD_DIM
    first1 = lax.broadcasted_iota(i32, (1, 2 * HEAD_DIM), 1) < HEAD_DIM
    sr = lax.broadcasted_iota(i32, (CONV_W * CHUNK, HALO + CHUNK), 0)
    se = lax.broadcasted_iota(i32, (CONV_W * CHUNK, HALO + CHUNK), 1)
    sel = jnp.where(se == (sr % CHUNK) + (sr // CHUNK) + HALO - (CONV_W - 1), 1.0, 0.0).astype(bf16)
    hh = lax.broadcasted_iota(i32, (HEADS_PER_GROUP, GROUP_W), 0)
    hl = lax.broadcasted_iota(i32, (HEADS_PER_GROUP, GROUP_W), 1)
    head_lanes = jnp.where(hl // HEAD_DIM == hh, 1.0, 0.0).astype(bf16)

    def group(g, ssq):
        lo = pl.multiple_of(g * GROUP_W, GROUP_W)
        blo = pl.multiple_of(g * N_STATE, N_STATE)
        xs = _conv_silu(extx_ref, sel, wx_ref, bx_ref, pl.ds(lo, GROUP_W))
        bm = _conv_silu(extbc_ref, sel, wbc_ref, bbc_ref, pl.ds(blo, N_STATE))
        cm = _conv_silu(extbc_ref, sel, wbc_ref, bbc_ref, pl.ds(N_GROUPS * N_STATE + blo, N_STATE))

        dt = jnp.where(row_ok, dt_ref[g], 0.0)
        dtt = jnp.where(col_ok, dtt_ref[g], 0.0)
        a_dt = dt * (-jnp.exp(alog_ref[g]))
        a_dtt = dtt * (-jnp.exp(alogt_ref[g]))
        a_cs = jnp.dot(tril, a_dt, preferred_element_type=f32, precision=lax.Precision.HIGHEST)
        a_cst = jnp.dot(a_dtt, triu, preferred_element_type=f32, precision=lax.Precision.HIGHEST)
        last = a_cs[CHUNK - 1:CHUNK, :]
        e_cs = jnp.exp(a_cs)
        e_end = jnp.exp(last - a_cs)
        e_last = jnp.exp(last)
        a2 = a_cs * LOG2E
        a2t = a_cst * LOG2E

        bmb = bm.astype(bf16)
        cmb = cm.astype(bf16)
        cb = lax.dot_general(cmb, bmb, (((1,), (1,)), ((), ())), preferred_element_type=f32)
        state = state_ref[g]
        yoff = jnp.dot(cmb, state.astype(bf16), preferred_element_type=f32)

        scales = jnp.concatenate([dt, e_cs, e_end], axis=0).astype(bf16)
        spread = jnp.dot(scales, head_lanes, preferred_element_type=f32)
        xdt_g = xs * spread[0:CHUNK]
        xd2 = (xdt_g * spread[2 * CHUNK:3 * CHUNK]).astype(bf16)
        cd = []
        for q in range(HEADS_PER_GROUP // 2):
            ha, hb = 2 * q, 2 * q + 1
            sl = slice(q * 2 * HEAD_DIM, (q + 1) * 2 * HEAD_DIM)
            gl = pl.ds(pl.multiple_of(lo + q * 2 * HEAD_DIM, 2 * HEAD_DIM), 2 * HEAD_DIM)
            xq = xs[:, sl]
            xdtb = xdt_g[:, sl].astype(bf16)
            wa = cb * jnp.exp2(jnp.where(causal, a2[:, ha:ha + 1] - a2t[ha:ha + 1, :], -1e30))
            wb = cb * jnp.exp2(jnp.where(causal, a2[:, hb:hb + 1] - a2t[hb:hb + 1, :], -1e30))
            ya = jnp.dot(wa.astype(bf16), xdtb, preferred_element_type=f32)
            yb = jnp.dot(wb.astype(bf16), xdtb, preferred_element_type=f32)
            y = jnp.where(first, ya, yb) + yoff[:, sl] * spread[CHUNK:2 * CHUNK, sl] + dskip_ref[:, gl] * xq
            cd.append(jnp.where(first1, e_last[:, ha:ha + 1], e_last[:, hb:hb + 1]))
            zq = z_ref[:, gl].astype(f32)
            v = y * (zq * _sigmoid(zq))
            ssq = ssq + v * v
            vw_ref[:, gl] = (v * nw_ref[:, gl]).astype(vw_ref.dtype)

        bt = jnp.transpose(bm).astype(bf16)
        s_new = jnp.dot(bt, xd2, preferred_element_type=f32)
        state_ref[g] = state * jnp.concatenate(cd, axis=1) + s_new
        return ssq

    ssq = lax.fori_loop(0, N_GROUPS, group, jnp.zeros((CHUNK, 2 * HEAD_DIM), f32))
    ssq_ref[...] = jnp.broadcast_to(jnp.sum(ssq, axis=1, keepdims=True), (CHUNK, LANES))


def _ssd_branch(proj, dt, conv_w, conv_b, a_log, d_skip, ssd_norm_w):
    bc_w = 2 * N_GROUPS * N_STATE
    dt_g = dt.reshape(R, N_GROUPS, HEADS_PER_GROUP).transpose(1, 0, 2)
    dt_gt = dt_g.transpose(0, 2, 1)
    alog = a_log.reshape(N_GROUPS, 1, HEADS_PER_GROUP)
    alogt = a_log.reshape(N_GROUPS, HEADS_PER_GROUP, 1)
    conv_b2 = conv_b.reshape(1, -1)
    dskip_e = jnp.repeat(d_skip, HEAD_DIM).reshape(1, D_INNER)

    def cur(col, width):
        return pl.BlockSpec((pl.Element(CHUNK), pl.Element(width)),
                            lambda c: (pl.multiple_of(c * CHUNK, CHUNK), col))

    def halo(col, width):
        return pl.BlockSpec((pl.Element(HALO), pl.Element(width)),
                            lambda c: (pl.multiple_of(jnp.maximum(c * CHUNK - HALO, 0), HALO), col))

    full = lambda shape: pl.BlockSpec(shape, lambda c: (0,) * len(shape))
    in_specs = [
        cur(COL_X, D_INNER), halo(COL_X, D_INNER),
        cur(COL_B, bc_w), halo(COL_B, bc_w),
        cur(COL_Z, D_INNER),
        pl.BlockSpec((N_GROUPS, CHUNK, HEADS_PER_GROUP), lambda c: (0, c, 0)),
        pl.BlockSpec((N_GROUPS, HEADS_PER_GROUP, CHUNK), lambda c: (0, 0, c)),
        full((N_GROUPS, 1, HEADS_PER_GROUP)),
        full((N_GROUPS, HEADS_PER_GROUP, 1)),
        pl.BlockSpec((CONV_W, D_INNER), lambda c: (0, 0)),
        pl.BlockSpec((CONV_W, bc_w), lambda c: (0, D_INNER // bc_w)),
        pl.BlockSpec((1, D_INNER), lambda c: (0, 0)),
        pl.BlockSpec((1, bc_w), lambda c: (0, D_INNER // bc_w)),
        full((1, D_INNER)),
        full((1, D_INNER)),
    ]
    return pl.pallas_call(
        _ssd_kernel,
        grid=(N_CHUNKS,),
        in_specs=in_specs,
        out_specs=[
            pl.BlockSpec((CHUNK, D_INNER), lambda c: (c, 0)),
            pl.BlockSpec((CHUNK, LANES), lambda c: (c, 0)),
        ],
        out_shape=[
            jax.ShapeDtypeStruct((R, D_INNER), bf16),
            jax.ShapeDtypeStruct((R, LANES), f32),
        ],
        scratch_shapes=[
            pltpu.VMEM((N_GROUPS, N_STATE, GROUP_W), f32),
            pltpu.VMEM((HALO + CHUNK, D_INNER), bf16),
            pltpu.VMEM((HALO + CHUNK, bc_w), bf16),
        ],
        compiler_params=_params(("arbitrary",)),
        name="ssd_mixer",
    )(proj, proj, proj, proj, proj, dt_g, dt_gt, alog, alogt,
      conv_w, conv_w, conv_b2, conv_b2, dskip_e, ssd_norm_w.reshape(1, D_INNER))


def _router_kernel(h_ref, nw_ref, rw_ref, rb_ref, u_ref, idx_ref, gate_ref, *, tm):
    i = pl.program_id(0)
    h = h_ref[...]
    ms = jnp.mean(h * h, axis=-1, keepdims=True)
    u = h * lax.rsqrt(ms + EPS) * nw_ref[...]
    rows = i * tm + lax.broadcasted_iota(i32, (tm, 1), 0)
    u = jnp.where(rows >= PAD_ROWS, u, 0.0)
    u_ref[...] = u
    logits = jnp.dot(u, rw_ref[...], preferred_element_type=f32, precision=lax.Precision.HIGHEST) + rb_ref[...]
    e_iota = lax.broadcasted_iota(i32, (tm, N_EXPERTS), 1)
    lane = lax.broadcasted_iota(i32, (tm, LANES), 1)
    idx_out = jnp.zeros((tm, LANES), i32)
    val_out = jnp.zeros((tm, LANES), f32)
    vals = []
    for k in range(TOP_K):
        m = jnp.max(logits, axis=1, keepdims=True)
        sel = jnp.min(jnp.where(logits == m, e_iota, N_EXPERTS), axis=1, keepdims=True)
        vals.append(m)
        idx_out = jnp.where(lane == k, sel, idx_out)
        logits = jnp.where(e_iota == sel, -jnp.inf, logits)
    exps = [jnp.exp(v - vals[0]) for v in vals]
    denom = exps[0] + exps[1] + exps[2] + exps[3]
    for k in range(TOP_K):
        val_out = jnp.where(lane == k, exps[k] / denom, val_out)
    idx_ref[...] = idx_out
    gate_ref[...] = val_out


def _router(h1, norm_w, router_w, router_b, tm=320):
    return pl.pallas_call(
        functools.partial(_router_kernel, tm=tm),
        grid=(R // tm,),
        in_specs=[
            pl.BlockSpec((tm, D_MODEL), lambda i: (i, 0)),
            pl.BlockSpec((1, D_MODEL), lambda i: (0, 0)),
            pl.BlockSpec((D_MODEL, N_EXPERTS), lambda i: (0, 0)),
            pl.BlockSpec((1, N_EXPERTS), lambda i: (0, 0)),
        ],
        out_specs=[
            pl.BlockSpec((tm, D_MODEL), lambda i: (i, 0)),
            pl.BlockSpec((tm, LANES), lambda i: (i, 0)),
            pl.BlockSpec((tm, LANES), lambda i: (i, 0)),
        ],
        out_shape=[
            jax.ShapeDtypeStruct((R, D_MODEL), f32),
            jax.ShapeDtypeStruct((R, LANES), i32),
            jax.ShapeDtypeStruct((R, LANES), f32),
        ],
        compiler_params=_params(("parallel",)),
        name="router",
    )(h1, norm_w.reshape(1, D_MODEL), router_w, router_b.reshape(1, N_EXPERTS))


def _gather_x_kernel(pos_ref, nu_ref, nv_ref, src_ref, o_ref, idx_ref, buf_ref, sem, *, tm):
    i = pl.program_id(0)
    n_used = nu_ref[0]

    @pl.when(i == 0)
    def _():
        def invert(b, carry):
            for u in range(DMA_ISSUE_UNROLL):
                p = b * DMA_ISSUE_UNROLL + u
                idx_ref[pos_ref[p]] = lax.shift_right_logical(p, 2) + PAD_ROWS
            return carry

        lax.fori_loop(0, N_PAIRS // DMA_ISSUE_UNROLL, invert, 0)

    def row_copy(src_row, slot, r):
        return pltpu.make_async_copy(src_ref.at[pl.ds(src_row, 1)], buf_ref.at[slot, pl.ds(r, 1)], sem.at[slot])

    def issue_tile(step):
        slot = step % 2
        n_blocks = nv_ref[step] // DMA_ISSUE_UNROLL

        def issue_block(b, carry):
            for u in range(DMA_ISSUE_UNROLL):
                r = b * DMA_ISSUE_UNROLL + u
                row_copy(idx_ref[step * tm + r], slot, r).start()
            return carry

        def issue(r, carry):
            row_copy(idx_ref[step * tm + r], slot, r).start()
            return carry

        lax.fori_loop(0, n_blocks, issue_block, 0)
        lax.fori_loop(n_blocks * DMA_ISSUE_UNROLL, nv_ref[step], issue, 0)

    @pl.when(i == 0)
    def _():
        buf_ref[...] = jnp.zeros_like(buf_ref)
        issue_tile(0)

    @pl.when(i + 1 < n_used)
    def _():
        issue_tile(i + 1)

    @pl.when(i < n_used)
    def _():
        slot = i % 2

        @pl.when(nv_ref[i] == tm)
        def _():
            pltpu.make_async_copy(src_ref.at[pl.ds(0, tm)], buf_ref.at[slot], sem.at[slot]).wait()

        @pl.when(nv_ref[i] < tm)
        def _():
            def drain(r, carry):
                row_copy(0, slot, 0).wait()
                return carry

            lax.fori_loop(0, nv_ref[i], drain, 0)

        o_ref[...] = buf_ref[slot].astype(o_ref.dtype)

    @pl.when(i >= n_used)
    def _():
        o_ref[...] = jnp.zeros_like(o_ref)


def _gather_x(pos_flat, n_used, tile_rows, u2, tm=MOE_TM):
    assert TOP_K == 4
    return pl.pallas_call(
        functools.partial(_gather_x_kernel, tm=tm),
        grid_spec=pltpu.PrefetchScalarGridSpec(
            num_scalar_prefetch=3,
            grid=(XS_ROWS // tm,),
            in_specs=[pl.BlockSpec(memory_space=pl.ANY)],
            out_specs=pl.BlockSpec((tm, D_MODEL), lambda i, idx, nu, nv: (i, 0)),
            scratch_shapes=[pltpu.SMEM((XS_ROWS,), i32), pltpu.VMEM((2, tm, D_MODEL), f32),
                            pltpu.SemaphoreType.DMA((2,))],
        ),
        out_shape=jax.ShapeDtypeStruct((XS_ROWS, D_MODEL), bf16),
        compiler_params=_params(("arbitrary",)),
        name="gather_x",
    )(pos_flat, n_used, tile_rows, u2)


def _grouped_kernel(ge_ref, gt_ref, gn_ref, gs_ref, ng_ref, nu_ref, x_ref, *refs, n_w, tn, nj, ocols, w_col_offsets,
                    tile_fn):
    w_hbm, b_refs, o_ref = refs[:n_w], refs[n_w:2 * n_w], refs[2 * n_w]
    wbuf_ref, wb_ref, obuf_ref, zbuf_ref, wsem, sem, zsem = refs[2 * n_w + 1:]
    g = pl.program_id(0)
    n_groups = ng_ref[0]
    active = g < n_groups
    gc = jnp.minimum(g, n_groups - 1)
    n_sub = jnp.where(active, gs_ref[gc], 0)
    tile0 = gt_ref[gc]
    total_tiles = o_ref.shape[0] // MOE_TM

    def weight_copies(grp, jj, wslot):
        e = ge_ref[grp]
        return [pltpu.make_async_copy(
            w_hbm[k].at[e, :, pl.ds(pl.multiple_of((w_col_offsets[k] + jj) * tn, tn), tn)],
            wbuf_ref.at[wslot, k], wsem.at[wslot]) for k in range(n_w)]

    @pl.when(g == 0)
    def _():
        for cp in weight_copies(0, 0, 0):
            cp.start()
        zbuf_ref[...] = jnp.zeros_like(zbuf_ref)

        def zero_copy(t, jc):
            rows = pl.ds(pl.multiple_of(t * MOE_TM, MOE_TM), MOE_TM)
            return pltpu.make_async_copy(zbuf_ref, o_ref.at[rows, pl.ds(jc * ocols, ocols)], zsem)

        def z_start(t, carry):
            for jc in range(nj):
                zero_copy(t, jc).start()
            return carry

        def z_wait(t, carry):
            for jc in range(nj):
                zero_copy(t, jc).wait()
            return carry

        lax.fori_loop(nu_ref[0], total_tiles, z_start, 0)
        lax.fori_loop(nu_ref[0], total_tiles, z_wait, 0)

    def out_copy(slot, n, tile, col):
        rows = pl.ds(pl.multiple_of(tile * MOE_TM, MOE_TM), n * MOE_TM)
        return pltpu.make_async_copy(obuf_ref.at[slot, pl.ds(0, n * MOE_TM)], o_ref.at[rows, pl.ds(col, ocols)],
                                     sem.at[slot])

    def wait_step(s):
        gs = s // nj
        sent = jnp.where(jnp.logical_and(s >= 0, gs < n_groups), gn_ref[jnp.clip(gs, 0, MAX_GROUPS - 1)], 0)
        for n in range(1, MOE_CAP + 1):

            @pl.when(sent == n)
            def _(n=n):
                out_copy(s % 2, n, 0, 0).wait()

    def col_block(j, carry):
        step = g * nj + j
        slot = step % 2

        @pl.when(j + 1 < nj)
        def _():
            for cp in weight_copies(gc, j + 1, 1 - slot):
                cp.start()

        @pl.when(jnp.logical_and(j + 1 == nj, g + 1 < n_groups))
        def _():
            for cp in weight_copies(g + 1, 0, 1 - slot):
                cp.start()

        for cp in weight_copies(gc, j, slot):
            cp.wait()
        for k in range(n_w):
            wb_ref[:, k * tn:(k + 1) * tn] = wbuf_ref[slot, k].astype(bf16)

        wait_step(step - 2)
        col = pl.multiple_of(j * ocols, ocols)
        biases = [b_refs[k][0, :, pl.ds(pl.multiple_of((w_col_offsets[k] + j) * tn, tn), tn)] for k in range(n_w)]
        for m in range(1, MOE_CAP * MOE_TM // MOE_SUB + 1):

            @pl.when(n_sub == m)
            def _(m=m):
                rows = m * MOE_SUB
                n = -(-rows // MOE_TM)
                acc = jnp.dot(x_ref[0:rows, :], wb_ref[...], preferred_element_type=f32)
                obuf_ref[slot, pl.ds(0, rows)] = tile_fn(acc, biases)
                if rows < n * MOE_TM:
                    obuf_ref[slot, pl.ds(rows, n * MOE_TM - rows)] = jnp.zeros((n * MOE_TM - rows, ocols), obuf_ref.dtype)
                out_copy(slot, n, tile0, col).start()

        return carry

    @pl.when(active)
    def _():
        lax.fori_loop(0, nj, col_block, 0)

    @pl.when(g == MAX_GROUPS - 1)
    def _():
        last = n_groups * nj - 1
        wait_step(last - 1)
        wait_step(last)


def _grouped_call(groups, n_used, x, ws, bs, *, k_dim, tn, nj, w_col_offsets, ocols, out_shape, tile_fn, name):
    g_expert, g_tile0, g_ntiles, g_nsub, n_groups = groups
    n_w = len(ws)

    def gi(g, ng):
        return jnp.minimum(g, ng[0] - 1)

    def x_map(g, ge, gt, gn, gs, ng, nu):
        return (pl.multiple_of(gt[gi(g, ng)] * MOE_TM, MOE_TM), 0)

    def b_map(g, ge, gt, gn, gs, ng, nu):
        return (ge[gi(g, ng)], 0, 0)

    return pl.pallas_call(
        functools.partial(_grouped_kernel, n_w=n_w, tn=tn, nj=nj, ocols=ocols, w_col_offsets=w_col_offsets,
                          tile_fn=tile_fn),
        grid_spec=pltpu.PrefetchScalarGridSpec(
            num_scalar_prefetch=6,
            grid=(MAX_GROUPS,),
            in_specs=[
                pl.BlockSpec((pl.Element(GROUP_ROWS), pl.Element(k_dim)), x_map),
                *[pl.BlockSpec(memory_space=pl.ANY) for _ in ws],
                *[pl.BlockSpec((1, 1, b.shape[2]), b_map) for b in bs],
            ],
            out_specs=pl.BlockSpec(memory_space=pl.ANY),
            scratch_shapes=[
                pltpu.VMEM((2, n_w, k_dim, tn), f32),
                pltpu.VMEM((k_dim, n_w * tn), bf16),
                pltpu.VMEM((2, GROUP_ROWS, ocols), out_shape.dtype),
                pltpu.VMEM((MOE_TM, ocols), out_shape.dtype),
                pltpu.SemaphoreType.DMA((2,)),
                pltpu.SemaphoreType.DMA((2,)),
                pltpu.SemaphoreType.DMA(()),
            ],
        ),
        out_shape=out_shape,
        compiler_params=_params(("arbitrary",)),
        name=name,
    )(g_expert, g_tile0, g_ntiles, g_nsub, n_groups, n_used, x, *ws, *bs)


def _moe_up(groups, n_used, xs, w_up, b_up, tn=256):
    nj = D_EXPERT // tn

    def tile_fn(acc, biases):
        gate = jnp.minimum(acc[:, :tn] + biases[0], LIMIT)
        up = jnp.clip(acc[:, tn:] + biases[1], -LIMIT, LIMIT)
        return ((up + 1.0) * (gate * _sigmoid(ALPHA * gate))).astype(bf16)

    b3 = b_up.reshape(N_EXPERTS, 1, -1)
    return _grouped_call(groups, n_used, xs, (w_up, w_up), (b3, b3), k_dim=D_MODEL, tn=tn, nj=nj,
                         w_col_offsets=(0, nj), ocols=tn,
                         out_shape=jax.ShapeDtypeStruct((XS_ROWS, D_EXPERT), bf16), tile_fn=tile_fn, name="moe_up")


def _moe_down(groups, n_used, act, w_down, b_down, tn=1024):
    return _grouped_call(groups, n_used, act, (w_down,), (b_down.reshape(N_EXPERTS, 1, -1),), k_dim=D_EXPERT,
                         tn=tn, nj=D_MODEL // tn, w_col_offsets=(0,), ocols=tn,
                         out_shape=jax.ShapeDtypeStruct((N_SLOTS, D_MODEL), f32),
                         tile_fn=lambda acc, biases: acc + biases[0], name="moe_down")


def _combine_kernel(pos_ref, h_ref, g_ref, nw_ref, y_ref, o_ref, buf_ref, sem, *, tm):
    i = pl.program_id(0)
    n = pl.num_programs(0)

    def row_copy(src_row, slot, k, r):
        return pltpu.make_async_copy(y_ref.at[pl.ds(src_row, 1)], buf_ref.at[slot, k, pl.ds(r, 1)], sem.at[slot])

    def issue_tile(step):
        slot = step % 2
        for k in range(TOP_K):

            def issue(r, carry, k=k):
                row_copy(pos_ref[k * SEQ + step * tm + r], slot, k, r).start()
                return carry

            lax.fori_loop(0, tm, issue, 0, unroll=DMA_ISSUE_UNROLL)

    @pl.when(i == 0)
    def _():
        issue_tile(0)

    @pl.when(i + 1 < n)
    def _():
        issue_tile(i + 1)

    slot = i % 2
    for k in range(TOP_K):
        pltpu.make_async_copy(y_ref.at[pl.ds(0, tm)], buf_ref.at[slot, k], sem.at[slot]).wait()

    g = g_ref[...]
    h = h_ref[...]
    for k in range(TOP_K):
        h = h + g[:, k:k + 1] * buf_ref[slot, k]
    ms = jnp.mean(h * h, axis=-1, keepdims=True)
    o_ref[...] = h * lax.rsqrt(ms + EPS) * nw_ref[...]


def _combine(pos_kt, h1, y, gates, norm_w, tm=128):
    nb = SEQ // tm
    off = (PAD_ROWS + N_META) // tm
    return pl.pallas_call(
        functools.partial(_combine_kernel, tm=tm),
        grid_spec=pltpu.PrefetchScalarGridSpec(
            num_scalar_prefetch=1,
            grid=(nb,),
            in_specs=[
                pl.BlockSpec((tm, D_MODEL), lambda i, pos: (i + off, 0)),
                pl.BlockSpec((tm, LANES), lambda i, pos: (i + off, 0)),
                pl.BlockSpec((1, D_MODEL), lambda i, pos: (0, 0)),
                pl.BlockSpec(memory_space=pl.ANY),
            ],
            out_specs=pl.BlockSpec((tm, D_MODEL), lambda i, pos: (i, 0)),
            scratch_shapes=[pltpu.VMEM((2, TOP_K, tm, D_MODEL), f32), pltpu.SemaphoreType.DMA((2,))],
        ),
        out_shape=jax.ShapeDtypeStruct((SEQ, D_MODEL), f32),
        compiler_params=_params(("arbitrary",)),
        name="combine_final_norm",
    )(pos_kt, h1, gates, norm_w.reshape(1, D_MODEL), y)


def _rank_kernel(idx_ref, rank_ref, cnt_ref, carry_ref, *, tm):
    i = pl.program_id(0)

    @pl.when(i == 0)
    def _():
        carry_ref[...] = jnp.zeros_like(carry_ref)

    idx = idx_ref[...]
    rows = i * tm + lax.broadcasted_iota(i32, (tm, 1), 0)
    valid = rows >= PAD_ROWS
    e_iota = lax.broadcasted_iota(i32, (tm, N_EXPERTS), 1)
    hits = [jnp.logical_and(idx[:, k:k + 1] == e_iota, valid) for k in range(TOP_K)]
    onehot = sum(jnp.where(h, 1.0, 0.0) for h in hits)
    earlier = lax.broadcasted_iota(i32, (tm, tm), 0) > lax.broadcasted_iota(i32, (tm, tm), 1)
    before = jnp.dot(jnp.where(earlier, 1.0, 0.0).astype(bf16), onehot.astype(bf16),
                     preferred_element_type=f32) + carry_ref[...]
    lane = lax.broadcasted_iota(i32, (tm, LANES), 1)
    out = jnp.zeros((tm, LANES), i32)
    for k in range(TOP_K):
        rk = jnp.sum(jnp.where(hits[k], before, 0.0), axis=1, keepdims=True)
        out = jnp.where(lane == k, rk.astype(i32), out)
    rank_ref[...] = out
    carry_ref[...] += jnp.sum(onehot, axis=0, keepdims=True)
    cnt_ref[...] = carry_ref[...]


def _rank_pairs(top_idx, tm=320):
    return pl.pallas_call(
        functools.partial(_rank_kernel, tm=tm),
        grid=(R // tm,),
        in_specs=[pl.BlockSpec((tm, LANES), lambda i: (i, 0))],
        out_specs=[pl.BlockSpec((tm, LANES), lambda i: (i, 0)), pl.BlockSpec((1, N_EXPERTS), lambda i: (0, 0))],
        out_shape=[jax.ShapeDtypeStruct((R, LANES), i32), jax.ShapeDtypeStruct((1, N_EXPERTS), f32)],
        scratch_shapes=[pltpu.VMEM((1, N_EXPERTS), f32)],
        compiler_params=_params(("arbitrary",)),
        name="rank_pairs",
    )(top_idx)


def _routing_tables(top_idx, rank, counts):
    counts = counts.reshape(N_EXPERTS).astype(i32)
    tiles = (counts + MOE_TM - 1) // MOE_TM
    tile_end = jnp.cumsum(tiles)
    tile_start = tile_end - tiles
    e_ids = jnp.arange(N_EXPERTS, dtype=i32)
    tok_e = top_idx[PAD_ROWS:, :TOP_K]
    tok_start = jnp.sum(jnp.where(tok_e[..., None] == e_ids, tile_start * MOE_TM, 0), axis=-1)
    pos = tok_start + rank[PAD_ROWS:, :TOP_K]
    n_used = tile_end[-1:].astype(i32)
    tile_ids = jnp.arange(XS_ROWS // MOE_TM, dtype=i32)
    tile_e = jnp.minimum(jnp.sum(tile_end[None, :] <= tile_ids[:, None], axis=1), N_EXPERTS - 1)
    tile_rows = jnp.clip(counts[tile_e] - (tile_ids - tile_start[tile_e]) * MOE_TM, 0, MOE_TM).astype(i32)
    tile_rows = jnp.where(tile_ids < n_used[0], tile_rows, 0)
    groups = (tiles + MOE_CAP - 1) // MOE_CAP
    group_end = jnp.cumsum(groups)
    g_ids = jnp.arange(MAX_GROUPS, dtype=i32)
    g_expert = jnp.minimum(jnp.sum(group_end[None, :] <= g_ids[:, None], axis=1), N_EXPERTS - 1).astype(i32)
    g_local = g_ids - (group_end - groups)[g_expert]
    g_tile0 = (tile_start[g_expert] + g_local * MOE_CAP).astype(i32)
    g_ntiles = jnp.clip(tiles[g_expert] - g_local * MOE_CAP, 0, MOE_CAP).astype(i32)
    g_rows = jnp.clip(counts[g_expert] - g_local * GROUP_ROWS, 0, GROUP_ROWS)
    g_nsub = ((g_rows + MOE_SUB - 1) // MOE_SUB).astype(i32)
    n_groups = group_end[-1:].astype(i32)
    return pos, n_used, tile_rows, (g_expert, g_tile0, g_ntiles, g_nsub, n_groups)


def kernel(x, meta_tokens, norm_mix_w, w_in, conv_w, conv_b, dt_bias, a_log, d_skip, ssd_norm_w, w_ssd_out,
           pool_w, pool_scale, w_out, norm_ffn_w, router_w, router_b, w_up, b_up, w_down, b_down, norm_final_w):
    assert x.shape == (1, SEQ, D_MODEL) and norm_mix_w.shape[0] == 1
    x2 = x[0]
    head = jnp.concatenate([jnp.zeros((PAD_ROWS, D_MODEL), f32), meta_tokens.astype(f32)], axis=0)

    u = _rms_norm_rows(x2, head, norm_mix_w[0])
    proj = _in_proj_main(u, w_in[0])
    dt = _in_proj_dt(u, w_in[0], dt_bias[0])
    gates = _in_proj_gates(u, w_in[0])
    ma = _pool_branch(proj, gates, pool_w[0], pool_scale[0])
    vw, ssq = _ssd_branch(proj, dt, conv_w[0], conv_b[0], a_log[0], d_skip[0], ssd_norm_w[0])
    merged = _ssd_out_merge(vw, w_ssd_out[0], ssq, ma, gates)
    h1 = _out_proj(merged, w_out[0], head, x2)

    u2, top_idx, top_gate = _router(h1, norm_ffn_w[0], router_w[0], router_b[0])
    rank, counts = _rank_pairs(top_idx)
    pos, n_used, tile_rows, groups = _routing_tables(top_idx, rank, counts)
    xs = _gather_x(pos.reshape(-1), n_used, tile_rows, u2)
    act = _moe_up(groups, n_used, xs, w_up[0], b_up[0])
    y = _moe_down(groups, n_used, act, w_down[0], b_down[0])
    pos_kt = pos[N_META:].T.reshape(-1)
    out = _combine(pos_kt, h1, y, top_gate, norm_final_w)
    return out[None]
```
